```python
import jax, jax.numpy as jnp
from jax import lax
import numpy as np

D_MODEL = 1024
BATCH = 4
SEQ = 4096
DEPTH = 1

FOX_HEADS = 8
FOX_HEAD_DIM = 64
FOX_WIDTH = FOX_HEADS * FOX_HEAD_DIM
Q_BLOCK = 128
RET_HEADS = 8
RET_QK_DIM = 64
RET_V_DIM = 128
RET_QK_WIDTH = RET_HEADS * RET_QK_DIM
RET_V_WIDTH = RET_HEADS * RET_V_DIM
RET_CHUNK = 128
ROPE_BASE = 10000.0
SPLIT_SIZES = (FOX_WIDTH, FOX_WIDTH, FOX_WIDTH, FOX_HEADS,
               RET_QK_WIDTH, RET_QK_WIDTH, RET_V_WIDTH, RET_V_WIDTH,
               D_MODEL, D_MODEL)
IN_WIDTH = 3 * FOX_WIDTH + FOX_HEADS + 2 * RET_QK_WIDTH + 2 * RET_V_WIDTH + 2 * D_MODEL
N_EXPERTS = 256
TOP_K = 8
N_GROUPS = 8
TOPK_GROUPS = 4
EXPERT_DIM = 256
SHARED_DIM = 256
ROUTED_SCALE = 2.5
EXPERT_BLOCK = 128
NORM_EPS = 1e-6

kernel_name = "fox_retnet_griffin_merge_dsv3_moe_adaln"


def rms_norm(x, g):
    xf = x.astype(jnp.float32)
    y = xf * lax.rsqrt(jnp.mean(xf * xf, axis=-1, keepdims=True) + NORM_EPS)
    return (y * g.astype(jnp.float32)).astype(x.dtype)


def forgetting_attention(q, k, v, log_f):
    b, s, h, dh = q.shape
    q = q.transpose(0, 2, 1, 3)
    k = k.transpose(0, 2, 1, 3)
    v = v.transpose(0, 2, 1, 3)
    cum_f = jnp.cumsum(log_f, axis=1).transpose(0, 2, 1)
    scale = dh ** -0.5
    outs = []
    for i in range(s // Q_BLOCK):
        q0, q1 = i * Q_BLOCK, (i + 1) * Q_BLOCK
        qb = q[:, :, q0:q1]
        kp, vp = k[:, :, :q1], v[:, :, :q1]
        logits = jnp.einsum('bhqd,bhkd->bhqk', qb, kp).astype(jnp.float32) * scale
        logits = logits + cum_f[:, :, q0:q1, None] - cum_f[:, :, None, :q1]
        causal = jnp.arange(q1)[None, :] <= (q0 + jnp.arange(Q_BLOCK))[:, None]
        logits = jnp.where(causal, logits, -jnp.inf)
        p = jax.nn.softmax(logits, axis=-1)
        outs.append(jnp.einsum('bhqk,bhkd->bhqd', p.astype(vp.dtype), vp))
    o = jnp.concatenate(outs, axis=2)
    return o.transpose(0, 2, 1, 3).reshape(b, s, h * dh)


def rotary(x, pos):
    half = x.shape[-1] // 2
    inv_freq = ROPE_BASE ** (-jnp.arange(half, dtype=jnp.float32) / half)
    ang = pos.astype(jnp.float32)[:, None] * inv_freq[None, :]
    cos, sin = jnp.cos(ang)[None, :, None, :], jnp.sin(ang)[None, :, None, :]
    x1, x2 = x[..., :half], x[..., half:]
    return jnp.concatenate([x1 * cos - x2 * sin, x1 * sin + x2 * cos], axis=-1)


def chunkwise_retention(q, k, v):
    b, s, h, dk = q.shape
    dv = v.shape[-1]
    n = s // RET_CHUNK
    c = RET_CHUNK
    log_gamma = jnp.log1p(-jnp.exp2(-5.0 - jnp.arange(h, dtype=jnp.float32)))
    to_chunks = lambda t: t.reshape(b, n, c, h, t.shape[-1]).transpose(0, 3, 1, 2, 4)
    qc, kc, vc = to_chunks(q), to_chunks(k * dk ** -0.5), to_chunks(v)
    idx = jnp.arange(c, dtype=jnp.float32)
    diff = idx[:, None] - idx[None, :]
    intra_decay = jnp.where(diff >= 0,
                            jnp.exp(jnp.maximum(diff, 0.0)[None] * log_gamma[:, None, None]),
                            0.0)
    scores = jnp.einsum('bhnid,bhnjd->bhnij', qc, kc) * intra_decay[None, :, None]
    intra = jnp.einsum('bhnij,bhnje->bhnie', scores, vc)
    zeta = jnp.exp((c - 1.0 - idx)[None, :] * log_gamma[:, None])
    xi = jnp.exp((idx + 1.0)[None, :] * log_gamma[:, None])
    chunk_decay = jnp.exp(c * log_gamma)[:, None, None]
    kv = jnp.einsum('bhncd,bhnce->bhnde', kc * zeta[None, :, None, :, None], vc)
    kv = jnp.moveaxis(kv, 2, 0)

    def step(state, kv_n):
        return chunk_decay * state + kv_n, state

    _, prev = lax.scan(step, jnp.zeros((b, h, dk, dv), jnp.float32), kv)
    prev = jnp.moveaxis(prev, 0, 2)
    cross = jnp.einsum('bhncd,bhnde->bhnce', qc, prev) * xi[None, :, None, :, None]
    o = intra + cross
    return o.transpose(0, 2, 3, 1, 4).reshape(b, s, h, dv)


def mixing_sublayer(h, w_in, b_forget, ret_gn_g, w_branch_a, w_branch_b, w_out):
    b, s, _ = h.shape
    proj = jnp.einsum('bsd,de->bse', h, w_in)
    fq, fk, fv, ff, rq, rk, rv, rg, ga, gb = jnp.split(
        proj, list(np.cumsum(SPLIT_SIZES)[:-1]), axis=-1)
    heads_a = lambda t: t.reshape(b, s, FOX_HEADS, FOX_HEAD_DIM)
    log_f = jax.nn.log_sigmoid(ff.astype(jnp.float32) + b_forget.astype(jnp.float32))
    y_a = forgetting_attention(heads_a(fq), heads_a(fk), heads_a(fv), log_f)
    y_a = jnp.einsum('bse,ed->bsd', y_a, w_branch_a)
    pos = jnp.arange(s)
    rq = rotary(rq.astype(jnp.float32).reshape(b, s, RET_HEADS, RET_QK_DIM), pos)
    rk = rotary(rk.astype(jnp.float32).reshape(b, s, RET_HEADS, RET_QK_DIM), pos)
    rv = rv.astype(jnp.float32).reshape(b, s, RET_HEADS, RET_V_DIM)
    o = chunkwise_retention(rq, rk, rv)
    mu = jnp.mean(o, axis=-1, keepdims=True)
    var = jnp.mean(jnp.square(o - mu), axis=-1, keepdims=True)
    o = ((o - mu) * lax.rsqrt(var + NORM_EPS)).reshape(b, s, RET_V_WIDTH)
    o = (o * ret_gn_g.astype(jnp.float32)).astype(h.dtype) * jax.nn.silu(rg)
    y_b = jnp.einsum('bse,ed->bsd', o, w_branch_b)
    merged = jax.nn.sigmoid(ga) * y_a + jax.nn.sigmoid(gb) * y_b
    return jnp.einsum('bsd,de->bse', merged, w_out)


def swiglu(x, wg, wu, wd):
    return (jax.nn.silu(x @ wg) * (x @ wu)) @ wd


def moe_sublayer(h, w_router, router_bias, w_exp_gate, w_exp_up, w_exp_down,
                 w_sh_gate, w_sh_up, w_sh_down):
    b, s, d = h.shape
    hf = h.reshape(b * s, d)
    t = hf.shape[0]
    scores = jax.nn.sigmoid((hf @ w_router).astype(jnp.float32))
    sel = scores + router_bias.astype(jnp.float32)
    grp = sel.reshape(t, N_GROUPS, N_EXPERTS // N_GROUPS)
    group_scores = lax.top_k(grp, 2)[0].sum(-1)
    top_groups = lax.top_k(group_scores, TOPK_GROUPS)[1]
    group_mask = jax.nn.one_hot(top_groups, N_GROUPS, dtype=jnp.float32).sum(-2) > 0
    expert_mask = jnp.repeat(group_mask, N_EXPERTS // N_GROUPS, axis=-1)
    _, top_e = lax.top_k(jnp.where(expert_mask, sel, -jnp.inf), TOP_K)
    top_w = jnp.take_along_axis(scores, top_e, axis=-1)
    top_w = top_w / jnp.sum(top_w, axis=-1, keepdims=True) * ROUTED_SCALE
    a = t * TOP_K
    e_flat = top_e.reshape(a)
    w_flat = top_w.reshape(a).astype(hf.dtype)
    tok_flat = jnp.repeat(jnp.arange(t, dtype=jnp.int32), TOP_K)
    order = jnp.argsort(e_flat)
    e_sorted = e_flat[order]
    counts = jnp.zeros((N_EXPERTS,), jnp.int32).at[e_flat].add(1)
    padded = (counts + EXPERT_BLOCK - 1) // EXPERT_BLOCK * EXPERT_BLOCK
    pad_end = jnp.cumsum(padded)
    pad_start = pad_end - padded
    start = jnp.cumsum(counts) - counts
    dest = pad_start[e_sorted] + jnp.arange(a, dtype=jnp.int32) - start[e_sorted]
    n_blocks = (a + N_EXPERTS * (EXPERT_BLOCK - 1) + EXPERT_BLOCK - 1) // EXPERT_BLOCK
    p = n_blocks * EXPERT_BLOCK
    buf_tok = jnp.full((p,), t, jnp.int32).at[dest].set(tok_flat[order])
    buf_w = jnp.zeros((p,), hf.dtype).at[dest].set(w_flat[order])
    block_expert = jnp.minimum(
        jnp.searchsorted(pad_end, jnp.arange(n_blocks, dtype=jnp.int32) * EXPERT_BLOCK,
                         side='right'), N_EXPERTS - 1)
    hp = jnp.concatenate([hf, jnp.zeros((1, d), hf.dtype)], axis=0)

    def block_fn(args):
        tok, wb, e = args
        xb = hp[tok]
        return swiglu(xb, w_exp_gate[e], w_exp_up[e], w_exp_down[e]) * wb[:, None]

    out = lax.map(block_fn, (buf_tok.reshape(n_blocks, EXPERT_BLOCK),
                             buf_w.reshape(n_blocks, EXPERT_BLOCK), block_expert))
    routed = jax.ops.segment_sum(out.reshape(p, d), buf_tok, num_segments=t + 1)[:t]
    shared = swiglu(hf, w_sh_gate, w_sh_up, w_sh_down)
    return (routed + shared).reshape(b, s, d)


def setup_inputs(seed: int = 0) -> dict:
    key = jax.random.key(seed)
    ks = jax.random.split(key, 24)
    nrm = lambda k, shape, sc: jax.random.normal(k, shape, jnp.float32) * sc
    return {
        "x": nrm(ks[0], (BATCH, SEQ, D_MODEL), 1.0),
        "c": nrm(ks[1], (BATCH, D_MODEL), 1.0),
        "w_ada": nrm(ks[2], (DEPTH, D_MODEL, 6 * D_MODEL), 0.1 * D_MODEL ** -0.5),
        "b_ada": nrm(ks[3], (DEPTH, 6 * D_MODEL), 0.1),
        "norm1_g": 1.0 + nrm(ks[4], (DEPTH, D_MODEL), 0.02),
        "w_in": nrm(ks[5], (DEPTH, D_MODEL, IN_WIDTH), D_MODEL ** -0.5),
        "b_forget": 3.0 + nrm(ks[6], (DEPTH, FOX_HEADS), 0.5),
        "ret_gn_g": 1.0 + nrm(ks[7], (DEPTH, RET_V_WIDTH), 0.02),
        "w_branch_a": nrm(ks[8], (DEPTH, FOX_WIDTH, D_MODEL), FOX_WIDTH ** -0.5),
        "w_branch_b": nrm(ks[9], (DEPTH, RET_V_WIDTH, D_MODEL), RET_V_WIDTH ** -0.5),
        "w_out": nrm(ks[10], (DEPTH, D_MODEL, D_MODEL), D_MODEL ** -0.5),
        "norm2_g": 1.0 + nrm(ks[11], (DEPTH, D_MODEL), 0.02),
        "w_router": nrm(ks[12], (DEPTH, D_MODEL, N_EXPERTS), D_MODEL ** -0.5),
        "router_bias": nrm(ks[13], (DEPTH, N_EXPERTS), 0.01),
        "w_exp_gate": nrm(ks[14], (DEPTH, N_EXPERTS, D_MODEL, EXPERT_DIM), D_MODEL ** -0.5),
        "w_exp_up": nrm(ks[15], (DEPTH, N_EXPERTS, D_MODEL, EXPERT_DIM), D_MODEL ** -0.5),
        "w_exp_down": nrm(ks[16], (DEPTH, N_EXPERTS, EXPERT_DIM, D_MODEL), EXPERT_DIM ** -0.5),
        "w_sh_gate": nrm(ks[17], (DEPTH, D_MODEL, SHARED_DIM), D_MODEL ** -0.5),
        "w_sh_up": nrm(ks[18], (DEPTH, D_MODEL, SHARED_DIM), D_MODEL ** -0.5),
        "w_sh_down": nrm(ks[19], (DEPTH, SHARED_DIM, D_MODEL), SHARED_DIM ** -0.5),
        "final_g": 1.0 + nrm(ks[20], (D_MODEL,), 0.02),
    }


def reference(x, c, w_ada, b_ada, norm1_g, w_in, b_forget, ret_gn_g, w_branch_a,
              w_branch_b, w_out, norm2_g, w_router, router_bias, w_exp_gate, w_exp_up,
              w_exp_down, w_sh_gate, w_sh_up, w_sh_down, final_g):
    for l in range(DEPTH):
        mod = jax.nn.silu(c) @ w_ada[l] + b_ada[l]
        shift1, scale1, gate1, shift2, scale2, gate2 = jnp.split(mod[:, None, :], 6, axis=-1)
        h = rms_norm(x, norm1_g[l]) * (1.0 + scale1) + shift1
        x = x + gate1 * mixing_sublayer(h, w_in[l], b_forget[l], ret_gn_g[l],
                                        w_branch_a[l], w_branch_b[l], w_out[l])
        h = rms_norm(x, norm2_g[l]) * (1.0 + scale2) + shift2
        x = x + gate2 * moe_sublayer(h, w_router[l], router_bias[l], w_exp_gate[l],
                                     w_exp_up[l], w_exp_down[l], w_sh_gate[l],
                                     w_sh_up[l], w_sh_down[l])
    return rms_norm(x, final_g)
```

```python
import functools

import jax
import jax.numpy as jnp
import numpy as np
from jax import lax
from jax.experimental import pallas as pl
from jax.experimental.pallas import tpu as pltpu

F32 = jnp.float32
BF16 = jnp.bfloat16

D_MODEL = 1024
FOX_HEADS = 8
FOX_HEAD_DIM = 64
FOX_WIDTH = 512
RET_HEADS = 8
RET_QK_DIM = 64
RET_V_DIM = 128
RET_QK_WIDTH = 512
RET_V_WIDTH = 1024
RET_CHUNK = 128
ROPE_BASE = 10000.0
N_EXPERTS = 256
TOP_K = 8
N_GROUPS = 8
TOPK_GROUPS = 4
EXPERT_DIM = 256
ROUTED_SCALE = 2.5
NORM_EPS = 1e-6

LANES = 128
VMEM_LIMIT = 56 * 1024 * 1024

COL_RV, COL_RG, COL_GA, COL_GB = 0, 1024, 2048, 3072
COL_FQ, COL_FK, COL_FV, COL_RQ, COL_RK = 4096, 4608, 5120, 5632, 6144
PROJ_WIDTH = 6656

TM_PROJ = 512
TN_PROJ = 512
T_ATTN = 512
T_RET = 512
TM_POST = 512
ROWS_MOE = 256
TM_FINAL = 1024


def _sigmoid(z):
    return 1.0 / (1.0 + jnp.exp(-z))


def _silu(z):
    return z * _sigmoid(z)


def _params(*sem):
    return pltpu.CompilerParams(dimension_semantics=sem, vmem_limit_bytes=VMEM_LIMIT)


def _adaln_kernel(c_ref, w_ref, b_ref, o_ref):
    a = _silu(c_ref[...]).astype(BF16)
    o_ref[...] = jnp.dot(a, w_ref[...].astype(BF16), preferred_element_type=F32) + b_ref[...]


def _adaln_mod(c_pad, w_ada, b_ada):
    n = w_ada.shape[1]
    tn = 1024
    return pl.pallas_call(
        _adaln_kernel,
        grid=(n // tn,),
        in_specs=[pl.BlockSpec((8, D_MODEL), lambda j: (0, 0)),
                  pl.BlockSpec((D_MODEL, tn), lambda j: (0, j)),
                  pl.BlockSpec((1, tn), lambda j: (0, j))],
        out_specs=pl.BlockSpec((8, tn), lambda j: (0, j)),
        out_shape=jax.ShapeDtypeStruct((8, n), F32),
        compiler_params=_params("arbitrary"),
        name="adaln_mod",
    )(c_pad, w_ada, b_ada.reshape(1, n))


def _inproj_kernel(x_ref, mod_ref, g_ref, w_ref, wff_ref, bf_ref, cos_ref, sin_ref,
                   proj_ref, logf_ref, h_sc, *, tiles_per_batch):
    i = pl.program_id(0)
    j = pl.program_id(1)

    @pl.when(j == 0)
    def _():
        b = i // tiles_per_batch
        x = x_ref[...]
        y = x * lax.rsqrt(jnp.mean(x * x, axis=-1, keepdims=True) + NORM_EPS) * g_ref[...]
        shift = mod_ref[pl.ds(b, 1), 0:D_MODEL]
        scale = mod_ref[pl.ds(b, 1), D_MODEL:2 * D_MODEL]
        hb = (y * (1.0 + scale) + shift).astype(BF16)
        h_sc[...] = hb
        z = jnp.dot(hb, wff_ref[...], preferred_element_type=F32) + bf_ref[...]
        logf_ref[...] = jnp.minimum(z, 0.0) - jnp.log1p(jnp.exp(-jnp.abs(z)))

    acc = jnp.dot(h_sc[...], w_ref[...], preferred_element_type=F32)

    j_ga, j_fq, j_fk, j_rq, j_rk = (COL_GA // TN_PROJ, COL_FQ // TN_PROJ, COL_FK // TN_PROJ,
                                    COL_RQ // TN_PROJ, COL_RK // TN_PROJ)

    @pl.when((j < j_ga) | ((j >= j_fk) & (j < j_rq)))
    def _():
        proj_ref[...] = acc.astype(BF16)

    @pl.when((j >= j_ga) & (j < j_fq))
    def _():
        proj_ref[...] = _sigmoid(acc).astype(BF16)

    @pl.when(j == j_fq)
    def _():
        proj_ref[...] = (acc * (FOX_HEAD_DIM ** -0.5)).astype(BF16)

    @pl.when(j >= j_rq)
    def _():
        cos = cos_ref[...]
        sin = sin_ref[...]
        lane = lax.broadcasted_iota(jnp.int32, cos.shape, 1)
        first_half = (lane % RET_QK_DIM) < (RET_QK_DIM // 2)
        k_scale = jnp.where(j == j_rk, RET_QK_DIM ** -0.5, 1.0).astype(F32)
        for cb in range(TN_PROJ // LANES):
            xs = acc[:, cb * LANES:(cb + 1) * LANES]
            up = pltpu.roll(xs, LANES - RET_QK_DIM // 2, axis=1)
            dn = pltpu.roll(xs, RET_QK_DIM // 2, axis=1)
            rot = xs * cos + jnp.where(first_half, up, dn) * sin
            proj_ref[:, cb * LANES:(cb + 1) * LANES] = (rot * k_scale).astype(BF16)


def _in_proj(x2, mod, norm_g, w_p, w_ff, b_ff, cos_t, sin_t, seq):
    t = x2.shape[0]
    tiles_per_batch = seq // TM_PROJ
    kern = functools.partial(_inproj_kernel, tiles_per_batch=tiles_per_batch)
    return pl.pallas_call(
        kern,
        grid=(t // TM_PROJ, PROJ_WIDTH // TN_PROJ),
        in_specs=[pl.BlockSpec((TM_PROJ, D_MODEL), lambda i, j: (i, 0)),
                  pl.BlockSpec(mod.shape, lambda i, j: (0, 0)),
                  pl.BlockSpec((1, D_MODEL), lambda i, j: (0, 0)),
                  pl.BlockSpec((D_MODEL, TN_PROJ), lambda i, j: (0, j)),
                  pl.BlockSpec((D_MODEL, LANES), lambda i, j: (0, 0)),
                  pl.BlockSpec((1, LANES), lambda i, j: (0, 0)),
                  pl.BlockSpec((TM_PROJ, LANES), lambda i, j: (i % tiles_per_batch, 0)),
                  pl.BlockSpec((TM_PROJ, LANES), lambda i, j: (i % tiles_per_batch, 0))],
        out_specs=[pl.BlockSpec((TM_PROJ, TN_PROJ), lambda i, j: (i, j)),
                   pl.BlockSpec((TM_PROJ, LANES), lambda i, j: (i, 0))],
        out_shape=[jax.ShapeDtypeStruct((t, PROJ_WIDTH), BF16),
                   jax.ShapeDtypeStruct((t, LANES), F32)],
        scratch_shapes=[pltpu.VMEM((TM_PROJ, D_MODEL), BF16)],
        compiler_params=_params("arbitrary", "arbitrary"),
        name="in_proj",
    )(x2, mod, norm_g, w_p, w_ff, b_ff, cos_t, sin_t)


def _attn_kernel(q_ref, k_ref, v_ref, f_ref, o_ref):
    hp = pl.program_id(1)
    qi = pl.program_id(2)
    tq = T_ATTN
    q = q_ref[...]
    lane = lax.broadcasted_iota(jnp.int32, (tq, LANES), 1)
    row = lax.broadcasted_iota(jnp.int32, (tq, tq), 0)
    col = lax.broadcasted_iota(jnp.int32, (tq, tq), 1)
    outs = []
    for hh in range(2):
        sel = (lane < FOX_HEAD_DIM) if hh == 0 else (lane >= FOX_HEAD_DIM)
        qm = jnp.where(sel, q, jnp.zeros_like(q))

        def step(kb, carry, masked, qm=qm, hh=hh):
            m, l, acc = carry
            ks = pl.multiple_of(kb * tq, tq)
            k = k_ref[pl.ds(ks, tq), :]
            v = v_ref[pl.ds(ks, tq), :]
            s = lax.dot_general(qm, k, (((1,), (1,)), ((), ())), preferred_element_type=F32)
            s = s - f_ref[0, pl.ds(2 * hp + hh, 1), pl.ds(ks, tq)]
            if masked:
                s = jnp.where(col <= row, s, -jnp.inf)
            m_new = jnp.maximum(m, jnp.max(s, axis=-1, keepdims=True))
            alpha = jnp.exp(m - m_new)
            p = jnp.exp(s - m_new)
            l = alpha * l + jnp.sum(p, axis=-1, keepdims=True)
            acc = alpha * acc + jnp.dot(p.astype(BF16), v, preferred_element_type=F32)
            return m_new, l, acc

        carry = (jnp.full((tq, 1), -jnp.inf, F32), jnp.zeros((tq, 1), F32),
                 jnp.zeros((tq, LANES), F32))
        carry = lax.fori_loop(0, qi, lambda kb, c: step(kb, c, False), carry)
        m, l, acc = step(qi, carry, True)
        outs.append(acc / l)
    o_ref[...] = jnp.where(lane < FOX_HEAD_DIM, outs[0], outs[1]).astype(BF16)


def _fox_attn(proj, cum_f, batch, seq):
    t = proj.shape[0]
    nq = seq // T_ATTN
    cq, ck, cv = COL_FQ // LANES, COL_FK // LANES, COL_FV // LANES
    return pl.pallas_call(
        _attn_kernel,
        grid=(batch, FOX_HEADS // 2, nq),
        in_specs=[pl.BlockSpec((T_ATTN, LANES), lambda b, hp, qi: (b * nq + qi, cq + hp)),
                  pl.BlockSpec((seq, LANES), lambda b, hp, qi: (b, ck + hp)),
                  pl.BlockSpec((seq, LANES), lambda b, hp, qi: (b, cv + hp)),
                  pl.BlockSpec((1, FOX_HEADS, seq), lambda b, hp, qi: (b, 0, 0))],
        out_specs=pl.BlockSpec((T_ATTN, LANES), lambda b, hp, qi: (b * nq + qi, hp)),
        out_shape=jax.ShapeDtypeStruct((t, FOX_WIDTH), BF16),
        compiler_params=_params("arbitrary", "arbitrary", "arbitrary"),
        name="fox_attn",
    )(proj, proj, proj, cum_f)


def _ret_kernel(q_ref, k_ref, v_ref, rg_ref, gn_ref, dec_ref, zeta_ref, xi_ref, cd_ref,
                o_ref, st_sc):
    h = pl.program_id(1)
    ri = pl.program_id(2)
    c = RET_CHUNK

    @pl.when(ri == 0)
    def _():
        st_sc[...] = jnp.zeros_like(st_sc)

    lane = lax.broadcasted_iota(jnp.int32, (T_RET, LANES), 1)
    lo = (h % 2) * RET_QK_DIM
    sel = (lane >= lo) & (lane < lo + RET_QK_DIM)
    qm = jnp.where(sel, q_ref[...], jnp.zeros((T_RET, LANES), BF16))
    km = jnp.where(sel, k_ref[...], jnp.zeros((T_RET, LANES), BF16))
    decay = dec_ref[0]
    zeta = zeta_ref[0]
    xi = xi_ref[0]
    state = st_sc[...]
    for n in range(T_RET // c):
        qc = qm[n * c:(n + 1) * c]
        kc = km[n * c:(n + 1) * c]
        vc = v_ref[n * c:(n + 1) * c, :]
        sc = lax.dot_general(qc, kc, (((1,), (1,)), ((), ())), preferred_element_type=F32) * decay
        intra = jnp.dot(sc.astype(BF16), vc, preferred_element_type=F32)
        cross = jnp.dot(qc, state.astype(BF16), preferred_element_type=F32) * xi
        o = intra + cross
        kz = (kc.astype(F32) * zeta).astype(BF16)
        kv = lax.dot_general(kz, vc, (((0,), (0,)), ((), ())), preferred_element_type=F32)
        state = cd_ref[0] * state + kv
        mu = jnp.mean(o, axis=-1, keepdims=True)
        d = o - mu
        var = jnp.mean(d * d, axis=-1, keepdims=True)
        on = d * lax.rsqrt(var + NORM_EPS) * gn_ref[...]
        rg = rg_ref[n * c:(n + 1) * c, :].astype(F32)
        o_ref[n * c:(n + 1) * c, :] = (on * _silu(rg)).astype(BF16)
    st_sc[...] = state


def _retention(proj, gn_g, dec_t, zeta_t, xi_t, cd_t, batch, seq):
    t = proj.shape[0]
    nr = seq // T_RET
    cq, ck, cv, cg = COL_RQ // LANES, COL_RK // LANES, COL_RV // LANES, COL_RG // LANES
    tok = lambda b, h, ri: b * nr + ri
    tab = pl.BlockSpec((1, RET_CHUNK, LANES), lambda b, h, ri: (h, 0, 0))
    return pl.pallas_call(
        _ret_kernel,
        grid=(batch, RET_HEADS, nr),
        in_specs=[pl.BlockSpec((T_RET, LANES), lambda b, h, ri: (tok(b, h, ri), cq + h // 2)),
                  pl.BlockSpec((T_RET, LANES), lambda b, h, ri: (tok(b, h, ri), ck + h // 2)),
                  pl.BlockSpec((T_RET, LANES), lambda b, h, ri: (tok(b, h, ri), cv + h)),
                  pl.BlockSpec((T_RET, LANES), lambda b, h, ri: (tok(b, h, ri), cg + h)),
                  pl.BlockSpec((1, LANES), lambda b, h, ri: (0, h)),
                  tab, tab, tab, tab],
        out_specs=pl.BlockSpec((T_RET, LANES), lambda b, h, ri: (tok(b, h, ri), h)),
        out_shape=jax.ShapeDtypeStruct((t, RET_V_WIDTH), BF16),
        scratch_shapes=[pltpu.VMEM((LANES, RET_V_DIM), F32)],
        compiler_params=_params("arbitrary", "arbitrary", "arbitrary"),
        name="retention",
    )(proj, proj, proj, proj, gn_g, dec_t, zeta_t, xi_t, cd_t)


def _retention_tables():
    c = RET_CHUNK
    log_gamma = jnp.log1p(-jnp.exp2(-5.0 - jnp.arange(RET_HEADS, dtype=F32)))
    idx = jnp.arange(c, dtype=F32)
    diff = idx[:, None] - idx[None, :]
    dec = jnp.where(diff >= 0,
                    jnp.exp(jnp.maximum(diff, 0.0)[None] * log_gamma[:, None, None]), 0.0)
    zeta = jnp.exp((c - 1.0 - idx)[None, :] * log_gamma[:, None])
    xi = jnp.exp((idx + 1.0)[None, :] * log_gamma[:, None])
    cd = jnp.exp(c * log_gamma)
    bc = lambda v: jnp.broadcast_to(v[:, :, None], (RET_HEADS, c, LANES))
    cd_t = jnp.broadcast_to(cd[:, None, None], (RET_HEADS, LANES, RET_V_DIM))
    return dec, bc(zeta), bc(xi), cd_t


def _post_kernel(attn_ref, ret_ref, ga_ref, gb_ref, x_ref, mod_ref, g2_ref,
                 wa_ref, wb_ref, wo_ref, wr_ref, wsg_ref, wsu_ref, wsd_ref,
                 base_ref, h2_ref, score_ref, *, tiles_per_batch):
    b = pl.program_id(0) // tiles_per_batch
    ya = jnp.dot(attn_ref[...], wa_ref[...], preferred_element_type=F32)
    yb = jnp.dot(ret_ref[...], wb_ref[...], preferred_element_type=F32)
    merged = ga_ref[...].astype(F32) * ya + gb_ref[...].astype(F32) * yb
    mix = jnp.dot(merged.astype(BF16), wo_ref[...], preferred_element_type=F32)
    d = D_MODEL
    gate1 = mod_ref[pl.ds(b, 1), 2 * d:3 * d]
    shift2 = mod_ref[pl.ds(b, 1), 3 * d:4 * d]
    scale2 = mod_ref[pl.ds(b, 1), 4 * d:5 * d]
    gate2 = mod_ref[pl.ds(b, 1), 5 * d:6 * d]
    x1 = x_ref[...] + gate1 * mix
    y = x1 * lax.rsqrt(jnp.mean(x1 * x1, axis=-1, keepdims=True) + NORM_EPS) * g2_ref[...]
    h2 = y * (1.0 + scale2) + shift2
    logits = jnp.dot(h2, wr_ref[...], preferred_element_type=F32,
                     precision=lax.Precision.HIGHEST)
    score_ref[...] = _sigmoid(logits)
    h2b = h2.astype(BF16)
    h2_ref[...] = h2b
    g = jnp.dot(h2b, wsg_ref[...], preferred_element_type=F32)
    u = jnp.dot(h2b, wsu_ref[...], preferred_element_type=F32)
    shared = jnp.dot((_silu(g) * u).astype(BF16), wsd_ref[...], preferred_element_type=F32)
    base_ref[...] = x1 + gate2 * shared


def _post_mix(attn, ret, proj, x2, mod, norm2_g, wa, wb, wo, wr, wsg, wsu, wsd, seq):
    t = x2.shape[0]
    tm = TM_POST
    kern = functools.partial(_post_kernel, tiles_per_batch=seq // tm)
    full = lambda a: pl.BlockSpec(a.shape, lambda i: (0,) * a.ndim)
    return pl.pallas_call(
        kern,
        grid=(t // tm,),
        in_specs=[pl.BlockSpec((tm, FOX_WIDTH), lambda i: (i, 0)),
                  pl.BlockSpec((tm, RET_V_WIDTH), lambda i: (i, 0)),
                  pl.BlockSpec((tm, D_MODEL), lambda i: (i, COL_GA // D_MODEL)),
                  pl.BlockSpec((tm, D_MODEL), lambda i: (i, COL_GB // D_MODEL)),
                  pl.BlockSpec((tm, D_MODEL), lambda i: (i, 0)),
                  full(mod), full(norm2_g), full(wa), full(wb), full(wo), full(wr),
                  full(wsg), full(wsu), full(wsd)],
        out_specs=[pl.BlockSpec((tm, D_MODEL), lambda i: (i, 0)),
                   pl.BlockSpec((tm, D_MODEL), lambda i: (i, 0)),
                   pl.BlockSpec((tm, N_EXPERTS), lambda i: (i, 0))],
        out_shape=[jax.ShapeDtypeStruct((t, D_MODEL), F32),
                   jax.ShapeDtypeStruct((t, D_MODEL), BF16),
                   jax.ShapeDtypeStruct((t, N_EXPERTS), F32)],
        compiler_params=_params("arbitrary"),
        name="post_mix",
    )(attn, ret, proj, proj, x2, mod, norm2_g, wa, wb, wo, wr, wsg, wsu, wsd)


def _moe_kernel(be_ref, nu_ref, x_ref, wg_ref, wu_ref, wd_ref, y_ref, wg_sc, wu_sc, wd_sc):
    i = pl.program_id(0)
    prev = be_ref[jnp.maximum(i - 1, 0)]

    @pl.when((i == 0) | (be_ref[i] != prev))
    def _():
        wg_sc[...] = wg_ref[0].astype(BF16)
        wu_sc[...] = wu_ref[0].astype(BF16)
        wd_sc[...] = wd_ref[0].astype(BF16)

    @pl.when(i < nu_ref[0])
    def _():
        xb = x_ref[...]
        g = jnp.dot(xb, wg_sc[...], preferred_element_type=F32)
        u = jnp.dot(xb, wu_sc[...], preferred_element_type=F32)
        y = jnp.dot((_silu(g) * u).astype(BF16), wd_sc[...], preferred_element_type=F32)
        y_ref[...] = y.astype(BF16)

    @pl.when(i >= nu_ref[0])
    def _():
        y_ref[...] = jnp.zeros_like(y_ref)


def _moe_experts(block_expert, n_used, x_sorted, w_gate, w_up, w_down):
    p = x_sorted.shape[0]
    n_blocks = p // ROWS_MOE
    wspec = lambda shp: pl.BlockSpec((1,) + shp, lambda i, be, nu: (be[i], 0, 0))
    grid_spec = pltpu.PrefetchScalarGridSpec(
        num_scalar_prefetch=2,
        grid=(n_blocks,),
        in_specs=[pl.BlockSpec((ROWS_MOE, D_MODEL),
                               lambda i, be, nu: (jnp.minimum(i, nu[0] - 1), 0)),
                  wspec((D_MODEL, EXPERT_DIM)), wspec((D_MODEL, EXPERT_DIM)),
                  wspec((EXPERT_DIM, D_MODEL))],
        out_specs=pl.BlockSpec((ROWS_MOE, D_MODEL), lambda i, be, nu: (i, 0)),
        scratch_shapes=[pltpu.VMEM((D_MODEL, EXPERT_DIM), BF16),
                        pltpu.VMEM((D_MODEL, EXPERT_DIM), BF16),
                        pltpu.VMEM((EXPERT_DIM, D_MODEL), BF16)],
    )
    return pl.pallas_call(
        _moe_kernel,
        grid_spec=grid_spec,
        out_shape=jax.ShapeDtypeStruct((p, D_MODEL), BF16),
        compiler_params=_params("arbitrary"),
        name="moe_experts",
    )(block_expert, n_used, x_sorted, w_gate, w_up, w_down)


def _final_kernel(base_ref, routed_ref, mod_ref, g_ref, o_ref, *, tiles_per_batch, last_layer):
    b = pl.program_id(0) // tiles_per_batch
    gate2 = mod_ref[pl.ds(b, 1), 5 * D_MODEL:6 * D_MODEL]
    x = base_ref[...] + gate2 * routed_ref[...]
    if last_layer:
        x = x * lax.rsqrt(jnp.mean(x * x, axis=-1, keepdims=True) + NORM_EPS) * g_ref[...]
    o_ref[...] = x


def _final_norm(base, routed, mod, final_g, seq, last_layer):
    t = base.shape[0]
    tm = TM_FINAL
    kern = functools.partial(_final_kernel, tiles_per_batch=seq // tm, last_layer=last_layer)
    return pl.pallas_call(
        kern,
        grid=(t // tm,),
        in_specs=[pl.BlockSpec((tm, D_MODEL), lambda i: (i, 0)),
                  pl.BlockSpec((tm, D_MODEL), lambda i: (i, 0)),
                  pl.BlockSpec(mod.shape, lambda i: (0, 0)),
                  pl.BlockSpec((1, D_MODEL), lambda i: (0, 0))],
        out_specs=pl.BlockSpec((tm, D_MODEL), lambda i: (i, 0)),
        out_shape=jax.ShapeDtypeStruct((t, D_MODEL), F32),
        compiler_params=_params("arbitrary"),
        name="final_norm",
    )(base, routed, mod, final_g)


def _route(scores, router_bias):
    t = scores.shape[0]
    sel = scores + router_bias.astype(F32)
    grp = sel.reshape(t, N_GROUPS, N_EXPERTS // N_GROUPS)
    group_scores = lax.top_k(grp, 2)[0].sum(-1)
    top_groups = lax.top_k(group_scores, TOPK_GROUPS)[1]
    group_mask = jax.nn.one_hot(top_groups, N_GROUPS, dtype=F32).sum(-2) > 0
    expert_mask = jnp.repeat(group_mask, N_EXPERTS // N_GROUPS, axis=-1)
    _, top_e = lax.top_k(jnp.where(expert_mask, sel, -jnp.inf), TOP_K)
    top_w = jnp.take_along_axis(scores, top_e, axis=-1)
    top_w = top_w / jnp.sum(top_w, axis=-1, keepdims=True) * ROUTED_SCALE
    return top_e, top_w


def _dispatch(top_e):
    t = top_e.shape[0]
    a = t * TOP_K
    r = ROWS_MOE
    e_flat = top_e.reshape(a)
    tok_flat = jnp.repeat(jnp.arange(t, dtype=jnp.int32), TOP_K)
    order = jnp.argsort(e_flat)
    e_sorted = e_flat[order]
    counts = jnp.zeros((N_EXPERTS,), jnp.int32).at[e_flat].add(1)
    padded = (counts + r - 1) // r * r
    pad_end = jnp.cumsum(padded)
    pad_start = pad_end - padded
    start = jnp.cumsum(counts) - counts
    dest = pad_start[e_sorted] + jnp.arange(a, dtype=jnp.int32) - start[e_sorted]
    n_blocks = (a + N_EXPERTS * (r - 1) + r - 1) // r
    p = n_blocks * r
    row_tok = jnp.zeros((p,), jnp.int32).at[dest].set(tok_flat[order])
    pos = jnp.zeros((a,), jnp.int32).at[order].set(dest).reshape(t, TOP_K)
    n_used = (pad_end[-1] // r).astype(jnp.int32)
    blk = jnp.minimum(jnp.arange(n_blocks, dtype=jnp.int32), n_used - 1) * r
    block_expert = jnp.minimum(jnp.searchsorted(pad_end, blk, side='right'),
                               N_EXPERTS - 1).astype(jnp.int32)
    return row_tok, pos, block_expert, n_used.reshape(1)


def kernel(x, c, w_ada, b_ada, norm1_g, w_in, b_forget, ret_gn_g, w_branch_a, w_branch_b, w_out,
           norm2_g, w_router, router_bias, w_exp_gate, w_exp_up, w_exp_down, w_sh_gate, w_sh_up,
           w_sh_down, final_g):
    batch, seq, d = x.shape
    t = batch * seq
    depth = w_ada.shape[0]
    x2 = x.reshape(t, d)

    half = RET_QK_DIM // 2
    inv_freq = ROPE_BASE ** (-jnp.arange(half, dtype=F32) / half)
    ang = jnp.arange(seq, dtype=F32)[:, None] * inv_freq[None, :]
    cos32, sin32 = jnp.cos(ang), jnp.sin(ang)
    cos_t = jnp.concatenate([cos32, cos32, cos32, cos32], axis=-1)
    sin_t = jnp.concatenate([-sin32, sin32, -sin32, sin32], axis=-1)
    dec_t, zeta_t, xi_t, cd_t = _retention_tables()
    c_pad = jnp.zeros((8, d), F32).at[:batch].set(c)

    for l in range(depth):
        mod = _adaln_mod(c_pad, w_ada[l], b_ada[l])

        w = w_in[l]
        o_fq, o_fk, o_fv, o_ff = 0, FOX_WIDTH, 2 * FOX_WIDTH, 3 * FOX_WIDTH
        o_rq = o_ff + FOX_HEADS
        o_rk = o_rq + RET_QK_WIDTH
        o_rv = o_rk + RET_QK_WIDTH
        o_rg = o_rv + RET_V_WIDTH
        o_ga = o_rg + RET_V_WIDTH
        o_gb = o_ga + D_MODEL
        w_p = jnp.concatenate(
            [w[:, o_rv:o_rg], w[:, o_rg:o_ga], w[:, o_ga:o_gb], w[:, o_gb:o_gb + D_MODEL],
             w[:, o_fq:o_fk], w[:, o_fk:o_fv], w[:, o_fv:o_ff], w[:, o_rq:o_rk], w[:, o_rk:o_rv]],
            axis=1).astype(BF16)
        w_ff = jnp.zeros((d, LANES), BF16).at[:, :FOX_HEADS].set(w[:, o_ff:o_rq].astype(BF16))
        b_ff = jnp.zeros((1, LANES), F32).at[0, :FOX_HEADS].set(b_forget[l].astype(F32))

        proj, logf = _in_proj(x2, mod, norm1_g[l].reshape(1, d), w_p, w_ff, b_ff, cos_t, sin_t, seq)
        cum_f = jnp.cumsum(logf[:, :FOX_HEADS].reshape(batch, seq, FOX_HEADS), axis=1)
        cum_f = cum_f.transpose(0, 2, 1)

        attn = _fox_attn(proj, cum_f, batch, seq)
        ret = _retention(proj, ret_gn_g[l].reshape(1, RET_V_WIDTH), dec_t, zeta_t, xi_t, cd_t,
                         batch, seq)

        base, h2b, scores = _post_mix(
            attn, ret, proj, x2, mod, norm2_g[l].reshape(1, d),
            w_branch_a[l].astype(BF16), w_branch_b[l].astype(BF16), w_out[l].astype(BF16),
            w_router[l], w_sh_gate[l].astype(BF16), w_sh_up[l].astype(BF16),
            w_sh_down[l].astype(BF16), seq)

        top_e, top_w = _route(scores, router_bias[l])
        row_tok, pos, block_expert, n_used = _dispatch(top_e)
        x_sorted = jnp.take(h2b, row_tok, axis=0)
        y = _moe_experts(block_expert, n_used, x_sorted, w_exp_gate[l], w_exp_up[l], w_exp_down[l])
        routed = jnp.sum(jnp.take(y, pos, axis=0).astype(F32) * top_w[:, :, None], axis=1)

        x2 = _final_norm(base, routed, mod, final_g.reshape(1, d), seq, l == depth - 1)
    return x2.reshape(batch, seq, d)
```

```python
import functools

import jax
import jax.numpy as jnp
import numpy as np
from jax import lax
from jax.experimental import pallas as pl
from jax.experimental.pallas import tpu as pltpu

F32 = jnp.float32
BF16 = jnp.bfloat16

D_MODEL = 1024
FOX_HEADS = 8
FOX_HEAD_DIM = 64
FOX_WIDTH = 512
RET_HEADS = 8
RET_QK_DIM = 64
RET_V_DIM = 128
RET_QK_WIDTH = 512
RET_V_WIDTH = 1024
RET_CHUNK = 128
ROPE_BASE = 10000.0
N_EXPERTS = 256
TOP_K = 8
N_GROUPS = 8
TOPK_GROUPS = 4
EXPERT_DIM = 256
ROUTED_SCALE = 2.5
NORM_EPS = 1e-6

LANES = 128
VMEM_LIMIT = 56 * 1024 * 1024

COL_RV, COL_RG, COL_GA, COL_GB = 0, 1024, 2048, 3072
COL_FQ, COL_FK, COL_FV, COL_RQ, COL_RK = 4096, 4608, 5120, 5632, 6144
PROJ_WIDTH = 6656

TM_PROJ = 512
TN_PROJ = 512
T_ATTN = 512
T_RET = 512
TM_POST = 512
T_ROUTE = 512
T_DISPATCH = 512
ROWS_MOE = 256
T_COMBINE = 256


def _sigmoid(z):
    return 1.0 / (1.0 + jnp.exp(-z))


def _silu(z):
    return z * _sigmoid(z)


def _params(*sem):
    return pltpu.CompilerParams(dimension_semantics=sem, vmem_limit_bytes=VMEM_LIMIT)


def _adaln_kernel(c_ref, w_ref, b_ref, o_ref):
    a = _silu(c_ref[...]).astype(BF16)
    o_ref[...] = jnp.dot(a, w_ref[...].astype(BF16), preferred_element_type=F32) + b_ref[...]


def _adaln_mod(c_pad, w_ada, b_ada):
    n = w_ada.shape[1]
    tn = 1024
    return pl.pallas_call(
        _adaln_kernel,
        grid=(n // tn,),
        in_specs=[pl.BlockSpec((8, D_MODEL), lambda j: (0, 0)),
                  pl.BlockSpec((D_MODEL, tn), lambda j: (0, j)),
                  pl.BlockSpec((1, tn), lambda j: (0, j))],
        out_specs=pl.BlockSpec((8, tn), lambda j: (0, j)),
        out_shape=jax.ShapeDtypeStruct((8, n), F32),
        compiler_params=_params("arbitrary"),
        name="adaln_mod",
    )(c_pad, w_ada, b_ada.reshape(1, n))


def _inproj_kernel(x_ref, mod_ref, g_ref, w_ref, wff_ref, bf_ref, cos_ref, sin_ref,
                   proj_ref, logf_ref, h_sc, *, tiles_per_batch):
    i = pl.program_id(0)
    j = pl.program_id(1)

    @pl.when(j == 0)
    def _():
        b = i // tiles_per_batch
        x = x_ref[...]
        y = x * lax.rsqrt(jnp.mean(x * x, axis=-1, keepdims=True) + NORM_EPS) * g_ref[...]
        shift = mod_ref[pl.ds(b, 1), 0:D_MODEL]
        scale = mod_ref[pl.ds(b, 1), D_MODEL:2 * D_MODEL]
        hb = (y * (1.0 + scale) + shift).astype(BF16)
        h_sc[...] = hb
        z = jnp.dot(hb, wff_ref[...], preferred_element_type=F32) + bf_ref[...]
        logf_ref[...] = jnp.minimum(z, 0.0) - jnp.log1p(jnp.exp(-jnp.abs(z)))

    acc = jnp.dot(h_sc[...], w_ref[...], preferred_element_type=F32)

    j_ga, j_fq, j_fk, j_rq, j_rk = (COL_GA // TN_PROJ, COL_FQ // TN_PROJ, COL_FK // TN_PROJ,
                                    COL_RQ // TN_PROJ, COL_RK // TN_PROJ)

    @pl.when((j < j_ga) | ((j >= j_fk) & (j < j_rq)))
    def _():
        proj_ref[...] = acc.astype(BF16)

    @pl.when((j >= j_ga) & (j < j_fq))
    def _():
        proj_ref[...] = _sigmoid(acc).astype(BF16)

    @pl.when(j == j_fq)
    def _():
        proj_ref[...] = (acc * (FOX_HEAD_DIM ** -0.5)).astype(BF16)

    @pl.when(j >= j_rq)
    def _():
        cos = cos_ref[...]
        sin = sin_ref[...]
        lane = lax.broadcasted_iota(jnp.int32, cos.shape, 1)
        first_half = (lane % RET_QK_DIM) < (RET_QK_DIM // 2)
        k_scale = jnp.where(j == j_rk, RET_QK_DIM ** -0.5, 1.0).astype(F32)
        for cb in range(TN_PROJ // LANES):
            xs = acc[:, cb * LANES:(cb + 1) * LANES]
            up = pltpu.roll(xs, LANES - RET_QK_DIM // 2, axis=1)
            dn = pltpu.roll(xs, RET_QK_DIM // 2, axis=1)
            rot = xs * cos + jnp.where(first_half, up, dn) * sin
            proj_ref[:, cb * LANES:(cb + 1) * LANES] = (rot * k_scale).astype(BF16)


def _in_proj(x2, mod, norm_g, w_p, w_ff, b_ff, cos_t, sin_t, seq):
    t = x2.shape[0]
    tiles_per_batch = seq // TM_PROJ
    kern = functools.partial(_inproj_kernel, tiles_per_batch=tiles_per_batch)
    return pl.pallas_call(
        kern,
        grid=(t // TM_PROJ, PROJ_WIDTH // TN_PROJ),
        in_specs=[pl.BlockSpec((TM_PROJ, D_MODEL), lambda i, j: (i, 0)),
                  pl.BlockSpec(mod.shape, lambda i, j: (0, 0)),
                  pl.BlockSpec((1, D_MODEL), lambda i, j: (0, 0)),
                  pl.BlockSpec((D_MODEL, TN_PROJ), lambda i, j: (0, j)),
                  pl.BlockSpec((D_MODEL, LANES), lambda i, j: (0, 0)),
                  pl.BlockSpec((1, LANES), lambda i, j: (0, 0)),
                  pl.BlockSpec((TM_PROJ, LANES), lambda i, j: (i % tiles_per_batch, 0)),
                  pl.BlockSpec((TM_PROJ, LANES), lambda i, j: (i % tiles_per_batch, 0))],
        out_specs=[pl.BlockSpec((TM_PROJ, TN_PROJ), lambda i, j: (i, j)),
                   pl.BlockSpec((TM_PROJ, LANES), lambda i, j: (i, 0))],
        out_shape=[jax.ShapeDtypeStruct((t, PROJ_WIDTH), BF16),
                   jax.ShapeDtypeStruct((t, LANES), F32)],
        scratch_shapes=[pltpu.VMEM((TM_PROJ, D_MODEL), BF16)],
        compiler_params=_params("arbitrary", "arbitrary"),
        name="in_proj",
    )(x2, mod, norm_g, w_p, w_ff, b_ff, cos_t, sin_t)


def _attn_kernel(q_ref, k_ref, v_ref, f_ref, o_ref):
    hp = pl.program_id(1)
    qi = pl.program_id(2)
    tq = T_ATTN
    q = q_ref[...]
    lane = lax.broadcasted_iota(jnp.int32, (tq, LANES), 1)
    row = lax.broadcasted_iota(jnp.int32, (tq, tq), 0)
    col = lax.broadcasted_iota(jnp.int32, (tq, tq), 1)
    outs = []
    for hh in range(2):
        sel = (lane < FOX_HEAD_DIM) if hh == 0 else (lane >= FOX_HEAD_DIM)
        qm = jnp.where(sel, q, jnp.zeros_like(q))

        def step(kb, carry, masked, qm=qm, hh=hh):
            m, l, acc = carry
            ks = pl.multiple_of(kb * tq, tq)
            k = k_ref[pl.ds(ks, tq), :]
            v = v_ref[pl.ds(ks, tq), :]
            s = lax.dot_general(qm, k, (((1,), (1,)), ((), ())), preferred_element_type=F32)
            s = s - f_ref[0, pl.ds(2 * hp + hh, 1), pl.ds(ks, tq)]
            if masked:
                s = jnp.where(col <= row, s, -jnp.inf)
            m_new = jnp.maximum(m, jnp.max(s, axis=-1, keepdims=True))
            alpha = jnp.exp(m - m_new)
            p = jnp.exp(s - m_new)
            l = alpha * l + jnp.sum(p, axis=-1, keepdims=True)
            acc = alpha * acc + jnp.dot(p.astype(BF16), v, preferred_element_type=F32)
            return m_new, l, acc

        carry = (jnp.full((tq, 1), -jnp.inf, F32), jnp.zeros((tq, 1), F32),
                 jnp.zeros((tq, LANES), F32))
        carry = lax.fori_loop(0, qi, lambda kb, c: step(kb, c, False), carry)
        m, l, acc = step(qi, carry, True)
        outs.append(acc / l)
    o_ref[...] = jnp.where(lane < FOX_HEAD_DIM, outs[0], outs[1]).astype(BF16)


def _fox_attn(proj, cum_f, batch, seq):
    t = proj.shape[0]
    nq = seq // T_ATTN
    cq, ck, cv = COL_FQ // LANES, COL_FK // LANES, COL_FV // LANES
    return pl.pallas_call(
        _attn_kernel,
        grid=(batch, FOX_HEADS // 2, nq),
        in_specs=[pl.BlockSpec((T_ATTN, LANES), lambda b, hp, qi: (b * nq + qi, cq + hp)),
                  pl.BlockSpec((seq, LANES), lambda b, hp, qi: (b, ck + hp)),
                  pl.BlockSpec((seq, LANES), lambda b, hp, qi: (b, cv + hp)),
                  pl.BlockSpec((1, FOX_HEADS, seq), lambda b, hp, qi: (b, 0, 0))],
        out_specs=pl.BlockSpec((T_ATTN, LANES), lambda b, hp, qi: (b * nq + qi, hp)),
        out_shape=jax.ShapeDtypeStruct((t, FOX_WIDTH), BF16),
        compiler_params=_params("arbitrary", "arbitrary", "arbitrary"),
        name="fox_attn",
    )(proj, proj, proj, cum_f)


def _ret_kernel(q_ref, k_ref, v_ref, rg_ref, gn_ref, dec_ref, zeta_ref, xi_ref, cd_ref,
                o_ref, st_sc):
    h = pl.program_id(1)
    ri = pl.program_id(2)
    c = RET_CHUNK

    @pl.when(ri == 0)
    def _():
        st_sc[...] = jnp.zeros_like(st_sc)

    lane = lax.broadcasted_iota(jnp.int32, (T_RET, LANES), 1)
    lo = (h % 2) * RET_QK_DIM
    sel = (lane >= lo) & (lane < lo + RET_QK_DIM)
    qm = jnp.where(sel, q_ref[...], jnp.zeros((T_RET, LANES), BF16))
    km = jnp.where(sel, k_ref[...], jnp.zeros((T_RET, LANES), BF16))
    decay = dec_ref[0]
    zeta = zeta_ref[0]
    xi = xi_ref[0]
    state = st_sc[...]
    for n in range(T_RET // c):
        qc = qm[n * c:(n + 1) * c]
        kc = km[n * c:(n + 1) * c]
        vc = v_ref[n * c:(n + 1) * c, :]
        sc = lax.dot_general(qc, kc, (((1,), (1,)), ((), ())), preferred_element_type=F32) * decay
        intra = jnp.dot(sc.astype(BF16), vc, preferred_element_type=F32)
        cross = jnp.dot(qc, state.astype(BF16), preferred_element_type=F32) * xi
        o = intra + cross
        kz = (kc.astype(F32) * zeta).astype(BF16)
        kv = lax.dot_general(kz, vc, (((0,), (0,)), ((), ())), preferred_element_type=F32)
        state = cd_ref[0] * state + kv
        mu = jnp.mean(o, axis=-1, keepdims=True)
        d = o - mu
        var = jnp.mean(d * d, axis=-1, keepdims=True)
        on = d * lax.rsqrt(var + NORM_EPS) * gn_ref[...]
        rg = rg_ref[n * c:(n + 1) * c, :].astype(F32)
        o_ref[n * c:(n + 1) * c, :] = (on * _silu(rg)).astype(BF16)
    st_sc[...] = state


def _retention(proj, gn_g, dec_t, zeta_t, xi_t, cd_t, batch, seq):
    t = proj.shape[0]
    nr = seq // T_RET
    cq, ck, cv, cg = COL_RQ // LANES, COL_RK // LANES, COL_RV // LANES, COL_RG // LANES
    tok = lambda b, h, ri: b * nr + ri
    tab = pl.BlockSpec((1, RET_CHUNK, LANES), lambda b, h, ri: (h, 0, 0))
    return pl.pallas_call(
        _ret_kernel,
        grid=(batch, RET_HEADS, nr),
        in_specs=[pl.BlockSpec((T_RET, LANES), lambda b, h, ri: (tok(b, h, ri), cq + h // 2)),
                  pl.BlockSpec((T_RET, LANES), lambda b, h, ri: (tok(b, h, ri), ck + h // 2)),
                  pl.BlockSpec((T_RET, LANES), lambda b, h, ri: (tok(b, h, ri), cv + h)),
                  pl.BlockSpec((T_RET, LANES), lambda b, h, ri: (tok(b, h, ri), cg + h)),
                  pl.BlockSpec((1, LANES), lambda b, h, ri: (0, h)),
                  tab, tab, tab, tab],
        out_specs=pl.BlockSpec((T_RET, LANES), lambda b, h, ri: (tok(b, h, ri), h)),
        out_shape=jax.ShapeDtypeStruct((t, RET_V_WIDTH), BF16),
        scratch_shapes=[pltpu.VMEM((LANES, RET_V_DIM), F32)],
        compiler_params=_params("arbitrary", "arbitrary", "arbitrary"),
        name="retention",
    )(proj, proj, proj, proj, gn_g, dec_t, zeta_t, xi_t, cd_t)


def _retention_tables():
    c = RET_CHUNK
    log_gamma = jnp.log1p(-jnp.exp2(-5.0 - jnp.arange(RET_HEADS, dtype=F32)))
    idx = jnp.arange(c, dtype=F32)
    diff = idx[:, None] - idx[None, :]
    dec = jnp.where(diff >= 0,
                    jnp.exp(jnp.maximum(diff, 0.0)[None] * log_gamma[:, None, None]), 0.0)
    zeta = jnp.exp((c - 1.0 - idx)[None, :] * log_gamma[:, None])
    xi = jnp.exp((idx + 1.0)[None, :] * log_gamma[:, None])
    cd = jnp.exp(c * log_gamma)
    bc = lambda v: jnp.broadcast_to(v[:, :, None], (RET_HEADS, c, LANES))
    cd_t = jnp.broadcast_to(cd[:, None, None], (RET_HEADS, LANES, RET_V_DIM))
    return dec, bc(zeta), bc(xi), cd_t


def _post_kernel(attn_ref, ret_ref, ga_ref, gb_ref, x_ref, mod_ref, g2_ref,
                 wa_ref, wb_ref, wo_ref, wr_ref, wsg_ref, wsu_ref, wsd_ref,
                 base_ref, h2_ref, score_ref, *, tiles_per_batch):
    b = pl.program_id(0) // tiles_per_batch
    ya = jnp.dot(attn_ref[...], wa_ref[...], preferred_element_type=F32)
    yb = jnp.dot(ret_ref[...], wb_ref[...], preferred_element_type=F32)
    merged = ga_ref[...].astype(F32) * ya + gb_ref[...].astype(F32) * yb
    mix = jnp.dot(merged.astype(BF16), wo_ref[...], preferred_element_type=F32)
    d = D_MODEL
    gate1 = mod_ref[pl.ds(b, 1), 2 * d:3 * d]
    shift2 = mod_ref[pl.ds(b, 1), 3 * d:4 * d]
    scale2 = mod_ref[pl.ds(b, 1), 4 * d:5 * d]
    gate2 = mod_ref[pl.ds(b, 1), 5 * d:6 * d]
    x1 = x_ref[...] + gate1 * mix
    y = x1 * lax.rsqrt(jnp.mean(x1 * x1, axis=-1, keepdims=True) + NORM_EPS) * g2_ref[...]
    h2 = y * (1.0 + scale2) + shift2
    logits_t = lax.dot_general(wr_ref[...], h2, (((1,), (1,)), ((), ())),
                               preferred_element_type=F32, precision=lax.Precision.HIGHEST)
    score_ref[...] = _sigmoid(logits_t)
    h2_ref[...] = h2
    h2b = h2.astype(BF16)
    g = jnp.dot(h2b, wsg_ref[...], preferred_element_type=F32)
    u = jnp.dot(h2b, wsu_ref[...], preferred_element_type=F32)
    shared = jnp.dot((_silu(g) * u).astype(BF16), wsd_ref[...], preferred_element_type=F32)
    base_ref[...] = x1 + gate2 * shared


def _post_mix(attn, ret, proj, x2, mod, norm2_g, wa, wb, wo, wr, wsg, wsu, wsd, seq):
    t = x2.shape[0]
    tm = TM_POST
    kern = functools.partial(_post_kernel, tiles_per_batch=seq // tm)
    full = lambda a: pl.BlockSpec(a.shape, lambda i: (0,) * a.ndim)
    return pl.pallas_call(
        kern,
        grid=(t // tm,),
        in_specs=[pl.BlockSpec((tm, FOX_WIDTH), lambda i: (i, 0)),
                  pl.BlockSpec((tm, RET_V_WIDTH), lambda i: (i, 0)),
                  pl.BlockSpec((tm, D_MODEL), lambda i: (i, COL_GA // D_MODEL)),
                  pl.BlockSpec((tm, D_MODEL), lambda i: (i, COL_GB // D_MODEL)),
                  pl.BlockSpec((tm, D_MODEL), lambda i: (i, 0)),
                  full(mod), full(norm2_g), full(wa), full(wb), full(wo), full(wr),
                  full(wsg), full(wsu), full(wsd)],
        out_specs=[pl.BlockSpec((tm, D_MODEL), lambda i: (i, 0)),
                   pl.BlockSpec((tm, D_MODEL), lambda i: (i, 0)),
                   pl.BlockSpec((N_EXPERTS, tm), lambda i: (0, i))],
        out_shape=[jax.ShapeDtypeStruct((t, D_MODEL), F32),
                   jax.ShapeDtypeStruct((t, D_MODEL), F32),
                   jax.ShapeDtypeStruct((N_EXPERTS, t), F32)],
        compiler_params=_params("arbitrary"),
        name="post_mix",
    )(attn, ret, proj, proj, x2, mod, norm2_g, wa, wb, wo, wr, wsg, wsu, wsd)


def _route_kernel(s_ref, bias_ref, e_ref, w_ref, rank_ref, cnt_ref):
    tr = s_ref.shape[1]
    gsz = N_EXPERTS // N_GROUPS

    @pl.when(pl.program_id(0) == 0)
    def _():
        cnt_ref[...] = jnp.zeros_like(cnt_ref)

    s = s_ref[...]
    sel = s + bias_ref[...]
    neg = jnp.float32(-jnp.inf)

    g3 = sel.reshape(N_GROUPS, gsz, tr)
    m1 = jnp.max(g3, axis=1)
    n_max = jnp.sum((g3 == m1[:, None, :]).astype(F32), axis=1)
    m2 = jnp.max(jnp.where(g3 < m1[:, None, :], g3, neg), axis=1)
    gs = m1 + jnp.where(n_max >= 2.0, m1, m2)

    gi = lax.broadcasted_iota(jnp.int32, (N_GROUPS, tr), 0)
    beaten = jnp.zeros((N_GROUPS, tr), F32)
    for j in range(N_GROUPS):
        gj = gs[j:j + 1, :]
        beaten = beaten + jnp.where((gj > gs) | ((gj == gs) & (gi > j)), 1.0, 0.0)
    keep = jnp.where(beaten < float(TOPK_GROUPS), 1.0, 0.0)
    keep_e = jnp.broadcast_to(keep[:, None, :], (N_GROUPS, gsz, tr)).reshape(N_EXPERTS, tr)

    ei = lax.broadcasted_iota(jnp.int32, (N_EXPERTS, tr), 0)
    val = jnp.where(keep_e > 0.0, sel, neg)
    member = jnp.zeros((N_EXPERTS, tr), F32)
    idx_rows, w_rows = [], []
    for _ in range(TOP_K):
        m = jnp.max(val, axis=0, keepdims=True)
        idx = jnp.min(jnp.where(val == m, ei, N_EXPERTS), axis=0, keepdims=True)
        hit = ei == idx
        w_rows.append(jnp.sum(jnp.where(hit, s, 0.0), axis=0, keepdims=True))
        idx_rows.append(idx)
        val = jnp.where(hit, neg, val)
        member = jnp.where(hit, 1.0, member)

    w_sum = w_rows[0]
    for k in range(1, TOP_K):
        w_sum = w_sum + w_rows[k]

    r_i = lax.broadcasted_iota(jnp.int32, (tr, tr), 0)
    c_i = lax.broadcasted_iota(jnp.int32, (tr, tr), 1)
    upper = jnp.where(r_i < c_i, 1.0, 0.0).astype(BF16)
    before = jnp.dot(member.astype(BF16), upper, preferred_element_type=F32) + cnt_ref[...]

    for k in range(TOP_K):
        hit = ei == idx_rows[k]
        e_ref[k:k + 1, :] = idx_rows[k]
        w_ref[k:k + 1, :] = w_rows[k] / w_sum * ROUTED_SCALE
        rank_ref[k:k + 1, :] = jnp.sum(jnp.where(hit, before, 0.0), axis=0,
                                       keepdims=True).astype(jnp.int32)
    cnt_ref[...] = cnt_ref[...] + jnp.sum(member, axis=1, keepdims=True)


def _route(scores_t, bias_col):
    t = scores_t.shape[1]
    tr = T_ROUTE
    row8 = lambda dt: jax.ShapeDtypeStruct((TOP_K, t), dt)
    return pl.pallas_call(
        _route_kernel,
        grid=(t // tr,),
        in_specs=[pl.BlockSpec((N_EXPERTS, tr), lambda i: (0, i)),
                  pl.BlockSpec((N_EXPERTS, 1), lambda i: (0, 0))],
        out_specs=[pl.BlockSpec((TOP_K, tr), lambda i: (0, i)),
                   pl.BlockSpec((TOP_K, tr), lambda i: (0, i)),
                   pl.BlockSpec((TOP_K, tr), lambda i: (0, i)),
                   pl.BlockSpec((N_EXPERTS, 1), lambda i: (0, 0))],
        out_shape=[row8(jnp.int32), row8(F32), row8(jnp.int32),
                   jax.ShapeDtypeStruct((N_EXPERTS, 1), F32)],
        compiler_params=_params("arbitrary"),
        name="route",
    )(scores_t, bias_col)


def _dest_kernel(e_ref, rank_ref, start_ref, dest_ref):
    tr = e_ref.shape[1]
    ei = lax.broadcasted_iota(jnp.int32, (N_EXPERTS, tr), 0)
    start = start_ref[...]
    for k in range(TOP_K):
        hit = ei == e_ref[k:k + 1, :]
        off = jnp.sum(jnp.where(hit, start, 0.0), axis=0, keepdims=True)
        dest_ref[k:k + 1, :] = rank_ref[k:k + 1, :] + off.astype(jnp.int32)


def _dest_rows(top_e, rank, start_col):
    t = top_e.shape[1]
    tr = T_ROUTE
    spec = pl.BlockSpec((TOP_K, tr), lambda i: (0, i))
    return pl.pallas_call(
        _dest_kernel,
        grid=(t // tr,),
        in_specs=[spec, spec, pl.BlockSpec((N_EXPERTS, 1), lambda i: (0, 0))],
        out_specs=spec,
        out_shape=jax.ShapeDtypeStruct((TOP_K, t), jnp.int32),
        compiler_params=_params("arbitrary"),
        name="dest_rows",
    )(top_e, rank, start_col)


def _row_copy(src_ref, src_row, dst_ref, dst_row, sem):
    return pltpu.make_async_copy(src_ref.at[pl.ds(src_row, 1), :],
                                 dst_ref.at[pl.ds(dst_row, 1), :], sem)


def _dispatch_kernel(dest_ref, h_ref, xs_ref, sem, *, n_tokens):
    td = h_ref.shape[0]
    base = pl.program_id(0) * td

    def issue(r, carry):
        for k in range(TOP_K):
            _row_copy(h_ref, r, xs_ref, dest_ref[k * n_tokens + base + r], sem).start()
        return carry

    lax.fori_loop(0, td, issue, 0)

    def drain(r, carry):
        for k in range(TOP_K):
            _row_copy(h_ref, 0, xs_ref, 0, sem).wait()
        return carry

    lax.fori_loop(0, td, drain, 0)


def _dispatch(dest_flat, h2, n_rows):
    t = h2.shape[0]
    td = T_DISPATCH
    grid_spec = pltpu.PrefetchScalarGridSpec(
        num_scalar_prefetch=1,
        grid=(t // td,),
        in_specs=[pl.BlockSpec((td, D_MODEL), lambda i, dest: (i, 0))],
        out_specs=pl.BlockSpec(memory_space=pl.ANY),
        scratch_shapes=[pltpu.SemaphoreType.DMA],
    )
    return pl.pallas_call(
        functools.partial(_dispatch_kernel, n_tokens=t),
        grid_spec=grid_spec,
        out_shape=jax.ShapeDtypeStruct((n_rows, D_MODEL), F32),
        compiler_params=_params("arbitrary"),
        name="dispatch",
    )(dest_flat, h2)


def _moe_kernel(be_ref, nv_ref, x_ref, wg_ref, wu_ref, wd_ref, y_ref, wg_sc, wu_sc, wd_sc):
    i = pl.program_id(0)
    prev = be_ref[jnp.maximum(i - 1, 0)]

    @pl.when((i == 0) | (be_ref[i] != prev))
    def _():
        wg_sc[...] = wg_ref[0].astype(BF16)
        wu_sc[...] = wu_ref[0].astype(BF16)
        wd_sc[...] = wd_ref[0].astype(BF16)

    @pl.when(nv_ref[i] > 0)
    def _():
        row = lax.broadcasted_iota(jnp.int32, x_ref.shape, 0)
        xb = jnp.where(row < nv_ref[i], x_ref[...], 0.0).astype(BF16)
        g = jnp.dot(xb, wg_sc[...], preferred_element_type=F32)
        u = jnp.dot(xb, wu_sc[...], preferred_element_type=F32)
        y_ref[...] = jnp.dot((_silu(g) * u).astype(BF16), wd_sc[...], preferred_element_type=F32)

    @pl.when(nv_ref[i] == 0)
    def _():
        y_ref[...] = jnp.zeros_like(y_ref)


def _moe_experts(block_expert, block_valid, last_block, x_sorted, w_gate, w_up, w_down):
    p = x_sorted.shape[0]
    n_blocks = p // ROWS_MOE
    wspec = lambda shp: pl.BlockSpec((1,) + shp, lambda i, be, nv, lb: (be[i], 0, 0))
    grid_spec = pltpu.PrefetchScalarGridSpec(
        num_scalar_prefetch=3,
        grid=(n_blocks,),
        in_specs=[pl.BlockSpec((ROWS_MOE, D_MODEL),
                               lambda i, be, nv, lb: (jnp.minimum(i, lb[0]), 0)),
                  wspec((D_MODEL, EXPERT_DIM)), wspec((D_MODEL, EXPERT_DIM)),
                  wspec((EXPERT_DIM, D_MODEL))],
        out_specs=pl.BlockSpec((ROWS_MOE, D_MODEL), lambda i, be, nv, lb: (i, 0)),
        scratch_shapes=[pltpu.VMEM((D_MODEL, EXPERT_DIM), BF16),
                        pltpu.VMEM((D_MODEL, EXPERT_DIM), BF16),
                        pltpu.VMEM((EXPERT_DIM, D_MODEL), BF16)],
    )

    def kern(be_ref, nv_ref, lb_ref, *refs):
        _moe_kernel(be_ref, nv_ref, *refs)

    return pl.pallas_call(
        kern,
        grid_spec=grid_spec,
        out_shape=jax.ShapeDtypeStruct((p, D_MODEL), F32),
        compiler_params=_params("arbitrary"),
        name="moe_experts",
    )(block_expert, block_valid, last_block, x_sorted, w_gate, w_up, w_down)


def _combine_kernel(dest_ref, base_ref, w_ref, mod_ref, g_ref, y_ref, o_ref, ybuf, sem, *,
                    n_tokens, tiles_per_batch, last_layer):
    tc = base_ref.shape[0]
    tok0 = pl.program_id(0) * tc

    def issue(r, carry):
        for k in range(TOP_K):
            _row_copy(y_ref, dest_ref[k * n_tokens + tok0 + r], ybuf.at[k], r, sem).start()
        return carry

    lax.fori_loop(0, tc, issue, 0)

    def drain(r, carry):
        for k in range(TOP_K):
            _row_copy(y_ref, 0, ybuf.at[k], 0, sem).wait()
        return carry

    lax.fori_loop(0, tc, drain, 0)

    routed = ybuf[0] * w_ref[:, 0:1]
    for k in range(1, TOP_K):
        routed = routed + ybuf[k] * w_ref[:, k:k + 1]
    b = pl.program_id(0) // tiles_per_batch
    gate2 = mod_ref[pl.ds(b, 1), 5 * D_MODEL:6 * D_MODEL]
    x = base_ref[...] + gate2 * routed
    if last_layer:
        x = x * lax.rsqrt(jnp.mean(x * x, axis=-1, keepdims=True) + NORM_EPS) * g_ref[...]
    o_ref[...] = x


def _combine(dest_flat, base, top_w, mod, final_g, y, seq, last_layer):
    t = base.shape[0]
    tc = T_COMBINE
    kern = functools.partial(_combine_kernel, n_tokens=t, tiles_per_batch=seq // tc,
                             last_layer=last_layer)
    grid_spec = pltpu.PrefetchScalarGridSpec(
        num_scalar_prefetch=1,
        grid=(t // tc,),
        in_specs=[pl.BlockSpec((tc, D_MODEL), lambda i, dest: (i, 0)),
                  pl.BlockSpec((tc, TOP_K), lambda i, dest: (i, 0)),
                  pl.BlockSpec(mod.shape, lambda i, dest: (0, 0)),
                  pl.BlockSpec((1, D_MODEL), lambda i, dest: (0, 0)),
                  pl.BlockSpec(memory_space=pl.ANY)],
        out_specs=pl.BlockSpec((tc, D_MODEL), lambda i, dest: (i, 0)),
        scratch_shapes=[pltpu.VMEM((TOP_K, tc, D_MODEL), F32), pltpu.SemaphoreType.DMA],
    )
    return pl.pallas_call(
        kern,
        grid_spec=grid_spec,
        out_shape=jax.ShapeDtypeStruct((t, D_MODEL), F32),
        compiler_params=_params("arbitrary"),
        name="combine",
    )(dest_flat, base, top_w, mod, final_g, y)


def _block_tables(counts, n_assign):
    r = ROWS_MOE
    cnt = counts.reshape(N_EXPERTS).astype(jnp.int32)
    nblk = (cnt + r - 1) // r
    blk_end = jnp.cumsum(nblk)
    blk_start = blk_end - nblk
    n_blocks = (n_assign + N_EXPERTS * (r - 1) + r - 1) // r
    n_used = blk_end[-1]
    bi = jnp.arange(n_blocks, dtype=jnp.int32)
    bc = jnp.minimum(bi, n_used - 1)
    block_expert = jnp.minimum(jnp.searchsorted(blk_end, bc, side='right'),
                               N_EXPERTS - 1).astype(jnp.int32)
    within = bc - blk_start[block_expert]
    valid = jnp.clip(cnt[block_expert] - within * r, 0, r)
    block_valid = jnp.where(bi < n_used, valid, 0).astype(jnp.int32)
    row_start = (blk_start * r).astype(F32).reshape(N_EXPERTS, 1)
    return row_start, block_expert, block_valid, (n_used - 1).reshape(1), n_blocks * r


def kernel(x, c, w_ada, b_ada, norm1_g, w_in, b_forget, ret_gn_g, w_branch_a, w_branch_b, w_out,
           norm2_g, w_router, router_bias, w_exp_gate, w_exp_up, w_exp_down, w_sh_gate, w_sh_up,
           w_sh_down, final_g):
    batch, seq, d = x.shape
    t = batch * seq
    depth = w_ada.shape[0]
    x2 = x.reshape(t, d)

    half = RET_QK_DIM // 2
    inv_freq = ROPE_BASE ** (-jnp.arange(half, dtype=F32) / half)
    ang = jnp.arange(seq, dtype=F32)[:, None] * inv_freq[None, :]
    cos32, sin32 = jnp.cos(ang), jnp.sin(ang)
    cos_t = jnp.concatenate([cos32, cos32, cos32, cos32], axis=-1)
    sin_t = jnp.concatenate([-sin32, sin32, -sin32, sin32], axis=-1)
    dec_t, zeta_t, xi_t, cd_t = _retention_tables()
    c_pad = jnp.zeros((8, d), F32).at[:batch].set(c)

    for l in range(depth):
        mod = _adaln_mod(c_pad, w_ada[l], b_ada[l])

        w = w_in[l]
        o_fq, o_fk, o_fv, o_ff = 0, FOX_WIDTH, 2 * FOX_WIDTH, 3 * FOX_WIDTH
        o_rq = o_ff + FOX_HEADS
        o_rk = o_rq + RET_QK_WIDTH
        o_rv = o_rk + RET_QK_WIDTH
        o_rg = o_rv + RET_V_WIDTH
        o_ga = o_rg + RET_V_WIDTH
        o_gb = o_ga + D_MODEL
        w_p = jnp.concatenate(
            [w[:, o_rv:o_rg], w[:, o_rg:o_ga], w[:, o_ga:o_gb], w[:, o_gb:o_gb + D_MODEL],
             w[:, o_fq:o_fk], w[:, o_fk:o_fv], w[:, o_fv:o_ff], w[:, o_rq:o_rk], w[:, o_rk:o_rv]],
            axis=1).astype(BF16)
        w_ff = jnp.zeros((d, LANES), BF16).at[:, :FOX_HEADS].set(w[:, o_ff:o_rq].astype(BF16))
        b_ff = jnp.zeros((1, LANES), F32).at[0, :FOX_HEADS].set(b_forget[l].astype(F32))

        proj, logf = _in_proj(x2, mod, norm1_g[l].reshape(1, d), w_p, w_ff, b_ff, cos_t, sin_t, seq)
        cum_f = jnp.cumsum(logf[:, :FOX_HEADS].reshape(batch, seq, FOX_HEADS), axis=1)
        cum_f = cum_f.transpose(0, 2, 1)

        attn = _fox_attn(proj, cum_f, batch, seq)
        ret = _retention(proj, ret_gn_g[l].reshape(1, RET_V_WIDTH), dec_t, zeta_t, xi_t, cd_t,
                         batch, seq)

        base, h2, scores_t = _post_mix(
            attn, ret, proj, x2, mod, norm2_g[l].reshape(1, d),
            w_branch_a[l].astype(BF16), w_branch_b[l].astype(BF16), w_out[l].astype(BF16),
            w_router[l].T, w_sh_gate[l].astype(BF16), w_sh_up[l].astype(BF16),
            w_sh_down[l].astype(BF16), seq)

        top_e, top_w, rank, counts = _route(scores_t,
                                            router_bias[l].astype(F32).reshape(N_EXPERTS, 1))
        row_start, block_expert, block_valid, last_block, n_rows = _block_tables(
            counts, t * TOP_K)
        dest_flat = _dest_rows(top_e, rank, row_start).reshape(TOP_K * t)
        x_sorted = _dispatch(dest_flat, h2, n_rows)
        y = _moe_experts(block_expert, block_valid, last_block, x_sorted,
                         w_exp_gate[l], w_exp_up[l], w_exp_down[l])
        x2 = _combine(dest_flat, base, top_w.T, mod, final_g.reshape(1, d), y, seq,
                      l == depth - 1)
    return x2.reshape(batch, seq, d)
```

```python
import functools

import jax
import jax.numpy as jnp
import numpy as np
from jax import lax
from jax.experimental import pallas as pl
from jax.experimental.pallas import tpu as pltpu

F32 = jnp.float32
BF16 = jnp.bfloat16

D_MODEL = 1024
FOX_HEADS = 8
FOX_HEAD_DIM = 64
FOX_WIDTH = 512
RET_HEADS = 8
RET_QK_DIM = 64
RET_V_DIM = 128
RET_QK_WIDTH = 512
RET_V_WIDTH = 1024
RET_CHUNK = 128
ROPE_BASE = 10000.0
N_EXPERTS = 256
TOP_K = 8
N_GROUPS = 8
TOPK_GROUPS = 4
EXPERT_DIM = 256
ROUTED_SCALE = 2.5
NORM_EPS = 1e-6

LANES = 128
VMEM_LIMIT = 56 * 1024 * 1024

COL_RV, COL_RG, COL_GA, COL_GB = 0, 1024, 2048, 3072
COL_FQ, COL_FK, COL_FV, COL_RQ, COL_RK = 4096, 4608, 5120, 5632, 6144
PROJ_WIDTH = 6656

TM_PROJ = 512
TN_PROJ = 512
T_ATTN = 512
T_RET = 512
TM_POST = 512
T_ROUTE = 512
ROWS_MOE = 256
T_COMBINE = 256


def _sigmoid(z):
    return 1.0 / (1.0 + jnp.exp(-z))


def _silu(z):
    return z * _sigmoid(z)


def _params(*sem):
    return pltpu.CompilerParams(dimension_semantics=sem, vmem_limit_bytes=VMEM_LIMIT)


def _adaln_kernel(c_ref, w_ref, b_ref, o_ref):
    a = _silu(c_ref[...]).astype(BF16)
    o_ref[...] = jnp.dot(a, w_ref[...].astype(BF16), preferred_element_type=F32) + b_ref[...]


def _adaln_mod(c_pad, w_ada, b_ada):
    n = w_ada.shape[1]
    tn = 1024
    return pl.pallas_call(
        _adaln_kernel,
        grid=(n // tn,),
        in_specs=[pl.BlockSpec((8, D_MODEL), lambda j: (0, 0)),
                  pl.BlockSpec((D_MODEL, tn), lambda j: (0, j)),
                  pl.BlockSpec((1, tn), lambda j: (0, j))],
        out_specs=pl.BlockSpec((8, tn), lambda j: (0, j)),
        out_shape=jax.ShapeDtypeStruct((8, n), F32),
        compiler_params=_params("arbitrary"),
        name="adaln_mod",
    )(c_pad, w_ada, b_ada.reshape(1, n))


def _inproj_kernel(x_ref, mod_ref, g_ref, w_ref, wff_ref, bf_ref, cos_ref, sin_ref,
                   proj_ref, logf_ref, h_sc, *, tiles_per_batch):
    i = pl.program_id(0)
    j = pl.program_id(1)

    @pl.when(j == 0)
    def _():
        b = i // tiles_per_batch
        x = x_ref[...]
        y = x * lax.rsqrt(jnp.mean(x * x, axis=-1, keepdims=True) + NORM_EPS) * g_ref[...]
        shift = mod_ref[pl.ds(b, 1), 0:D_MODEL]
        scale = mod_ref[pl.ds(b, 1), D_MODEL:2 * D_MODEL]
        hb = (y * (1.0 + scale) + shift).astype(BF16)
        h_sc[...] = hb
        z = jnp.dot(hb, wff_ref[...], preferred_element_type=F32) + bf_ref[...]
        logf_ref[...] = jnp.minimum(z, 0.0) - jnp.log1p(jnp.exp(-jnp.abs(z)))

    acc = jnp.dot(h_sc[...], w_ref[...], preferred_element_type=F32)

    j_ga, j_fq, j_fk, j_rq, j_rk = (COL_GA // TN_PROJ, COL_FQ // TN_PROJ, COL_FK // TN_PROJ,
                                    COL_RQ // TN_PROJ, COL_RK // TN_PROJ)

    @pl.when((j < j_ga) | ((j >= j_fk) & (j < j_rq)))
    def _():
        proj_ref[...] = acc.astype(BF16)

    @pl.when((j >= j_ga) & (j < j_fq))
    def _():
        proj_ref[...] = _sigmoid(acc).astype(BF16)

    @pl.when(j == j_fq)
    def _():
        proj_ref[...] = (acc * (FOX_HEAD_DIM ** -0.5)).astype(BF16)

    @pl.when(j >= j_rq)
    def _():
        cos = cos_ref[...]
        sin = sin_ref[...]
        lane = lax.broadcasted_iota(jnp.int32, cos.shape, 1)
        first_half = (lane % RET_QK_DIM) < (RET_QK_DIM // 2)
        k_scale = jnp.where(j == j_rk, RET_QK_DIM ** -0.5, 1.0).astype(F32)
        for cb in range(TN_PROJ // LANES):
            xs = acc[:, cb * LANES:(cb + 1) * LANES]
            up = pltpu.roll(xs, LANES - RET_QK_DIM // 2, axis=1)
            dn = pltpu.roll(xs, RET_QK_DIM // 2, axis=1)
            rot = xs * cos + jnp.where(first_half, up, dn) * sin
            proj_ref[:, cb * LANES:(cb + 1) * LANES] = (rot * k_scale).astype(BF16)


def _in_proj(x2, mod, norm_g, w_p, w_ff, b_ff, cos_t, sin_t, seq):
    t = x2.shape[0]
    tiles_per_batch = seq // TM_PROJ
    kern = functools.partial(_inproj_kernel, tiles_per_batch=tiles_per_batch)
    return pl.pallas_call(
        kern,
        grid=(t // TM_PROJ, PROJ_WIDTH // TN_PROJ),
        in_specs=[pl.BlockSpec((TM_PROJ, D_MODEL), lambda i, j: (i, 0)),
                  pl.BlockSpec(mod.shape, lambda i, j: (0, 0)),
                  pl.BlockSpec((1, D_MODEL), lambda i, j: (0, 0)),
                  pl.BlockSpec((D_MODEL, TN_PROJ), lambda i, j: (0, j)),
                  pl.BlockSpec((D_MODEL, LANES), lambda i, j: (0, 0)),
                  pl.BlockSpec((1, LANES), lambda i, j: (0, 0)),
                  pl.BlockSpec((TM_PROJ, LANES), lambda i, j: (i % tiles_per_batch, 0)),
                  pl.BlockSpec((TM_PROJ, LANES), lambda i, j: (i % tiles_per_batch, 0))],
        out_specs=[pl.BlockSpec((TM_PROJ, TN_PROJ), lambda i, j: (i, j)),
                   pl.BlockSpec((TM_PROJ, LANES), lambda i, j: (i, 0))],
        out_shape=[jax.ShapeDtypeStruct((t, PROJ_WIDTH), BF16),
                   jax.ShapeDtypeStruct((t, LANES), F32)],
        scratch_shapes=[pltpu.VMEM((TM_PROJ, D_MODEL), BF16)],
        compiler_params=_params("arbitrary", "arbitrary"),
        name="in_proj",
    )(x2, mod, norm_g, w_p, w_ff, b_ff, cos_t, sin_t)


def _attn_kernel(q_ref, k_ref, v_ref, f_ref, o_ref):
    hp = pl.program_id(1)
    qi = pl.program_id(2)
    tq = T_ATTN
    q = q_ref[...]
    lane = lax.broadcasted_iota(jnp.int32, (tq, LANES), 1)
    row = lax.broadcasted_iota(jnp.int32, (tq, tq), 0)
    col = lax.broadcasted_iota(jnp.int32, (tq, tq), 1)
    outs = []
    for hh in range(2):
        sel = (lane < FOX_HEAD_DIM) if hh == 0 else (lane >= FOX_HEAD_DIM)
        qm = jnp.where(sel, q, jnp.zeros_like(q))

        def step(kb, carry, masked, qm=qm, hh=hh):
            m, l, acc = carry
            ks = pl.multiple_of(kb * tq, tq)
            k = k_ref[pl.ds(ks, tq), :]
            v = v_ref[pl.ds(ks, tq), :]
            s = lax.dot_general(qm, k, (((1,), (1,)), ((), ())), preferred_element_type=F32)
            s = s - f_ref[0, pl.ds(2 * hp + hh, 1), pl.ds(ks, tq)]
            if masked:
                s = jnp.where(col <= row, s, -jnp.inf)
            m_new = jnp.maximum(m, jnp.max(s, axis=-1, keepdims=True))
            alpha = jnp.exp(m - m_new)
            p = jnp.exp(s - m_new)
            l = alpha * l + jnp.sum(p, axis=-1, keepdims=True)
            acc = alpha * acc + jnp.dot(p.astype(BF16), v, preferred_element_type=F32)
            return m_new, l, acc

        carry = (jnp.full((tq, 1), -jnp.inf, F32), jnp.zeros((tq, 1), F32),
                 jnp.zeros((tq, LANES), F32))
        carry = lax.fori_loop(0, qi, lambda kb, c: step(kb, c, False), carry)
        m, l, acc = step(qi, carry, True)
        outs.append(acc / l)
    o_ref[...] = jnp.where(lane < FOX_HEAD_DIM, outs[0], outs[1]).astype(BF16)


def _fox_attn(proj, cum_f, batch, seq):
    t = proj.shape[0]
    nq = seq // T_ATTN
    cq, ck, cv = COL_FQ // LANES, COL_FK // LANES, COL_FV // LANES
    return pl.pallas_call(
        _attn_kernel,
        grid=(batch, FOX_HEADS // 2, nq),
        in_specs=[pl.BlockSpec((T_ATTN, LANES), lambda b, hp, qi: (b * nq + qi, cq + hp)),
                  pl.BlockSpec((seq, LANES), lambda b, hp, qi: (b, ck + hp)),
                  pl.BlockSpec((seq, LANES), lambda b, hp, qi: (b, cv + hp)),
                  pl.BlockSpec((1, FOX_HEADS, seq), lambda b, hp, qi: (b, 0, 0))],
        out_specs=pl.BlockSpec((T_ATTN, LANES), lambda b, hp, qi: (b * nq + qi, hp)),
        out_shape=jax.ShapeDtypeStruct((t, FOX_WIDTH), BF16),
        compiler_params=_params("arbitrary", "arbitrary", "arbitrary"),
        name="fox_attn",
    )(proj, proj, proj, cum_f)


def _ret_kernel(q_ref, k_ref, v_ref, rg_ref, gn_ref, dec_ref, zeta_ref, xi_ref, cd_ref,
                o_ref, st_sc):
    h = pl.program_id(1)
    ri = pl.program_id(2)
    c = RET_CHUNK

    @pl.when(ri == 0)
    def _():
        st_sc[...] = jnp.zeros_like(st_sc)

    lane = lax.broadcasted_iota(jnp.int32, (T_RET, LANES), 1)
    lo = (h % 2) * RET_QK_DIM
    sel = (lane >= lo) & (lane < lo + RET_QK_DIM)
    qm = jnp.where(sel, q_ref[...], jnp.zeros((T_RET, LANES), BF16))
    km = jnp.where(sel, k_ref[...], jnp.zeros((T_RET, LANES), BF16))
    decay = dec_ref[0]
    zeta = zeta_ref[0]
    xi = xi_ref[0]
    state = st_sc[...]
    for n in range(T_RET // c):
        qc = qm[n * c:(n + 1) * c]
        kc = km[n * c:(n + 1) * c]
        vc = v_ref[n * c:(n + 1) * c, :]
        sc = lax.dot_general(qc, kc, (((1,), (1,)), ((), ())), preferred_element_type=F32) * decay
        intra = jnp.dot(sc.astype(BF16), vc, preferred_element_type=F32)
        cross = jnp.dot(qc, state.astype(BF16), preferred_element_type=F32) * xi
        o = intra + cross
        kz = (kc.astype(F32) * zeta).astype(BF16)
        kv = lax.dot_general(kz, vc, (((0,), (0,)), ((), ())), preferred_element_type=F32)
        state = cd_ref[0] * state + kv
        mu = jnp.mean(o, axis=-1, keepdims=True)
        d = o - mu
        var = jnp.mean(d * d, axis=-1, keepdims=True)
        on = d * lax.rsqrt(var + NORM_EPS) * gn_ref[...]
        rg = rg_ref[n * c:(n + 1) * c, :].astype(F32)
        o_ref[n * c:(n + 1) * c, :] = (on * _silu(rg)).astype(BF16)
    st_sc[...] = state


def _retention(proj, gn_g, dec_t, zeta_t, xi_t, cd_t, batch, seq):
    t = proj.shape[0]
    nr = seq // T_RET
    cq, ck, cv, cg = COL_RQ // LANES, COL_RK // LANES, COL_RV // LANES, COL_RG // LANES
    tok = lambda b, h, ri: b * nr + ri
    tab = pl.BlockSpec((1, RET_CHUNK, LANES), lambda b, h, ri: (h, 0, 0))
    return pl.pallas_call(
        _ret_kernel,
        grid=(batch, RET_HEADS, nr),
        in_specs=[pl.BlockSpec((T_RET, LANES), lambda b, h, ri: (tok(b, h, ri), cq + h // 2)),
                  pl.BlockSpec((T_RET, LANES), lambda b, h, ri: (tok(b, h, ri), ck + h // 2)),
                  pl.BlockSpec((T_RET, LANES), lambda b, h, ri: (tok(b, h, ri), cv + h)),
                  pl.BlockSpec((T_RET, LANES), lambda b, h, ri: (tok(b, h, ri), cg + h)),
                  pl.BlockSpec((1, LANES), lambda b, h, ri: (0, h)),
                  tab, tab, tab, tab],
        out_specs=pl.BlockSpec((T_RET, LANES), lambda b, h, ri: (tok(b, h, ri), h)),
        out_shape=jax.ShapeDtypeStruct((t, RET_V_WIDTH), BF16),
        scratch_shapes=[pltpu.VMEM((LANES, RET_V_DIM), F32)],
        compiler_params=_params("arbitrary", "arbitrary", "arbitrary"),
        name="retention",
    )(proj, proj, proj, proj, gn_g, dec_t, zeta_t, xi_t, cd_t)


def _retention_tables():
    c = RET_CHUNK
    log_gamma = jnp.log1p(-jnp.exp2(-5.0 - jnp.arange(RET_HEADS, dtype=F32)))
    idx = jnp.arange(c, dtype=F32)
    diff = idx[:, None] - idx[None, :]
    dec = jnp.where(diff >= 0,
                    jnp.exp(jnp.maximum(diff, 0.0)[None] * log_gamma[:, None, None]), 0.0)
    zeta = jnp.exp((c - 1.0 - idx)[None, :] * log_gamma[:, None])
    xi = jnp.exp((idx + 1.0)[None, :] * log_gamma[:, None])
    cd = jnp.exp(c * log_gamma)
    bc = lambda v: jnp.broadcast_to(v[:, :, None], (RET_HEADS, c, LANES))
    cd_t = jnp.broadcast_to(cd[:, None, None], (RET_HEADS, LANES, RET_V_DIM))
    return dec, bc(zeta), bc(xi), cd_t


def _post_kernel(attn_ref, ret_ref, ga_ref, gb_ref, x_ref, mod_ref, g2_ref,
                 wa_ref, wb_ref, wo_ref, wr_ref, wsg_ref, wsu_ref, wsd_ref,
                 base_ref, h2_ref, score_ref, *, tiles_per_batch):
    b = pl.program_id(0) // tiles_per_batch
    ya = jnp.dot(attn_ref[...], wa_ref[...], preferred_element_type=F32)
    yb = jnp.dot(ret_ref[...], wb_ref[...], preferred_element_type=F32)
    merged = ga_ref[...].astype(F32) * ya + gb_ref[...].astype(F32) * yb
    mix = jnp.dot(merged.astype(BF16), wo_ref[...], preferred_element_type=F32)
    d = D_MODEL
    gate1 = mod_ref[pl.ds(b, 1), 2 * d:3 * d]
    shift2 = mod_ref[pl.ds(b, 1), 3 * d:4 * d]
    scale2 = mod_ref[pl.ds(b, 1), 4 * d:5 * d]
    gate2 = mod_ref[pl.ds(b, 1), 5 * d:6 * d]
    x1 = x_ref[...] + gate1 * mix
    y = x1 * lax.rsqrt(jnp.mean(x1 * x1, axis=-1, keepdims=True) + NORM_EPS) * g2_ref[...]
    h2 = y * (1.0 + scale2) + shift2
    logits_t = lax.dot_general(wr_ref[...], h2, (((1,), (1,)), ((), ())),
                               preferred_element_type=F32, precision=lax.Precision.HIGHEST)
    score_ref[...] = _sigmoid(logits_t)
    h2_ref[...] = h2
    h2b = h2.astype(BF16)
    g = jnp.dot(h2b, wsg_ref[...], preferred_element_type=F32)
    u = jnp.dot(h2b, wsu_ref[...], preferred_element_type=F32)
    shared = jnp.dot((_silu(g) * u).astype(BF16), wsd_ref[...], preferred_element_type=F32)
    base_ref[...] = x1 + gate2 * shared


def _post_mix(attn, ret, proj, x2, mod, norm2_g, wa, wb, wo, wr, wsg, wsu, wsd, seq):
    t = x2.shape[0]
    tm = TM_POST
    kern = functools.partial(_post_kernel, tiles_per_batch=seq // tm)
    full = lambda a: pl.BlockSpec(a.shape, lambda i: (0,) * a.ndim)
    return pl.pallas_call(
        kern,
        grid=(t // tm,),
        in_specs=[pl.BlockSpec((tm, FOX_WIDTH), lambda i: (i, 0)),
                  pl.BlockSpec((tm, RET_V_WIDTH), lambda i: (i, 0)),
                  pl.BlockSpec((tm, D_MODEL), lambda i: (i, COL_GA // D_MODEL)),
                  pl.BlockSpec((tm, D_MODEL), lambda i: (i, COL_GB // D_MODEL)),
                  pl.BlockSpec((tm, D_MODEL), lambda i: (i, 0)),
                  full(mod), full(norm2_g), full(wa), full(wb), full(wo), full(wr),
                  full(wsg), full(wsu), full(wsd)],
        out_specs=[pl.BlockSpec((tm, D_MODEL), lambda i: (i, 0)),
                   pl.BlockSpec((tm, D_MODEL), lambda i: (i, 0)),
                   pl.BlockSpec((N_EXPERTS, tm), lambda i: (0, i))],
        out_shape=[jax.ShapeDtypeStruct((t, D_MODEL), F32),
                   jax.ShapeDtypeStruct((t, D_MODEL), F32),
                   jax.ShapeDtypeStruct((N_EXPERTS, t), F32)],
        compiler_params=_params("arbitrary"),
        name="post_mix",
    )(attn, ret, proj, proj, x2, mod, norm2_g, wa, wb, wo, wr, wsg, wsu, wsd)


def _route_kernel(s_ref, bias_ref, e_ref, w_ref, rank_ref, cnt_ref):
    tr = s_ref.shape[1]
    gsz = N_EXPERTS // N_GROUPS

    @pl.when(pl.program_id(0) == 0)
    def _():
        cnt_ref[...] = jnp.zeros_like(cnt_ref)

    s = s_ref[...]
    sel = s + bias_ref[...]
    neg = jnp.float32(-jnp.inf)

    g3 = sel.reshape(N_GROUPS, gsz, tr)
    m1 = jnp.max(g3, axis=1)
    n_max = jnp.sum((g3 == m1[:, None, :]).astype(F32), axis=1)
    m2 = jnp.max(jnp.where(g3 < m1[:, None, :], g3, neg), axis=1)
    gs = m1 + jnp.where(n_max >= 2.0, m1, m2)

    gi = lax.broadcasted_iota(jnp.int32, (N_GROUPS, tr), 0)
    beaten = jnp.zeros((N_GROUPS, tr), F32)
    for j in range(N_GROUPS):
        gj = gs[j:j + 1, :]
        beaten = beaten + jnp.where((gj > gs) | ((gj == gs) & (gi > j)), 1.0, 0.0)
    keep = jnp.where(beaten < float(TOPK_GROUPS), 1.0, 0.0)
    keep_e = jnp.broadcast_to(keep[:, None, :], (N_GROUPS, gsz, tr)).reshape(N_EXPERTS, tr)

    ei = lax.broadcasted_iota(jnp.int32, (N_EXPERTS, tr), 0)
    val = jnp.where(keep_e > 0.0, sel, neg)
    member = jnp.zeros((N_EXPERTS, tr), F32)
    idx_rows, w_rows = [], []
    for _ in range(TOP_K):
        m = jnp.max(val, axis=0, keepdims=True)
        idx = jnp.min(jnp.where(val == m, ei, N_EXPERTS), axis=0, keepdims=True)
        hit = ei == idx
        w_rows.append(jnp.sum(jnp.where(hit, s, 0.0), axis=0, keepdims=True))
        idx_rows.append(idx)
        val = jnp.where(hit, neg, val)
        member = jnp.where(hit, 1.0, member)

    w_sum = w_rows[0]
    for k in range(1, TOP_K):
        w_sum = w_sum + w_rows[k]

    r_i = lax.broadcasted_iota(jnp.int32, (tr, tr), 0)
    c_i = lax.broadcasted_iota(jnp.int32, (tr, tr), 1)
    upper = jnp.where(r_i < c_i, 1.0, 0.0).astype(BF16)
    before = jnp.dot(member.astype(BF16), upper, preferred_element_type=F32) + cnt_ref[...]

    for k in range(TOP_K):
        hit = ei == idx_rows[k]
        e_ref[k:k + 1, :] = idx_rows[k]
        w_ref[k:k + 1, :] = w_rows[k] / w_sum * ROUTED_SCALE
        rank_ref[k:k + 1, :] = jnp.sum(jnp.where(hit, before, 0.0), axis=0,
                                       keepdims=True).astype(jnp.int32)
    cnt_ref[...] = cnt_ref[...] + jnp.sum(member, axis=1, keepdims=True)


def _route(scores_t, bias_col):
    t = scores_t.shape[1]
    tr = T_ROUTE
    row8 = lambda dt: jax.ShapeDtypeStruct((TOP_K, t), dt)
    return pl.pallas_call(
        _route_kernel,
        grid=(t // tr,),
        in_specs=[pl.BlockSpec((N_EXPERTS, tr), lambda i: (0, i)),
                  pl.BlockSpec((N_EXPERTS, 1), lambda i: (0, 0))],
        out_specs=[pl.BlockSpec((TOP_K, tr), lambda i: (0, i)),
                   pl.BlockSpec((TOP_K, tr), lambda i: (0, i)),
                   pl.BlockSpec((TOP_K, tr), lambda i: (0, i)),
                   pl.BlockSpec((N_EXPERTS, 1), lambda i: (0, 0))],
        out_shape=[row8(jnp.int32), row8(F32), row8(jnp.int32),
                   jax.ShapeDtypeStruct((N_EXPERTS, 1), F32)],
        compiler_params=_params("arbitrary"),
        name="route",
    )(scores_t, bias_col)


def _dest_kernel(e_ref, rank_ref, start_ref, dest_ref):
    tr = e_ref.shape[1]
    ei = lax.broadcasted_iota(jnp.int32, (N_EXPERTS, tr), 0)
    start = start_ref[...]
    for k in range(TOP_K):
        hit = ei == e_ref[k:k + 1, :]
        off = jnp.sum(jnp.where(hit, start, 0.0), axis=0, keepdims=True)
        dest_ref[k:k + 1, :] = rank_ref[k:k + 1, :] + off.astype(jnp.int32)


def _dest_rows(top_e, rank, start_col):
    t = top_e.shape[1]
    tr = T_ROUTE
    spec = pl.BlockSpec((TOP_K, tr), lambda i: (0, i))
    return pl.pallas_call(
        _dest_kernel,
        grid=(t // tr,),
        in_specs=[spec, spec, pl.BlockSpec((N_EXPERTS, 1), lambda i: (0, 0))],
        out_specs=spec,
        out_shape=jax.ShapeDtypeStruct((TOP_K, t), jnp.int32),
        compiler_params=_params("arbitrary"),
        name="dest_rows",
    )(top_e, rank, start_col)


def _row_copy(src_ref, src_row, dst_ref, dst_row, sem):
    return pltpu.make_async_copy(src_ref.at[pl.ds(src_row, 1), :],
                                 dst_ref.at[pl.ds(dst_row, 1), :], sem)


def _rows_wait(src_ref, dst_ref, n, sem):
    def wait8(_, carry):
        pltpu.make_async_copy(src_ref.at[pl.ds(0, 8), :], dst_ref.at[pl.ds(0, 8), :], sem).wait()
        return carry

    def wait1(_, carry):
        _row_copy(src_ref, 0, dst_ref, 0, sem).wait()
        return carry

    lax.fori_loop(0, n // 8, wait8, 0)
    lax.fori_loop(0, n % 8, wait1, 0)


def _moe_kernel(order_ref, be_ref, nv_ref, cs_ref, h_ref, wg_ref, wu_ref, wd_ref, y_ref,
                xbuf, ybuf, wg_sc, wu_sc, wd_sc, gsem, ssem, *, n_tokens):
    i = pl.program_id(0)
    n_steps = pl.num_programs(0)
    slot = i % 2

    def gather(blk, s):
        def body(j, carry):
            tok = order_ref[cs_ref[blk] + j] & (n_tokens - 1)
            _row_copy(h_ref, tok, xbuf.at[s], j, gsem.at[s]).start()
            return carry
        lax.fori_loop(0, nv_ref[blk], body, 0)

    def scatter_wait(blk, s):
        _rows_wait(ybuf.at[s], y_ref, nv_ref[blk], ssem.at[s])

    @pl.when(i == 0)
    def _():
        gather(0, 0)

    @pl.when(i + 1 < n_steps)
    def _():
        gather(i + 1, 1 - slot)

    prev = be_ref[jnp.maximum(i - 1, 0)]

    @pl.when((i == 0) | (be_ref[i] != prev))
    def _():
        wg_sc[...] = wg_ref[0].astype(BF16)
        wu_sc[...] = wu_ref[0].astype(BF16)
        wd_sc[...] = wd_ref[0].astype(BF16)

    @pl.when(i >= 2)
    def _():
        scatter_wait(i - 2, slot)

    nv = nv_ref[i]
    _rows_wait(h_ref, xbuf.at[slot], nv, gsem.at[slot])

    @pl.when(nv > 0)
    def _():
        row = lax.broadcasted_iota(jnp.int32, (ROWS_MOE, D_MODEL), 0)
        xb = jnp.where(row < nv, xbuf[slot], 0.0).astype(BF16)
        g = jnp.dot(xb, wg_sc[...], preferred_element_type=F32)
        u = jnp.dot(xb, wu_sc[...], preferred_element_type=F32)
        ybuf[slot] = jnp.dot((_silu(g) * u).astype(BF16), wd_sc[...], preferred_element_type=F32)

        def body(j, carry):
            _row_copy(ybuf.at[slot], j, y_ref, order_ref[cs_ref[i] + j], ssem.at[slot]).start()
            return carry
        lax.fori_loop(0, nv, body, 0)

    @pl.when(i == n_steps - 1)
    def _():
        scatter_wait(i - 1, 1 - slot)
        scatter_wait(i, slot)


def _moe_experts(order, block_expert, block_valid, block_start, h2, w_gate, w_up, w_down):
    t = h2.shape[0]
    assert t & (t - 1) == 0, "token count must be a power of two"
    n_blocks = block_expert.shape[0]
    wspec = lambda shp: pl.BlockSpec((1,) + shp, lambda i, od, be, nv, cs: (be[i], 0, 0))
    grid_spec = pltpu.PrefetchScalarGridSpec(
        num_scalar_prefetch=4,
        grid=(n_blocks,),
        in_specs=[pl.BlockSpec(memory_space=pl.ANY),
                  wspec((D_MODEL, EXPERT_DIM)), wspec((D_MODEL, EXPERT_DIM)),
                  wspec((EXPERT_DIM, D_MODEL))],
        out_specs=pl.BlockSpec(memory_space=pl.ANY),
        scratch_shapes=[pltpu.VMEM((2, ROWS_MOE, D_MODEL), F32),
                        pltpu.VMEM((2, ROWS_MOE, D_MODEL), F32),
                        pltpu.VMEM((D_MODEL, EXPERT_DIM), BF16),
                        pltpu.VMEM((D_MODEL, EXPERT_DIM), BF16),
                        pltpu.VMEM((EXPERT_DIM, D_MODEL), BF16),
                        pltpu.SemaphoreType.DMA((2,)),
                        pltpu.SemaphoreType.DMA((2,))],
    )
    return pl.pallas_call(
        functools.partial(_moe_kernel, n_tokens=t),
        grid_spec=grid_spec,
        out_shape=jax.ShapeDtypeStruct((TOP_K * t, D_MODEL), F32),
        compiler_params=_params("arbitrary"),
        name="moe_experts",
    )(order, block_expert, block_valid, block_start, h2, w_gate, w_up, w_down)


def _combine_kernel(base_ref, w_ref, mod_ref, g_ref, y_ref, o_ref, *, tiles_per_batch, last_layer):
    routed = y_ref[0] * w_ref[:, 0:1]
    for k in range(1, TOP_K):
        routed = routed + y_ref[k] * w_ref[:, k:k + 1]
    b = pl.program_id(0) // tiles_per_batch
    gate2 = mod_ref[pl.ds(b, 1), 5 * D_MODEL:6 * D_MODEL]
    x = base_ref[...] + gate2 * routed
    if last_layer:
        x = x * lax.rsqrt(jnp.mean(x * x, axis=-1, keepdims=True) + NORM_EPS) * g_ref[...]
    o_ref[...] = x


def _combine(base, top_w, mod, final_g, y, seq, last_layer):
    t = base.shape[0]
    tc = T_COMBINE
    kern = functools.partial(_combine_kernel, tiles_per_batch=seq // tc, last_layer=last_layer)
    return pl.pallas_call(
        kern,
        grid=(t // tc,),
        in_specs=[pl.BlockSpec((tc, D_MODEL), lambda i: (i, 0)),
                  pl.BlockSpec((tc, TOP_K), lambda i: (i, 0)),
                  pl.BlockSpec(mod.shape, lambda i: (0, 0)),
                  pl.BlockSpec((1, D_MODEL), lambda i: (0, 0)),
                  pl.BlockSpec((TOP_K, tc, D_MODEL), lambda i: (0, i, 0))],
        out_specs=pl.BlockSpec((tc, D_MODEL), lambda i: (i, 0)),
        out_shape=jax.ShapeDtypeStruct((t, D_MODEL), F32),
        compiler_params=_params("arbitrary"),
        name="combine",
    )(base, top_w, mod, final_g, y.reshape(TOP_K, t, D_MODEL))


def _block_tables(counts, n_assign):
    r = ROWS_MOE
    cnt = counts.reshape(N_EXPERTS).astype(jnp.int32)
    first = jnp.cumsum(cnt) - cnt
    nblk = (cnt + r - 1) // r
    blk_end = jnp.cumsum(nblk)
    blk_start = blk_end - nblk
    n_blocks = (n_assign + N_EXPERTS * (r - 1) + r - 1) // r
    bi = jnp.arange(n_blocks, dtype=jnp.int32)
    bc = jnp.minimum(bi, blk_end[-1] - 1)
    block_expert = jnp.minimum(jnp.sum((blk_end[None, :] <= bc[:, None]).astype(jnp.int32), axis=1),
                               N_EXPERTS - 1)
    onehot = (block_expert[:, None] == jnp.arange(N_EXPERTS, dtype=jnp.int32)[None, :])
    pick = lambda v: jnp.sum(jnp.where(onehot, v[None, :], 0), axis=1)
    within = bc - pick(blk_start)
    block_valid = jnp.where(bi < blk_end[-1], jnp.clip(pick(cnt) - within * r, 0, r), 0)
    block_start = pick(first) + within * r
    return (first.astype(F32).reshape(N_EXPERTS, 1), block_expert.astype(jnp.int32),
            block_valid.astype(jnp.int32), block_start.astype(jnp.int32))


def kernel(x, c, w_ada, b_ada, norm1_g, w_in, b_forget, ret_gn_g, w_branch_a, w_branch_b, w_out,
           norm2_g, w_router, router_bias, w_exp_gate, w_exp_up, w_exp_down, w_sh_gate, w_sh_up,
           w_sh_down, final_g):
    batch, seq, d = x.shape
    t = batch * seq
    depth = w_ada.shape[0]
    x2 = x.reshape(t, d)

    half = RET_QK_DIM // 2
    inv_freq = ROPE_BASE ** (-jnp.arange(half, dtype=F32) / half)
    ang = jnp.arange(seq, dtype=F32)[:, None] * inv_freq[None, :]
    cos32, sin32 = jnp.cos(ang), jnp.sin(ang)
    cos_t = jnp.concatenate([cos32, cos32, cos32, cos32], axis=-1)
    sin_t = jnp.concatenate([-sin32, sin32, -sin32, sin32], axis=-1)
    dec_t, zeta_t, xi_t, cd_t = _retention_tables()
    c_pad = jnp.zeros((8, d), F32).at[:batch].set(c)

    for l in range(depth):
        mod = _adaln_mod(c_pad, w_ada[l], b_ada[l])

        w = w_in[l]
        o_fq, o_fk, o_fv, o_ff = 0, FOX_WIDTH, 2 * FOX_WIDTH, 3 * FOX_WIDTH
        o_rq = o_ff + FOX_HEADS
        o_rk = o_rq + RET_QK_WIDTH
        o_rv = o_rk + RET_QK_WIDTH
        o_rg = o_rv + RET_V_WIDTH
        o_ga = o_rg + RET_V_WIDTH
        o_gb = o_ga + D_MODEL
        w_p = jnp.concatenate(
            [w[:, o_rv:o_rg], w[:, o_rg:o_ga], w[:, o_ga:o_gb], w[:, o_gb:o_gb + D_MODEL],
             w[:, o_fq:o_fk], w[:, o_fk:o_fv], w[:, o_fv:o_ff], w[:, o_rq:o_rk], w[:, o_rk:o_rv]],
            axis=1).astype(BF16)
        w_ff = jnp.zeros((d, LANES), BF16).at[:, :FOX_HEADS].set(w[:, o_ff:o_rq].astype(BF16))
        b_ff = jnp.zeros((1, LANES), F32).at[0, :FOX_HEADS].set(b_forget[l].astype(F32))

        proj, logf = _in_proj(x2, mod, norm1_g[l].reshape(1, d), w_p, w_ff, b_ff, cos_t, sin_t, seq)
        cum_f = jnp.cumsum(logf[:, :FOX_HEADS].reshape(batch, seq, FOX_HEADS), axis=1)
        cum_f = cum_f.transpose(0, 2, 1)

        attn = _fox_attn(proj, cum_f, batch, seq)
        ret = _retention(proj, ret_gn_g[l].reshape(1, RET_V_WIDTH), dec_t, zeta_t, xi_t, cd_t,
                         batch, seq)

        base, h2, scores_t = _post_mix(
            attn, ret, proj, x2, mod, norm2_g[l].reshape(1, d),
            w_branch_a[l].astype(BF16), w_branch_b[l].astype(BF16), w_out[l].astype(BF16),
            w_router[l].T, w_sh_gate[l].astype(BF16), w_sh_up[l].astype(BF16),
            w_sh_down[l].astype(BF16), seq)

        top_e, top_w, rank, counts = _route(scores_t,
                                            router_bias[l].astype(F32).reshape(N_EXPERTS, 1))
        first, block_expert, block_valid, block_start = _block_tables(counts, t * TOP_K)
        sorted_pos = _dest_rows(top_e, rank, first).reshape(TOP_K * t)
        order = jnp.argsort(sorted_pos).astype(jnp.int32)
        y = _moe_experts(order, block_expert, block_valid, block_start, h2,
                         w_exp_gate[l], w_exp_up[l], w_exp_down[l])
        x2 = _combine(base, top_w.T, mod, final_g.reshape(1, d), y, seq, l == depth - 1)
    return x2.reshape(batch, seq, d)
```

```python
import functools

import jax
import jax.numpy as jnp
import numpy as np
from jax import lax
from jax.experimental import pallas as pl
from jax.experimental.pallas import tpu as pltpu

F32 = jnp.float32
BF16 = jnp.bfloat16

D_MODEL = 1024
FOX_HEADS = 8
FOX_HEAD_DIM = 64
FOX_WIDTH = 512
RET_HEADS = 8
RET_QK_DIM = 64
RET_V_DIM = 128
RET_QK_WIDTH = 512
RET_V_WIDTH = 1024
RET_CHUNK = 128
ROPE_BASE = 10000.0
N_EXPERTS = 256
TOP_K = 8
N_GROUPS = 8
TOPK_GROUPS = 4
EXPERT_DIM = 256
ROUTED_SCALE = 2.5
NORM_EPS = 1e-6

LANES = 128
VMEM_LIMIT = 56 * 1024 * 1024

COL_RV, COL_RG, COL_GA, COL_GB = 0, 1024, 2048, 3072
COL_FQ, COL_FK, COL_FV, COL_RQ, COL_RK = 4096, 4608, 5120, 5632, 6144
PROJ_WIDTH = 6656

TM_PROJ = 512
TN_PROJ = 512
T_ATTN = 512
T_RET = 512
TM_POST = 512
T_ROUTE = 512
ROWS_MOE = 256
T_COMBINE = 256


def _sigmoid(z):
    return 1.0 / (1.0 + jnp.exp(-z))


def _silu(z):
    return z * _sigmoid(z)


def _params(*sem):
    return pltpu.CompilerParams(dimension_semantics=sem, vmem_limit_bytes=VMEM_LIMIT)


def _adaln_kernel(c_ref, w_ref, b_ref, o_ref):
    a = _silu(c_ref[...]).astype(BF16)
    o_ref[...] = jnp.dot(a, w_ref[...].astype(BF16), preferred_element_type=F32) + b_ref[...]


def _adaln_mod(c_pad, w_ada, b_ada):
    n = w_ada.shape[1]
    tn = 1024
    return pl.pallas_call(
        _adaln_kernel,
        grid=(n // tn,),
        in_specs=[pl.BlockSpec((8, D_MODEL), lambda j: (0, 0)),
                  pl.BlockSpec((D_MODEL, tn), lambda j: (0, j)),
                  pl.BlockSpec((1, tn), lambda j: (0, j))],
        out_specs=pl.BlockSpec((8, tn), lambda j: (0, j)),
        out_shape=jax.ShapeDtypeStruct((8, n), F32),
        compiler_params=_params("arbitrary"),
        name="adaln_mod",
    )(c_pad, w_ada, b_ada.reshape(1, n))


def _inproj_kernel(x_ref, mod_ref, g_ref, w_ref, wff_ref, bf_ref, cos_ref, sin_ref,
                   proj_ref, logf_ref, h_sc, *, tiles_per_batch):
    i = pl.program_id(0)
    j = pl.program_id(1)

    @pl.when(j == 0)
    def _():
        b = i // tiles_per_batch
        x = x_ref[...]
        y = x * lax.rsqrt(jnp.mean(x * x, axis=-1, keepdims=True) + NORM_EPS) * g_ref[...]
        shift = mod_ref[pl.ds(b, 1), 0:D_MODEL]
        scale = mod_ref[pl.ds(b, 1), D_MODEL:2 * D_MODEL]
        hb = (y * (1.0 + scale) + shift).astype(BF16)
        h_sc[...] = hb
        z = jnp.dot(hb, wff_ref[...], preferred_element_type=F32) + bf_ref[...]
        logf_ref[...] = jnp.minimum(z, 0.0) - jnp.log1p(jnp.exp(-jnp.abs(z)))

    acc = jnp.dot(h_sc[...], w_ref[...], preferred_element_type=F32)

    j_ga, j_fq, j_fk, j_rq, j_rk = (COL_GA // TN_PROJ, COL_FQ // TN_PROJ, COL_FK // TN_PROJ,
                                    COL_RQ // TN_PROJ, COL_RK // TN_PROJ)

    @pl.when((j < j_ga) | ((j >= j_fk) & (j < j_rq)))
    def _():
        proj_ref[...] = acc.astype(BF16)

    @pl.when((j >= j_ga) & (j < j_fq))
    def _():
        proj_ref[...] = _sigmoid(acc).astype(BF16)

    @pl.when(j == j_fq)
    def _():
        proj_ref[...] = (acc * (FOX_HEAD_DIM ** -0.5)).astype(BF16)

    @pl.when(j >= j_rq)
    def _():
        cos = cos_ref[...]
        sin = sin_ref[...]
        lane = lax.broadcasted_iota(jnp.int32, cos.shape, 1)
        first_half = (lane % RET_QK_DIM) < (RET_QK_DIM // 2)
        k_scale = jnp.where(j == j_rk, RET_QK_DIM ** -0.5, 1.0).astype(F32)
        for cb in range(TN_PROJ // LANES):
            xs = acc[:, cb * LANES:(cb + 1) * LANES]
            up = pltpu.roll(xs, LANES - RET_QK_DIM // 2, axis=1)
            dn = pltpu.roll(xs, RET_QK_DIM // 2, axis=1)
            rot = xs * cos + jnp.where(first_half, up, dn) * sin
            proj_ref[:, cb * LANES:(cb + 1) * LANES] = (rot * k_scale).astype(BF16)


def _in_proj(x2, mod, norm_g, w_p, w_ff, b_ff, cos_t, sin_t, seq):
    t = x2.shape[0]
    tiles_per_batch = seq // TM_PROJ
    kern = functools.partial(_inproj_kernel, tiles_per_batch=tiles_per_batch)
    return pl.pallas_call(
        kern,
        grid=(t // TM_PROJ, PROJ_WIDTH // TN_PROJ),
        in_specs=[pl.BlockSpec((TM_PROJ, D_MODEL), lambda i, j: (i, 0)),
                  pl.BlockSpec(mod.shape, lambda i, j: (0, 0)),
                  pl.BlockSpec((1, D_MODEL), lambda i, j: (0, 0)),
                  pl.BlockSpec((D_MODEL, TN_PROJ), lambda i, j: (0, j)),
                  pl.BlockSpec((D_MODEL, LANES), lambda i, j: (0, 0)),
                  pl.BlockSpec((1, LANES), lambda i, j: (0, 0)),
                  pl.BlockSpec((TM_PROJ, LANES), lambda i, j: (i % tiles_per_batch, 0)),
                  pl.BlockSpec((TM_PROJ, LANES), lambda i, j: (i % tiles_per_batch, 0))],
        out_specs=[pl.BlockSpec((TM_PROJ, TN_PROJ), lambda i, j: (i, j)),
                   pl.BlockSpec((TM_PROJ, LANES), lambda i, j: (i, 0))],
        out_shape=[jax.ShapeDtypeStruct((t, PROJ_WIDTH), BF16),
                   jax.ShapeDtypeStruct((t, LANES), F32)],
        scratch_shapes=[pltpu.VMEM((TM_PROJ, D_MODEL), BF16)],
        compiler_params=_params("arbitrary", "arbitrary"),
        name="in_proj",
    )(x2, mod, norm_g, w_p, w_ff, b_ff, cos_t, sin_t)


def _attn_kernel(q_ref, k_ref, v_ref, f_ref, o_ref):
    hp = pl.program_id(1)
    qi = pl.program_id(2)
    tq = T_ATTN
    q = q_ref[...]
    lane = lax.broadcasted_iota(jnp.int32, (tq, LANES), 1)
    row = lax.broadcasted_iota(jnp.int32, (tq, tq), 0)
    col = lax.broadcasted_iota(jnp.int32, (tq, tq), 1)
    outs = []
    for hh in range(2):
        sel = (lane < FOX_HEAD_DIM) if hh == 0 else (lane >= FOX_HEAD_DIM)
        qm = jnp.where(sel, q, jnp.zeros_like(q))

        def step(kb, carry, masked, qm=qm, hh=hh):
            m, l, acc = carry
            ks = pl.multiple_of(kb * tq, tq)
            k = k_ref[pl.ds(ks, tq), :]
            v = v_ref[pl.ds(ks, tq), :]
            s = lax.dot_general(qm, k, (((1,), (1,)), ((), ())), preferred_element_type=F32)
            s = s - f_ref[0, pl.ds(2 * hp + hh, 1), pl.ds(ks, tq)]
            if masked:
                s = jnp.where(col <= row, s, -jnp.inf)
            m_new = jnp.maximum(m, jnp.max(s, axis=-1, keepdims=True))
            alpha = jnp.exp(m - m_new)
            p = jnp.exp(s - m_new)
            l = alpha * l + jnp.sum(p, axis=-1, keepdims=True)
            acc = alpha * acc + jnp.dot(p.astype(BF16), v, preferred_element_type=F32)
            return m_new, l, acc

        carry = (jnp.full((tq, 1), -jnp.inf, F32), jnp.zeros((tq, 1), F32),
                 jnp.zeros((tq, LANES), F32))
        carry = lax.fori_loop(0, qi, lambda kb, c: step(kb, c, False), carry)
        m, l, acc = step(qi, carry, True)
        outs.append(acc / l)
    o_ref[...] = jnp.where(lane < FOX_HEAD_DIM, outs[0], outs[1]).astype(BF16)


def _fox_attn(proj, cum_f, batch, seq):
    t = proj.shape[0]
    nq = seq // T_ATTN
    cq, ck, cv = COL_FQ // LANES, COL_FK // LANES, COL_FV // LANES
    return pl.pallas_call(
        _attn_kernel,
        grid=(batch, FOX_HEADS // 2, nq),
        in_specs=[pl.BlockSpec((T_ATTN, LANES), lambda b, hp, qi: (b * nq + qi, cq + hp)),
                  pl.BlockSpec((seq, LANES), lambda b, hp, qi: (b, ck + hp)),
                  pl.BlockSpec((seq, LANES), lambda b, hp, qi: (b, cv + hp)),
                  pl.BlockSpec((1, FOX_HEADS, seq), lambda b, hp, qi: (b, 0, 0))],
        out_specs=pl.BlockSpec((T_ATTN, LANES), lambda b, hp, qi: (b * nq + qi, hp)),
        out_shape=jax.ShapeDtypeStruct((t, FOX_WIDTH), BF16),
        compiler_params=_params("arbitrary", "arbitrary", "arbitrary"),
        name="fox_attn",
    )(proj, proj, proj, cum_f)


def _ret_kernel(q_ref, k_ref, v_ref, rg_ref, gn_ref, dec_ref, zeta_ref, xi_ref, cd_ref,
                o_ref, st_sc):
    h = pl.program_id(1)
    ri = pl.program_id(2)
    c = RET_CHUNK

    @pl.when(ri == 0)
    def _():
        st_sc[...] = jnp.zeros_like(st_sc)

    lane = lax.broadcasted_iota(jnp.int32, (T_RET, LANES), 1)
    lo = (h % 2) * RET_QK_DIM
    sel = (lane >= lo) & (lane < lo + RET_QK_DIM)
    qm = jnp.where(sel, q_ref[...], jnp.zeros((T_RET, LANES), BF16))
    km = jnp.where(sel, k_ref[...], jnp.zeros((T_RET, LANES), BF16))
    decay = dec_ref[0]
    zeta = zeta_ref[0]
    xi = xi_ref[0]
    state = st_sc[...]
    for n in range(T_RET // c):
        qc = qm[n * c:(n + 1) * c]
        kc = km[n * c:(n + 1) * c]
        vc = v_ref[n * c:(n + 1) * c, :]
        sc = lax.dot_general(qc, kc, (((1,), (1,)), ((), ())), preferred_element_type=F32) * decay
        intra = jnp.dot(sc.astype(BF16), vc, preferred_element_type=F32)
        cross = jnp.dot(qc, state.astype(BF16), preferred_element_type=F32) * xi
        o = intra + cross
        kz = (kc.astype(F32) * zeta).astype(BF16)
        kv = lax.dot_general(kz, vc, (((0,), (0,)), ((), ())), preferred_element_type=F32)
        state = cd_ref[0] * state + kv
        mu = jnp.mean(o, axis=-1, keepdims=True)
        d = o - mu
        var = jnp.mean(d * d, axis=-1, keepdims=True)
        on = d * lax.rsqrt(var + NORM_EPS) * gn_ref[...]
        rg = rg_ref[n * c:(n + 1) * c, :].astype(F32)
        o_ref[n * c:(n + 1) * c, :] = (on * _silu(rg)).astype(BF16)
    st_sc[...] = state


def _retention(proj, gn_g, dec_t, zeta_t, xi_t, cd_t, batch, seq):
    t = proj.shape[0]
    nr = seq // T_RET
    cq, ck, cv, cg = COL_RQ // LANES, COL_RK // LANES, COL_RV // LANES, COL_RG // LANES
    tok = lambda b, h, ri: b * nr + ri
    tab = pl.BlockSpec((1, RET_CHUNK, LANES), lambda b, h, ri: (h, 0, 0))
    return pl.pallas_call(
        _ret_kernel,
        grid=(batch, RET_HEADS, nr),
        in_specs=[pl.BlockSpec((T_RET, LANES), lambda b, h, ri: (tok(b, h, ri), cq + h // 2)),
                  pl.BlockSpec((T_RET, LANES), lambda b, h, ri: (tok(b, h, ri), ck + h // 2)),
                  pl.BlockSpec((T_RET, LANES), lambda b, h, ri: (tok(b, h, ri), cv + h)),
                  pl.BlockSpec((T_RET, LANES), lambda b, h, ri: (tok(b, h, ri), cg + h)),
                  pl.BlockSpec((1, LANES), lambda b, h, ri: (0, h)),
                  tab, tab, tab, tab],
        out_specs=pl.BlockSpec((T_RET, LANES), lambda b, h, ri: (tok(b, h, ri), h)),
        out_shape=jax.ShapeDtypeStruct((t, RET_V_WIDTH), BF16),
        scratch_shapes=[pltpu.VMEM((LANES, RET_V_DIM), F32)],
        compiler_params=_params("arbitrary", "arbitrary", "arbitrary"),
        name="retention",
    )(proj, proj, proj, proj, gn_g, dec_t, zeta_t, xi_t, cd_t)


def _retention_tables():
    c = RET_CHUNK
    log_gamma = jnp.log1p(-jnp.exp2(-5.0 - jnp.arange(RET_HEADS, dtype=F32)))
    idx = jnp.arange(c, dtype=F32)
    diff = idx[:, None] - idx[None, :]
    dec = jnp.where(diff >= 0,
                    jnp.exp(jnp.maximum(diff, 0.0)[None] * log_gamma[:, None, None]), 0.0)
    zeta = jnp.exp((c - 1.0 - idx)[None, :] * log_gamma[:, None])
    xi = jnp.exp((idx + 1.0)[None, :] * log_gamma[:, None])
    cd = jnp.exp(c * log_gamma)
    bc = lambda v: jnp.broadcast_to(v[:, :, None], (RET_HEADS, c, LANES))
    cd_t = jnp.broadcast_to(cd[:, None, None], (RET_HEADS, LANES, RET_V_DIM))
    return dec, bc(zeta), bc(xi), cd_t


def _post_kernel(attn_ref, ret_ref, ga_ref, gb_ref, x_ref, mod_ref, g2_ref,
                 wa_ref, wb_ref, wo_ref, wr_ref, wsg_ref, wsu_ref, wsd_ref,
                 base_ref, h2_ref, score_ref, *, tiles_per_batch):
    b = pl.program_id(0) // tiles_per_batch
    ya = jnp.dot(attn_ref[...], wa_ref[...], preferred_element_type=F32)
    yb = jnp.dot(ret_ref[...], wb_ref[...], preferred_element_type=F32)
    merged = ga_ref[...].astype(F32) * ya + gb_ref[...].astype(F32) * yb
    mix = jnp.dot(merged.astype(BF16), wo_ref[...], preferred_element_type=F32)
    d = D_MODEL
    gate1 = mod_ref[pl.ds(b, 1), 2 * d:3 * d]
    shift2 = mod_ref[pl.ds(b, 1), 3 * d:4 * d]
    scale2 = mod_ref[pl.ds(b, 1), 4 * d:5 * d]
    gate2 = mod_ref[pl.ds(b, 1), 5 * d:6 * d]
    x1 = x_ref[...] + gate1 * mix
    y = x1 * lax.rsqrt(jnp.mean(x1 * x1, axis=-1, keepdims=True) + NORM_EPS) * g2_ref[...]
    h2 = y * (1.0 + scale2) + shift2
    logits_t = lax.dot_general(wr_ref[...], h2, (((1,), (1,)), ((), ())),
                               preferred_element_type=F32, precision=lax.Precision.HIGHEST)
    score_ref[...] = _sigmoid(logits_t)
    h2_ref[...] = h2
    h2b = h2.astype(BF16)
    g = jnp.dot(h2b, wsg_ref[...], preferred_element_type=F32)
    u = jnp.dot(h2b, wsu_ref[...], preferred_element_type=F32)
    shared = jnp.dot((_silu(g) * u).astype(BF16), wsd_ref[...], preferred_element_type=F32)
    base_ref[...] = x1 + gate2 * shared


def _post_mix(attn, ret, proj, x2, mod, norm2_g, wa, wb, wo, wr, wsg, wsu, wsd, seq):
    t = x2.shape[0]
    tm = TM_POST
    kern = functools.partial(_post_kernel, tiles_per_batch=seq // tm)
    full = lambda a: pl.BlockSpec(a.shape, lambda i: (0,) * a.ndim)
    return pl.pallas_call(
        kern,
        grid=(t // tm,),
        in_specs=[pl.BlockSpec((tm, FOX_WIDTH), lambda i: (i, 0)),
                  pl.BlockSpec((tm, RET_V_WIDTH), lambda i: (i, 0)),
                  pl.BlockSpec((tm, D_MODEL), lambda i: (i, COL_GA // D_MODEL)),
                  pl.BlockSpec((tm, D_MODEL), lambda i: (i, COL_GB // D_MODEL)),
                  pl.BlockSpec((tm, D_MODEL), lambda i: (i, 0)),
                  full(mod), full(norm2_g), full(wa), full(wb), full(wo), full(wr),
                  full(wsg), full(wsu), full(wsd)],
        out_specs=[pl.BlockSpec((tm, D_MODEL), lambda i: (i, 0)),
                   pl.BlockSpec((tm, D_MODEL), lambda i: (i, 0)),
                   pl.BlockSpec((N_EXPERTS, tm), lambda i: (0, i))],
        out_shape=[jax.ShapeDtypeStruct((t, D_MODEL), F32),
                   jax.ShapeDtypeStruct((t, D_MODEL), F32),
                   jax.ShapeDtypeStruct((N_EXPERTS, t), F32)],
        compiler_params=_params("arbitrary"),
        name="post_mix",
    )(attn, ret, proj, proj, x2, mod, norm2_g, wa, wb, wo, wr, wsg, wsu, wsd)


def _route_kernel(s_ref, bias_ref, e_ref, w_ref, rank_ref, cnt_ref):
    tr = s_ref.shape[1]
    gsz = N_EXPERTS // N_GROUPS

    @pl.when(pl.program_id(0) == 0)
    def _():
        cnt_ref[...] = jnp.zeros_like(cnt_ref)

    s = s_ref[...]
    sel = s + bias_ref[...]
    neg = jnp.float32(-jnp.inf)

    g3 = sel.reshape(N_GROUPS, gsz, tr)
    m1 = jnp.max(g3, axis=1)
    n_max = jnp.sum((g3 == m1[:, None, :]).astype(F32), axis=1)
    m2 = jnp.max(jnp.where(g3 < m1[:, None, :], g3, neg), axis=1)
    gs = m1 + jnp.where(n_max >= 2.0, m1, m2)

    gi = lax.broadcasted_iota(jnp.int32, (N_GROUPS, tr), 0)
    beaten = jnp.zeros((N_GROUPS, tr), F32)
    for j in range(N_GROUPS):
        gj = gs[j:j + 1, :]
        beaten = beaten + jnp.where((gj > gs) | ((gj == gs) & (gi > j)), 1.0, 0.0)
    keep = jnp.where(beaten < float(TOPK_GROUPS), 1.0, 0.0)
    keep_e = jnp.broadcast_to(keep[:, None, :], (N_GROUPS, gsz, tr)).reshape(N_EXPERTS, tr)

    ei = lax.broadcasted_iota(jnp.int32, (N_EXPERTS, tr), 0)
    val = jnp.where(keep_e > 0.0, sel, neg)
    member = jnp.zeros((N_EXPERTS, tr), F32)
    idx_rows, w_rows = [], []
    for _ in range(TOP_K):
        m = jnp.max(val, axis=0, keepdims=True)
        idx = jnp.min(jnp.where(val == m, ei, N_EXPERTS), axis=0, keepdims=True)
        hit = ei == idx
        w_rows.append(jnp.sum(jnp.where(hit, s, 0.0), axis=0, keepdims=True))
        idx_rows.append(idx)
        val = jnp.where(hit, neg, val)
        member = jnp.where(hit, 1.0, member)

    w_sum = w_rows[0]
    for k in range(1, TOP_K):
        w_sum = w_sum + w_rows[k]

    r_i = lax.broadcasted_iota(jnp.int32, (tr, tr), 0)
    c_i = lax.broadcasted_iota(jnp.int32, (tr, tr), 1)
    upper = jnp.where(r_i < c_i, 1.0, 0.0).astype(BF16)
    before = jnp.dot(member.astype(BF16), upper, preferred_element_type=F32) + cnt_ref[...]

    for k in range(TOP_K):
        hit = ei == idx_rows[k]
        e_ref[k:k + 1, :] = idx_rows[k]
        w_ref[k:k + 1, :] = w_rows[k] / w_sum * ROUTED_SCALE
        rank_ref[k:k + 1, :] = jnp.sum(jnp.where(hit, before, 0.0), axis=0,
                                       keepdims=True).astype(jnp.int32)
    cnt_ref[...] = cnt_ref[...] + jnp.sum(member, axis=1, keepdims=True)


def _route(scores_t, bias_col):
    t = scores_t.shape[1]
    tr = T_ROUTE
    row8 = lambda dt: jax.ShapeDtypeStruct((TOP_K, t), dt)
    return pl.pallas_call(
        _route_kernel,
        grid=(t // tr,),
        in_specs=[pl.BlockSpec((N_EXPERTS, tr), lambda i: (0, i)),
                  pl.BlockSpec((N_EXPERTS, 1), lambda i: (0, 0))],
        out_specs=[pl.BlockSpec((TOP_K, tr), lambda i: (0, i)),
                   pl.BlockSpec((TOP_K, tr), lambda i: (0, i)),
                   pl.BlockSpec((TOP_K, tr), lambda i: (0, i)),
                   pl.BlockSpec((N_EXPERTS, 1), lambda i: (0, 0))],
        out_shape=[row8(jnp.int32), row8(F32), row8(jnp.int32),
                   jax.ShapeDtypeStruct((N_EXPERTS, 1), F32)],
        compiler_params=_params("arbitrary"),
        name="route",
    )(scores_t, bias_col)


def _dest_kernel(e_ref, rank_ref, start_ref, dest_ref):
    tr = e_ref.shape[1]
    ei = lax.broadcasted_iota(jnp.int32, (N_EXPERTS, tr), 0)
    start = start_ref[...]
    for k in range(TOP_K):
        hit = ei == e_ref[k:k + 1, :]
        off = jnp.sum(jnp.where(hit, start, 0.0), axis=0, keepdims=True)
        dest_ref[k:k + 1, :] = rank_ref[k:k + 1, :] + off.astype(jnp.int32)


def _dest_rows(top_e, rank, start_col):
    t = top_e.shape[1]
    tr = T_ROUTE
    spec = pl.BlockSpec((TOP_K, tr), lambda i: (0, i))
    return pl.pallas_call(
        _dest_kernel,
        grid=(t // tr,),
        in_specs=[spec, spec, pl.BlockSpec((N_EXPERTS, 1), lambda i: (0, 0))],
        out_specs=spec,
        out_shape=jax.ShapeDtypeStruct((TOP_K, t), jnp.int32),
        compiler_params=_params("arbitrary"),
        name="dest_rows",
    )(top_e, rank, start_col)


def _for_rows(n, fn):
    def group(g, carry):
        for u in range(8):
            fn(g * 8 + u, g, u)
        return carry

    def single(j, carry):
        fn(j, j // 8, j % 8)
        return carry

    lax.fori_loop(0, n // 8, group, 0)
    lax.fori_loop(n // 8 * 8, n, single, 0)


def _rows_wait(hbm_ref, buf_ref, n, sem):
    def wait8(_, carry):
        pltpu.make_async_copy(hbm_ref.at[pl.ds(0, 8), :], buf_ref.at[0], sem).wait()
        return carry

    def wait1(_, carry):
        pltpu.make_async_copy(hbm_ref.at[pl.ds(0, 1), :], buf_ref.at[0, pl.ds(0, 1), :], sem).wait()
        return carry

    lax.fori_loop(0, n // 8, wait8, 0)
    lax.fori_loop(0, n % 8, wait1, 0)


def _moe_kernel(order_ref, be_ref, nv_ref, cs_ref, h_ref, wg_ref, wu_ref, wd_ref, y_ref,
                xbuf, ybuf, wg_sc, wu_sc, wd_sc, gsem, ssem, *, n_tokens):
    i = pl.program_id(0)
    n_steps = pl.num_programs(0)
    slot = i % 2

    def gather(blk, s):
        base = cs_ref[blk]

        def start(j, g, u):
            tok = order_ref[base + j] & (n_tokens - 1)
            pltpu.make_async_copy(h_ref.at[pl.ds(tok, 1), :], xbuf.at[s, g, pl.ds(u, 1), :],
                                  gsem.at[s]).start()
        _for_rows(nv_ref[blk], start)

    def scatter_wait(blk, s):
        _rows_wait(y_ref, ybuf.at[s], nv_ref[blk], ssem.at[s])

    @pl.when(i == 0)
    def _():
        gather(0, 0)

    @pl.when(i + 1 < n_steps)
    def _():
        gather(i + 1, 1 - slot)

    prev = be_ref[jnp.maximum(i - 1, 0)]

    @pl.when((i == 0) | (be_ref[i] != prev))
    def _():
        wg_sc[...] = wg_ref[0].astype(BF16)
        wu_sc[...] = wu_ref[0].astype(BF16)
        wd_sc[...] = wd_ref[0].astype(BF16)

    @pl.when(i >= 2)
    def _():
        scatter_wait(i - 2, slot)

    nv = nv_ref[i]
    _rows_wait(h_ref, xbuf.at[slot], nv, gsem.at[slot])

    @pl.when(nv > 0)
    def _():
        row = lax.broadcasted_iota(jnp.int32, (ROWS_MOE, D_MODEL), 0)
        xb = jnp.where(row < nv, xbuf[slot].reshape(ROWS_MOE, D_MODEL), 0.0).astype(BF16)
        g = jnp.dot(xb, wg_sc[...], preferred_element_type=F32)
        u = jnp.dot(xb, wu_sc[...], preferred_element_type=F32)
        y = jnp.dot((_silu(g) * u).astype(BF16), wd_sc[...], preferred_element_type=F32)
        ybuf[slot] = y.reshape(ROWS_MOE // 8, 8, D_MODEL)

        base = cs_ref[i]

        def start(j, g, u):
            pltpu.make_async_copy(ybuf.at[slot, g, pl.ds(u, 1), :],
                                  y_ref.at[pl.ds(order_ref[base + j], 1), :],
                                  ssem.at[slot]).start()
        _for_rows(nv, start)

    @pl.when(i == n_steps - 1)
    def _():
        scatter_wait(i - 1, 1 - slot)
        scatter_wait(i, slot)


def _moe_experts(order, block_expert, block_valid, block_start, h2, w_gate, w_up, w_down):
    t = h2.shape[0]
    assert t & (t - 1) == 0, "token count must be a power of two"
    n_blocks = block_expert.shape[0]
    wspec = lambda shp: pl.BlockSpec((1,) + shp, lambda i, od, be, nv, cs: (be[i], 0, 0))
    grid_spec = pltpu.PrefetchScalarGridSpec(
        num_scalar_prefetch=4,
        grid=(n_blocks,),
        in_specs=[pl.BlockSpec(memory_space=pl.ANY),
                  wspec((D_MODEL, EXPERT_DIM)), wspec((D_MODEL, EXPERT_DIM)),
                  wspec((EXPERT_DIM, D_MODEL))],
        out_specs=pl.BlockSpec(memory_space=pl.ANY),
        scratch_shapes=[pltpu.VMEM((2, ROWS_MOE // 8, 8, D_MODEL), F32),
                        pltpu.VMEM((2, ROWS_MOE // 8, 8, D_MODEL), F32),
                        pltpu.VMEM((D_MODEL, EXPERT_DIM), BF16),
                        pltpu.VMEM((D_MODEL, EXPERT_DIM), BF16),
                        pltpu.VMEM((EXPERT_DIM, D_MODEL), BF16),
                        pltpu.SemaphoreType.DMA((2,)),
                        pltpu.SemaphoreType.DMA((2,))],
    )
    return pl.pallas_call(
        functools.partial(_moe_kernel, n_tokens=t),
        grid_spec=grid_spec,
        out_shape=jax.ShapeDtypeStruct((TOP_K * t, D_MODEL), F32),
        compiler_params=_params("arbitrary"),
        name="moe_experts",
    )(order, block_expert, block_valid, block_start, h2, w_gate, w_up, w_down)


def _combine_kernel(base_ref, w_ref, mod_ref, g_ref, y_ref, o_ref, *, tiles_per_batch, last_layer):
    routed = y_ref[0] * w_ref[:, 0:1]
    for k in range(1, TOP_K):
        routed = routed + y_ref[k] * w_ref[:, k:k + 1]
    b = pl.program_id(0) // tiles_per_batch
    gate2 = mod_ref[pl.ds(b, 1), 5 * D_MODEL:6 * D_MODEL]
    x = base_ref[...] + gate2 * routed
    if last_layer:
        x = x * lax.rsqrt(jnp.mean(x * x, axis=-1, keepdims=True) + NORM_EPS) * g_ref[...]
    o_ref[...] = x


def _combine(base, top_w, mod, final_g, y, seq, last_layer):
    t = base.shape[0]
    tc = T_COMBINE
    kern = functools.partial(_combine_kernel, tiles_per_batch=seq // tc, last_layer=last_layer)
    return pl.pallas_call(
        kern,
        grid=(t // tc,),
        in_specs=[pl.BlockSpec((tc, D_MODEL), lambda i: (i, 0)),
                  pl.BlockSpec((tc, TOP_K), lambda i: (i, 0)),
                  pl.BlockSpec(mod.shape, lambda i: (0, 0)),
                  pl.BlockSpec((1, D_MODEL), lambda i: (0, 0)),
                  pl.BlockSpec((TOP_K, tc, D_MODEL), lambda i: (0, i, 0))],
        out_specs=pl.BlockSpec((tc, D_MODEL), lambda i: (i, 0)),
        out_shape=jax.ShapeDtypeStruct((t, D_MODEL), F32),
        compiler_params=_params("arbitrary"),
        name="combine",
    )(base, top_w, mod, final_g, y.reshape(TOP_K, t, D_MODEL))


def _block_tables(counts, n_assign):
    r = ROWS_MOE
    cnt = counts.reshape(N_EXPERTS).astype(jnp.int32)
    first = jnp.cumsum(cnt) - cnt
    nblk = (cnt + r - 1) // r
    blk_end = jnp.cumsum(nblk)
    blk_start = blk_end - nblk
    n_blocks = (n_assign + N_EXPERTS * (r - 1) + r - 1) // r
    bi = jnp.arange(n_blocks, dtype=jnp.int32)
    bc = jnp.minimum(bi, blk_end[-1] - 1)
    block_expert = jnp.minimum(jnp.sum((blk_end[None, :] <= bc[:, None]).astype(jnp.int32), axis=1),
                               N_EXPERTS - 1)
    onehot = (block_expert[:, None] == jnp.arange(N_EXPERTS, dtype=jnp.int32)[None, :])
    pick = lambda v: jnp.sum(jnp.where(onehot, v[None, :], 0), axis=1)
    within = bc - pick(blk_start)
    block_valid = jnp.where(bi < blk_end[-1], jnp.clip(pick(cnt) - within * r, 0, r), 0)
    block_start = pick(first) + within * r
    return (first.astype(F32).reshape(N_EXPERTS, 1), block_expert.astype(jnp.int32),
            block_valid.astype(jnp.int32), block_start.astype(jnp.int32))


def kernel(x, c, w_ada, b_ada, norm1_g, w_in, b_forget, ret_gn_g, w_branch_a, w_branch_b, w_out,
           norm2_g, w_router, router_bias, w_exp_gate, w_exp_up, w_exp_down, w_sh_gate, w_sh_up,
           w_sh_down, final_g):
    batch, seq, d = x.shape
    t = batch * seq
    depth = w_ada.shape[0]
    x2 = x.reshape(t, d)

    half = RET_QK_DIM // 2
    inv_freq = ROPE_BASE ** (-jnp.arange(half, dtype=F32) / half)
    ang = jnp.arange(seq, dtype=F32)[:, None] * inv_freq[None, :]
    cos32, sin32 = jnp.cos(ang), jnp.sin(ang)
    cos_t = jnp.concatenate([cos32, cos32, cos32, cos32], axis=-1)
    sin_t = jnp.concatenate([-sin32, sin32, -sin32, sin32], axis=-1)
    dec_t, zeta_t, xi_t, cd_t = _retention_tables()
    c_pad = jnp.zeros((8, d), F32).at[:batch].set(c)

    for l in range(depth):
        mod = _adaln_mod(c_pad, w_ada[l], b_ada[l])

        w = w_in[l]
        o_fq, o_fk, o_fv, o_ff = 0, FOX_WIDTH, 2 * FOX_WIDTH, 3 * FOX_WIDTH
        o_rq = o_ff + FOX_HEADS
        o_rk = o_rq + RET_QK_WIDTH
        o_rv = o_rk + RET_QK_WIDTH
        o_rg = o_rv + RET_V_WIDTH
        o_ga = o_rg + RET_V_WIDTH
        o_gb = o_ga + D_MODEL
        w_p = jnp.concatenate(
            [w[:, o_rv:o_rg], w[:, o_rg:o_ga], w[:, o_ga:o_gb], w[:, o_gb:o_gb + D_MODEL],
             w[:, o_fq:o_fk], w[:, o_fk:o_fv], w[:, o_fv:o_ff], w[:, o_rq:o_rk], w[:, o_rk:o_rv]],
            axis=1).astype(BF16)
        w_ff = jnp.zeros((d, LANES), BF16).at[:, :FOX_HEADS].set(w[:, o_ff:o_rq].astype(BF16))
        b_ff = jnp.zeros((1, LANES), F32).at[0, :FOX_HEADS].set(b_forget[l].astype(F32))

        proj, logf = _in_proj(x2, mod, norm1_g[l].reshape(1, d), w_p, w_ff, b_ff, cos_t, sin_t, seq)
        cum_f = jnp.cumsum(logf[:, :FOX_HEADS].reshape(batch, seq, FOX_HEADS), axis=1)
        cum_f = cum_f.transpose(0, 2, 1)

        attn = _fox_attn(proj, cum_f, batch, seq)
        ret = _retention(proj, ret_gn_g[l].reshape(1, RET_V_WIDTH), dec_t, zeta_t, xi_t, cd_t,
                         batch, seq)

        base, h2, scores_t = _post_mix(
            attn, ret, proj, x2, mod, norm2_g[l].reshape(1, d),
            w_branch_a[l].astype(BF16), w_branch_b[l].astype(BF16), w_out[l].astype(BF16),
            w_router[l].T, w_sh_gate[l].astype(BF16), w_sh_up[l].astype(BF16),
            w_sh_down[l].astype(BF16), seq)

        top_e, top_w, rank, counts = _route(scores_t,
                                            router_bias[l].astype(F32).reshape(N_EXPERTS, 1))
        first, block_expert, block_valid, block_start = _block_tables(counts, t * TOP_K)
        sorted_pos = _dest_rows(top_e, rank, first).reshape(TOP_K * t)
        order = jnp.argsort(sorted_pos).astype(jnp.int32)
        y = _moe_experts(order, block_expert, block_valid, block_start, h2,
                         w_exp_gate[l], w_exp_up[l], w_exp_down[l])
        x2 = _combine(base, top_w.T, mod, final_g.reshape(1, d), y, seq, l == depth - 1)
    return x2.reshape(batch, seq, d)
```

```python
import functools

import jax
import jax.numpy as jnp
import numpy as np
from jax import lax
from jax.experimental import pallas as pl
from jax.experimental.pallas import tpu as pltpu

F32 = jnp.float32
BF16 = jnp.bfloat16

D_MODEL = 1024
FOX_HEADS = 8
FOX_HEAD_DIM = 64
FOX_WIDTH = 512
RET_HEADS = 8
RET_QK_DIM = 64
RET_V_DIM = 128
RET_QK_WIDTH = 512
RET_V_WIDTH = 1024
RET_CHUNK = 128
ROPE_BASE = 10000.0
N_EXPERTS = 256
TOP_K = 8
N_GROUPS = 8
TOPK_GROUPS = 4
EXPERT_DIM = 256
ROUTED_SCALE = 2.5
NORM_EPS = 1e-6

LANES = 128
VMEM_LIMIT = 56 * 1024 * 1024

COL_RV, COL_RG, COL_GA, COL_GB = 0, 1024, 2048, 3072
COL_FQ, COL_FK, COL_FV, COL_RQ, COL_RK = 4096, 4608, 5120, 5632, 6144
PROJ_WIDTH = 6656

TM_PROJ = 512
TN_PROJ = 512
T_ATTN = 512
T_RET = 512
TM_POST = 512
T_ROUTE = 512
ROWS_MOE = 256
T_COMBINE = 256


def _sigmoid(z):
    return 1.0 / (1.0 + jnp.exp(-z))


def _silu(z):
    return z * _sigmoid(z)


def _pack_halves(xb):
    n = xb.shape[1] // 2
    lo = lax.bitcast_convert_type(xb[:, :n].astype(F32), jnp.uint32)
    hi = lax.bitcast_convert_type(xb[:, n:].astype(F32), jnp.uint32)
    return (lo >> 16) | hi


def _unpack_halves(xp):
    lo = lax.bitcast_convert_type(xp << 16, F32)
    hi = lax.bitcast_convert_type(xp & jnp.uint32(0xFFFF0000), F32)
    return lo.astype(BF16), hi.astype(BF16)


def _params(*sem):
    return pltpu.CompilerParams(dimension_semantics=sem, vmem_limit_bytes=VMEM_LIMIT)


def _adaln_kernel(c_ref, w_ref, b_ref, o_ref):
    a = _silu(c_ref[...]).astype(BF16)
    o_ref[...] = jnp.dot(a, w_ref[...].astype(BF16), preferred_element_type=F32) + b_ref[...]


def _adaln_mod(c_pad, w_ada, b_ada):
    n = w_ada.shape[1]
    tn = 1024
    return pl.pallas_call(
        _adaln_kernel,
        grid=(n // tn,),
        in_specs=[pl.BlockSpec((8, D_MODEL), lambda j: (0, 0)),
                  pl.BlockSpec((D_MODEL, tn), lambda j: (0, j)),
                  pl.BlockSpec((1, tn), lambda j: (0, j))],
        out_specs=pl.BlockSpec((8, tn), lambda j: (0, j)),
        out_shape=jax.ShapeDtypeStruct((8, n), F32),
        compiler_params=_params("arbitrary"),
        name="adaln_mod",
    )(c_pad, w_ada, b_ada.reshape(1, n))


def _inproj_kernel(x_ref, mod_ref, g_ref, w_ref, wff_ref, bf_ref, cos_ref, sin_ref,
                   proj_ref, logf_ref, h_sc, *, tiles_per_batch):
    i = pl.program_id(0)
    j = pl.program_id(1)

    @pl.when(j == 0)
    def _():
        b = i // tiles_per_batch
        x = x_ref[...]
        y = x * lax.rsqrt(jnp.mean(x * x, axis=-1, keepdims=True) + NORM_EPS) * g_ref[...]
        shift = mod_ref[pl.ds(b, 1), 0:D_MODEL]
        scale = mod_ref[pl.ds(b, 1), D_MODEL:2 * D_MODEL]
        hb = (y * (1.0 + scale) + shift).astype(BF16)
        h_sc[...] = hb
        z = jnp.dot(hb, wff_ref[...], preferred_element_type=F32) + bf_ref[...]
        logf_ref[...] = jnp.minimum(z, 0.0) - jnp.log1p(jnp.exp(-jnp.abs(z)))

    acc = jnp.dot(h_sc[...], w_ref[...], preferred_element_type=F32)

    j_ga, j_fq, j_fk, j_rq, j_rk = (COL_GA // TN_PROJ, COL_FQ // TN_PROJ, COL_FK // TN_PROJ,
                                    COL_RQ // TN_PROJ, COL_RK // TN_PROJ)

    @pl.when((j < j_ga) | ((j >= j_fk) & (j < j_rq)))
    def _():
        proj_ref[...] = acc.astype(BF16)

    @pl.when((j >= j_ga) & (j < j_fq))
    def _():
        proj_ref[...] = _sigmoid(acc).astype(BF16)

    @pl.when(j == j_fq)
    def _():
        proj_ref[...] = (acc * (FOX_HEAD_DIM ** -0.5)).astype(BF16)

    @pl.when(j >= j_rq)
    def _():
        cos = cos_ref[...]
        sin = sin_ref[...]
        lane = lax.broadcasted_iota(jnp.int32, cos.shape, 1)
        first_half = (lane % RET_QK_DIM) < (RET_QK_DIM // 2)
        k_scale = jnp.where(j == j_rk, RET_QK_DIM ** -0.5, 1.0).astype(F32)
        for cb in range(TN_PROJ // LANES):
            xs = acc[:, cb * LANES:(cb + 1) * LANES]
            up = pltpu.roll(xs, LANES - RET_QK_DIM // 2, axis=1)
            dn = pltpu.roll(xs, RET_QK_DIM // 2, axis=1)
            rot = xs * cos + jnp.where(first_half, up, dn) * sin
            proj_ref[:, cb * LANES:(cb + 1) * LANES] = (rot * k_scale).astype(BF16)


def _in_proj(x2, mod, norm_g, w_p, w_ff, b_ff, cos_t, sin_t, seq):
    t = x2.shape[0]
    tiles_per_batch = seq // TM_PROJ
    kern = functools.partial(_inproj_kernel, tiles_per_batch=tiles_per_batch)
    return pl.pallas_call(
        kern,
        grid=(t // TM_PROJ, PROJ_WIDTH // TN_PROJ),
        in_specs=[pl.BlockSpec((TM_PROJ, D_MODEL), lambda i, j: (i, 0)),
                  pl.BlockSpec(mod.shape, lambda i, j: (0, 0)),
                  pl.BlockSpec((1, D_MODEL), lambda i, j: (0, 0)),
                  pl.BlockSpec((D_MODEL, TN_PROJ), lambda i, j: (0, j)),
                  pl.BlockSpec((D_MODEL, LANES), lambda i, j: (0, 0)),
                  pl.BlockSpec((1, LANES), lambda i, j: (0, 0)),
                  pl.BlockSpec((TM_PROJ, LANES), lambda i, j: (i % tiles_per_batch, 0)),
                  pl.BlockSpec((TM_PROJ, LANES), lambda i, j: (i % tiles_per_batch, 0))],
        out_specs=[pl.BlockSpec((TM_PROJ, TN_PROJ), lambda i, j: (i, j)),
                   pl.BlockSpec((TM_PROJ, LANES), lambda i, j: (i, 0))],
        out_shape=[jax.ShapeDtypeStruct((t, PROJ_WIDTH), BF16),
                   jax.ShapeDtypeStruct((t, LANES), F32)],
        scratch_shapes=[pltpu.VMEM((TM_PROJ, D_MODEL), BF16)],
        compiler_params=_params("arbitrary", "arbitrary"),
        name="in_proj",
    )(x2, mod, norm_g, w_p, w_ff, b_ff, cos_t, sin_t)


def _attn_kernel(q_ref, k_ref, v_ref, f_ref, o_ref):
    hp = pl.program_id(1)
    qi = pl.program_id(2)
    tq = T_ATTN
    q = q_ref[...]
    lane = lax.broadcasted_iota(jnp.int32, (tq, LANES), 1)
    row = lax.broadcasted_iota(jnp.int32, (tq, tq), 0)
    col = lax.broadcasted_iota(jnp.int32, (tq, tq), 1)
    outs = []
    for hh in range(2):
        sel = (lane < FOX_HEAD_DIM) if hh == 0 else (lane >= FOX_HEAD_DIM)
        qm = jnp.where(sel, q, jnp.zeros_like(q))

        def step(kb, carry, masked, qm=qm, hh=hh):
            m, l, acc = carry
            ks = pl.multiple_of(kb * tq, tq)
            k = k_ref[pl.ds(ks, tq), :]
            v = v_ref[pl.ds(ks, tq), :]
            s = lax.dot_general(qm, k, (((1,), (1,)), ((), ())), preferred_element_type=F32)
            s = s - f_ref[0, pl.ds(2 * hp + hh, 1), pl.ds(ks, tq)]
            if masked:
                s = jnp.where(col <= row, s, -jnp.inf)
            m_new = jnp.maximum(m, jnp.max(s, axis=-1, keepdims=True))
            alpha = jnp.exp(m - m_new)
            p = jnp.exp(s - m_new)
            l = alpha * l + jnp.sum(p, axis=-1, keepdims=True)
            acc = alpha * acc + jnp.dot(p.astype(BF16), v, preferred_element_type=F32)
            return m_new, l, acc

        carry = (jnp.full((tq, 1), -jnp.inf, F32), jnp.zeros((tq, 1), F32),
                 jnp.zeros((tq, LANES), F32))
        carry = lax.fori_loop(0, qi, lambda kb, c: step(kb, c, False), carry)
        m, l, acc = step(qi, carry, True)
        outs.append(acc / l)
    o_ref[...] = jnp.where(lane < FOX_HEAD_DIM, outs[0], outs[1]).astype(BF16)


def _fox_attn(proj, cum_f, batch, seq):
    t = proj.shape[0]
    nq = seq // T_ATTN
    cq, ck, cv = COL_FQ // LANES, COL_FK // LANES, COL_FV // LANES
    return pl.pallas_call(
        _attn_kernel,
        grid=(batch, FOX_HEADS // 2, nq),
        in_specs=[pl.BlockSpec((T_ATTN, LANES), lambda b, hp, qi: (b * nq + qi, cq + hp)),
                  pl.BlockSpec((seq, LANES), lambda b, hp, qi: (b, ck + hp)),
                  pl.BlockSpec((seq, LANES), lambda b, hp, qi: (b, cv + hp)),
                  pl.BlockSpec((1, FOX_HEADS, seq), lambda b, hp, qi: (b, 0, 0))],
        out_specs=pl.BlockSpec((T_ATTN, LANES), lambda b, hp, qi: (b * nq + qi, hp)),
        out_shape=jax.ShapeDtypeStruct((t, FOX_WIDTH), BF16),
        compiler_params=_params("arbitrary", "arbitrary", "arbitrary"),
        name="fox_attn",
    )(proj, proj, proj, cum_f)


def _ret_kernel(q_ref, k_ref, v_ref, rg_ref, gn_ref, dec_ref, zeta_ref, xi_ref, cd_ref,
                o_ref, st_sc):
    h = pl.program_id(1)
    ri = pl.program_id(2)
    c = RET_CHUNK

    @pl.when(ri == 0)
    def _():
        st_sc[...] = jnp.zeros_like(st_sc)

    lane = lax.broadcasted_iota(jnp.int32, (T_RET, LANES), 1)
    lo = (h % 2) * RET_QK_DIM
    sel = (lane >= lo) & (lane < lo + RET_QK_DIM)
    qm = jnp.where(sel, q_ref[...], jnp.zeros((T_RET, LANES), BF16))
    km = jnp.where(sel, k_ref[...], jnp.zeros((T_RET, LANES), BF16))
    decay = dec_ref[0]
    zeta = zeta_ref[0]
    xi = xi_ref[0]
    state = st_sc[...]
    for n in range(T_RET // c):
        qc = qm[n * c:(n + 1) * c]
        kc = km[n * c:(n + 1) * c]
        vc = v_ref[n * c:(n + 1) * c, :]
        sc = lax.dot_general(qc, kc, (((1,), (1,)), ((), ())), preferred_element_type=F32) * decay
        intra = jnp.dot(sc.astype(BF16), vc, preferred_element_type=F32)
        cross = jnp.dot(qc, state.astype(BF16), preferred_element_type=F32) * xi
        o = intra + cross
        kz = (kc.astype(F32) * zeta).astype(BF16)
        kv = lax.dot_general(kz, vc, (((0,), (0,)), ((), ())), preferred_element_type=F32)
        state = cd_ref[0] * state + kv
        mu = jnp.mean(o, axis=-1, keepdims=True)
        d = o - mu
        var = jnp.mean(d * d, axis=-1, keepdims=True)
        on = d * lax.rsqrt(var + NORM_EPS) * gn_ref[...]
        rg = rg_ref[n * c:(n + 1) * c, :].astype(F32)
        o_ref[n * c:(n + 1) * c, :] = (on * _silu(rg)).astype(BF16)
    st_sc[...] = state


def _retention(proj, gn_g, dec_t, zeta_t, xi_t, cd_t, batch, seq):
    t = proj.shape[0]
    nr = seq // T_RET
    cq, ck, cv, cg = COL_RQ // LANES, COL_RK // LANES, COL_RV // LANES, COL_RG // LANES
    tok = lambda b, h, ri: b * nr + ri
    tab = pl.BlockSpec((1, RET_CHUNK, LANES), lambda b, h, ri: (h, 0, 0))
    return pl.pallas_call(
        _ret_kernel,
        grid=(batch, RET_HEADS, nr),
        in_specs=[pl.BlockSpec((T_RET, LANES), lambda b, h, ri: (tok(b, h, ri), cq + h // 2)),
                  pl.BlockSpec((T_RET, LANES), lambda b, h, ri: (tok(b, h, ri), ck + h // 2)),
                  pl.BlockSpec((T_RET, LANES), lambda b, h, ri: (tok(b, h, ri), cv + h)),
                  pl.BlockSpec((T_RET, LANES), lambda b, h, ri: (tok(b, h, ri), cg + h)),
                  pl.BlockSpec((1, LANES), lambda b, h, ri: (0, h)),
                  tab, tab, tab, tab],
        out_specs=pl.BlockSpec((T_RET, LANES), lambda b, h, ri: (tok(b, h, ri), h)),
        out_shape=jax.ShapeDtypeStruct((t, RET_V_WIDTH), BF16),
        scratch_shapes=[pltpu.VMEM((LANES, RET_V_DIM), F32)],
        compiler_params=_params("arbitrary", "arbitrary", "arbitrary"),
        name="retention",
    )(proj, proj, proj, proj, gn_g, dec_t, zeta_t, xi_t, cd_t)


def _retention_tables():
    c = RET_CHUNK
    log_gamma = jnp.log1p(-jnp.exp2(-5.0 - jnp.arange(RET_HEADS, dtype=F32)))
    idx = jnp.arange(c, dtype=F32)
    diff = idx[:, None] - idx[None, :]
    dec = jnp.where(diff >= 0,
                    jnp.exp(jnp.maximum(diff, 0.0)[None] * log_gamma[:, None, None]), 0.0)
    zeta = jnp.exp((c - 1.0 - idx)[None, :] * log_gamma[:, None])
    xi = jnp.exp((idx + 1.0)[None, :] * log_gamma[:, None])
    cd = jnp.exp(c * log_gamma)
    bc = lambda v: jnp.broadcast_to(v[:, :, None], (RET_HEADS, c, LANES))
    cd_t = jnp.broadcast_to(cd[:, None, None], (RET_HEADS, LANES, RET_V_DIM))
    return dec, bc(zeta), bc(xi), cd_t


def _post_kernel(attn_ref, ret_ref, ga_ref, gb_ref, x_ref, mod_ref, g2_ref,
                 wa_ref, wb_ref, wo_ref, wr_ref, wsg_ref, wsu_ref, wsd_ref,
                 base_ref, h2_ref, score_ref, *, tiles_per_batch):
    b = pl.program_id(0) // tiles_per_batch
    ya = jnp.dot(attn_ref[...], wa_ref[...], preferred_element_type=F32)
    yb = jnp.dot(ret_ref[...], wb_ref[...], preferred_element_type=F32)
    merged = ga_ref[...].astype(F32) * ya + gb_ref[...].astype(F32) * yb
    mix = jnp.dot(merged.astype(BF16), wo_ref[...], preferred_element_type=F32)
    d = D_MODEL
    gate1 = mod_ref[pl.ds(b, 1), 2 * d:3 * d]
    shift2 = mod_ref[pl.ds(b, 1), 3 * d:4 * d]
    scale2 = mod_ref[pl.ds(b, 1), 4 * d:5 * d]
    gate2 = mod_ref[pl.ds(b, 1), 5 * d:6 * d]
    x1 = x_ref[...] + gate1 * mix
    y = x1 * lax.rsqrt(jnp.mean(x1 * x1, axis=-1, keepdims=True) + NORM_EPS) * g2_ref[...]
    h2 = y * (1.0 + scale2) + shift2
    logits_t = lax.dot_general(wr_ref[...], h2, (((1,), (1,)), ((), ())),
                               preferred_element_type=F32, precision=lax.Precision.HIGHEST)
    score_ref[...] = _sigmoid(logits_t)
    h2b = h2.astype(BF16)
    h2_ref[...] = _pack_halves(h2b)
    g = jnp.dot(h2b, wsg_ref[...], preferred_element_type=F32)
    u = jnp.dot(h2b, wsu_ref[...], preferred_element_type=F32)
    shared = jnp.dot((_silu(g) * u).astype(BF16), wsd_ref[...], preferred_element_type=F32)
    base_ref[...] = x1 + gate2 * shared


def _post_mix(attn, ret, proj, x2, mod, norm2_g, wa, wb, wo, wr, wsg, wsu, wsd, seq):
    t = x2.shape[0]
    tm = TM_POST
    kern = functools.partial(_post_kernel, tiles_per_batch=seq // tm)
    full = lambda a: pl.BlockSpec(a.shape, lambda i: (0,) * a.ndim)
    return pl.pallas_call(
        kern,
        grid=(t // tm,),
        in_specs=[pl.BlockSpec((tm, FOX_WIDTH), lambda i: (i, 0)),
                  pl.BlockSpec((tm, RET_V_WIDTH), lambda i: (i, 0)),
                  pl.BlockSpec((tm, D_MODEL), lambda i: (i, COL_GA // D_MODEL)),
                  pl.BlockSpec((tm, D_MODEL), lambda i: (i, COL_GB // D_MODEL)),
                  pl.BlockSpec((tm, D_MODEL), lambda i: (i, 0)),
                  full(mod), full(norm2_g), full(wa), full(wb), full(wo), full(wr),
                  full(wsg), full(wsu), full(wsd)],
        out_specs=[pl.BlockSpec((tm, D_MODEL), lambda i: (i, 0)),
                   pl.BlockSpec((tm, D_MODEL // 2), lambda i: (i, 0)),
                   pl.BlockSpec((N_EXPERTS, tm), lambda i: (0, i))],
        out_shape=[jax.ShapeDtypeStruct((t, D_MODEL), F32),
                   jax.ShapeDtypeStruct((t, D_MODEL // 2), jnp.uint32),
                   jax.ShapeDtypeStruct((N_EXPERTS, t), F32)],
        compiler_params=_params("arbitrary"),
        name="post_mix",
    )(attn, ret, proj, proj, x2, mod, norm2_g, wa, wb, wo, wr, wsg, wsu, wsd)


def _route_kernel(s_ref, bias_ref, e_ref, w_ref, rank_ref, cnt_ref):
    tr = s_ref.shape[1]
    gsz = N_EXPERTS // N_GROUPS

    @pl.when(pl.program_id(0) == 0)
    def _():
        cnt_ref[...] = jnp.zeros_like(cnt_ref)

    s = s_ref[...]
    sel = s + bias_ref[...]
    neg = jnp.float32(-jnp.inf)

    g3 = sel.reshape(N_GROUPS, gsz, tr)
    m1 = jnp.max(g3, axis=1)
    n_max = jnp.sum((g3 == m1[:, None, :]).astype(F32), axis=1)
    m2 = jnp.max(jnp.where(g3 < m1[:, None, :], g3, neg), axis=1)
    gs = m1 + jnp.where(n_max >= 2.0, m1, m2)

    gi = lax.broadcasted_iota(jnp.int32, (N_GROUPS, tr), 0)
    beaten = jnp.zeros((N_GROUPS, tr), F32)
    for j in range(N_GROUPS):
        gj = gs[j:j + 1, :]
        beaten = beaten + jnp.where((gj > gs) | ((gj == gs) & (gi > j)), 1.0, 0.0)
    keep = jnp.where(beaten < float(TOPK_GROUPS), 1.0, 0.0)
    keep_e = jnp.broadcast_to(keep[:, None, :], (N_GROUPS, gsz, tr)).reshape(N_EXPERTS, tr)

    ei = lax.broadcasted_iota(jnp.int32, (N_EXPERTS, tr), 0)
    val = jnp.where(keep_e > 0.0, sel, neg)
    member = jnp.zeros((N_EXPERTS, tr), F32)
    idx_rows, w_rows = [], []
    for _ in range(TOP_K):
        m = jnp.max(val, axis=0, keepdims=True)
        idx = jnp.min(jnp.where(val == m, ei, N_EXPERTS), axis=0, keepdims=True)
        hit = ei == idx
        w_rows.append(jnp.sum(jnp.where(hit, s, 0.0), axis=0, keepdims=True))
        idx_rows.append(idx)
        val = jnp.where(hit, neg, val)
        member = jnp.where(hit, 1.0, member)

    w_sum = w_rows[0]
    for k in range(1, TOP_K):
        w_sum = w_sum + w_rows[k]

    r_i = lax.broadcasted_iota(jnp.int32, (tr, tr), 0)
    c_i = lax.broadcasted_iota(jnp.int32, (tr, tr), 1)
    upper = jnp.where(r_i < c_i, 1.0, 0.0).astype(BF16)
    before = jnp.dot(member.astype(BF16), upper, preferred_element_type=F32) + cnt_ref[...]

    for k in range(TOP_K):
        hit = ei == idx_rows[k]
        e_ref[k:k + 1, :] = idx_rows[k]
        w_ref[k:k + 1, :] = w_rows[k] / w_sum * ROUTED_SCALE
        rank_ref[k:k + 1, :] = jnp.sum(jnp.where(hit, before, 0.0), axis=0,
                                       keepdims=True).astype(jnp.int32)
    cnt_ref[...] = cnt_ref[...] + jnp.sum(member, axis=1, keepdims=True)


def _route(scores_t, bias_col):
    t = scores_t.shape[1]
    tr = T_ROUTE
    row8 = lambda dt: jax.ShapeDtypeStruct((TOP_K, t), dt)
    return pl.pallas_call(
        _route_kernel,
        grid=(t // tr,),
        in_specs=[pl.BlockSpec((N_EXPERTS, tr), lambda i: (0, i)),
                  pl.BlockSpec((N_EXPERTS, 1), lambda i: (0, 0))],
        out_specs=[pl.BlockSpec((TOP_K, tr), lambda i: (0, i)),
                   pl.BlockSpec((TOP_K, tr), lambda i: (0, i)),
                   pl.BlockSpec((TOP_K, tr), lambda i: (0, i)),
                   pl.BlockSpec((N_EXPERTS, 1), lambda i: (0, 0))],
        out_shape=[row8(jnp.int32), row8(F32), row8(jnp.int32),
                   jax.ShapeDtypeStruct((N_EXPERTS, 1), F32)],
        compiler_params=_params("arbitrary"),
        name="route",
    )(scores_t, bias_col)


def _dest_kernel(e_ref, rank_ref, start_ref, dest_ref):
    tr = e_ref.shape[1]
    ei = lax.broadcasted_iota(jnp.int32, (N_EXPERTS, tr), 0)
    start = start_ref[...]
    for k in range(TOP_K):
        hit = ei == e_ref[k:k + 1, :]
        off = jnp.sum(jnp.where(hit, start, 0.0), axis=0, keepdims=True)
        dest_ref[k:k + 1, :] = rank_ref[k:k + 1, :] + off.astype(jnp.int32)


def _dest_rows(top_e, rank, start_col):
    t = top_e.shape[1]
    tr = T_ROUTE
    spec = pl.BlockSpec((TOP_K, tr), lambda i: (0, i))
    return pl.pallas_call(
        _dest_kernel,
        grid=(t // tr,),
        in_specs=[spec, spec, pl.BlockSpec((N_EXPERTS, 1), lambda i: (0, 0))],
        out_specs=spec,
        out_shape=jax.ShapeDtypeStruct((TOP_K, t), jnp.int32),
        compiler_params=_params("arbitrary"),
        name="dest_rows",
    )(top_e, rank, start_col)


def _for_rows(n, fn):
    def group(g, carry):
        for u in range(8):
            fn(g * 8 + u, g, u)
        return carry

    def single(j, carry):
        fn(j, j // 8, j % 8)
        return carry

    lax.fori_loop(0, n // 8, group, 0)
    lax.fori_loop(n // 8 * 8, n, single, 0)


def _rows_wait(hbm_ref, buf_ref, n, sem):
    def wait8(_, carry):
        pltpu.make_async_copy(hbm_ref.at[pl.ds(0, 8), :], buf_ref.at[0], sem).wait()
        return carry

    def wait1(_, carry):
        pltpu.make_async_copy(hbm_ref.at[pl.ds(0, 1), :], buf_ref.at[0, pl.ds(0, 1), :], sem).wait()
        return carry

    lax.fori_loop(0, n // 8, wait8, 0)
    lax.fori_loop(0, n % 8, wait1, 0)


def _moe_kernel(order_ref, be_ref, nv_ref, cs_ref, h_ref, wg_ref, wu_ref, wd_ref, y_ref,
                hbuf, xbuf, ybuf, wg_sc, wu_sc, wd_sc, hsem, ssem, *, n_tokens):
    i = pl.program_id(0)
    n_steps = pl.num_programs(0)
    slot = i % 2
    nv = nv_ref[i]

    @pl.when(i == 0)
    def _():
        load = pltpu.make_async_copy(h_ref, hbuf, hsem)
        load.start()
        load.wait()

    def scatter_wait(blk, s):
        _rows_wait(y_ref, ybuf.at[s], nv_ref[blk], ssem.at[s])

    base = cs_ref[i]

    def copy_row(j, g, u):
        tok = order_ref[base + j] & (n_tokens - 1)
        xbuf[g, pl.ds(u, 1), :] = hbuf[pl.ds(tok, 1), :]
    _for_rows(nv, copy_row)

    prev = be_ref[jnp.maximum(i - 1, 0)]

    @pl.when((i == 0) | (be_ref[i] != prev))
    def _():
        wg_sc[...] = wg_ref[0].astype(BF16)
        wu_sc[...] = wu_ref[0].astype(BF16)
        wd_sc[...] = wd_ref[0].astype(BF16)

    @pl.when(i >= 2)
    def _():
        scatter_wait(i - 2, slot)

    @pl.when(nv > 0)
    def _():
        half = D_MODEL // 2
        row = lax.broadcasted_iota(jnp.int32, (ROWS_MOE, half), 0)
        xp = jnp.where(row < nv, xbuf[...].reshape(ROWS_MOE, half), jnp.uint32(0))
        x_lo, x_hi = _unpack_halves(xp)
        g = (jnp.dot(x_lo, wg_sc[:half, :], preferred_element_type=F32)
             + jnp.dot(x_hi, wg_sc[half:, :], preferred_element_type=F32))
        u = (jnp.dot(x_lo, wu_sc[:half, :], preferred_element_type=F32)
             + jnp.dot(x_hi, wu_sc[half:, :], preferred_element_type=F32))
        y = jnp.dot((_silu(g) * u).astype(BF16), wd_sc[...], preferred_element_type=F32)
        ybuf[slot] = y.reshape(ROWS_MOE // 8, 8, D_MODEL)

        def start(j, g, u):
            pltpu.make_async_copy(ybuf.at[slot, g, pl.ds(u, 1), :],
                                  y_ref.at[pl.ds(order_ref[base + j], 1), :],
                                  ssem.at[slot]).start()
        _for_rows(nv, start)

    @pl.when(i == n_steps - 1)
    def _():
        scatter_wait(i - 1, 1 - slot)
        scatter_wait(i, slot)


def _moe_experts(order, block_expert, block_valid, block_start, h2, w_gate, w_up, w_down):
    t = h2.shape[0]
    assert t & (t - 1) == 0, "token count must be a power of two"
    n_blocks = block_expert.shape[0]
    wspec = lambda shp: pl.BlockSpec((1,) + shp, lambda i, od, be, nv, cs: (be[i], 0, 0))
    grid_spec = pltpu.PrefetchScalarGridSpec(
        num_scalar_prefetch=4,
        grid=(n_blocks,),
        in_specs=[pl.BlockSpec(memory_space=pl.ANY),
                  wspec((D_MODEL, EXPERT_DIM)), wspec((D_MODEL, EXPERT_DIM)),
                  wspec((EXPERT_DIM, D_MODEL))],
        out_specs=pl.BlockSpec(memory_space=pl.ANY),
        scratch_shapes=[pltpu.VMEM((t, D_MODEL // 2), jnp.uint32),
                        pltpu.VMEM((ROWS_MOE // 8, 8, D_MODEL // 2), jnp.uint32),
                        pltpu.VMEM((2, ROWS_MOE // 8, 8, D_MODEL), F32),
                        pltpu.VMEM((D_MODEL, EXPERT_DIM), BF16),
                        pltpu.VMEM((D_MODEL, EXPERT_DIM), BF16),
                        pltpu.VMEM((EXPERT_DIM, D_MODEL), BF16),
                        pltpu.SemaphoreType.DMA,
                        pltpu.SemaphoreType.DMA((2,))],
    )
    return pl.pallas_call(
        functools.partial(_moe_kernel, n_tokens=t),
        grid_spec=grid_spec,
        out_shape=jax.ShapeDtypeStruct((TOP_K * t, D_MODEL), F32),
        compiler_params=_params("arbitrary"),
        name="moe_experts",
    )(order, block_expert, block_valid, block_start, h2, w_gate, w_up, w_down)


def _combine_kernel(base_ref, w_ref, mod_ref, g_ref, y_ref, o_ref, *, tiles_per_batch, last_layer):
    routed = y_ref[0] * w_ref[:, 0:1]
    for k in range(1, TOP_K):
        routed = routed + y_ref[k] * w_ref[:, k:k + 1]
    b = pl.program_id(0) // tiles_per_batch
    gate2 = mod_ref[pl.ds(b, 1), 5 * D_MODEL:6 * D_MODEL]
    x = base_ref[...] + gate2 * routed
    if last_layer:
        x = x * lax.rsqrt(jnp.mean(x * x, axis=-1, keepdims=True) + NORM_EPS) * g_ref[...]
    o_ref[...] = x


def _combine(base, top_w, mod, final_g, y, seq, last_layer):
    t = base.shape[0]
    tc = T_COMBINE
    kern = functools.partial(_combine_kernel, tiles_per_batch=seq // tc, last_layer=last_layer)
    return pl.pallas_call(
        kern,
        grid=(t // tc,),
        in_specs=[pl.BlockSpec((tc, D_MODEL), lambda i: (i, 0)),
                  pl.BlockSpec((tc, TOP_K), lambda i: (i, 0)),
                  pl.BlockSpec(mod.shape, lambda i: (0, 0)),
                  pl.BlockSpec((1, D_MODEL), lambda i: (0, 0)),
                  pl.BlockSpec((TOP_K, tc, D_MODEL), lambda i: (0, i, 0))],
        out_specs=pl.BlockSpec((tc, D_MODEL), lambda i: (i, 0)),
        out_shape=jax.ShapeDtypeStruct((t, D_MODEL), F32),
        compiler_params=_params("arbitrary"),
        name="combine",
    )(base, top_w, mod, final_g, y.reshape(TOP_K, t, D_MODEL))


def _block_tables(counts, n_assign):
    r = ROWS_MOE
    cnt = counts.reshape(N_EXPERTS).astype(jnp.int32)
    first = jnp.cumsum(cnt) - cnt
    nblk = (cnt + r - 1) // r
    blk_end = jnp.cumsum(nblk)
    blk_start = blk_end - nblk
    n_blocks = (n_assign + N_EXPERTS * (r - 1) + r - 1) // r
    bi = jnp.arange(n_blocks, dtype=jnp.int32)
    bc = jnp.minimum(bi, blk_end[-1] - 1)
    block_expert = jnp.minimum(jnp.sum((blk_end[None, :] <= bc[:, None]).astype(jnp.int32), axis=1),
                               N_EXPERTS - 1)
    onehot = (block_expert[:, None] == jnp.arange(N_EXPERTS, dtype=jnp.int32)[None, :])
    pick = lambda v: jnp.sum(jnp.where(onehot, v[None, :], 0), axis=1)
    within = bc - pick(blk_start)
    block_valid = jnp.where(bi < blk_end[-1], jnp.clip(pick(cnt) - within * r, 0, r), 0)
    block_start = pick(first) + within * r
    return (first.astype(F32).reshape(N_EXPERTS, 1), block_expert.astype(jnp.int32),
            block_valid.astype(jnp.int32), block_start.astype(jnp.int32))


def kernel(x, c, w_ada, b_ada, norm1_g, w_in, b_forget, ret_gn_g, w_branch_a, w_branch_b, w_out,
           norm2_g, w_router, router_bias, w_exp_gate, w_exp_up, w_exp_down, w_sh_gate, w_sh_up,
           w_sh_down, final_g):
    batch, seq, d = x.shape
    t = batch * seq
    depth = w_ada.shape[0]
    x2 = x.reshape(t, d)

    half = RET_QK_DIM // 2
    inv_freq = ROPE_BASE ** (-jnp.arange(half, dtype=F32) / half)
    ang = jnp.arange(seq, dtype=F32)[:, None] * inv_freq[None, :]
    cos32, sin32 = jnp.cos(ang), jnp.sin(ang)
    cos_t = jnp.concatenate([cos32, cos32, cos32, cos32], axis=-1)
    sin_t = jnp.concatenate([-sin32, sin32, -sin32, sin32], axis=-1)
    dec_t, zeta_t, xi_t, cd_t = _retention_tables()
    c_pad = jnp.zeros((8, d), F32).at[:batch].set(c)

    for l in range(depth):
        mod = _adaln_mod(c_pad, w_ada[l], b_ada[l])

        w = w_in[l]
        o_fq, o_fk, o_fv, o_ff = 0, FOX_WIDTH, 2 * FOX_WIDTH, 3 * FOX_WIDTH
        o_rq = o_ff + FOX_HEADS
        o_rk = o_rq + RET_QK_WIDTH
        o_rv = o_rk + RET_QK_WIDTH
        o_rg = o_rv + RET_V_WIDTH
        o_ga = o_rg + RET_V_WIDTH
        o_gb = o_ga + D_MODEL
        w_p = jnp.concatenate(
            [w[:, o_rv:o_rg], w[:, o_rg:o_ga], w[:, o_ga:o_gb], w[:, o_gb:o_gb + D_MODEL],
             w[:, o_fq:o_fk], w[:, o_fk:o_fv], w[:, o_fv:o_ff], w[:, o_rq:o_rk], w[:, o_rk:o_rv]],
            axis=1).astype(BF16)
        w_ff = jnp.zeros((d, LANES), BF16).at[:, :FOX_HEADS].set(w[:, o_ff:o_rq].astype(BF16))
        b_ff = jnp.zeros((1, LANES), F32).at[0, :FOX_HEADS].set(b_forget[l].astype(F32))

        proj, logf = _in_proj(x2, mod, norm1_g[l].reshape(1, d), w_p, w_ff, b_ff, cos_t, sin_t, seq)
        cum_f = jnp.cumsum(logf[:, :FOX_HEADS].reshape(batch, seq, FOX_HEADS), axis=1)
        cum_f = cum_f.transpose(0, 2, 1)

        attn = _fox_attn(proj, cum_f, batch, seq)
        ret = _retention(proj, ret_gn_g[l].reshape(1, RET_V_WIDTH), dec_t, zeta_t, xi_t, cd_t,
                         batch, seq)

        base, h2, scores_t = _post_mix(
            attn, ret, proj, x2, mod, norm2_g[l].reshape(1, d),
            w_branch_a[l].astype(BF16), w_branch_b[l].astype(BF16), w_out[l].astype(BF16),
            w_router[l].T, w_sh_gate[l].astype(BF16), w_sh_up[l].astype(BF16),
            w_sh_down[l].astype(BF16), seq)

        top_e, top_w, rank, counts = _route(scores_t,
                                            router_bias[l].astype(F32).reshape(N_EXPERTS, 1))
        first, block_expert, block_valid, block_start = _block_tables(counts, t * TOP_K)
        sorted_pos = _dest_rows(top_e, rank, first).reshape(TOP_K * t)
        order = jnp.argsort(sorted_pos).astype(jnp.int32)
        y = _moe_experts(order, block_expert, block_valid, block_start, h2,
                         w_exp_gate[l], w_exp_up[l], w_exp_down[l])
        x2 = _combine(base, top_w.T, mod, final_g.reshape(1, d), y, seq, l == depth - 1)
    return x2.reshape(batch, seq, d)
```

```python
import functools

import jax
import jax.numpy as jnp
import numpy as np
from jax import lax
from jax.experimental import pallas as pl
from jax.experimental.pallas import tpu as pltpu

F32 = jnp.float32
BF16 = jnp.bfloat16

D_MODEL = 1024
FOX_HEADS = 8
FOX_HEAD_DIM = 64
FOX_WIDTH = 512
RET_HEADS = 8
RET_QK_DIM = 64
RET_V_DIM = 128
RET_QK_WIDTH = 512
RET_V_WIDTH = 1024
RET_CHUNK = 128
ROPE_BASE = 10000.0
N_EXPERTS = 256
TOP_K = 8
N_GROUPS = 8
TOPK_GROUPS = 4
EXPERT_DIM = 256
ROUTED_SCALE = 2.5
NORM_EPS = 1e-6
LOG2_E = 1.4426950408889634

LANES = 128
PACKED_ROWS = D_MODEL // 2 // LANES
VMEM_LIMIT = 56 * 1024 * 1024

COL_RV, COL_RG, COL_GA, COL_GB = 0, 1024, 2048, 3072
COL_FQ, COL_FK, COL_FV, COL_RQ, COL_RK = 4096, 4608, 5120, 5632, 6144
PROJ_WIDTH = 6656

TM_PROJ = 512
TN_PROJ = 512
TQ_ATTN = 512
TK_ATTN = 512
T_RET = 512
TM_POST = 512
T_ROUTE = 512
ROWS_MOE = 256
T_COMBINE = 256


def _sigmoid(z):
    return 1.0 / (1.0 + jnp.exp(-z))


def _silu(z):
    return z * _sigmoid(z)


def _pack_halves(xb):
    n = xb.shape[1] // 2
    lo = lax.bitcast_convert_type(xb[:, :n].astype(F32), jnp.uint32)
    hi = lax.bitcast_convert_type(xb[:, n:].astype(F32), jnp.uint32)
    return (lo >> 16) | hi


def _unpack_halves(xp):
    lo = lax.bitcast_convert_type(xp << 16, F32)
    hi = lax.bitcast_convert_type(xp & jnp.uint32(0xFFFF0000), F32)
    return lo.astype(BF16), hi.astype(BF16)


def _params(*sem):
    return pltpu.CompilerParams(dimension_semantics=sem, vmem_limit_bytes=VMEM_LIMIT)


def _adaln_kernel(c_ref, w_ref, b_ref, o_ref):
    a = _silu(c_ref[...]).astype(BF16)
    o_ref[...] = jnp.dot(a, w_ref[...].astype(BF16), preferred_element_type=F32) + b_ref[...]


def _adaln_mod(c_pad, w_ada, b_ada):
    n = w_ada.shape[1]
    tn = 1024
    return pl.pallas_call(
        _adaln_kernel,
        grid=(n // tn,),
        in_specs=[pl.BlockSpec((8, D_MODEL), lambda j: (0, 0)),
                  pl.BlockSpec((D_MODEL, tn), lambda j: (0, j)),
                  pl.BlockSpec((1, tn), lambda j: (0, j))],
        out_specs=pl.BlockSpec((8, tn), lambda j: (0, j)),
        out_shape=jax.ShapeDtypeStruct((8, n), F32),
        compiler_params=_params("arbitrary"),
        name="adaln_mod",
    )(c_pad, w_ada, b_ada.reshape(1, n))


def _inproj_kernel(x_ref, mod_ref, g_ref, w_ref, wff_ref, bf_ref, cos_ref, sin_ref,
                   proj_ref, logf_ref, *, tiles_per_batch):
    b = pl.program_id(0) // tiles_per_batch
    x = x_ref[...]
    y = x * lax.rsqrt(jnp.mean(x * x, axis=-1, keepdims=True) + NORM_EPS) * g_ref[...]
    shift = mod_ref[pl.ds(b, 1), 0:D_MODEL]
    scale = mod_ref[pl.ds(b, 1), D_MODEL:2 * D_MODEL]
    hb = (y * (1.0 + scale) + shift).astype(BF16)
    z = jnp.dot(hb, wff_ref[...], preferred_element_type=F32) + bf_ref[...]
    logf_ref[...] = jnp.minimum(z, 0.0) - jnp.log1p(jnp.exp(-jnp.abs(z)))

    cos = cos_ref[...]
    sin = sin_ref[...]
    lane = lax.broadcasted_iota(jnp.int32, cos.shape, 1)
    first_half = (lane % RET_QK_DIM) < (RET_QK_DIM // 2)

    for c0 in range(0, PROJ_WIDTH, TN_PROJ):
        cols = slice(c0, c0 + TN_PROJ)
        acc = jnp.dot(hb, w_ref[:, cols], preferred_element_type=F32)
        if COL_GA <= c0 < COL_FQ:
            proj_ref[:, cols] = _sigmoid(acc).astype(BF16)
        elif COL_FQ <= c0 < COL_FK:
            proj_ref[:, cols] = (acc * (LOG2_E * FOX_HEAD_DIM ** -0.5)).astype(BF16)
        elif c0 >= COL_RQ:
            k_scale = RET_QK_DIM ** -0.5 if c0 >= COL_RK else 1.0
            for cb in range(c0, c0 + TN_PROJ, LANES):
                xs = acc[:, cb - c0:cb - c0 + LANES]
                up = pltpu.roll(xs, LANES - RET_QK_DIM // 2, axis=1)
                dn = pltpu.roll(xs, RET_QK_DIM // 2, axis=1)
                rot = xs * cos + jnp.where(first_half, up, dn) * sin
                proj_ref[:, cb:cb + LANES] = (rot * k_scale).astype(BF16)
        else:
            proj_ref[:, cols] = acc.astype(BF16)


def _in_proj(x2, mod, norm_g, w_p, w_ff, b_ff, cos_t, sin_t, seq):
    t = x2.shape[0]
    tiles_per_batch = seq // TM_PROJ
    kern = functools.partial(_inproj_kernel, tiles_per_batch=tiles_per_batch)
    return pl.pallas_call(
        kern,
        grid=(t // TM_PROJ,),
        in_specs=[pl.BlockSpec((TM_PROJ, D_MODEL), lambda i: (i, 0)),
                  pl.BlockSpec(mod.shape, lambda i: (0, 0)),
                  pl.BlockSpec((1, D_MODEL), lambda i: (0, 0)),
                  pl.BlockSpec((D_MODEL, PROJ_WIDTH), lambda i: (0, 0)),
                  pl.BlockSpec((D_MODEL, LANES), lambda i: (0, 0)),
                  pl.BlockSpec((1, LANES), lambda i: (0, 0)),
                  pl.BlockSpec((TM_PROJ, LANES), lambda i: (i % tiles_per_batch, 0)),
                  pl.BlockSpec((TM_PROJ, LANES), lambda i: (i % tiles_per_batch, 0))],
        out_specs=[pl.BlockSpec((TM_PROJ, PROJ_WIDTH), lambda i: (i, 0)),
                   pl.BlockSpec((TM_PROJ, LANES), lambda i: (i, 0))],
        out_shape=[jax.ShapeDtypeStruct((t, PROJ_WIDTH), BF16),
                   jax.ShapeDtypeStruct((t, LANES), F32)],
        compiler_params=_params("arbitrary"),
        name="in_proj",
    )(x2, mod, norm_g, w_p, w_ff, b_ff, cos_t, sin_t)


def _attn_kernel(q_ref, k_ref, v_ref, f_ref, o_ref):
    hp = pl.program_id(1)
    qi = pl.program_id(2)
    tq, tk = TQ_ATTN, TK_ATTN
    q = q_ref[...]
    lane = lax.broadcasted_iota(jnp.int32, (tq, LANES), 1)
    row = qi * tq + lax.broadcasted_iota(jnp.int32, (tq, tk), 0)
    col = lax.broadcasted_iota(jnp.int32, (tq, tk), 1)
    zeros = jnp.zeros_like(q)
    qm = (jnp.where(lane < FOX_HEAD_DIM, q, zeros), jnp.where(lane >= FOX_HEAD_DIM, q, zeros))

    def step(kb, carry, masked):
        ks = pl.multiple_of(kb * tk, tk)
        k = k_ref[pl.ds(ks, tk), :]
        v = v_ref[pl.ds(ks, tk), :]
        out = []
        for hh in range(2):
            m, l, acc = carry[hh]
            s = lax.dot_general(qm[hh], k, (((1,), (1,)), ((), ())), preferred_element_type=F32)
            s = s - f_ref[0, pl.ds(2 * hp + hh, 1), pl.ds(ks, tk)]
            if masked:
                s = jnp.where(col + ks <= row, s, -jnp.inf)
            m_new = jnp.maximum(m, jnp.max(s, axis=-1, keepdims=True))
            alpha = jnp.exp2(m - m_new)
            p = jnp.exp2(s - m_new)
            l = alpha * l + jnp.sum(p, axis=-1, keepdims=True)
            acc = alpha * acc + jnp.dot(p.astype(BF16), v, preferred_element_type=F32)
            out.append((m_new, l, acc))
        return tuple(out)

    init = (jnp.full((tq, 1), -jnp.inf, F32), jnp.zeros((tq, 1), F32),
            jnp.zeros((tq, LANES), F32))
    n_full = (qi * tq) // tk
    carry = lax.fori_loop(0, n_full, lambda kb, c: step(kb, c, False), (init, init))
    for d in range(max(tq // tk, 1)):
        carry = step(n_full + d, carry, True)
    (_, l0, acc0), (_, l1, acc1) = carry
    o_ref[...] = jnp.where(lane < FOX_HEAD_DIM, acc0 / l0, acc1 / l1).astype(BF16)


def _fox_attn(proj, cum_f, batch, seq):
    t = proj.shape[0]
    nq = seq // TQ_ATTN
    cq, ck, cv = COL_FQ // LANES, COL_FK // LANES, COL_FV // LANES
    return pl.pallas_call(
        _attn_kernel,
        grid=(batch, FOX_HEADS // 2, nq),
        in_specs=[pl.BlockSpec((TQ_ATTN, LANES), lambda b, hp, qi: (b * nq + qi, cq + hp)),
                  pl.BlockSpec((seq, LANES), lambda b, hp, qi: (b, ck + hp)),
                  pl.BlockSpec((seq, LANES), lambda b, hp, qi: (b, cv + hp)),
                  pl.BlockSpec((1, FOX_HEADS, seq), lambda b, hp, qi: (b, 0, 0))],
        out_specs=pl.BlockSpec((TQ_ATTN, LANES), lambda b, hp, qi: (b * nq + qi, hp)),
        out_shape=jax.ShapeDtypeStruct((t, FOX_WIDTH), BF16),
        compiler_params=_params("arbitrary", "arbitrary", "arbitrary"),
        name="fox_attn",
    )(proj, proj, proj, cum_f)


def _ret_kernel(q_ref, k_ref, v_ref, rg_ref, gn_ref, dec_ref, zeta_ref, xi_ref, cd_ref,
                o_ref, st_sc):
    ri = pl.program_id(1)
    c = RET_CHUNK

    @pl.when(ri == 0)
    def _():
        st_sc[...] = jnp.zeros_like(st_sc)

    lane = lax.broadcasted_iota(jnp.int32, (c, LANES), 1)
    zeros = jnp.zeros((c, LANES), BF16)
    states = [st_sc[h] for h in range(RET_HEADS)]
    for n in range(T_RET // c):
        rows = slice(n * c, (n + 1) * c)
        for h in range(RET_HEADS):
            pair = slice((h // 2) * LANES, (h // 2 + 1) * LANES)
            vcol = slice(h * RET_V_DIM, (h + 1) * RET_V_DIM)
            sel = (lane < RET_QK_DIM) if h % 2 == 0 else (lane >= RET_QK_DIM)
            qc = jnp.where(sel, q_ref[rows, pair], zeros)
            kc = jnp.where(sel, k_ref[rows, pair], zeros)
            vc = v_ref[rows, vcol]
            sc = lax.dot_general(qc, kc, (((1,), (1,)), ((), ())),
                                 preferred_element_type=F32) * dec_ref[h]
            intra = jnp.dot(sc.astype(BF16), vc, preferred_element_type=F32)
            cross = jnp.dot(qc, states[h].astype(BF16), preferred_element_type=F32) * xi_ref[h]
            o = intra + cross
            kz = (kc.astype(F32) * zeta_ref[h]).astype(BF16)
            kv = lax.dot_general(kz, vc, (((0,), (0,)), ((), ())), preferred_element_type=F32)
            states[h] = cd_ref[h] * states[h] + kv
            mu = jnp.mean(o, axis=-1, keepdims=True)
            d = o - mu
            var = jnp.mean(d * d, axis=-1, keepdims=True)
            on = d * lax.rsqrt(var + NORM_EPS) * gn_ref[:, vcol]
            rg = rg_ref[rows, vcol].astype(F32)
            o_ref[rows, vcol] = (on * _silu(rg)).astype(BF16)
    for h in range(RET_HEADS):
        st_sc[h] = states[h]


def _retention(proj, gn_g, dec_t, zeta_t, xi_t, cd_t, batch, seq):
    t = proj.shape[0]
    nr = seq // T_RET
    tok = lambda b, ri: b * nr + ri
    tab = pl.BlockSpec((RET_HEADS, RET_CHUNK, LANES), lambda b, ri: (0, 0, 0))
    return pl.pallas_call(
        _ret_kernel,
        grid=(batch, nr),
        in_specs=[pl.BlockSpec((T_RET, RET_QK_WIDTH), lambda b, ri: (tok(b, ri), COL_RQ // RET_QK_WIDTH)),
                  pl.BlockSpec((T_RET, RET_QK_WIDTH), lambda b, ri: (tok(b, ri), COL_RK // RET_QK_WIDTH)),
                  pl.BlockSpec((T_RET, RET_V_WIDTH), lambda b, ri: (tok(b, ri), COL_RV // RET_V_WIDTH)),
                  pl.BlockSpec((T_RET, RET_V_WIDTH), lambda b, ri: (tok(b, ri), COL_RG // RET_V_WIDTH)),
                  pl.BlockSpec((1, RET_V_WIDTH), lambda b, ri: (0, 0)),
                  tab, tab, tab, tab],
        out_specs=pl.BlockSpec((T_RET, RET_V_WIDTH), lambda b, ri: (tok(b, ri), 0)),
        out_shape=jax.ShapeDtypeStruct((t, RET_V_WIDTH), BF16),
        scratch_shapes=[pltpu.VMEM((RET_HEADS, LANES, RET_V_DIM), F32)],
        compiler_params=_params("arbitrary", "arbitrary"),
        name="retention",
    )(proj, proj, proj, proj, gn_g, dec_t, zeta_t, xi_t, cd_t)


def _retention_tables():
    c = RET_CHUNK
    log_gamma = jnp.log1p(-jnp.exp2(-5.0 - jnp.arange(RET_HEADS, dtype=F32)))
    idx = jnp.arange(c, dtype=F32)
    diff = idx[:, None] - idx[None, :]
    dec = jnp.where(diff >= 0,
                    jnp.exp(jnp.maximum(diff, 0.0)[None] * log_gamma[:, None, None]), 0.0)
    zeta = jnp.exp((c - 1.0 - idx)[None, :] * log_gamma[:, None])
    xi = jnp.exp((idx + 1.0)[None, :] * log_gamma[:, None])
    cd = jnp.exp(c * log_gamma)
    bc = lambda v: jnp.broadcast_to(v[:, :, None], (RET_HEADS, c, LANES))
    cd_t = jnp.broadcast_to(cd[:, None, None], (RET_HEADS, LANES, RET_V_DIM))
    return dec, bc(zeta), bc(xi), cd_t


def _post_kernel(attn_ref, ret_ref, ga_ref, gb_ref, x_ref, mod_ref, g2_ref,
                 wa_ref, wb_ref, wo_ref, wr_ref, wsg_ref, wsu_ref, wsd_ref,
                 base_ref, h2_ref, score_ref, *, tiles_per_batch):
    b = pl.program_id(0) // tiles_per_batch
    ya = jnp.dot(attn_ref[...], wa_ref[...], preferred_element_type=F32)
    yb = jnp.dot(ret_ref[...], wb_ref[...], preferred_element_type=F32)
    merged = ga_ref[...].astype(F32) * ya + gb_ref[...].astype(F32) * yb
    mix = jnp.dot(merged.astype(BF16), wo_ref[...], preferred_element_type=F32)
    d = D_MODEL
    gate1 = mod_ref[pl.ds(b, 1), 2 * d:3 * d]
    shift2 = mod_ref[pl.ds(b, 1), 3 * d:4 * d]
    scale2 = mod_ref[pl.ds(b, 1), 4 * d:5 * d]
    gate2 = mod_ref[pl.ds(b, 1), 5 * d:6 * d]
    x1 = x_ref[...] + gate1 * mix
    y = x1 * lax.rsqrt(jnp.mean(x1 * x1, axis=-1, keepdims=True) + NORM_EPS) * g2_ref[...]
    h2 = y * (1.0 + scale2) + shift2
    logits_t = lax.dot_general(wr_ref[...], h2, (((1,), (1,)), ((), ())),
                               preferred_element_type=F32, precision=lax.Precision.HIGHEST)
    score_ref[...] = _sigmoid(logits_t)
    h2b = h2.astype(BF16)
    h2p = _pack_halves(h2b)
    n_chunks = h2p.shape[1] // LANES
    for cb in range(n_chunks):
        h2_ref[pl.ds(cb, h2p.shape[0], stride=n_chunks), :] = h2p[:, cb * LANES:(cb + 1) * LANES]
    g = jnp.dot(h2b, wsg_ref[...], preferred_element_type=F32)
    u = jnp.dot(h2b, wsu_ref[...], preferred_element_type=F32)
    shared = jnp.dot((_silu(g) * u).astype(BF16), wsd_ref[...], preferred_element_type=F32)
    base_ref[...] = x1 + gate2 * shared


def _post_mix(attn, ret, proj, x2, mod, norm2_g, wa, wb, wo, wr, wsg, wsu, wsd, seq):
    t = x2.shape[0]
    tm = TM_POST
    kern = functools.partial(_post_kernel, tiles_per_batch=seq // tm)
    full = lambda a: pl.BlockSpec(a.shape, lambda i: (0,) * a.ndim)
    return pl.pallas_call(
        kern,
        grid=(t // tm,),
        in_specs=[pl.BlockSpec((tm, FOX_WIDTH), lambda i: (i, 0)),
                  pl.BlockSpec((tm, RET_V_WIDTH), lambda i: (i, 0)),
                  pl.BlockSpec((tm, D_MODEL), lambda i: (i, COL_GA // D_MODEL)),
                  pl.BlockSpec((tm, D_MODEL), lambda i: (i, COL_GB // D_MODEL)),
                  pl.BlockSpec((tm, D_MODEL), lambda i: (i, 0)),
                  full(mod), full(norm2_g), full(wa), full(wb), full(wo), full(wr),
                  full(wsg), full(wsu), full(wsd)],
        out_specs=[pl.BlockSpec((tm, D_MODEL), lambda i: (i, 0)),
                   pl.BlockSpec((tm * PACKED_ROWS, LANES), lambda i: (i, 0)),
                   pl.BlockSpec((N_EXPERTS, tm), lambda i: (0, i))],
        out_shape=[jax.ShapeDtypeStruct((t, D_MODEL), F32),
                   jax.ShapeDtypeStruct((t * PACKED_ROWS, LANES), jnp.uint32),
                   jax.ShapeDtypeStruct((N_EXPERTS, t), F32)],
        compiler_params=_params("arbitrary"),
        name="post_mix",
    )(attn, ret, proj, proj, x2, mod, norm2_g, wa, wb, wo, wr, wsg, wsu, wsd)


def _route_kernel(s_ref, bias_ref, e_ref, w_ref, rank_ref, cnt_ref):
    tr = s_ref.shape[1]
    gsz = N_EXPERTS // N_GROUPS

    @pl.when(pl.program_id(0) == 0)
    def _():
        cnt_ref[...] = jnp.zeros_like(cnt_ref)

    s = s_ref[...]
    sel = s + bias_ref[...]
    neg = jnp.float32(-jnp.inf)

    g3 = sel.reshape(N_GROUPS, gsz, tr)
    m1 = jnp.max(g3, axis=1)
    n_max = jnp.sum((g3 == m1[:, None, :]).astype(F32), axis=1)
    m2 = jnp.max(jnp.where(g3 < m1[:, None, :], g3, neg), axis=1)
    gs = m1 + jnp.where(n_max >= 2.0, m1, m2)

    gi = lax.broadcasted_iota(jnp.int32, (N_GROUPS, tr), 0)
    beaten = jnp.zeros((N_GROUPS, tr), F32)
    for j in range(N_GROUPS):
        gj = gs[j:j + 1, :]
        beaten = beaten + jnp.where((gj > gs) | ((gj == gs) & (gi > j)), 1.0, 0.0)
    keep = jnp.where(beaten < float(TOPK_GROUPS), 1.0, 0.0)
    keep_e = jnp.broadcast_to(keep[:, None, :], (N_GROUPS, gsz, tr)).reshape(N_EXPERTS, tr)

    ei = lax.broadcasted_iota(jnp.int32, (N_EXPERTS, tr), 0)
    val = jnp.where(keep_e > 0.0, sel, neg)
    member = jnp.zeros((N_EXPERTS, tr), F32)
    idx_rows, w_rows = [], []
    for _ in range(TOP_K):
        m = jnp.max(val, axis=0, keepdims=True)
        idx = jnp.min(jnp.where(val == m, ei, N_EXPERTS), axis=0, keepdims=True)
        hit = ei == idx
        w_rows.append(jnp.sum(jnp.where(hit, s, 0.0), axis=0, keepdims=True))
        idx_rows.append(idx)
        val = jnp.where(hit, neg, val)
        member = jnp.where(hit, 1.0, member)

    w_sum = w_rows[0]
    for k in range(1, TOP_K):
        w_sum = w_sum + w_rows[k]

    r_i = lax.broadcasted_iota(jnp.int32, (tr, tr), 0)
    c_i = lax.broadcasted_iota(jnp.int32, (tr, tr), 1)
    upper = jnp.where(r_i < c_i, 1.0, 0.0).astype(BF16)
    before = jnp.dot(member.astype(BF16), upper, preferred_element_type=F32) + cnt_ref[...]

    for k in range(TOP_K):
        hit = ei == idx_rows[k]
        e_ref[k:k + 1, :] = idx_rows[k]
        w_ref[k:k + 1, :] = w_rows[k] / w_sum * ROUTED_SCALE
        rank_ref[k:k + 1, :] = jnp.sum(jnp.where(hit, before, 0.0), axis=0,
                                       keepdims=True).astype(jnp.int32)
    cnt_ref[...] = cnt_ref[...] + jnp.sum(member, axis=1, keepdims=True)


def _route(scores_t, bias_col):
    t = scores_t.shape[1]
    tr = T_ROUTE
    row8 = lambda dt: jax.ShapeDtypeStruct((TOP_K, t), dt)
    return pl.pallas_call(
        _route_kernel,
        grid=(t // tr,),
        in_specs=[pl.BlockSpec((N_EXPERTS, tr), lambda i: (0, i)),
                  pl.BlockSpec((N_EXPERTS, 1), lambda i: (0, 0))],
        out_specs=[pl.BlockSpec((TOP_K, tr), lambda i: (0, i)),
                   pl.BlockSpec((TOP_K, tr), lambda i: (0, i)),
                   pl.BlockSpec((TOP_K, tr), lambda i: (0, i)),
                   pl.BlockSpec((N_EXPERTS, 1), lambda i: (0, 0))],
        out_shape=[row8(jnp.int32), row8(F32), row8(jnp.int32),
                   jax.ShapeDtypeStruct((N_EXPERTS, 1), F32)],
        compiler_params=_params("arbitrary"),
        name="route",
    )(scores_t, bias_col)


def _dest_kernel(e_ref, rank_ref, start_ref, dest_ref):
    tr = e_ref.shape[1]
    ei = lax.broadcasted_iota(jnp.int32, (N_EXPERTS, tr), 0)
    start = start_ref[...]
    for k in range(TOP_K):
        hit = ei == e_ref[k:k + 1, :]
        off = jnp.sum(jnp.where(hit, start, 0.0), axis=0, keepdims=True)
        dest_ref[k:k + 1, :] = rank_ref[k:k + 1, :] + off.astype(jnp.int32)


def _dest_rows(top_e, rank, start_col):
    t = top_e.shape[1]
    tr = T_ROUTE
    spec = pl.BlockSpec((TOP_K, tr), lambda i: (0, i))
    return pl.pallas_call(
        _dest_kernel,
        grid=(t // tr,),
        in_specs=[spec, spec, pl.BlockSpec((N_EXPERTS, 1), lambda i: (0, 0))],
        out_specs=spec,
        out_shape=jax.ShapeDtypeStruct((TOP_K, t), jnp.int32),
        compiler_params=_params("arbitrary"),
        name="dest_rows",
    )(top_e, rank, start_col)


def _for_rows(n, fn):
    def group(g, carry):
        for u in range(8):
            fn(g * 8 + u, g, u)
        return carry

    def single(j, carry):
        fn(j, j // 8, j % 8)
        return carry

    lax.fori_loop(0, n // 8, group, 0)
    lax.fori_loop(n // 8 * 8, n, single, 0)


def _rows_wait(hbm_ref, buf_ref, n, sem):
    def wait8(_, carry):
        pltpu.make_async_copy(hbm_ref.at[pl.ds(0, 8)], hbm_ref.at[pl.ds(8, 8)], sem).wait()
        return carry

    def wait1(_, carry):
        pltpu.make_async_copy(hbm_ref.at[0], buf_ref.at[0, pl.ds(0, 1), :], sem).wait()
        return carry

    lax.fori_loop(0, n // 8, wait8, 0)
    lax.fori_loop(0, n % 8, wait1, 0)


def _moe_kernel(order_ref, be_ref, nv_ref, cs_ref, h_ref, wg_ref, wu_ref, wd_ref, y_ref,
                hbuf, xbuf, ybuf, wg_sc, wu_sc, wd_sc, hsem, ssem, *, n_tokens):
    i = pl.program_id(0)
    n_steps = pl.num_programs(0)
    slot = i % 2
    nv = nv_ref[i]

    @pl.when(i == 0)
    def _():
        load = pltpu.make_async_copy(h_ref, hbuf, hsem)
        load.start()
        load.wait()

    def scatter_wait(blk, s):
        _rows_wait(y_ref, ybuf.at[s], nv_ref[blk], ssem.at[s])

    base = cs_ref[i]

    def copy_row(j, g, u):
        tok = order_ref[base + j] & (n_tokens - 1)
        xbuf[g, pl.ds(u, 1), :] = hbuf[tok]
    _for_rows(nv, copy_row)

    prev = be_ref[jnp.maximum(i - 1, 0)]

    @pl.when((i == 0) | (be_ref[i] != prev))
    def _():
        wg_sc[...] = wg_ref[0].astype(BF16)
        wu_sc[...] = wu_ref[0].astype(BF16)
        wd_sc[...] = wd_ref[0].astype(BF16)

    @pl.when(i >= 2)
    def _():
        scatter_wait(i - 2, slot)

    @pl.when(nv > 0)
    def _():
        half = D_MODEL // 2
        row = lax.broadcasted_iota(jnp.int32, (ROWS_MOE, half), 0)
        xp = jnp.where(row < nv, xbuf[...].reshape(ROWS_MOE, half), jnp.uint32(0))
        x_lo, x_hi = _unpack_halves(xp)
        g = (jnp.dot(x_lo, wg_sc[:half, :], preferred_element_type=F32)
             + jnp.dot(x_hi, wg_sc[half:, :], preferred_element_type=F32))
        u = (jnp.dot(x_lo, wu_sc[:half, :], preferred_element_type=F32)
             + jnp.dot(x_hi, wu_sc[half:, :], preferred_element_type=F32))
        y = jnp.dot((_silu(g) * u).astype(BF16), wd_sc[...], preferred_element_type=F32)
        ybuf[slot] = _pack_halves(y.astype(BF16)).reshape(ROWS_MOE // 8, 8, half)

        def start(j, g, u):
            pltpu.make_async_copy(ybuf.at[slot, g, pl.ds(u, 1), :],
                                  y_ref.at[order_ref[base + j]],
                                  ssem.at[slot]).start()
        _for_rows(nv, start)

    @pl.when(i == n_steps - 1)
    def _():
        scatter_wait(i - 1, 1 - slot)
        scatter_wait(i, slot)


def _moe_experts(order, block_expert, block_valid, block_start, h2, w_gate, w_up, w_down):
    t = h2.shape[0]
    assert t & (t - 1) == 0, "token count must be a power of two"
    n_blocks = block_expert.shape[0]
    wspec = lambda shp: pl.BlockSpec((1,) + shp, lambda i, od, be, nv, cs: (be[i], 0, 0))
    grid_spec = pltpu.PrefetchScalarGridSpec(
        num_scalar_prefetch=4,
        grid=(n_blocks,),
        in_specs=[pl.BlockSpec(memory_space=pl.ANY),
                  wspec((D_MODEL, EXPERT_DIM)), wspec((D_MODEL, EXPERT_DIM)),
                  wspec((EXPERT_DIM, D_MODEL))],
        out_specs=pl.BlockSpec(memory_space=pl.ANY),
        scratch_shapes=[pltpu.VMEM((t, 1, D_MODEL // 2), jnp.uint32),
                        pltpu.VMEM((ROWS_MOE // 8, 8, D_MODEL // 2), jnp.uint32),
                        pltpu.VMEM((2, ROWS_MOE // 8, 8, D_MODEL // 2), jnp.uint32),
                        pltpu.VMEM((D_MODEL, EXPERT_DIM), BF16),
                        pltpu.VMEM((D_MODEL, EXPERT_DIM), BF16),
                        pltpu.VMEM((EXPERT_DIM, D_MODEL), BF16),
                        pltpu.SemaphoreType.DMA,
                        pltpu.SemaphoreType.DMA((2,))],
    )
    return pl.pallas_call(
        functools.partial(_moe_kernel, n_tokens=t),
        grid_spec=grid_spec,
        out_shape=jax.ShapeDtypeStruct((TOP_K * t, 1, D_MODEL // 2), jnp.uint32),
        compiler_params=_params("arbitrary"),
        name="moe_experts",
    )(order, block_expert, block_valid, block_start, h2, w_gate, w_up, w_down)


def _combine_kernel(base_ref, w_ref, mod_ref, g_ref, y_ref, o_ref, *, tiles_per_batch, last_layer):
    tc = base_ref.shape[0]
    lo = [None] * PACKED_ROWS
    hi = [None] * PACKED_ROWS
    for k in range(TOP_K):
        wk = w_ref[:, k:k + 1]
        for cb in range(PACKED_ROWS):
            yp = y_ref[k, pl.ds(cb, tc, stride=PACKED_ROWS), :]
            y_lo = lax.bitcast_convert_type(yp << 16, F32) * wk
            y_hi = lax.bitcast_convert_type(yp & jnp.uint32(0xFFFF0000), F32) * wk
            lo[cb] = y_lo if lo[cb] is None else lo[cb] + y_lo
            hi[cb] = y_hi if hi[cb] is None else hi[cb] + y_hi
    routed = jnp.concatenate(lo + hi, axis=1)
    b = pl.program_id(0) // tiles_per_batch
    gate2 = mod_ref[pl.ds(b, 1), 5 * D_MODEL:6 * D_MODEL]
    x = base_ref[...] + gate2 * routed
    if last_layer:
        x = x * lax.rsqrt(jnp.mean(x * x, axis=-1, keepdims=True) + NORM_EPS) * g_ref[...]
    o_ref[...] = x


def _combine(base, top_w, mod, final_g, y, seq, last_layer):
    t = base.shape[0]
    tc = T_COMBINE
    kern = functools.partial(_combine_kernel, tiles_per_batch=seq // tc, last_layer=last_layer)
    return pl.pallas_call(
        kern,
        grid=(t // tc,),
        in_specs=[pl.BlockSpec((tc, D_MODEL), lambda i: (i, 0)),
                  pl.BlockSpec((tc, TOP_K), lambda i: (i, 0)),
                  pl.BlockSpec(mod.shape, lambda i: (0, 0)),
                  pl.BlockSpec((1, D_MODEL), lambda i: (0, 0)),
                  pl.BlockSpec((TOP_K, tc * PACKED_ROWS, LANES), lambda i: (0, i, 0))],
        out_specs=pl.BlockSpec((tc, D_MODEL), lambda i: (i, 0)),
        out_shape=jax.ShapeDtypeStruct((t, D_MODEL), F32),
        compiler_params=_params("arbitrary"),
        name="combine",
    )(base, top_w, mod, final_g, y.reshape(TOP_K, t * PACKED_ROWS, LANES))


def _block_tables(counts, n_assign):
    r = ROWS_MOE
    cnt = counts.reshape(N_EXPERTS).astype(jnp.int32)
    first = jnp.cumsum(cnt) - cnt
    nblk = (cnt + r - 1) // r
    blk_end = jnp.cumsum(nblk)
    blk_start = blk_end - nblk
    n_blocks = (n_assign + N_EXPERTS * (r - 1) + r - 1) // r
    bi = jnp.arange(n_blocks, dtype=jnp.int32)
    bc = jnp.minimum(bi, blk_end[-1] - 1)
    block_expert = jnp.minimum(jnp.sum((blk_end[None, :] <= bc[:, None]).astype(jnp.int32), axis=1),
                               N_EXPERTS - 1)
    onehot = (block_expert[:, None] == jnp.arange(N_EXPERTS, dtype=jnp.int32)[None, :])
    pick = lambda v: jnp.sum(jnp.where(onehot, v[None, :], 0), axis=1)
    within = bc - pick(blk_start)
    block_valid = jnp.where(bi < blk_end[-1], jnp.clip(pick(cnt) - within * r, 0, r), 0)
    block_start = pick(first) + within * r
    return (first.astype(F32).reshape(N_EXPERTS, 1), block_expert.astype(jnp.int32),
            block_valid.astype(jnp.int32), block_start.astype(jnp.int32))


def kernel(x, c, w_ada, b_ada, norm1_g, w_in, b_forget, ret_gn_g, w_branch_a, w_branch_b, w_out,
           norm2_g, w_router, router_bias, w_exp_gate, w_exp_up, w_exp_down, w_sh_gate, w_sh_up,
           w_sh_down, final_g):
    batch, seq, d = x.shape
    t = batch * seq
    depth = w_ada.shape[0]
    x2 = x.reshape(t, d)

    half = RET_QK_DIM // 2
    inv_freq = ROPE_BASE ** (-jnp.arange(half, dtype=F32) / half)
    ang = jnp.arange(seq, dtype=F32)[:, None] * inv_freq[None, :]
    cos32, sin32 = jnp.cos(ang), jnp.sin(ang)
    cos_t = jnp.concatenate([cos32, cos32, cos32, cos32], axis=-1)
    sin_t = jnp.concatenate([-sin32, sin32, -sin32, sin32], axis=-1)
    dec_t, zeta_t, xi_t, cd_t = _retention_tables()
    c_pad = jnp.zeros((8, d), F32).at[:batch].set(c)

    for l in range(depth):
        mod = _adaln_mod(c_pad, w_ada[l], b_ada[l])

        w = w_in[l]
        o_fq, o_fk, o_fv, o_ff = 0, FOX_WIDTH, 2 * FOX_WIDTH, 3 * FOX_WIDTH
        o_rq = o_ff + FOX_HEADS
        o_rk = o_rq + RET_QK_WIDTH
        o_rv = o_rk + RET_QK_WIDTH
        o_rg = o_rv + RET_V_WIDTH
        o_ga = o_rg + RET_V_WIDTH
        o_gb = o_ga + D_MODEL
        w_p = jnp.concatenate(
            [w[:, o_rv:o_rg], w[:, o_rg:o_ga], w[:, o_ga:o_gb], w[:, o_gb:o_gb + D_MODEL],
             w[:, o_fq:o_fk], w[:, o_fk:o_fv], w[:, o_fv:o_ff], w[:, o_rq:o_rk], w[:, o_rk:o_rv]],
            axis=1).astype(BF16)
        w_ff = jnp.zeros((d, LANES), BF16).at[:, :FOX_HEADS].set(w[:, o_ff:o_rq].astype(BF16))
        b_ff = jnp.zeros((1, LANES), F32).at[0, :FOX_HEADS].set(b_forget[l].astype(F32))

        proj, logf = _in_proj(x2, mod, norm1_g[l].reshape(1, d), w_p, w_ff, b_ff, cos_t, sin_t, seq)
        cum_f = jnp.cumsum(logf[:, :FOX_HEADS].reshape(batch, seq, FOX_HEADS), axis=1)
        cum_f = cum_f.transpose(0, 2, 1) * LOG2_E

        attn = _fox_attn(proj, cum_f, batch, seq)
        ret = _retention(proj, ret_gn_g[l].reshape(1, RET_V_WIDTH), dec_t, zeta_t, xi_t, cd_t,
                         batch, seq)

        base, h2, scores_t = _post_mix(
            attn, ret, proj, x2, mod, norm2_g[l].reshape(1, d),
            w_branch_a[l].astype(BF16), w_branch_b[l].astype(BF16), w_out[l].astype(BF16),
            w_router[l].T, w_sh_gate[l].astype(BF16), w_sh_up[l].astype(BF16),
            w_sh_down[l].astype(BF16), seq)

        top_e, top_w, rank, counts = _route(scores_t,
                                            router_bias[l].astype(F32).reshape(N_EXPERTS, 1))
        first, block_expert, block_valid, block_start = _block_tables(counts, t * TOP_K)
        sorted_pos = _dest_rows(top_e, rank, first).reshape(TOP_K * t)
        order = jnp.argsort(sorted_pos).astype(jnp.int32)
        y = _moe_experts(order, block_expert, block_valid, block_start,
                         h2.reshape(t, 1, d // 2),
                         w_exp_gate[l], w_exp_up[l], w_exp_down[l])
        x2 = _combine(base, top_w.T, mod, final_g.reshape(1, d), y, seq, l == depth - 1)
    return x2.reshape(batch, seq, d)
```

```python
import functools

import jax
import jax.numpy as jnp
import numpy as np
from jax import lax
from jax.experimental import pallas as pl
from jax.experimental.pallas import tpu as pltpu

F32 = jnp.float32
BF16 = jnp.bfloat16

D_MODEL = 1024
FOX_HEADS = 8
FOX_HEAD_DIM = 64
FOX_WIDTH = 512
RET_HEADS = 8
RET_QK_DIM = 64
RET_V_DIM = 128
RET_QK_WIDTH = 512
RET_V_WIDTH = 1024
RET_CHUNK = 128
ROPE_BASE = 10000.0
N_EXPERTS = 256
TOP_K = 8
N_GROUPS = 8
TOPK_GROUPS = 4
EXPERT_DIM = 256
ROUTED_SCALE = 2.5
NORM_EPS = 1e-6
LOG2_E = 1.4426950408889634

LANES = 128
PACKED_ROWS = D_MODEL // 2 // LANES
VMEM_LIMIT = 56 * 1024 * 1024

COL_RV, COL_RG, COL_GA, COL_GB = 0, 1024, 2048, 3072
COL_FQ, COL_FK, COL_FV, COL_RQ, COL_RK = 4096, 4608, 5120, 5632, 6144
PROJ_WIDTH = 6656

TM_PROJ = 512
TN_PROJ = 512
TQ_ATTN = 512
TK_ATTN = 512
T_RET = 512
TM_POST = 512
T_ROUTE = 512
ROWS_MOE = 256
T_COMBINE = 256


def _sigmoid(z):
    return 1.0 / (1.0 + jnp.exp(-z))


def _silu(z):
    return z * _sigmoid(z)


def _pack_halves(xb):
    n = xb.shape[1] // 2
    lo = lax.bitcast_convert_type(xb[:, :n].astype(F32), jnp.uint32)
    hi = lax.bitcast_convert_type(xb[:, n:].astype(F32), jnp.uint32)
    return (lo >> 16) | hi


def _unpack_halves(xp):
    lo = lax.bitcast_convert_type(xp << 16, F32)
    hi = lax.bitcast_convert_type(xp & jnp.uint32(0xFFFF0000), F32)
    return lo.astype(BF16), hi.astype(BF16)


def _params(*sem):
    return pltpu.CompilerParams(dimension_semantics=sem, vmem_limit_bytes=VMEM_LIMIT)


def _adaln_kernel(c_ref, w_ref, b_ref, o_ref):
    a = _silu(c_ref[...]).astype(BF16)
    o_ref[...] = jnp.dot(a, w_ref[...].astype(BF16), preferred_element_type=F32) + b_ref[...]


def _adaln_mod(c_pad, w_ada, b_ada):
    n = w_ada.shape[1]
    tn = 1024
    return pl.pallas_call(
        _adaln_kernel,
        grid=(n // tn,),
        in_specs=[pl.BlockSpec((8, D_MODEL), lambda j: (0, 0)),
                  pl.BlockSpec((D_MODEL, tn), lambda j: (0, j)),
                  pl.BlockSpec((1, tn), lambda j: (0, j))],
        out_specs=pl.BlockSpec((8, tn), lambda j: (0, j)),
        out_shape=jax.ShapeDtypeStruct((8, n), F32),
        compiler_params=_params("arbitrary"),
        name="adaln_mod",
    )(c_pad, w_ada, b_ada.reshape(1, n))


def _inproj_kernel(x_ref, mod_ref, g_ref, w_ref, wff_ref, bf_ref, cos_ref, sin_ref,
                   proj_ref, logf_ref, *, tiles_per_batch):
    b = pl.program_id(0) // tiles_per_batch
    x = x_ref[...]
    y = x * lax.rsqrt(jnp.mean(x * x, axis=-1, keepdims=True) + NORM_EPS) * g_ref[...]
    shift = mod_ref[pl.ds(b, 1), 0:D_MODEL]
    scale = mod_ref[pl.ds(b, 1), D_MODEL:2 * D_MODEL]
    hb = (y * (1.0 + scale) + shift).astype(BF16)
    z = jnp.dot(hb, wff_ref[...], preferred_element_type=F32) + bf_ref[...]
    logf_ref[...] = jnp.minimum(z, 0.0) - jnp.log1p(jnp.exp(-jnp.abs(z)))

    cos = cos_ref[...]
    sin = sin_ref[...]
    lane = lax.broadcasted_iota(jnp.int32, cos.shape, 1)
    first_half = (lane % RET_QK_DIM) < (RET_QK_DIM // 2)

    for c0 in range(0, PROJ_WIDTH, TN_PROJ):
        cols = slice(c0, c0 + TN_PROJ)
        acc = jnp.dot(hb, w_ref[:, cols], preferred_element_type=F32)
        if COL_GA <= c0 < COL_FQ:
            proj_ref[:, cols] = _sigmoid(acc).astype(BF16)
        elif COL_FQ <= c0 < COL_FK:
            proj_ref[:, cols] = (acc * (LOG2_E * FOX_HEAD_DIM ** -0.5)).astype(BF16)
        elif c0 >= COL_RQ:
            k_scale = RET_QK_DIM ** -0.5 if c0 >= COL_RK else 1.0
            for cb in range(c0, c0 + TN_PROJ, LANES):
                xs = acc[:, cb - c0:cb - c0 + LANES]
                up = pltpu.roll(xs, LANES - RET_QK_DIM // 2, axis=1)
                dn = pltpu.roll(xs, RET_QK_DIM // 2, axis=1)
                rot = xs * cos + jnp.where(first_half, up, dn) * sin
                proj_ref[:, cb:cb + LANES] = (rot * k_scale).astype(BF16)
        else:
            proj_ref[:, cols] = acc.astype(BF16)


def _in_proj(x2, mod, norm_g, w_p, w_ff, b_ff, cos_t, sin_t, seq):
    t = x2.shape[0]
    tiles_per_batch = seq // TM_PROJ
    kern = functools.partial(_inproj_kernel, tiles_per_batch=tiles_per_batch)
    return pl.pallas_call(
        kern,
        grid=(t // TM_PROJ,),
        in_specs=[pl.BlockSpec((TM_PROJ, D_MODEL), lambda i: (i, 0)),
                  pl.BlockSpec(mod.shape, lambda i: (0, 0)),
                  pl.BlockSpec((1, D_MODEL), lambda i: (0, 0)),
                  pl.BlockSpec((D_MODEL, PROJ_WIDTH), lambda i: (0, 0)),
                  pl.BlockSpec((D_MODEL, LANES), lambda i: (0, 0)),
                  pl.BlockSpec((1, LANES), lambda i: (0, 0)),
                  pl.BlockSpec((TM_PROJ, LANES), lambda i: (i % tiles_per_batch, 0)),
                  pl.BlockSpec((TM_PROJ, LANES), lambda i: (i % tiles_per_batch, 0))],
        out_specs=[pl.BlockSpec((TM_PROJ, PROJ_WIDTH), lambda i: (i, 0)),
                   pl.BlockSpec((TM_PROJ, LANES), lambda i: (i, 0))],
        out_shape=[jax.ShapeDtypeStruct((t, PROJ_WIDTH), BF16),
                   jax.ShapeDtypeStruct((t, LANES), F32)],
        compiler_params=_params("arbitrary"),
        name="in_proj",
    )(x2, mod, norm_g, w_p, w_ff, b_ff, cos_t, sin_t)


def _attn_kernel(q_ref, k_ref, v_ref, f_ref, o_ref):
    hp = pl.program_id(1)
    qi = pl.program_id(2)
    tq, tk = TQ_ATTN, TK_ATTN
    q = q_ref[...]
    lane = lax.broadcasted_iota(jnp.int32, (tq, LANES), 1)
    row = qi * tq + lax.broadcasted_iota(jnp.int32, (tq, tk), 0)
    col = lax.broadcasted_iota(jnp.int32, (tq, tk), 1)
    zeros = jnp.zeros_like(q)
    qm = (jnp.where(lane < FOX_HEAD_DIM, q, zeros), jnp.where(lane >= FOX_HEAD_DIM, q, zeros))

    def step(kb, carry, masked):
        ks = pl.multiple_of(kb * tk, tk)
        k = k_ref[pl.ds(ks, tk), :]
        v = v_ref[pl.ds(ks, tk), :]
        out = []
        for hh in range(2):
            m, l, acc = carry[hh]
            s = lax.dot_general(qm[hh], k, (((1,), (1,)), ((), ())), preferred_element_type=F32)
            s = s - f_ref[0, pl.ds(2 * hp + hh, 1), pl.ds(ks, tk)]
            if masked:
                s = jnp.where(col + ks <= row, s, -jnp.inf)
            m_new = jnp.maximum(m, jnp.max(s, axis=-1, keepdims=True))
            alpha = jnp.exp2(m - m_new)
            p = jnp.exp2(s - m_new)
            l = alpha * l + jnp.sum(p, axis=-1, keepdims=True)
            acc = alpha * acc + jnp.dot(p.astype(BF16), v, preferred_element_type=F32)
            out.append((m_new, l, acc))
        return tuple(out)

    init = (jnp.full((tq, 1), -jnp.inf, F32), jnp.zeros((tq, 1), F32),
            jnp.zeros((tq, LANES), F32))
    n_full = (qi * tq) // tk
    carry = lax.fori_loop(0, n_full, lambda kb, c: step(kb, c, False), (init, init))
    for d in range(max(tq // tk, 1)):
        carry = step(n_full + d, carry, True)
    (_, l0, acc0), (_, l1, acc1) = carry
    o_ref[...] = jnp.where(lane < FOX_HEAD_DIM, acc0 / l0, acc1 / l1).astype(BF16)


def _fox_attn(proj, cum_f, batch, seq):
    t = proj.shape[0]
    nq = seq // TQ_ATTN
    cq, ck, cv = COL_FQ // LANES, COL_FK // LANES, COL_FV // LANES
    return pl.pallas_call(
        _attn_kernel,
        grid=(batch, FOX_HEADS // 2, nq),
        in_specs=[pl.BlockSpec((TQ_ATTN, LANES), lambda b, hp, qi: (b * nq + qi, cq + hp)),
                  pl.BlockSpec((seq, LANES), lambda b, hp, qi: (b, ck + hp)),
                  pl.BlockSpec((seq, LANES), lambda b, hp, qi: (b, cv + hp)),
                  pl.BlockSpec((1, FOX_HEADS, seq), lambda b, hp, qi: (b, 0, 0))],
        out_specs=pl.BlockSpec((TQ_ATTN, LANES), lambda b, hp, qi: (b * nq + qi, hp)),
        out_shape=jax.ShapeDtypeStruct((t, FOX_WIDTH), BF16),
        compiler_params=_params("arbitrary", "arbitrary", "arbitrary"),
        name="fox_attn",
    )(proj, proj, proj, cum_f)


def _ret_kernel(q_ref, k_ref, v_ref, rg_ref, gn_ref, dec_ref, zeta_ref, xi_ref, cd_ref,
                o_ref, st_sc):
    ri = pl.program_id(1)
    c = RET_CHUNK

    @pl.when(ri == 0)
    def _():
        st_sc[...] = jnp.zeros_like(st_sc)

    lane = lax.broadcasted_iota(jnp.int32, (c, LANES), 1)
    zeros = jnp.zeros((c, LANES), BF16)
    states = [st_sc[h] for h in range(RET_HEADS)]
    for n in range(T_RET // c):
        rows = slice(n * c, (n + 1) * c)
        for h in range(RET_HEADS):
            pair = slice((h // 2) * LANES, (h // 2 + 1) * LANES)
            vcol = slice(h * RET_V_DIM, (h + 1) * RET_V_DIM)
            sel = (lane < RET_QK_DIM) if h % 2 == 0 else (lane >= RET_QK_DIM)
            qc = jnp.where(sel, q_ref[rows, pair], zeros)
            kc = jnp.where(sel, k_ref[rows, pair], zeros)
            vc = v_ref[rows, vcol]
            sc = lax.dot_general(qc, kc, (((1,), (1,)), ((), ())),
                                 preferred_element_type=F32) * dec_ref[h]
            intra = jnp.dot(sc.astype(BF16), vc, preferred_element_type=F32)
            cross = jnp.dot(qc, states[h].astype(BF16), preferred_element_type=F32) * xi_ref[h]
            o = intra + cross
            kz = (kc.astype(F32) * zeta_ref[h]).astype(BF16)
            kv = lax.dot_general(kz, vc, (((0,), (0,)), ((), ())), preferred_element_type=F32)
            states[h] = cd_ref[h] * states[h] + kv
            mu = jnp.mean(o, axis=-1, keepdims=True)
            d = o - mu
            var = jnp.mean(d * d, axis=-1, keepdims=True)
            on = d * lax.rsqrt(var + NORM_EPS) * gn_ref[:, vcol]
            rg = rg_ref[rows, vcol].astype(F32)
            o_ref[rows, vcol] = (on * _silu(rg)).astype(BF16)
    for h in range(RET_HEADS):
        st_sc[h] = states[h]


def _retention(proj, gn_g, dec_t, zeta_t, xi_t, cd_t, batch, seq):
    t = proj.shape[0]
    nr = seq // T_RET
    tok = lambda b, ri: b * nr + ri
    tab = pl.BlockSpec((RET_HEADS, RET_CHUNK, LANES), lambda b, ri: (0, 0, 0))
    return pl.pallas_call(
        _ret_kernel,
        grid=(batch, nr),
        in_specs=[pl.BlockSpec((T_RET, RET_QK_WIDTH), lambda b, ri: (tok(b, ri), COL_RQ // RET_QK_WIDTH)),
                  pl.BlockSpec((T_RET, RET_QK_WIDTH), lambda b, ri: (tok(b, ri), COL_RK // RET_QK_WIDTH)),
                  pl.BlockSpec((T_RET, RET_V_WIDTH), lambda b, ri: (tok(b, ri), COL_RV // RET_V_WIDTH)),
                  pl.BlockSpec((T_RET, RET_V_WIDTH), lambda b, ri: (tok(b, ri), COL_RG // RET_V_WIDTH)),
                  pl.BlockSpec((1, RET_V_WIDTH), lambda b, ri: (0, 0)),
                  tab, tab, tab, tab],
        out_specs=pl.BlockSpec((T_RET, RET_V_WIDTH), lambda b, ri: (tok(b, ri), 0)),
        out_shape=jax.ShapeDtypeStruct((t, RET_V_WIDTH), BF16),
        scratch_shapes=[pltpu.VMEM((RET_HEADS, LANES, RET_V_DIM), F32)],
        compiler_params=_params("arbitrary", "arbitrary"),
        name="retention",
    )(proj, proj, proj, proj, gn_g, dec_t, zeta_t, xi_t, cd_t)


def _retention_tables():
    c = RET_CHUNK
    log_gamma = jnp.log1p(-jnp.exp2(-5.0 - jnp.arange(RET_HEADS, dtype=F32)))
    idx = jnp.arange(c, dtype=F32)
    diff = idx[:, None] - idx[None, :]
    dec = jnp.where(diff >= 0,
                    jnp.exp(jnp.maximum(diff, 0.0)[None] * log_gamma[:, None, None]), 0.0)
    zeta = jnp.exp((c - 1.0 - idx)[None, :] * log_gamma[:, None])
    xi = jnp.exp((idx + 1.0)[None, :] * log_gamma[:, None])
    cd = jnp.exp(c * log_gamma)
    bc = lambda v: jnp.broadcast_to(v[:, :, None], (RET_HEADS, c, LANES))
    cd_t = jnp.broadcast_to(cd[:, None, None], (RET_HEADS, LANES, RET_V_DIM))
    return dec, bc(zeta), bc(xi), cd_t


def _post_kernel(attn_ref, ret_ref, ga_ref, gb_ref, x_ref, mod_ref, g2_ref,
                 wa_ref, wb_ref, wo_ref, wr_ref, wsg_ref, wsu_ref, wsd_ref,
                 base_ref, h2_ref, score_ref, *, tiles_per_batch):
    b = pl.program_id(0) // tiles_per_batch
    ya = jnp.dot(attn_ref[...], wa_ref[...], preferred_element_type=F32)
    yb = jnp.dot(ret_ref[...], wb_ref[...], preferred_element_type=F32)
    merged = ga_ref[...].astype(F32) * ya + gb_ref[...].astype(F32) * yb
    mix = jnp.dot(merged.astype(BF16), wo_ref[...], preferred_element_type=F32)
    d = D_MODEL
    gate1 = mod_ref[pl.ds(b, 1), 2 * d:3 * d]
    shift2 = mod_ref[pl.ds(b, 1), 3 * d:4 * d]
    scale2 = mod_ref[pl.ds(b, 1), 4 * d:5 * d]
    gate2 = mod_ref[pl.ds(b, 1), 5 * d:6 * d]
    x1 = x_ref[...] + gate1 * mix
    y = x1 * lax.rsqrt(jnp.mean(x1 * x1, axis=-1, keepdims=True) + NORM_EPS) * g2_ref[...]
    h2 = y * (1.0 + scale2) + shift2
    logits_t = lax.dot_general(wr_ref[...], h2, (((1,), (1,)), ((), ())),
                               preferred_element_type=F32, precision=lax.Precision.HIGHEST)
    score_ref[...] = _sigmoid(logits_t)
    h2b = h2.astype(BF16)
    h2p = _pack_halves(h2b)
    n_chunks = h2p.shape[1] // LANES
    for cb in range(n_chunks):
        h2_ref[pl.ds(cb, h2p.shape[0], stride=n_chunks), :] = h2p[:, cb * LANES:(cb + 1) * LANES]
    g = jnp.dot(h2b, wsg_ref[...], preferred_element_type=F32)
    u = jnp.dot(h2b, wsu_ref[...], preferred_element_type=F32)
    shared = jnp.dot((_silu(g) * u).astype(BF16), wsd_ref[...], preferred_element_type=F32)
    base_ref[...] = x1 + gate2 * shared


def _post_mix(attn, ret, proj, x2, mod, norm2_g, wa, wb, wo, wr, wsg, wsu, wsd, seq):
    t = x2.shape[0]
    tm = TM_POST
    kern = functools.partial(_post_kernel, tiles_per_batch=seq // tm)
    full = lambda a: pl.BlockSpec(a.shape, lambda i: (0,) * a.ndim)
    return pl.pallas_call(
        kern,
        grid=(t // tm,),
        in_specs=[pl.BlockSpec((tm, FOX_WIDTH), lambda i: (i, 0)),
                  pl.BlockSpec((tm, RET_V_WIDTH), lambda i: (i, 0)),
                  pl.BlockSpec((tm, D_MODEL), lambda i: (i, COL_GA // D_MODEL)),
                  pl.BlockSpec((tm, D_MODEL), lambda i: (i, COL_GB // D_MODEL)),
                  pl.BlockSpec((tm, D_MODEL), lambda i: (i, 0)),
                  full(mod), full(norm2_g), full(wa), full(wb), full(wo), full(wr),
                  full(wsg), full(wsu), full(wsd)],
        out_specs=[pl.BlockSpec((tm, D_MODEL), lambda i: (i, 0)),
                   pl.BlockSpec((tm * PACKED_ROWS, LANES), lambda i: (i, 0)),
                   pl.BlockSpec((N_EXPERTS, tm), lambda i: (0, i))],
        out_shape=[jax.ShapeDtypeStruct((t, D_MODEL), F32),
                   jax.ShapeDtypeStruct((t * PACKED_ROWS, LANES), jnp.uint32),
                   jax.ShapeDtypeStruct((N_EXPERTS, t), F32)],
        compiler_params=_params("arbitrary"),
        name="post_mix",
    )(attn, ret, proj, proj, x2, mod, norm2_g, wa, wb, wo, wr, wsg, wsu, wsd)


def _route_kernel(s_ref, bias_ref, e_ref, w_ref, rank_ref, cnt_ref):
    tr = s_ref.shape[1]
    gsz = N_EXPERTS // N_GROUPS

    @pl.when(pl.program_id(0) == 0)
    def _():
        cnt_ref[...] = jnp.zeros_like(cnt_ref)

    s = s_ref[...]
    sel = s + bias_ref[...]
    neg = jnp.float32(-jnp.inf)

    g3 = sel.reshape(N_GROUPS, gsz, tr)
    m1 = jnp.max(g3, axis=1)
    n_max = jnp.sum((g3 == m1[:, None, :]).astype(F32), axis=1)
    m2 = jnp.max(jnp.where(g3 < m1[:, None, :], g3, neg), axis=1)
    gs = m1 + jnp.where(n_max >= 2.0, m1, m2)

    gi = lax.broadcasted_iota(jnp.int32, (N_GROUPS, tr), 0)
    beaten = jnp.zeros((N_GROUPS, tr), F32)
    for j in range(N_GROUPS):
        gj = gs[j:j + 1, :]
        beaten = beaten + jnp.where((gj > gs) | ((gj == gs) & (gi > j)), 1.0, 0.0)
    keep = jnp.where(beaten < float(TOPK_GROUPS), 1.0, 0.0)
    keep_e = jnp.broadcast_to(keep[:, None, :], (N_GROUPS, gsz, tr)).reshape(N_EXPERTS, tr)

    ei = lax.broadcasted_iota(jnp.int32, (N_EXPERTS, tr), 0)
    val = jnp.where(keep_e > 0.0, sel, neg)
    member = jnp.zeros((N_EXPERTS, tr), F32)
    idx_rows, w_rows = [], []
    for _ in range(TOP_K):
        m = jnp.max(val, axis=0, keepdims=True)
        idx = jnp.min(jnp.where(val == m, ei, N_EXPERTS), axis=0, keepdims=True)
        hit = ei == idx
        w_rows.append(jnp.sum(jnp.where(hit, s, 0.0), axis=0, keepdims=True))
        idx_rows.append(idx)
        val = jnp.where(hit, neg, val)
        member = jnp.where(hit, 1.0, member)

    w_sum = w_rows[0]
    for k in range(1, TOP_K):
        w_sum = w_sum + w_rows[k]

    r_i = lax.broadcasted_iota(jnp.int32, (tr, tr), 0)
    c_i = lax.broadcasted_iota(jnp.int32, (tr, tr), 1)
    upper = jnp.where(r_i < c_i, 1.0, 0.0).astype(BF16)
    before = jnp.dot(member.astype(BF16), upper, preferred_element_type=F32) + cnt_ref[...]

    for k in range(TOP_K):
        hit = ei == idx_rows[k]
        e_ref[k:k + 1, :] = idx_rows[k]
        w_ref[k:k + 1, :] = w_rows[k] / w_sum * ROUTED_SCALE
        rank_ref[k:k + 1, :] = jnp.sum(jnp.where(hit, before, 0.0), axis=0,
                                       keepdims=True).astype(jnp.int32)
    cnt_ref[...] = cnt_ref[...] + jnp.sum(member, axis=1, keepdims=True)


def _route(scores_t, bias_col):
    t = scores_t.shape[1]
    tr = T_ROUTE
    row8 = lambda dt: jax.ShapeDtypeStruct((TOP_K, t), dt)
    return pl.pallas_call(
        _route_kernel,
        grid=(t // tr,),
        in_specs=[pl.BlockSpec((N_EXPERTS, tr), lambda i: (0, i)),
                  pl.BlockSpec((N_EXPERTS, 1), lambda i: (0, 0))],
        out_specs=[pl.BlockSpec((TOP_K, tr), lambda i: (0, i)),
                   pl.BlockSpec((TOP_K, tr), lambda i: (0, i)),
                   pl.BlockSpec((TOP_K, tr), lambda i: (0, i)),
                   pl.BlockSpec((N_EXPERTS, 1), lambda i: (0, 0))],
        out_shape=[row8(jnp.int32), row8(F32), row8(jnp.int32),
                   jax.ShapeDtypeStruct((N_EXPERTS, 1), F32)],
        compiler_params=_params("arbitrary"),
        name="route",
    )(scores_t, bias_col)


def _dest_kernel(e_ref, rank_ref, start_ref, dest_ref):
    tr = e_ref.shape[1]
    ei = lax.broadcasted_iota(jnp.int32, (N_EXPERTS, tr), 0)
    start = start_ref[...]
    for k in range(TOP_K):
        hit = ei == e_ref[k:k + 1, :]
        off = jnp.sum(jnp.where(hit, start, 0.0), axis=0, keepdims=True)
        dest_ref[k:k + 1, :] = rank_ref[k:k + 1, :] + off.astype(jnp.int32)


def _dest_rows(top_e, rank, start_col):
    t = top_e.shape[1]
    tr = T_ROUTE
    spec = pl.BlockSpec((TOP_K, tr), lambda i: (0, i))
    return pl.pallas_call(
        _dest_kernel,
        grid=(t // tr,),
        in_specs=[spec, spec, pl.BlockSpec((N_EXPERTS, 1), lambda i: (0, 0))],
        out_specs=spec,
        out_shape=jax.ShapeDtypeStruct((TOP_K, t), jnp.int32),
        compiler_params=_params("arbitrary"),
        name="dest_rows",
    )(top_e, rank, start_col)


def _moe_kernel(order_ref, be_ref, nv_ref, cs_ref, nu_ref, h_ref, wg_ref, wu_ref, wd_ref, y_ref,
                hbuf, xb0, xb1, yb0, yb1, wg_sc, wu_sc, wd_sc, hsem, ssem, *, n_tokens, n_assign):
    i = pl.program_id(0)
    n_steps = pl.num_programs(0)
    r = ROWS_MOE
    half = D_MODEL // 2
    active = i <= nu_ref[0]

    def copy_rows(blk, xdst):
        base = cs_ref[blk]
        for j in range(r):
            tok = order_ref[base + j] & (n_tokens - 1)
            xdst[j // 8, pl.ds(j % 8, 1), :] = hbuf[tok]

    def compute(blk, xsrc, ydst):
        row = lax.broadcasted_iota(jnp.int32, (r, half), 0)
        xp = jnp.where(row < nv_ref[blk], xsrc[...].reshape(r, half), jnp.uint32(0))
        x_lo, x_hi = _unpack_halves(xp)
        g = (jnp.dot(x_lo, wg_sc[:half, :], preferred_element_type=F32)
             + jnp.dot(x_hi, wg_sc[half:, :], preferred_element_type=F32))
        u = (jnp.dot(x_lo, wu_sc[:half, :], preferred_element_type=F32)
             + jnp.dot(x_hi, wu_sc[half:, :], preferred_element_type=F32))
        y = jnp.dot((_silu(g) * u).astype(BF16), wd_sc[...], preferred_element_type=F32)
        ydst[...] = _pack_halves(y.astype(BF16)).reshape(r // 8, 8, half)

    def scatter(blk, n_rows, ysrc, spare):
        base = cs_ref[blk]
        for j in range(r):
            dst = jnp.where(j < n_rows, order_ref[base + j], spare + j)
            pltpu.make_async_copy(ysrc.at[j // 8, pl.ds(j % 8, 1), :], y_ref.at[dst], ssem).start()

    def scatter_wait():
        pltpu.make_async_copy(y_ref.at[pl.ds(0, r)], y_ref.at[pl.ds(r, r)], ssem).wait()

    @pl.when(i == 0)
    def _():
        load = pltpu.make_async_copy(h_ref, hbuf, hsem)
        load.start()
        load.wait()
        copy_rows(0, xb0)
        yb1[...] = jnp.zeros_like(yb1)

    @pl.when((i >= 1) & (i - 1 <= nu_ref[0]))
    def _():
        scatter_wait()

    prev = be_ref[jnp.maximum(i - 1, 0)]

    @pl.when((i == 0) | (be_ref[i] != prev))
    def _():
        wg_sc[...] = wg_ref[0].astype(BF16)
        wu_sc[...] = wu_ref[0].astype(BF16)
        wd_sc[...] = wd_ref[0].astype(BF16)

    def step(x_cur, x_next, y_cur, y_prev, spare):
        last = jnp.maximum(i - 1, 0)
        scatter(last, jnp.where(i > 0, nv_ref[last], 0), y_prev, spare)
        copy_rows(i + 1, x_next)
        compute(i, x_cur, y_cur)

    @pl.when(active & (i % 2 == 0))
    def _():
        step(xb0, xb1, yb0, yb1, n_assign + r)

    @pl.when(active & (i % 2 == 1))
    def _():
        step(xb1, xb0, yb1, yb0, n_assign)

    @pl.when((i == n_steps - 1) & active)
    def _():
        scatter_wait()


def _moe_experts(order, block_expert, block_valid, block_start, n_used, h2, w_gate, w_up, w_down):
    t = h2.shape[0]
    assert t & (t - 1) == 0, "token count must be a power of two"
    n_blocks = block_expert.shape[0] - 2
    n_assign = TOP_K * t
    wspec = lambda shp: pl.BlockSpec((1,) + shp, lambda i, od, be, nv, cs, nu: (be[i], 0, 0))
    row_buf = pltpu.VMEM((ROWS_MOE // 8, 8, D_MODEL // 2), jnp.uint32)
    grid_spec = pltpu.PrefetchScalarGridSpec(
        num_scalar_prefetch=5,
        grid=(n_blocks + 1,),
        in_specs=[pl.BlockSpec(memory_space=pl.ANY),
                  wspec((D_MODEL, EXPERT_DIM)), wspec((D_MODEL, EXPERT_DIM)),
                  wspec((EXPERT_DIM, D_MODEL))],
        out_specs=pl.BlockSpec(memory_space=pl.ANY),
        scratch_shapes=[pltpu.VMEM((t, 1, D_MODEL // 2), jnp.uint32),
                        row_buf, row_buf, row_buf, row_buf,
                        pltpu.VMEM((D_MODEL, EXPERT_DIM), BF16),
                        pltpu.VMEM((D_MODEL, EXPERT_DIM), BF16),
                        pltpu.VMEM((EXPERT_DIM, D_MODEL), BF16),
                        pltpu.SemaphoreType.DMA,
                        pltpu.SemaphoreType.DMA],
    )
    return pl.pallas_call(
        functools.partial(_moe_kernel, n_tokens=t, n_assign=n_assign),
        grid_spec=grid_spec,
        out_shape=jax.ShapeDtypeStruct((n_assign + 2 * ROWS_MOE, 1, D_MODEL // 2), jnp.uint32),
        compiler_params=_params("arbitrary"),
        name="moe_experts",
    )(order, block_expert, block_valid, block_start, n_used, h2, w_gate, w_up, w_down)


def _combine_kernel(base_ref, w_ref, mod_ref, g_ref, *refs, tiles_per_batch, last_layer):
    y_refs, o_ref = refs[:TOP_K], refs[TOP_K]
    tc = base_ref.shape[0]
    lo = [None] * PACKED_ROWS
    hi = [None] * PACKED_ROWS
    for k in range(TOP_K):
        wk = w_ref[:, k:k + 1]
        for cb in range(PACKED_ROWS):
            yp = y_refs[k][pl.ds(cb, tc, stride=PACKED_ROWS), :]
            y_lo = lax.bitcast_convert_type(yp << 16, F32) * wk
            y_hi = lax.bitcast_convert_type(yp & jnp.uint32(0xFFFF0000), F32) * wk
            lo[cb] = y_lo if lo[cb] is None else lo[cb] + y_lo
            hi[cb] = y_hi if hi[cb] is None else hi[cb] + y_hi
    routed = jnp.concatenate(lo + hi, axis=1)
    b = pl.program_id(0) // tiles_per_batch
    gate2 = mod_ref[pl.ds(b, 1), 5 * D_MODEL:6 * D_MODEL]
    x = base_ref[...] + gate2 * routed
    if last_layer:
        x = x * lax.rsqrt(jnp.mean(x * x, axis=-1, keepdims=True) + NORM_EPS) * g_ref[...]
    o_ref[...] = x


def _combine(base, top_w, mod, final_g, y, seq, last_layer):
    t = base.shape[0]
    tc = T_COMBINE
    kern = functools.partial(_combine_kernel, tiles_per_batch=seq // tc, last_layer=last_layer)
    y_rows = y.reshape(y.shape[0] * PACKED_ROWS, LANES)
    tiles = t // tc
    y_specs = [pl.BlockSpec((tc * PACKED_ROWS, LANES), lambda i, k=k: (k * tiles + i, 0))
               for k in range(TOP_K)]
    return pl.pallas_call(
        kern,
        grid=(tiles,),
        in_specs=[pl.BlockSpec((tc, D_MODEL), lambda i: (i, 0)),
                  pl.BlockSpec((tc, TOP_K), lambda i: (i, 0)),
                  pl.BlockSpec(mod.shape, lambda i: (0, 0)),
                  pl.BlockSpec((1, D_MODEL), lambda i: (0, 0))] + y_specs,
        out_specs=pl.BlockSpec((tc, D_MODEL), lambda i: (i, 0)),
        out_shape=jax.ShapeDtypeStruct((t, D_MODEL), F32),
        compiler_params=_params("arbitrary"),
        name="combine",
    )(base, top_w, mod, final_g, *([y_rows] * TOP_K))


def _block_tables(counts, n_assign):
    r = ROWS_MOE
    cnt = counts.reshape(N_EXPERTS).astype(jnp.int32)
    first = jnp.cumsum(cnt) - cnt
    nblk = (cnt + r - 1) // r
    blk_end = jnp.cumsum(nblk)
    blk_start = blk_end - nblk
    n_blocks = (n_assign + N_EXPERTS * (r - 1) + r - 1) // r
    bi = jnp.arange(n_blocks + 2, dtype=jnp.int32)
    bc = jnp.minimum(bi, blk_end[-1] - 1)
    block_expert = jnp.minimum(jnp.sum((blk_end[None, :] <= bc[:, None]).astype(jnp.int32), axis=1),
                               N_EXPERTS - 1)
    onehot = (block_expert[:, None] == jnp.arange(N_EXPERTS, dtype=jnp.int32)[None, :])
    pick = lambda v: jnp.sum(jnp.where(onehot, v[None, :], 0), axis=1)
    within = bc - pick(blk_start)
    block_valid = jnp.where(bi < blk_end[-1], jnp.clip(pick(cnt) - within * r, 0, r), 0)
    block_start = pick(first) + within * r
    return (first.astype(F32).reshape(N_EXPERTS, 1), block_expert.astype(jnp.int32),
            block_valid.astype(jnp.int32), block_start.astype(jnp.int32),
            blk_end[-1:].astype(jnp.int32))


def kernel(x, c, w_ada, b_ada, norm1_g, w_in, b_forget, ret_gn_g, w_branch_a, w_branch_b, w_out,
           norm2_g, w_router, router_bias, w_exp_gate, w_exp_up, w_exp_down, w_sh_gate, w_sh_up,
           w_sh_down, final_g):
    batch, seq, d = x.shape
    t = batch * seq
    depth = w_ada.shape[0]
    x2 = x.reshape(t, d)

    half = RET_QK_DIM // 2
    inv_freq = ROPE_BASE ** (-jnp.arange(half, dtype=F32) / half)
    ang = jnp.arange(seq, dtype=F32)[:, None] * inv_freq[None, :]
    cos32, sin32 = jnp.cos(ang), jnp.sin(ang)
    cos_t = jnp.concatenate([cos32, cos32, cos32, cos32], axis=-1)
    sin_t = jnp.concatenate([-sin32, sin32, -sin32, sin32], axis=-1)
    dec_t, zeta_t, xi_t, cd_t = _retention_tables()
    c_pad = jnp.zeros((8, d), F32).at[:batch].set(c)

    for l in range(depth):
        mod = _adaln_mod(c_pad, w_ada[l], b_ada[l])

        w = w_in[l]
        o_fq, o_fk, o_fv, o_ff = 0, FOX_WIDTH, 2 * FOX_WIDTH, 3 * FOX_WIDTH
        o_rq = o_ff + FOX_HEADS
        o_rk = o_rq + RET_QK_WIDTH
        o_rv = o_rk + RET_QK_WIDTH
        o_rg = o_rv + RET_V_WIDTH
        o_ga = o_rg + RET_V_WIDTH
        o_gb = o_ga + D_MODEL
        w_p = jnp.concatenate(
            [w[:, o_rv:o_rg], w[:, o_rg:o_ga], w[:, o_ga:o_gb], w[:, o_gb:o_gb + D_MODEL],
             w[:, o_fq:o_fk], w[:, o_fk:o_fv], w[:, o_fv:o_ff], w[:, o_rq:o_rk], w[:, o_rk:o_rv]],
            axis=1).astype(BF16)
        w_ff = jnp.zeros((d, LANES), BF16).at[:, :FOX_HEADS].set(w[:, o_ff:o_rq].astype(BF16))
        b_ff = jnp.zeros((1, LANES), F32).at[0, :FOX_HEADS].set(b_forget[l].astype(F32))

        proj, logf = _in_proj(x2, mod, norm1_g[l].reshape(1, d), w_p, w_ff, b_ff, cos_t, sin_t, seq)
        cum_f = jnp.cumsum(logf[:, :FOX_HEADS].reshape(batch, seq, FOX_HEADS), axis=1)
        cum_f = cum_f.transpose(0, 2, 1) * LOG2_E

        attn = _fox_attn(proj, cum_f, batch, seq)
        ret = _retention(proj, ret_gn_g[l].reshape(1, RET_V_WIDTH), dec_t, zeta_t, xi_t, cd_t,
                         batch, seq)

        base, h2, scores_t = _post_mix(
            attn, ret, proj, x2, mod, norm2_g[l].reshape(1, d),
            w_branch_a[l].astype(BF16), w_branch_b[l].astype(BF16), w_out[l].astype(BF16),
            w_router[l].T, w_sh_gate[l].astype(BF16), w_sh_up[l].astype(BF16),
            w_sh_down[l].astype(BF16), seq)

        top_e, top_w, rank, counts = _route(scores_t,
                                            router_bias[l].astype(F32).reshape(N_EXPERTS, 1))
        first, block_expert, block_valid, block_start, n_used = _block_tables(counts, t * TOP_K)
        sorted_pos = _dest_rows(top_e, rank, first).reshape(TOP_K * t)
        order = jnp.concatenate([jnp.argsort(sorted_pos).astype(jnp.int32),
                                 jnp.zeros((ROWS_MOE,), jnp.int32)])
        y = _moe_experts(order, block_expert, block_valid, block_start, n_used,
                         h2.reshape(t, 1, d // 2),
                         w_exp_gate[l], w_exp_up[l], w_exp_down[l])
        x2 = _combine(base, top_w.T, mod, final_g.reshape(1, d), y, seq, l == depth - 1)
    return x2.reshape(batch, seq, d)
```

```python
import functools

import jax
import jax.numpy as jnp
import numpy as np
from jax import lax
from jax.experimental import pallas as pl
from jax.experimental.pallas import tpu as pltpu

F32 = jnp.float32
BF16 = jnp.bfloat16

D_MODEL = 1024
FOX_HEADS = 8
FOX_HEAD_DIM = 64
FOX_WIDTH = 512
RET_HEADS = 8
RET_QK_DIM = 64
RET_V_DIM = 128
RET_QK_WIDTH = 512
RET_V_WIDTH = 1024
RET_CHUNK = 128
ROPE_BASE = 10000.0
N_EXPERTS = 256
TOP_K = 8
N_GROUPS = 8
TOPK_GROUPS = 4
EXPERT_DIM = 256
ROUTED_SCALE = 2.5
NORM_EPS = 1e-6
LOG2_E = 1.4426950408889634

LANES = 128
PACKED_ROWS = D_MODEL // 2 // LANES
VMEM_LIMIT = 56 * 1024 * 1024

COL_RV, COL_RG, COL_GA, COL_GB = 0, 1024, 2048, 3072
COL_FQ, COL_FK, COL_FV, COL_RQ, COL_RK = 4096, 4608, 5120, 5632, 6144
PROJ_WIDTH = 6656

TM_PROJ = 512
TN_PROJ = 512
TQ_ATTN = 512
TK_ATTN = 512
T_RET = 512
TM_POST = 512
T_ROUTE = 512
ROWS_MOE = 256
T_COMBINE = 256


def _sigmoid(z):
    return 1.0 / (1.0 + jnp.exp(-z))


def _silu(z):
    return z * _sigmoid(z)


def _pack_halves(xb):
    n = xb.shape[1] // 2
    lo = lax.bitcast_convert_type(xb[:, :n].astype(F32), jnp.uint32)
    hi = lax.bitcast_convert_type(xb[:, n:].astype(F32), jnp.uint32)
    return (lo >> 16) | hi


def _unpack_halves(xp):
    lo = lax.bitcast_convert_type(xp << 16, F32)
    hi = lax.bitcast_convert_type(xp & jnp.uint32(0xFFFF0000), F32)
    return lo.astype(BF16), hi.astype(BF16)


def _params(*sem):
    return pltpu.CompilerParams(dimension_semantics=sem, vmem_limit_bytes=VMEM_LIMIT)


def _adaln_kernel(c_ref, w_ref, b_ref, o_ref):
    a = _silu(c_ref[...]).astype(BF16)
    o_ref[...] = jnp.dot(a, w_ref[...].astype(BF16), preferred_element_type=F32) + b_ref[...]


def _adaln_mod(c_pad, w_ada, b_ada):
    n = w_ada.shape[1]
    tn = 1024
    return pl.pallas_call(
        _adaln_kernel,
        grid=(n // tn,),
        in_specs=[pl.BlockSpec((8, D_MODEL), lambda j: (0, 0)),
                  pl.BlockSpec((D_MODEL, tn), lambda j: (0, j)),
                  pl.BlockSpec((1, tn), lambda j: (0, j))],
        out_specs=pl.BlockSpec((8, tn), lambda j: (0, j)),
        out_shape=jax.ShapeDtypeStruct((8, n), F32),
        compiler_params=_params("arbitrary"),
        name="adaln_mod",
    )(c_pad, w_ada, b_ada.reshape(1, n))


def _inproj_kernel(x_ref, mod_ref, g_ref, w_ref, wff_ref, bf_ref, cos_ref, sin_ref,
                   proj_ref, logf_ref, *, tiles_per_batch):
    b = pl.program_id(0) // tiles_per_batch
    x = x_ref[...]
    y = x * lax.rsqrt(jnp.mean(x * x, axis=-1, keepdims=True) + NORM_EPS) * g_ref[...]
    shift = mod_ref[pl.ds(b, 1), 0:D_MODEL]
    scale = mod_ref[pl.ds(b, 1), D_MODEL:2 * D_MODEL]
    hb = (y * (1.0 + scale) + shift).astype(BF16)
    z = jnp.dot(hb, wff_ref[...], preferred_element_type=F32) + bf_ref[...]
    logf_ref[...] = jnp.minimum(z, 0.0) - jnp.log1p(jnp.exp(-jnp.abs(z)))

    cos = cos_ref[...]
    sin = sin_ref[...]
    lane = lax.broadcasted_iota(jnp.int32, cos.shape, 1)
    first_half = (lane % RET_QK_DIM) < (RET_QK_DIM // 2)

    for c0 in range(0, PROJ_WIDTH, TN_PROJ):
        cols = slice(c0, c0 + TN_PROJ)
        acc = jnp.dot(hb, w_ref[:, cols], preferred_element_type=F32)
        if COL_GA <= c0 < COL_FQ:
            proj_ref[:, cols] = _sigmoid(acc).astype(BF16)
        elif COL_FQ <= c0 < COL_FK:
            proj_ref[:, cols] = (acc * (LOG2_E * FOX_HEAD_DIM ** -0.5)).astype(BF16)
        elif c0 >= COL_RQ:
            k_scale = RET_QK_DIM ** -0.5 if c0 >= COL_RK else 1.0
            for cb in range(c0, c0 + TN_PROJ, LANES):
                xs = acc[:, cb - c0:cb - c0 + LANES]
                up = pltpu.roll(xs, LANES - RET_QK_DIM // 2, axis=1)
                dn = pltpu.roll(xs, RET_QK_DIM // 2, axis=1)
                rot = xs * cos + jnp.where(first_half, up, dn) * sin
                proj_ref[:, cb:cb + LANES] = (rot * k_scale).astype(BF16)
        else:
            proj_ref[:, cols] = acc.astype(BF16)


def _in_proj(x2, mod, norm_g, w_p, w_ff, b_ff, cos_t, sin_t, seq):
    t = x2.shape[0]
    tiles_per_batch = seq // TM_PROJ
    kern = functools.partial(_inproj_kernel, tiles_per_batch=tiles_per_batch)
    return pl.pallas_call(
        kern,
        grid=(t // TM_PROJ,),
        in_specs=[pl.BlockSpec((TM_PROJ, D_MODEL), lambda i: (i, 0)),
                  pl.BlockSpec(mod.shape, lambda i: (0, 0)),
                  pl.BlockSpec((1, D_MODEL), lambda i: (0, 0)),
                  pl.BlockSpec((D_MODEL, PROJ_WIDTH), lambda i: (0, 0)),
                  pl.BlockSpec((D_MODEL, LANES), lambda i: (0, 0)),
                  pl.BlockSpec((1, LANES), lambda i: (0, 0)),
                  pl.BlockSpec((TM_PROJ, LANES), lambda i: (i % tiles_per_batch, 0)),
                  pl.BlockSpec((TM_PROJ, LANES), lambda i: (i % tiles_per_batch, 0))],
        out_specs=[pl.BlockSpec((TM_PROJ, PROJ_WIDTH), lambda i: (i, 0)),
                   pl.BlockSpec((TM_PROJ, LANES), lambda i: (i, 0))],
        out_shape=[jax.ShapeDtypeStruct((t, PROJ_WIDTH), BF16),
                   jax.ShapeDtypeStruct((t, LANES), F32)],
        compiler_params=_params("arbitrary"),
        name="in_proj",
    )(x2, mod, norm_g, w_p, w_ff, b_ff, cos_t, sin_t)


def _attn_kernel(q_ref, k_ref, v_ref, f_ref, o_ref):
    hp = pl.program_id(1)
    qi = pl.program_id(2)
    tq, tk = TQ_ATTN, TK_ATTN
    q = q_ref[...]
    lane = lax.broadcasted_iota(jnp.int32, (tq, LANES), 1)
    row = qi * tq + lax.broadcasted_iota(jnp.int32, (tq, tk), 0)
    col = lax.broadcasted_iota(jnp.int32, (tq, tk), 1)
    zeros = jnp.zeros_like(q)
    qm = (jnp.where(lane < FOX_HEAD_DIM, q, zeros), jnp.where(lane >= FOX_HEAD_DIM, q, zeros))

    def step(kb, carry, masked):
        ks = pl.multiple_of(kb * tk, tk)
        k = k_ref[pl.ds(ks, tk), :]
        v = v_ref[pl.ds(ks, tk), :]
        out = []
        for hh in range(2):
            m, l, acc = carry[hh]
            s = lax.dot_general(qm[hh], k, (((1,), (1,)), ((), ())), preferred_element_type=F32)
            s = s - f_ref[0, pl.ds(2 * hp + hh, 1), pl.ds(ks, tk)]
            if masked:
                s = jnp.where(col + ks <= row, s, -jnp.inf)
            m_new = jnp.maximum(m, jnp.max(s, axis=-1, keepdims=True))
            alpha = jnp.exp2(m - m_new)
            p = jnp.exp2(s - m_new)
            l = alpha * l + jnp.sum(p, axis=-1, keepdims=True)
            acc = alpha * acc + jnp.dot(p.astype(BF16), v, preferred_element_type=F32)
            out.append((m_new, l, acc))
        return tuple(out)

    init = (jnp.full((tq, 1), -jnp.inf, F32), jnp.zeros((tq, 1), F32),
            jnp.zeros((tq, LANES), F32))
    n_full = (qi * tq) // tk
    carry = lax.fori_loop(0, n_full, lambda kb, c: step(kb, c, False), (init, init))
    for d in range(max(tq // tk, 1)):
        carry = step(n_full + d, carry, True)
    (_, l0, acc0), (_, l1, acc1) = carry
    o_ref[...] = jnp.where(lane < FOX_HEAD_DIM, acc0 / l0, acc1 / l1).astype(BF16)


def _fox_attn(proj, cum_f, batch, seq):
    t = proj.shape[0]
    nq = seq // TQ_ATTN
    cq, ck, cv = COL_FQ // LANES, COL_FK // LANES, COL_FV // LANES
    return pl.pallas_call(
        _attn_kernel,
        grid=(batch, FOX_HEADS // 2, nq),
        in_specs=[pl.BlockSpec((TQ_ATTN, LANES), lambda b, hp, qi: (b * nq + qi, cq + hp)),
                  pl.BlockSpec((seq, LANES), lambda b, hp, qi: (b, ck + hp)),
                  pl.BlockSpec((seq, LANES), lambda b, hp, qi: (b, cv + hp)),
                  pl.BlockSpec((1, FOX_HEADS, seq), lambda b, hp, qi: (b, 0, 0))],
        out_specs=pl.BlockSpec((TQ_ATTN, LANES), lambda b, hp, qi: (b * nq + qi, hp)),
        out_shape=jax.ShapeDtypeStruct((t, FOX_WIDTH), BF16),
        compiler_params=_params("arbitrary", "arbitrary", "arbitrary"),
        name="fox_attn",
    )(proj, proj, proj, cum_f)


def _ret_kernel(q_ref, k_ref, v_ref, rg_ref, gn_ref, dec_ref, zeta_ref, xi_ref, cd_ref,
                o_ref, st_sc):
    ri = pl.program_id(1)
    c = RET_CHUNK

    @pl.when(ri == 0)
    def _():
        st_sc[...] = jnp.zeros_like(st_sc)

    lane = lax.broadcasted_iota(jnp.int32, (c, LANES), 1)
    zeros = jnp.zeros((c, LANES), BF16)
    states = [st_sc[h] for h in range(RET_HEADS)]
    for n in range(T_RET // c):
        rows = slice(n * c, (n + 1) * c)
        for h in range(RET_HEADS):
            pair = slice((h // 2) * LANES, (h // 2 + 1) * LANES)
            vcol = slice(h * RET_V_DIM, (h + 1) * RET_V_DIM)
            sel = (lane < RET_QK_DIM) if h % 2 == 0 else (lane >= RET_QK_DIM)
            qc = jnp.where(sel, q_ref[rows, pair], zeros)
            kc = jnp.where(sel, k_ref[rows, pair], zeros)
            vc = v_ref[rows, vcol]
            sc = lax.dot_general(qc, kc, (((1,), (1,)), ((), ())),
                                 preferred_element_type=F32) * dec_ref[h]
            intra = jnp.dot(sc.astype(BF16), vc, preferred_element_type=F32)
            cross = jnp.dot(qc, states[h].astype(BF16), preferred_element_type=F32) * xi_ref[h]
            o = intra + cross
            kz = (kc.astype(F32) * zeta_ref[h]).astype(BF16)
            kv = lax.dot_general(kz, vc, (((0,), (0,)), ((), ())), preferred_element_type=F32)
            states[h] = cd_ref[h] * states[h] + kv
            mu = jnp.mean(o, axis=-1, keepdims=True)
            d = o - mu
            var = jnp.mean(d * d, axis=-1, keepdims=True)
            on = d * lax.rsqrt(var + NORM_EPS) * gn_ref[:, vcol]
            rg = rg_ref[rows, vcol].astype(F32)
            o_ref[rows, vcol] = (on * _silu(rg)).astype(BF16)
    for h in range(RET_HEADS):
        st_sc[h] = states[h]


def _retention(proj, gn_g, dec_t, zeta_t, xi_t, cd_t, batch, seq):
    t = proj.shape[0]
    nr = seq // T_RET
    tok = lambda b, ri: b * nr + ri
    tab = pl.BlockSpec((RET_HEADS, RET_CHUNK, LANES), lambda b, ri: (0, 0, 0))
    return pl.pallas_call(
        _ret_kernel,
        grid=(batch, nr),
        in_specs=[pl.BlockSpec((T_RET, RET_QK_WIDTH), lambda b, ri: (tok(b, ri), COL_RQ // RET_QK_WIDTH)),
                  pl.BlockSpec((T_RET, RET_QK_WIDTH), lambda b, ri: (tok(b, ri), COL_RK // RET_QK_WIDTH)),
                  pl.BlockSpec((T_RET, RET_V_WIDTH), lambda b, ri: (tok(b, ri), COL_RV // RET_V_WIDTH)),
                  pl.BlockSpec((T_RET, RET_V_WIDTH), lambda b, ri: (tok(b, ri), COL_RG // RET_V_WIDTH)),
                  pl.BlockSpec((1, RET_V_WIDTH), lambda b, ri: (0, 0)),
                  tab, tab, tab, tab],
        out_specs=pl.BlockSpec((T_RET, RET_V_WIDTH), lambda b, ri: (tok(b, ri), 0)),
        out_shape=jax.ShapeDtypeStruct((t, RET_V_WIDTH), BF16),
        scratch_shapes=[pltpu.VMEM((RET_HEADS, LANES, RET_V_DIM), F32)],
        compiler_params=_params("arbitrary", "arbitrary"),
        name="retention",
    )(proj, proj, proj, proj, gn_g, dec_t, zeta_t, xi_t, cd_t)


def _retention_tables():
    c = RET_CHUNK
    log_gamma = jnp.log1p(-jnp.exp2(-5.0 - jnp.arange(RET_HEADS, dtype=F32)))
    idx = jnp.arange(c, dtype=F32)
    diff = idx[:, None] - idx[None, :]
    dec = jnp.where(diff >= 0,
                    jnp.exp(jnp.maximum(diff, 0.0)[None] * log_gamma[:, None, None]), 0.0)
    zeta = jnp.exp((c - 1.0 - idx)[None, :] * log_gamma[:, None])
    xi = jnp.exp((idx + 1.0)[None, :] * log_gamma[:, None])
    cd = jnp.exp(c * log_gamma)
    bc = lambda v: jnp.broadcast_to(v[:, :, None], (RET_HEADS, c, LANES))
    cd_t = jnp.broadcast_to(cd[:, None, None], (RET_HEADS, LANES, RET_V_DIM))
    return dec, bc(zeta), bc(xi), cd_t


def _post_kernel(attn_ref, ret_ref, ga_ref, gb_ref, x_ref, mod_ref, g2_ref,
                 wa_ref, wb_ref, wo_ref, wr_ref, wsg_ref, wsu_ref, wsd_ref,
                 base_ref, h2_ref, score_ref, *, tiles_per_batch):
    b = pl.program_id(0) // tiles_per_batch
    ya = jnp.dot(attn_ref[...], wa_ref[...], preferred_element_type=F32)
    yb = jnp.dot(ret_ref[...], wb_ref[...], preferred_element_type=F32)
    merged = ga_ref[...].astype(F32) * ya + gb_ref[...].astype(F32) * yb
    mix = jnp.dot(merged.astype(BF16), wo_ref[...], preferred_element_type=F32)
    d = D_MODEL
    gate1 = mod_ref[pl.ds(b, 1), 2 * d:3 * d]
    shift2 = mod_ref[pl.ds(b, 1), 3 * d:4 * d]
    scale2 = mod_ref[pl.ds(b, 1), 4 * d:5 * d]
    gate2 = mod_ref[pl.ds(b, 1), 5 * d:6 * d]
    x1 = x_ref[...] + gate1 * mix
    y = x1 * lax.rsqrt(jnp.mean(x1 * x1, axis=-1, keepdims=True) + NORM_EPS) * g2_ref[...]
    h2 = y * (1.0 + scale2) + shift2
    logits_t = lax.dot_general(wr_ref[...], h2, (((1,), (1,)), ((), ())),
                               preferred_element_type=F32, precision=lax.Precision.HIGHEST)
    score_ref[...] = _sigmoid(logits_t)
    h2b = h2.astype(BF16)
    h2p = _pack_halves(h2b)
    n_chunks = h2p.shape[1] // LANES
    for cb in range(n_chunks):
        h2_ref[pl.ds(cb, h2p.shape[0], stride=n_chunks), :] = h2p[:, cb * LANES:(cb + 1) * LANES]
    g = jnp.dot(h2b, wsg_ref[...], preferred_element_type=F32)
    u = jnp.dot(h2b, wsu_ref[...], preferred_element_type=F32)
    shared = jnp.dot((_silu(g) * u).astype(BF16), wsd_ref[...], preferred_element_type=F32)
    base_ref[...] = x1 + gate2 * shared


def _post_mix(attn, ret, proj, x2, mod, norm2_g, wa, wb, wo, wr, wsg, wsu, wsd, seq):
    t = x2.shape[0]
    tm = TM_POST
    kern = functools.partial(_post_kernel, tiles_per_batch=seq // tm)
    full = lambda a: pl.BlockSpec(a.shape, lambda i: (0,) * a.ndim)
    return pl.pallas_call(
        kern,
        grid=(t // tm,),
        in_specs=[pl.BlockSpec((tm, FOX_WIDTH), lambda i: (i, 0)),
                  pl.BlockSpec((tm, RET_V_WIDTH), lambda i: (i, 0)),
                  pl.BlockSpec((tm, D_MODEL), lambda i: (i, COL_GA // D_MODEL)),
                  pl.BlockSpec((tm, D_MODEL), lambda i: (i, COL_GB // D_MODEL)),
                  pl.BlockSpec((tm, D_MODEL), lambda i: (i, 0)),
                  full(mod), full(norm2_g), full(wa), full(wb), full(wo), full(wr),
                  full(wsg), full(wsu), full(wsd)],
        out_specs=[pl.BlockSpec((tm, D_MODEL), lambda i: (i, 0)),
                   pl.BlockSpec((tm * PACKED_ROWS, LANES), lambda i: (i, 0)),
                   pl.BlockSpec((N_EXPERTS, tm), lambda i: (0, i))],
        out_shape=[jax.ShapeDtypeStruct((t, D_MODEL), F32),
                   jax.ShapeDtypeStruct((t * PACKED_ROWS, LANES), jnp.uint32),
                   jax.ShapeDtypeStruct((N_EXPERTS, t), F32)],
        compiler_params=_params("arbitrary"),
        name="post_mix",
    )(attn, ret, proj, proj, x2, mod, norm2_g, wa, wb, wo, wr, wsg, wsu, wsd)


def _route_kernel(s_ref, bias_ref, e_ref, w_ref, rank_ref, cnt_ref):
    tr = s_ref.shape[1]
    gsz = N_EXPERTS // N_GROUPS

    @pl.when(pl.program_id(0) == 0)
    def _():
        cnt_ref[...] = jnp.zeros_like(cnt_ref)

    s = s_ref[...]
    sel = s + bias_ref[...]
    neg = jnp.float32(-jnp.inf)

    g3 = sel.reshape(N_GROUPS, gsz, tr)
    m1 = jnp.max(g3, axis=1)
    n_max = jnp.sum((g3 == m1[:, None, :]).astype(F32), axis=1)
    m2 = jnp.max(jnp.where(g3 < m1[:, None, :], g3, neg), axis=1)
    gs = m1 + jnp.where(n_max >= 2.0, m1, m2)

    gi = lax.broadcasted_iota(jnp.int32, (N_GROUPS, tr), 0)
    beaten = jnp.zeros((N_GROUPS, tr), F32)
    for j in range(N_GROUPS):
        gj = gs[j:j + 1, :]
        beaten = beaten + jnp.where((gj > gs) | ((gj == gs) & (gi > j)), 1.0, 0.0)
    keep = jnp.where(beaten < float(TOPK_GROUPS), 1.0, 0.0)
    keep_e = jnp.broadcast_to(keep[:, None, :], (N_GROUPS, gsz, tr)).reshape(N_EXPERTS, tr)

    ei = lax.broadcasted_iota(jnp.int32, (N_EXPERTS, tr), 0)
    val = jnp.where(keep_e > 0.0, sel, neg)
    member = jnp.zeros((N_EXPERTS, tr), F32)
    idx_rows, w_rows = [], []
    for _ in range(TOP_K):
        m = jnp.max(val, axis=0, keepdims=True)
        idx = jnp.min(jnp.where(val == m, ei, N_EXPERTS), axis=0, keepdims=True)
        hit = ei == idx
        w_rows.append(jnp.sum(jnp.where(hit, s, 0.0), axis=0, keepdims=True))
        idx_rows.append(idx)
        val = jnp.where(hit, neg, val)
        member = jnp.where(hit, 1.0, member)

    w_sum = w_rows[0]
    for k in range(1, TOP_K):
        w_sum = w_sum + w_rows[k]

    r_i = lax.broadcasted_iota(jnp.int32, (tr, tr), 0)
    c_i = lax.broadcasted_iota(jnp.int32, (tr, tr), 1)
    upper = jnp.where(r_i < c_i, 1.0, 0.0).astype(BF16)
    before = jnp.dot(member.astype(BF16), upper, preferred_element_type=F32) + cnt_ref[...]

    for k in range(TOP_K):
        hit = ei == idx_rows[k]
        e_ref[k:k + 1, :] = idx_rows[k]
        w_ref[k:k + 1, :] = w_rows[k] / w_sum * ROUTED_SCALE
        rank_ref[k:k + 1, :] = jnp.sum(jnp.where(hit, before, 0.0), axis=0,
                                       keepdims=True).astype(jnp.int32)
    cnt_ref[...] = cnt_ref[...] + jnp.sum(member, axis=1, keepdims=True)


def _route(scores_t, bias_col):
    t = scores_t.shape[1]
    tr = T_ROUTE
    row8 = lambda dt: jax.ShapeDtypeStruct((TOP_K, t), dt)
    return pl.pallas_call(
        _route_kernel,
        grid=(t // tr,),
        in_specs=[pl.BlockSpec((N_EXPERTS, tr), lambda i: (0, i)),
                  pl.BlockSpec((N_EXPERTS, 1), lambda i: (0, 0))],
        out_specs=[pl.BlockSpec((TOP_K, tr), lambda i: (0, i)),
                   pl.BlockSpec((TOP_K, tr), lambda i: (0, i)),
                   pl.BlockSpec((TOP_K, tr), lambda i: (0, i)),
                   pl.BlockSpec((N_EXPERTS, 1), lambda i: (0, 0))],
        out_shape=[row8(jnp.int32), row8(F32), row8(jnp.int32),
                   jax.ShapeDtypeStruct((N_EXPERTS, 1), F32)],
        compiler_params=_params("arbitrary"),
        name="route",
    )(scores_t, bias_col)


def _dest_kernel(e_ref, rank_ref, start_ref, dest_ref):
    tr = e_ref.shape[1]
    ei = lax.broadcasted_iota(jnp.int32, (N_EXPERTS, tr), 0)
    start = start_ref[...]
    for k in range(TOP_K):
        hit = ei == e_ref[k:k + 1, :]
        off = jnp.sum(jnp.where(hit, start, 0.0), axis=0, keepdims=True)
        dest_ref[k:k + 1, :] = rank_ref[k:k + 1, :] + off.astype(jnp.int32)


def _dest_rows(top_e, rank, start_col):
    t = top_e.shape[1]
    tr = T_ROUTE
    spec = pl.BlockSpec((TOP_K, tr), lambda i: (0, i))
    return pl.pallas_call(
        _dest_kernel,
        grid=(t // tr,),
        in_specs=[spec, spec, pl.BlockSpec((N_EXPERTS, 1), lambda i: (0, 0))],
        out_specs=spec,
        out_shape=jax.ShapeDtypeStruct((TOP_K, t), jnp.int32),
        compiler_params=_params("arbitrary"),
        name="dest_rows",
    )(top_e, rank, start_col)


def _moe_kernel(order_ref, be_ref, nv_ref, cs_ref, nu_ref, h_ref, wg_ref, wu_ref, wd_ref, y_ref,
                hbuf, xb0, xb1, yb0, yb1, wg_sc, wu_sc, wd_sc, hsem, ssem, *, n_tokens, n_assign):
    i = pl.program_id(0)
    n_steps = pl.num_programs(0)
    r = ROWS_MOE
    half = D_MODEL // 2
    active = i <= nu_ref[0]

    def copy_rows(blk, xdst):
        base = cs_ref[blk]
        for j in range(r):
            tok = order_ref[base + j] & (n_tokens - 1)
            xdst[j // 8, pl.ds(j % 8, 1), :] = hbuf[tok]

    def compute(blk, xsrc, ydst):
        row = lax.broadcasted_iota(jnp.int32, (r, half), 0)
        xp = jnp.where(row < nv_ref[blk], xsrc[...].reshape(r, half), jnp.uint32(0))
        x_lo, x_hi = _unpack_halves(xp)
        g = (jnp.dot(x_lo, wg_sc[:half, :], preferred_element_type=F32)
             + jnp.dot(x_hi, wg_sc[half:, :], preferred_element_type=F32))
        u = (jnp.dot(x_lo, wu_sc[:half, :], preferred_element_type=F32)
             + jnp.dot(x_hi, wu_sc[half:, :], preferred_element_type=F32))
        y = jnp.dot((_silu(g) * u).astype(BF16), wd_sc[...], preferred_element_type=F32)
        ydst[...] = _pack_halves(y.astype(BF16)).reshape(r // 8, 8, half)

    def scatter(blk, n_rows, ysrc, spare):
        base = cs_ref[blk]
        for j in range(r):
            dst = jnp.where(j < n_rows, order_ref[base + j], spare + j)
            pltpu.make_async_copy(ysrc.at[j // 8, pl.ds(j % 8, 1), :], y_ref.at[dst],
                                  ssem).start(priority=j % 2)

    def scatter_wait():
        pltpu.make_async_copy(y_ref.at[pl.ds(0, r)], y_ref.at[pl.ds(r, r)], ssem).wait()

    @pl.when(i == 0)
    def _():
        load = pltpu.make_async_copy(h_ref, hbuf, hsem)
        load.start()
        load.wait()
        copy_rows(0, xb0)
        yb1[...] = jnp.zeros_like(yb1)

    @pl.when((i >= 1) & (i - 1 <= nu_ref[0]))
    def _():
        scatter_wait()

    prev = be_ref[jnp.maximum(i - 1, 0)]

    @pl.when((i == 0) | (be_ref[i] != prev))
    def _():
        wg_sc[...] = wg_ref[0].astype(BF16)
        wu_sc[...] = wu_ref[0].astype(BF16)
        wd_sc[...] = wd_ref[0].astype(BF16)

    def step(x_cur, x_next, y_cur, y_prev, spare):
        last = jnp.maximum(i - 1, 0)
        scatter(last, jnp.where(i > 0, nv_ref[last], 0), y_prev, spare)
        copy_rows(i + 1, x_next)
        compute(i, x_cur, y_cur)

    @pl.when(active & (i % 2 == 0))
    def _():
        step(xb0, xb1, yb0, yb1, n_assign + r)

    @pl.when(active & (i % 2 == 1))
    def _():
        step(xb1, xb0, yb1, yb0, n_assign)

    @pl.when((i == n_steps - 1) & active)
    def _():
        scatter_wait()


def _moe_experts(order, block_expert, block_valid, block_start, n_used, h2, w_gate, w_up, w_down):
    t = h2.shape[0]
    assert t & (t - 1) == 0, "token count must be a power of two"
    n_blocks = block_expert.shape[0] - 2
    n_assign = TOP_K * t
    wspec = lambda shp: pl.BlockSpec((1,) + shp, lambda i, od, be, nv, cs, nu: (be[i], 0, 0))
    row_buf = pltpu.VMEM((ROWS_MOE // 8, 8, D_MODEL // 2), jnp.uint32)
    grid_spec = pltpu.PrefetchScalarGridSpec(
        num_scalar_prefetch=5,
        grid=(n_blocks + 1,),
        in_specs=[pl.BlockSpec(memory_space=pl.ANY),
                  wspec((D_MODEL, EXPERT_DIM)), wspec((D_MODEL, EXPERT_DIM)),
                  wspec((EXPERT_DIM, D_MODEL))],
        out_specs=pl.BlockSpec(memory_space=pl.ANY),
        scratch_shapes=[pltpu.VMEM((t, 1, D_MODEL // 2), jnp.uint32),
                        row_buf, row_buf, row_buf, row_buf,
                        pltpu.VMEM((D_MODEL, EXPERT_DIM), BF16),
                        pltpu.VMEM((D_MODEL, EXPERT_DIM), BF16),
                        pltpu.VMEM((EXPERT_DIM, D_MODEL), BF16),
                        pltpu.SemaphoreType.DMA,
                        pltpu.SemaphoreType.DMA],
    )
    return pl.pallas_call(
        functools.partial(_moe_kernel, n_tokens=t, n_assign=n_assign),
        grid_spec=grid_spec,
        out_shape=jax.ShapeDtypeStruct((n_assign + 2 * ROWS_MOE, 1, D_MODEL // 2), jnp.uint32),
        compiler_params=_params("arbitrary"),
        name="moe_experts",
    )(order, block_expert, block_valid, block_start, n_used, h2, w_gate, w_up, w_down)


def _combine_kernel(base_ref, w_ref, mod_ref, g_ref, *refs, tiles_per_batch, last_layer):
    y_refs, o_ref = refs[:TOP_K], refs[TOP_K]
    tc = base_ref.shape[0]
    lo = [None] * PACKED_ROWS
    hi = [None] * PACKED_ROWS
    for k in range(TOP_K):
        wk = w_ref[:, k:k + 1]
        for cb in range(PACKED_ROWS):
            yp = y_refs[k][pl.ds(cb, tc, stride=PACKED_ROWS), :]
            y_lo = lax.bitcast_convert_type(yp << 16, F32) * wk
            y_hi = lax.bitcast_convert_type(yp & jnp.uint32(0xFFFF0000), F32) * wk
            lo[cb] = y_lo if lo[cb] is None else lo[cb] + y_lo
            hi[cb] = y_hi if hi[cb] is None else hi[cb] + y_hi
    routed = jnp.concatenate(lo + hi, axis=1)
    b = pl.program_id(0) // tiles_per_batch
    gate2 = mod_ref[pl.ds(b, 1), 5 * D_MODEL:6 * D_MODEL]
    x = base_ref[...] + gate2 * routed
    if last_layer:
        x = x * lax.rsqrt(jnp.mean(x * x, axis=-1, keepdims=True) + NORM_EPS) * g_ref[...]
    o_ref[...] = x


def _combine(base, top_w, mod, final_g, y, seq, last_layer):
    t = base.shape[0]
    tc = T_COMBINE
    kern = functools.partial(_combine_kernel, tiles_per_batch=seq // tc, last_layer=last_layer)
    y_rows = y.reshape(y.shape[0] * PACKED_ROWS, LANES)
    tiles = t // tc
    y_specs = [pl.BlockSpec((tc * PACKED_ROWS, LANES), lambda i, k=k: (k * tiles + i, 0))
               for k in range(TOP_K)]
    return pl.pallas_call(
        kern,
        grid=(tiles,),
        in_specs=[pl.BlockSpec((tc, D_MODEL), lambda i: (i, 0)),
                  pl.BlockSpec((tc, TOP_K), lambda i: (i, 0)),
                  pl.BlockSpec(mod.shape, lambda i: (0, 0)),
                  pl.BlockSpec((1, D_MODEL), lambda i: (0, 0))] + y_specs,
        out_specs=pl.BlockSpec((tc, D_MODEL), lambda i: (i, 0)),
        out_shape=jax.ShapeDtypeStruct((t, D_MODEL), F32),
        compiler_params=_params("arbitrary"),
        name="combine",
    )(base, top_w, mod, final_g, *([y_rows] * TOP_K))


def _block_tables(counts, n_assign):
    r = ROWS_MOE
    cnt = counts.reshape(N_EXPERTS).astype(jnp.int32)
    first = jnp.cumsum(cnt) - cnt
    nblk = (cnt + r - 1) // r
    blk_end = jnp.cumsum(nblk)
    blk_start = blk_end - nblk
    n_blocks = (n_assign + N_EXPERTS * (r - 1) + r - 1) // r
    bi = jnp.arange(n_blocks + 2, dtype=jnp.int32)
    bc = jnp.minimum(bi, blk_end[-1] - 1)
    block_expert = jnp.minimum(jnp.sum((blk_end[None, :] <= bc[:, None]).astype(jnp.int32), axis=1),
                               N_EXPERTS - 1)
    onehot = (block_expert[:, None] == jnp.arange(N_EXPERTS, dtype=jnp.int32)[None, :])
    pick = lambda v: jnp.sum(jnp.where(onehot, v[None, :], 0), axis=1)
    within = bc - pick(blk_start)
    block_valid = jnp.where(bi < blk_end[-1], jnp.clip(pick(cnt) - within * r, 0, r), 0)
    block_start = pick(first) + within * r
    return (first.astype(F32).reshape(N_EXPERTS, 1), block_expert.astype(jnp.int32),
            block_valid.astype(jnp.int32), block_start.astype(jnp.int32),
            blk_end[-1:].astype(jnp.int32))


def kernel(x, c, w_ada, b_ada, norm1_g, w_in, b_forget, ret_gn_g, w_branch_a, w_branch_b, w_out,
           norm2_g, w_router, router_bias, w_exp_gate, w_exp_up, w_exp_down, w_sh_gate, w_sh_up,
           w_sh_down, final_g):
    batch, seq, d = x.shape
    t = batch * seq
    depth = w_ada.shape[0]
    x2 = x.reshape(t, d)

    half = RET_QK_DIM // 2
    inv_freq = ROPE_BASE ** (-jnp.arange(half, dtype=F32) / half)
    ang = jnp.arange(seq, dtype=F32)[:, None] * inv_freq[None, :]
    cos32, sin32 = jnp.cos(ang), jnp.sin(ang)
    cos_t = jnp.concatenate([cos32, cos32, cos32, cos32], axis=-1)
    sin_t = jnp.concatenate([-sin32, sin32, -sin32, sin32], axis=-1)
    dec_t, zeta_t, xi_t, cd_t = _retention_tables()
    c_pad = jnp.zeros((8, d), F32).at[:batch].set(c)

    for l in range(depth):
        mod = _adaln_mod(c_pad, w_ada[l], b_ada[l])

        w = w_in[l]
        o_fq, o_fk, o_fv, o_ff = 0, FOX_WIDTH, 2 * FOX_WIDTH, 3 * FOX_WIDTH
        o_rq = o_ff + FOX_HEADS
        o_rk = o_rq + RET_QK_WIDTH
        o_rv = o_rk + RET_QK_WIDTH
        o_rg = o_rv + RET_V_WIDTH
        o_ga = o_rg + RET_V_WIDTH
        o_gb = o_ga + D_MODEL
        w_p = jnp.concatenate(
            [w[:, o_rv:o_rg], w[:, o_rg:o_ga], w[:, o_ga:o_gb], w[:, o_gb:o_gb + D_MODEL],
             w[:, o_fq:o_fk], w[:, o_fk:o_fv], w[:, o_fv:o_ff], w[:, o_rq:o_rk], w[:, o_rk:o_rv]],
            axis=1).astype(BF16)
        w_ff = jnp.zeros((d, LANES), BF16).at[:, :FOX_HEADS].set(w[:, o_ff:o_rq].astype(BF16))
        b_ff = jnp.zeros((1, LANES), F32).at[0, :FOX_HEADS].set(b_forget[l].astype(F32))

        proj, logf = _in_proj(x2, mod, norm1_g[l].reshape(1, d), w_p, w_ff, b_ff, cos_t, sin_t, seq)
        cum_f = jnp.cumsum(logf[:, :FOX_HEADS].reshape(batch, seq, FOX_HEADS), axis=1)
        cum_f = cum_f.transpose(0, 2, 1) * LOG2_E

        attn = _fox_attn(proj, cum_f, batch, seq)
        ret = _retention(proj, ret_gn_g[l].reshape(1, RET_V_WIDTH), dec_t, zeta_t, xi_t, cd_t,
                         batch, seq)

        base, h2, scores_t = _post_mix(
            attn, ret, proj, x2, mod, norm2_g[l].reshape(1, d),
            w_branch_a[l].astype(BF16), w_branch_b[l].astype(BF16), w_out[l].astype(BF16),
            w_router[l].T, w_sh_gate[l].astype(BF16), w_sh_up[l].astype(BF16),
            w_sh_down[l].astype(BF16), seq)

        top_e, top_w, rank, counts = _route(scores_t,
                                            router_bias[l].astype(F32).reshape(N_EXPERTS, 1))
        first, block_expert, block_valid, block_start, n_used = _block_tables(counts, t * TOP_K)
        sorted_pos = _dest_rows(top_e, rank, first).reshape(TOP_K * t)
        order = jnp.concatenate([jnp.argsort(sorted_pos).astype(jnp.int32),
                                 jnp.zeros((ROWS_MOE,), jnp.int32)])
        y = _moe_experts(order, block_expert, block_valid, block_start, n_used,
                         h2.reshape(t, 1, d // 2),
                         w_exp_gate[l], w_exp_up[l], w_exp_down[l])
        x2 = _combine(base, top_w.T, mod, final_g.reshape(1, d), y, seq, l == depth - 1)
    return x2.reshape(batch, seq, d)
```

```python
import functools

import jax
import jax.numpy as jnp
import numpy as np
from jax import lax
from jax.experimental import pallas as pl
from jax.experimental.pallas import tpu as pltpu

F32 = jnp.float32
BF16 = jnp.bfloat16

D_MODEL = 1024
FOX_HEADS = 8
FOX_HEAD_DIM = 64
FOX_WIDTH = 512
RET_HEADS = 8
RET_QK_DIM = 64
RET_V_DIM = 128
RET_QK_WIDTH = 512
RET_V_WIDTH = 1024
RET_CHUNK = 256
ROPE_BASE = 10000.0
N_EXPERTS = 256
TOP_K = 8
N_GROUPS = 8
TOPK_GROUPS = 4
EXPERT_DIM = 256
ROUTED_SCALE = 2.5
NORM_EPS = 1e-6
LOG2_E = 1.4426950408889634

LANES = 128
PACKED_ROWS = D_MODEL // 2 // LANES
VMEM_LIMIT = 56 * 1024 * 1024

COL_RV, COL_RG, COL_GA, COL_GB = 0, 1024, 2048, 3072
COL_FQ, COL_FK, COL_FV, COL_RQ, COL_RK = 4096, 4608, 5120, 5632, 6144
PROJ_WIDTH = 6656

TM_PROJ = 512
TN_PROJ = 512
TQ_ATTN = 512
TK_ATTN = 512
T_RET = 512
TM_POST = 512
T_ROUTE = 512
ROWS_MOE = 256
MOE_BUFS = 3
T_COMBINE = 256


def _sigmoid(z):
    return 1.0 / (1.0 + jnp.exp(-z))


def _silu(z):
    return z * _sigmoid(z)


def _pack_halves(xb):
    n = xb.shape[1] // 2
    lo = lax.bitcast_convert_type(xb[:, :n].astype(F32), jnp.uint32)
    hi = lax.bitcast_convert_type(xb[:, n:].astype(F32), jnp.uint32)
    return (lo >> 16) | hi


def _unpack_halves(xp):
    lo = lax.bitcast_convert_type(xp << 16, F32)
    hi = lax.bitcast_convert_type(xp & jnp.uint32(0xFFFF0000), F32)
    return lo.astype(BF16), hi.astype(BF16)


def _params(*sem):
    return pltpu.CompilerParams(dimension_semantics=sem, vmem_limit_bytes=VMEM_LIMIT)


def _adaln_kernel(c_ref, w_ref, b_ref, o_ref):
    a = _silu(c_ref[...]).astype(BF16)
    o_ref[...] = jnp.dot(a, w_ref[...].astype(BF16), preferred_element_type=F32) + b_ref[...]


def _adaln_mod(c_pad, w_ada, b_ada):
    n = w_ada.shape[1]
    tn = 1024
    return pl.pallas_call(
        _adaln_kernel,
        grid=(n // tn,),
        in_specs=[pl.BlockSpec((8, D_MODEL), lambda j: (0, 0)),
                  pl.BlockSpec((D_MODEL, tn), lambda j: (0, j)),
                  pl.BlockSpec((1, tn), lambda j: (0, j))],
        out_specs=pl.BlockSpec((8, tn), lambda j: (0, j)),
        out_shape=jax.ShapeDtypeStruct((8, n), F32),
        compiler_params=_params("arbitrary"),
        name="adaln_mod",
    )(c_pad, w_ada, b_ada.reshape(1, n))


def _inproj_kernel(x_ref, mod_ref, g_ref, w_ref, wff_ref, bf_ref, cos_ref, sin_ref,
                   proj_ref, logf_ref, *, tiles_per_batch):
    b = pl.program_id(0) // tiles_per_batch
    x = x_ref[...]
    y = x * lax.rsqrt(jnp.mean(x * x, axis=-1, keepdims=True) + NORM_EPS) * g_ref[...]
    shift = mod_ref[pl.ds(b, 1), 0:D_MODEL]
    scale = mod_ref[pl.ds(b, 1), D_MODEL:2 * D_MODEL]
    hb = (y * (1.0 + scale) + shift).astype(BF16)
    z = jnp.dot(hb, wff_ref[...], preferred_element_type=F32) + bf_ref[...]
    logf_ref[...] = jnp.minimum(z, 0.0) - jnp.log1p(jnp.exp(-jnp.abs(z)))

    cos = cos_ref[...]
    sin = sin_ref[...]
    lane = lax.broadcasted_iota(jnp.int32, cos.shape, 1)
    first_half = (lane % RET_QK_DIM) < (RET_QK_DIM // 2)

    for c0 in range(0, PROJ_WIDTH, TN_PROJ):
        cols = slice(c0, c0 + TN_PROJ)
        acc = jnp.dot(hb, w_ref[:, cols], preferred_element_type=F32)
        if COL_GA <= c0 < COL_FQ:
            proj_ref[:, cols] = _sigmoid(acc).astype(BF16)
        elif COL_FQ <= c0 < COL_FK:
            proj_ref[:, cols] = (acc * (LOG2_E * FOX_HEAD_DIM ** -0.5)).astype(BF16)
        elif c0 >= COL_RQ:
            k_scale = RET_QK_DIM ** -0.5 if c0 >= COL_RK else 1.0
            for cb in range(c0, c0 + TN_PROJ, LANES):
                xs = acc[:, cb - c0:cb - c0 + LANES]
                up = pltpu.roll(xs, LANES - RET_QK_DIM // 2, axis=1)
                dn = pltpu.roll(xs, RET_QK_DIM // 2, axis=1)
                rot = xs * cos + jnp.where(first_half, up, dn) * sin
                proj_ref[:, cb:cb + LANES] = (rot * k_scale).astype(BF16)
        else:
            proj_ref[:, cols] = acc.astype(BF16)


def _in_proj(x2, mod, norm_g, w_p, w_ff, b_ff, cos_t, sin_t, seq):
    t = x2.shape[0]
    tiles_per_batch = seq // TM_PROJ
    kern = functools.partial(_inproj_kernel, tiles_per_batch=tiles_per_batch)
    return pl.pallas_call(
        kern,
        grid=(t // TM_PROJ,),
        in_specs=[pl.BlockSpec((TM_PROJ, D_MODEL), lambda i: (i, 0)),
                  pl.BlockSpec(mod.shape, lambda i: (0, 0)),
                  pl.BlockSpec((1, D_MODEL), lambda i: (0, 0)),
                  pl.BlockSpec((D_MODEL, PROJ_WIDTH), lambda i: (0, 0)),
                  pl.BlockSpec((D_MODEL, LANES), lambda i: (0, 0)),
                  pl.BlockSpec((1, LANES), lambda i: (0, 0)),
                  pl.BlockSpec((TM_PROJ, LANES), lambda i: (i % tiles_per_batch, 0)),
                  pl.BlockSpec((TM_PROJ, LANES), lambda i: (i % tiles_per_batch, 0))],
        out_specs=[pl.BlockSpec((TM_PROJ, PROJ_WIDTH), lambda i: (i, 0)),
                   pl.BlockSpec((TM_PROJ, LANES), lambda i: (i, 0))],
        out_shape=[jax.ShapeDtypeStruct((t, PROJ_WIDTH), BF16),
                   jax.ShapeDtypeStruct((t, LANES), F32)],
        compiler_params=_params("arbitrary"),
        name="in_proj",
    )(x2, mod, norm_g, w_p, w_ff, b_ff, cos_t, sin_t)


def _attn_kernel(q_ref, k_ref, v_ref, f_ref, o_ref):
    hp = pl.program_id(1)
    qi = pl.program_id(2)
    tq, tk = TQ_ATTN, TK_ATTN
    q = q_ref[...]
    lane = lax.broadcasted_iota(jnp.int32, (tq, LANES), 1)
    row = qi * tq + lax.broadcasted_iota(jnp.int32, (tq, tk), 0)
    col = lax.broadcasted_iota(jnp.int32, (tq, tk), 1)
    zeros = jnp.zeros_like(q)
    qm = (jnp.where(lane < FOX_HEAD_DIM, q, zeros), jnp.where(lane >= FOX_HEAD_DIM, q, zeros))

    def step(kb, carry, masked):
        ks = pl.multiple_of(kb * tk, tk)
        k = k_ref[pl.ds(ks, tk), :]
        v = v_ref[pl.ds(ks, tk), :]
        out = []
        for hh in range(2):
            m, l, acc = carry[hh]
            s = lax.dot_general(qm[hh], k, (((1,), (1,)), ((), ())), preferred_element_type=F32)
            s = s - f_ref[0, pl.ds(2 * hp + hh, 1), pl.ds(ks, tk)]
            if masked:
                s = jnp.where(col + ks <= row, s, -jnp.inf)
            m_new = jnp.maximum(m, jnp.max(s, axis=-1, keepdims=True))
            alpha = jnp.exp2(m - m_new)
            p = jnp.exp2(s - m_new)
            l = alpha * l + jnp.sum(p, axis=-1, keepdims=True)
            acc = alpha * acc + jnp.dot(p.astype(BF16), v, preferred_element_type=F32)
            out.append((m_new, l, acc))
        return tuple(out)

    init = (jnp.full((tq, 1), -jnp.inf, F32), jnp.zeros((tq, 1), F32),
            jnp.zeros((tq, LANES), F32))
    n_full = (qi * tq) // tk
    def pair(kp, c):
        return step(2 * kp + 1, step(2 * kp, c, False), False)

    carry = lax.fori_loop(0, n_full // 2, pair, (init, init))
    carry = lax.cond(n_full % 2 == 1, lambda c: step(n_full - 1, c, False), lambda c: c, carry)
    for d in range(max(tq // tk, 1)):
        carry = step(n_full + d, carry, True)
    (_, l0, acc0), (_, l1, acc1) = carry
    o_ref[...] = jnp.where(lane < FOX_HEAD_DIM, acc0 / l0, acc1 / l1).astype(BF16)


def _fox_attn(proj, cum_f, batch, seq):
    t = proj.shape[0]
    nq = seq // TQ_ATTN
    cq, ck, cv = COL_FQ // LANES, COL_FK // LANES, COL_FV // LANES
    return pl.pallas_call(
        _attn_kernel,
        grid=(batch, FOX_HEADS // 2, nq),
        in_specs=[pl.BlockSpec((TQ_ATTN, LANES), lambda b, hp, qi: (b * nq + qi, cq + hp)),
                  pl.BlockSpec((seq, LANES), lambda b, hp, qi: (b, ck + hp)),
                  pl.BlockSpec((seq, LANES), lambda b, hp, qi: (b, cv + hp)),
                  pl.BlockSpec((1, FOX_HEADS, seq), lambda b, hp, qi: (b, 0, 0))],
        out_specs=pl.BlockSpec((TQ_ATTN, LANES), lambda b, hp, qi: (b * nq + qi, hp)),
        out_shape=jax.ShapeDtypeStruct((t, FOX_WIDTH), BF16),
        compiler_params=_params("arbitrary", "arbitrary", "arbitrary"),
        name="fox_attn",
    )(proj, proj, proj, cum_f)


def _ret_kernel(q_ref, k_ref, v_ref, rg_ref, gn_ref, dec_ref, zeta_ref, xi_ref, cd_ref,
                o_ref, st_sc):
    ri = pl.program_id(1)
    c = RET_CHUNK

    @pl.when(ri == 0)
    def _():
        st_sc[...] = jnp.zeros_like(st_sc)

    lane = lax.broadcasted_iota(jnp.int32, (c, LANES), 1)
    zeros = jnp.zeros((c, LANES), BF16)
    states = [st_sc[h] for h in range(RET_HEADS)]
    for n in range(T_RET // c):
        rows = slice(n * c, (n + 1) * c)
        for h in range(RET_HEADS):
            pair = slice((h // 2) * LANES, (h // 2 + 1) * LANES)
            vcol = slice(h * RET_V_DIM, (h + 1) * RET_V_DIM)
            sel = (lane < RET_QK_DIM) if h % 2 == 0 else (lane >= RET_QK_DIM)
            qc = jnp.where(sel, q_ref[rows, pair], zeros)
            kc = jnp.where(sel, k_ref[rows, pair], zeros)
            vc = v_ref[rows, vcol]
            sc = lax.dot_general(qc, kc, (((1,), (1,)), ((), ())),
                                 preferred_element_type=F32) * dec_ref[h]
            intra = jnp.dot(sc.astype(BF16), vc, preferred_element_type=F32)
            cross = jnp.dot(qc, states[h].astype(BF16), preferred_element_type=F32) * xi_ref[h]
            o = intra + cross
            kz = (kc.astype(F32) * zeta_ref[h]).astype(BF16)
            kv = lax.dot_general(kz, vc, (((0,), (0,)), ((), ())), preferred_element_type=F32)
            states[h] = cd_ref[h] * states[h] + kv
            mu = jnp.mean(o, axis=-1, keepdims=True)
            d = o - mu
            var = jnp.mean(d * d, axis=-1, keepdims=True)
            on = d * lax.rsqrt(var + NORM_EPS) * gn_ref[:, vcol]
            rg = rg_ref[rows, vcol].astype(F32)
            o_ref[rows, vcol] = (on * _silu(rg)).astype(BF16)
    for h in range(RET_HEADS):
        st_sc[h] = states[h]


def _retention(proj, gn_g, dec_t, zeta_t, xi_t, cd_t, batch, seq):
    t = proj.shape[0]
    nr = seq // T_RET
    tok = lambda b, ri: b * nr + ri
    full = lambda a: pl.BlockSpec(a.shape, lambda b, ri: (0,) * a.ndim)
    return pl.pallas_call(
        _ret_kernel,
        grid=(batch, nr),
        in_specs=[pl.BlockSpec((T_RET, RET_QK_WIDTH), lambda b, ri: (tok(b, ri), COL_RQ // RET_QK_WIDTH)),
                  pl.BlockSpec((T_RET, RET_QK_WIDTH), lambda b, ri: (tok(b, ri), COL_RK // RET_QK_WIDTH)),
                  pl.BlockSpec((T_RET, RET_V_WIDTH), lambda b, ri: (tok(b, ri), COL_RV // RET_V_WIDTH)),
                  pl.BlockSpec((T_RET, RET_V_WIDTH), lambda b, ri: (tok(b, ri), COL_RG // RET_V_WIDTH)),
                  pl.BlockSpec((1, RET_V_WIDTH), lambda b, ri: (0, 0)),
                  full(dec_t), full(zeta_t), full(xi_t), full(cd_t)],
        out_specs=pl.BlockSpec((T_RET, RET_V_WIDTH), lambda b, ri: (tok(b, ri), 0)),
        out_shape=jax.ShapeDtypeStruct((t, RET_V_WIDTH), BF16),
        scratch_shapes=[pltpu.VMEM((RET_HEADS, LANES, RET_V_DIM), F32)],
        compiler_params=_params("arbitrary", "arbitrary"),
        name="retention",
    )(proj, proj, proj, proj, gn_g, dec_t, zeta_t, xi_t, cd_t)


def _retention_tables():
    c = RET_CHUNK
    log_gamma = jnp.log1p(-jnp.exp2(-5.0 - jnp.arange(RET_HEADS, dtype=F32)))
    idx = jnp.arange(c, dtype=F32)
    diff = idx[:, None] - idx[None, :]
    dec = jnp.where(diff >= 0,
                    jnp.exp(jnp.maximum(diff, 0.0)[None] * log_gamma[:, None, None]), 0.0)
    zeta = jnp.exp((c - 1.0 - idx)[None, :] * log_gamma[:, None])
    xi = jnp.exp((idx + 1.0)[None, :] * log_gamma[:, None])
    cd = jnp.exp(c * log_gamma)
    bc = lambda v: jnp.broadcast_to(v[:, :, None], (RET_HEADS, c, LANES))
    cd_t = jnp.broadcast_to(cd[:, None, None], (RET_HEADS, LANES, RET_V_DIM))
    return dec, bc(zeta), bc(xi), cd_t


def _post_kernel(attn_ref, ret_ref, ga_ref, gb_ref, x_ref, mod_ref, g2_ref,
                 wa_ref, wb_ref, wo_ref, wr_ref, wrr_ref, wsg_ref, wsu_ref, wsd_ref,
                 base_ref, h2_ref, score_ref, *, tiles_per_batch):
    b = pl.program_id(0) // tiles_per_batch
    ya = jnp.dot(attn_ref[...], wa_ref[...], preferred_element_type=F32)
    yb = jnp.dot(ret_ref[...], wb_ref[...], preferred_element_type=F32)
    merged = ga_ref[...].astype(F32) * ya + gb_ref[...].astype(F32) * yb
    mix = jnp.dot(merged.astype(BF16), wo_ref[...], preferred_element_type=F32)
    d = D_MODEL
    gate1 = mod_ref[pl.ds(b, 1), 2 * d:3 * d]
    shift2 = mod_ref[pl.ds(b, 1), 3 * d:4 * d]
    scale2 = mod_ref[pl.ds(b, 1), 4 * d:5 * d]
    gate2 = mod_ref[pl.ds(b, 1), 5 * d:6 * d]
    x1 = x_ref[...] + gate1 * mix
    y = x1 * lax.rsqrt(jnp.mean(x1 * x1, axis=-1, keepdims=True) + NORM_EPS) * g2_ref[...]
    h2 = y * (1.0 + scale2) + shift2
    h2b = h2.astype(BF16)
    h2r = (h2 - h2b.astype(F32)).astype(BF16)
    nt = (((1,), (1,)), ((), ()))
    logits_t = (lax.dot_general(wr_ref[...], h2b, nt, preferred_element_type=F32)
                + lax.dot_general(wr_ref[...], h2r, nt, preferred_element_type=F32)
                + lax.dot_general(wrr_ref[...], h2b, nt, preferred_element_type=F32))
    score_ref[...] = _sigmoid(logits_t)
    h2p = _pack_halves(h2b)
    n_chunks = h2p.shape[1] // LANES
    for cb in range(n_chunks):
        h2_ref[pl.ds(cb, h2p.shape[0], stride=n_chunks), :] = h2p[:, cb * LANES:(cb + 1) * LANES]
    g = jnp.dot(h2b, wsg_ref[...], preferred_element_type=F32)
    u = jnp.dot(h2b, wsu_ref[...], preferred_element_type=F32)
    shared = jnp.dot((_silu(g) * u).astype(BF16), wsd_ref[...], preferred_element_type=F32)
    base_ref[...] = x1 + gate2 * shared


def _post_mix(attn, ret, proj, x2, mod, norm2_g, wa, wb, wo, wr, wrr, wsg, wsu, wsd, seq):
    t = x2.shape[0]
    tm = TM_POST
    kern = functools.partial(_post_kernel, tiles_per_batch=seq // tm)
    full = lambda a: pl.BlockSpec(a.shape, lambda i: (0,) * a.ndim)
    return pl.pallas_call(
        kern,
        grid=(t // tm,),
        in_specs=[pl.BlockSpec((tm, FOX_WIDTH), lambda i: (i, 0)),
                  pl.BlockSpec((tm, RET_V_WIDTH), lambda i: (i, 0)),
                  pl.BlockSpec((tm, D_MODEL), lambda i: (i, COL_GA // D_MODEL)),
                  pl.BlockSpec((tm, D_MODEL), lambda i: (i, COL_GB // D_MODEL)),
                  pl.BlockSpec((tm, D_MODEL), lambda i: (i, 0)),
                  full(mod), full(norm2_g), full(wa), full(wb), full(wo), full(wr), full(wrr),
                  full(wsg), full(wsu), full(wsd)],
        out_specs=[pl.BlockSpec((tm, D_MODEL), lambda i: (i, 0)),
                   pl.BlockSpec((tm * PACKED_ROWS, LANES), lambda i: (i, 0)),
                   pl.BlockSpec((N_EXPERTS, tm), lambda i: (0, i))],
        out_shape=[jax.ShapeDtypeStruct((t, D_MODEL), F32),
                   jax.ShapeDtypeStruct((t * PACKED_ROWS, LANES), jnp.uint32),
                   jax.ShapeDtypeStruct((N_EXPERTS, t), F32)],
        compiler_params=_params("arbitrary"),
        name="post_mix",
    )(attn, ret, proj, proj, x2, mod, norm2_g, wa, wb, wo, wr, wrr, wsg, wsu, wsd)


def _route_kernel(s_ref, bias_ref, e_ref, w_ref, rank_ref, cnt_ref):
    tr = s_ref.shape[1]
    gsz = N_EXPERTS // N_GROUPS

    @pl.when(pl.program_id(0) == 0)
    def _():
        cnt_ref[...] = jnp.zeros_like(cnt_ref)

    s = s_ref[...]
    sel = s + bias_ref[...]
    neg = jnp.float32(-jnp.inf)

    g3 = sel.reshape(N_GROUPS, gsz, tr)
    m1 = jnp.max(g3, axis=1)
    n_max = jnp.sum((g3 == m1[:, None, :]).astype(F32), axis=1)
    m2 = jnp.max(jnp.where(g3 < m1[:, None, :], g3, neg), axis=1)
    gs = m1 + jnp.where(n_max >= 2.0, m1, m2)

    gi = lax.broadcasted_iota(jnp.int32, (N_GROUPS, tr), 0)
    beaten = jnp.zeros((N_GROUPS, tr), F32)
    for j in range(N_GROUPS):
        gj = gs[j:j + 1, :]
        beaten = beaten + jnp.where((gj > gs) | ((gj == gs) & (gi > j)), 1.0, 0.0)
    keep = jnp.where(beaten < float(TOPK_GROUPS), 1.0, 0.0)
    keep_e = jnp.broadcast_to(keep[:, None, :], (N_GROUPS, gsz, tr)).reshape(N_EXPERTS, tr)

    ei = lax.broadcasted_iota(jnp.int32, (N_EXPERTS, tr), 0)
    val = jnp.where(keep_e > 0.0, sel, neg)
    member = jnp.zeros((N_EXPERTS, tr), F32)
    idx_rows, w_rows = [], []
    for _ in range(TOP_K):
        m = jnp.max(val, axis=0, keepdims=True)
        idx = jnp.min(jnp.where(val == m, ei, N_EXPERTS), axis=0, keepdims=True)
        hit = ei == idx
        w_rows.append(jnp.sum(jnp.where(hit, s, 0.0), axis=0, keepdims=True))
        idx_rows.append(idx)
        val = jnp.where(hit, neg, val)
        member = jnp.where(hit, 1.0, member)

    w_sum = w_rows[0]
    for k in range(1, TOP_K):
        w_sum = w_sum + w_rows[k]

    r_i = lax.broadcasted_iota(jnp.int32, (tr, tr), 0)
    c_i = lax.broadcasted_iota(jnp.int32, (tr, tr), 1)
    upper = jnp.where(r_i < c_i, 1.0, 0.0).astype(BF16)
    before = jnp.dot(member.astype(BF16), upper, preferred_element_type=F32) + cnt_ref[...]

    for k in range(TOP_K):
        hit = ei == idx_rows[k]
        e_ref[k:k + 1, :] = idx_rows[k]
        w_ref[k:k + 1, :] = w_rows[k] / w_sum * ROUTED_SCALE
        rank_ref[k:k + 1, :] = jnp.sum(jnp.where(hit, before, 0.0), axis=0,
                                       keepdims=True).astype(jnp.int32)
    cnt_ref[...] = cnt_ref[...] + jnp.sum(member, axis=1, keepdims=True)


def _route(scores_t, bias_col):
    t = scores_t.shape[1]
    tr = T_ROUTE
    row8 = lambda dt: jax.ShapeDtypeStruct((TOP_K, t), dt)
    return pl.pallas_call(
        _route_kernel,
        grid=(t // tr,),
        in_specs=[pl.BlockSpec((N_EXPERTS, tr), lambda i: (0, i)),
                  pl.BlockSpec((N_EXPERTS, 1), lambda i: (0, 0))],
        out_specs=[pl.BlockSpec((TOP_K, tr), lambda i: (0, i)),
                   pl.BlockSpec((TOP_K, tr), lambda i: (0, i)),
                   pl.BlockSpec((TOP_K, tr), lambda i: (0, i)),
                   pl.BlockSpec((N_EXPERTS, 1), lambda i: (0, 0))],
        out_shape=[row8(jnp.int32), row8(F32), row8(jnp.int32),
                   jax.ShapeDtypeStruct((N_EXPERTS, 1), F32)],
        compiler_params=_params("arbitrary"),
        name="route",
    )(scores_t, bias_col)


def _dest_kernel(e_ref, rank_ref, start_ref, dest_ref):
    tr = e_ref.shape[1]
    ei = lax.broadcasted_iota(jnp.int32, (N_EXPERTS, tr), 0)
    start = start_ref[...]
    for k in range(TOP_K):
        hit = ei == e_ref[k:k + 1, :]
        off = jnp.sum(jnp.where(hit, start, 0.0), axis=0, keepdims=True)
        dest_ref[k:k + 1, :] = rank_ref[k:k + 1, :] + off.astype(jnp.int32)


def _dest_rows(top_e, rank, start_col):
    t = top_e.shape[1]
    tr = T_ROUTE
    spec = pl.BlockSpec((TOP_K, tr), lambda i: (0, i))
    return pl.pallas_call(
        _dest_kernel,
        grid=(t // tr,),
        in_specs=[spec, spec, pl.BlockSpec((N_EXPERTS, 1), lambda i: (0, 0))],
        out_specs=spec,
        out_shape=jax.ShapeDtypeStruct((TOP_K, t), jnp.int32),
        compiler_params=_params("arbitrary"),
        name="dest_rows",
    )(top_e, rank, start_col)


def _moe_kernel(order_ref, be_ref, nv_ref, cs_ref, nu_ref, h_ref, wg_ref, wu_ref, wd_ref, y_ref,
                hbuf, *scratch, n_tokens, n_assign):
    xbufs, ybufs = scratch[:MOE_BUFS], scratch[MOE_BUFS:2 * MOE_BUFS]
    wg_sc, wu_sc, wd_sc, hsem = scratch[2 * MOE_BUFS:2 * MOE_BUFS + 4]
    ssems = scratch[2 * MOE_BUFS + 4:]
    i = pl.program_id(0)
    n_steps = pl.num_programs(0)
    r = ROWS_MOE
    half = D_MODEL // 2
    n_used = nu_ref[0]
    active = i <= n_used

    def copy_rows(blk, xdst):
        base = cs_ref[blk]
        for j in range(r):
            tok = order_ref[base + j] & (n_tokens - 1)
            xdst[j // 8, pl.ds(j % 8, 1), :] = hbuf[tok]

    def compute(blk, xsrc, ydst):
        row = lax.broadcasted_iota(jnp.int32, (r, half), 0)
        xp = jnp.where(row < nv_ref[blk], xsrc[...].reshape(r, half), jnp.uint32(0))
        x_lo, x_hi = _unpack_halves(xp)
        g = (jnp.dot(x_lo, wg_sc[:half, :], preferred_element_type=F32)
             + jnp.dot(x_hi, wg_sc[half:, :], preferred_element_type=F32))
        u = (jnp.dot(x_lo, wu_sc[:half, :], preferred_element_type=F32)
             + jnp.dot(x_hi, wu_sc[half:, :], preferred_element_type=F32))
        y = jnp.dot((_silu(g) * u).astype(BF16), wd_sc[...], preferred_element_type=F32)
        ydst[...] = _pack_halves(y.astype(BF16)).reshape(r // 8, 8, half)

    def scatter(blk, n_rows, v):
        base = cs_ref[blk]
        for j in range(r):
            dst = jnp.where(j < n_rows, order_ref[base + j], n_assign + v * r + j)
            pltpu.make_async_copy(ybufs[v].at[j // 8, pl.ds(j % 8, 1), :], y_ref.at[dst],
                                  ssems[v]).start(priority=j % 2)

    def scatter_wait(v):
        pltpu.make_async_copy(y_ref.at[pl.ds(0, r)], y_ref.at[pl.ds(r, r)], ssems[v]).wait()

    def for_buffer(step_no, fn):
        for v in range(MOE_BUFS):
            pl.when(step_no % MOE_BUFS == v)(functools.partial(fn, v))

    @pl.when(i == 0)
    def _():
        load = pltpu.make_async_copy(h_ref, hbuf, hsem)
        load.start()
        load.wait()
        copy_rows(0, xbufs[0])
        zbuf = ybufs[MOE_BUFS - 1]
        zbuf[...] = jnp.zeros_like(zbuf)
        for j in range(MOE_BUFS * r):
            pltpu.make_async_copy(zbuf.at[(j % r) // 8, pl.ds(j % 8, 1), :],
                                  y_ref.at[n_assign + j], hsem).start(priority=j % 2)
        for _ in range(MOE_BUFS):
            pltpu.make_async_copy(y_ref.at[pl.ds(0, r)], y_ref.at[pl.ds(r, r)], hsem).wait()

    @pl.when((i >= 2) & (i - 2 <= n_used))
    def _():
        for_buffer(i + MOE_BUFS - 3, scatter_wait)

    prev = be_ref[jnp.maximum(i - 1, 0)]

    @pl.when((i == 0) | (be_ref[i] != prev))
    def _():
        wg_sc[...] = wg_ref[0].astype(BF16)
        wu_sc[...] = wu_ref[0].astype(BF16)
        wd_sc[...] = wd_ref[0].astype(BF16)

    def step(v):
        last = jnp.maximum(i - 1, 0)
        scatter(last, jnp.where(i > 0, nv_ref[last], 0), (v - 1) % MOE_BUFS)
        copy_rows(i + 1, xbufs[(v + 1) % MOE_BUFS])
        compute(i, xbufs[v], ybufs[v])

    @pl.when(active)
    def _():
        for_buffer(i, step)

    @pl.when(i == n_steps - 1)
    def _():
        @pl.when(i - 1 <= n_used)
        def _():
            for_buffer(i + MOE_BUFS - 2, scatter_wait)

        @pl.when(active)
        def _():
            for_buffer(i + MOE_BUFS - 1, scatter_wait)


def _moe_experts(order, block_expert, block_valid, block_start, n_used, h2, w_gate, w_up, w_down):
    t = h2.shape[0]
    assert t & (t - 1) == 0, "token count must be a power of two"
    n_blocks = block_expert.shape[0] - 2
    n_assign = TOP_K * t
    wspec = lambda shp: pl.BlockSpec((1,) + shp, lambda i, od, be, nv, cs, nu: (be[i], 0, 0))
    row_buf = pltpu.VMEM((ROWS_MOE // 8, 8, D_MODEL // 2), jnp.uint32)
    grid_spec = pltpu.PrefetchScalarGridSpec(
        num_scalar_prefetch=5,
        grid=(n_blocks + 1,),
        in_specs=[pl.BlockSpec(memory_space=pl.ANY),
                  wspec((D_MODEL, EXPERT_DIM)), wspec((D_MODEL, EXPERT_DIM)),
                  wspec((EXPERT_DIM, D_MODEL))],
        out_specs=pl.BlockSpec(memory_space=pl.ANY),
        scratch_shapes=([pltpu.VMEM((t, 1, D_MODEL // 2), jnp.uint32)]
                        + [row_buf] * (2 * MOE_BUFS)
                        + [pltpu.VMEM((D_MODEL, EXPERT_DIM), BF16),
                           pltpu.VMEM((D_MODEL, EXPERT_DIM), BF16),
                           pltpu.VMEM((EXPERT_DIM, D_MODEL), BF16),
                           pltpu.SemaphoreType.DMA]
                        + [pltpu.SemaphoreType.DMA] * MOE_BUFS),
    )
    return pl.pallas_call(
        functools.partial(_moe_kernel, n_tokens=t, n_assign=n_assign),
        grid_spec=grid_spec,
        out_shape=jax.ShapeDtypeStruct((n_assign + MOE_BUFS * ROWS_MOE, 1, D_MODEL // 2),
                                       jnp.uint32),
        compiler_params=_params("arbitrary"),
        name="moe_experts",
    )(order, block_expert, block_valid, block_start, n_used, h2, w_gate, w_up, w_down)


def _combine_kernel(base_ref, w_ref, mod_ref, g_ref, *refs, tiles_per_batch, last_layer):
    y_refs, o_ref = refs[:TOP_K], refs[TOP_K]
    tc = base_ref.shape[0]
    lo = [None] * PACKED_ROWS
    hi = [None] * PACKED_ROWS
    for k in range(TOP_K):
        wk = w_ref[:, k:k + 1]
        for cb in range(PACKED_ROWS):
            yp = y_refs[k][pl.ds(cb, tc, stride=PACKED_ROWS), :]
            y_lo = lax.bitcast_convert_type(yp << 16, F32) * wk
            y_hi = lax.bitcast_convert_type(yp & jnp.uint32(0xFFFF0000), F32) * wk
            lo[cb] = y_lo if lo[cb] is None else lo[cb] + y_lo
            hi[cb] = y_hi if hi[cb] is None else hi[cb] + y_hi
    routed = jnp.concatenate(lo + hi, axis=1)
    b = pl.program_id(0) // tiles_per_batch
    gate2 = mod_ref[pl.ds(b, 1), 5 * D_MODEL:6 * D_MODEL]
    x = base_ref[...] + gate2 * routed
    if last_layer:
        x = x * lax.rsqrt(jnp.mean(x * x, axis=-1, keepdims=True) + NORM_EPS) * g_ref[...]
    o_ref[...] = x


def _combine(base, top_w, mod, final_g, y, seq, last_layer):
    t = base.shape[0]
    tc = T_COMBINE
    kern = functools.partial(_combine_kernel, tiles_per_batch=seq // tc, last_layer=last_layer)
    y_rows = y.reshape(y.shape[0] * PACKED_ROWS, LANES)
    tiles = t // tc
    y_specs = [pl.BlockSpec((tc * PACKED_ROWS, LANES), lambda i, k=k: (k * tiles + i, 0))
               for k in range(TOP_K)]
    return pl.pallas_call(
        kern,
        grid=(tiles,),
        in_specs=[pl.BlockSpec((tc, D_MODEL), lambda i: (i, 0)),
                  pl.BlockSpec((tc, TOP_K), lambda i: (i, 0)),
                  pl.BlockSpec(mod.shape, lambda i: (0, 0)),
                  pl.BlockSpec((1, D_MODEL), lambda i: (0, 0))] + y_specs,
        out_specs=pl.BlockSpec((tc, D_MODEL), lambda i: (i, 0)),
        out_shape=jax.ShapeDtypeStruct((t, D_MODEL), F32),
        compiler_params=_params("arbitrary"),
        name="combine",
    )(base, top_w, mod, final_g, *([y_rows] * TOP_K))


def _block_tables(counts, n_assign):
    r = ROWS_MOE
    cnt = counts.reshape(N_EXPERTS).astype(jnp.int32)
    first = jnp.cumsum(cnt) - cnt
    nblk = (cnt + r - 1) // r
    blk_end = jnp.cumsum(nblk)
    blk_start = blk_end - nblk
    n_blocks = (n_assign + N_EXPERTS * (r - 1) + r - 1) // r
    bi = jnp.arange(n_blocks + 2, dtype=jnp.int32)
    bc = jnp.minimum(bi, blk_end[-1] - 1)
    block_expert = jnp.minimum(jnp.sum((blk_end[None, :] <= bc[:, None]).astype(jnp.int32), axis=1),
                               N_EXPERTS - 1)
    onehot = (block_expert[:, None] == jnp.arange(N_EXPERTS, dtype=jnp.int32)[None, :])
    pick = lambda v: jnp.sum(jnp.where(onehot, v[None, :], 0), axis=1)
    within = bc - pick(blk_start)
    block_valid = jnp.where(bi < blk_end[-1], jnp.clip(pick(cnt) - within * r, 0, r), 0)
    block_start = pick(first) + within * r
    return (first.astype(F32).reshape(N_EXPERTS, 1), block_expert.astype(jnp.int32),
            block_valid.astype(jnp.int32), block_start.astype(jnp.int32),
            blk_end[-1:].astype(jnp.int32))


def kernel(x, c, w_ada, b_ada, norm1_g, w_in, b_forget, ret_gn_g, w_branch_a, w_branch_b, w_out,
           norm2_g, w_router, router_bias, w_exp_gate, w_exp_up, w_exp_down, w_sh_gate, w_sh_up,
           w_sh_down, final_g):
    batch, seq, d = x.shape
    t = batch * seq
    depth = w_ada.shape[0]
    x2 = x.reshape(t, d)

    half = RET_QK_DIM // 2
    inv_freq = ROPE_BASE ** (-jnp.arange(half, dtype=F32) / half)
    ang = jnp.arange(seq, dtype=F32)[:, None] * inv_freq[None, :]
    cos32, sin32 = jnp.cos(ang), jnp.sin(ang)
    cos_t = jnp.concatenate([cos32, cos32, cos32, cos32], axis=-1)
    sin_t = jnp.concatenate([-sin32, sin32, -sin32, sin32], axis=-1)
    dec_t, zeta_t, xi_t, cd_t = _retention_tables()
    c_pad = jnp.zeros((8, d), F32).at[:batch].set(c)

    for l in range(depth):
        mod = _adaln_mod(c_pad, w_ada[l], b_ada[l])

        w = w_in[l]
        o_fq, o_fk, o_fv, o_ff = 0, FOX_WIDTH, 2 * FOX_WIDTH, 3 * FOX_WIDTH
        o_rq = o_ff + FOX_HEADS
        o_rk = o_rq + RET_QK_WIDTH
        o_rv = o_rk + RET_QK_WIDTH
        o_rg = o_rv + RET_V_WIDTH
        o_ga = o_rg + RET_V_WIDTH
        o_gb = o_ga + D_MODEL
        w_p = jnp.concatenate(
            [w[:, o_rv:o_rg], w[:, o_rg:o_ga], w[:, o_ga:o_gb], w[:, o_gb:o_gb + D_MODEL],
             w[:, o_fq:o_fk], w[:, o_fk:o_fv], w[:, o_fv:o_ff], w[:, o_rq:o_rk], w[:, o_rk:o_rv]],
            axis=1).astype(BF16)
        w_ff = jnp.zeros((d, LANES), BF16).at[:, :FOX_HEADS].set(w[:, o_ff:o_rq].astype(BF16))
        b_ff = jnp.zeros((1, LANES), F32).at[0, :FOX_HEADS].set(b_forget[l].astype(F32))

        proj, logf = _in_proj(x2, mod, norm1_g[l].reshape(1, d), w_p, w_ff, b_ff, cos_t, sin_t, seq)
        cum_f = jnp.cumsum(logf[:, :FOX_HEADS].reshape(batch, seq, FOX_HEADS), axis=1)
        cum_f = cum_f.transpose(0, 2, 1) * LOG2_E

        attn = _fox_attn(proj, cum_f, batch, seq)
        ret = _retention(proj, ret_gn_g[l].reshape(1, RET_V_WIDTH), dec_t, zeta_t, xi_t, cd_t,
                         batch, seq)

        wr_t = w_router[l].T.astype(F32)
        wr_head = wr_t.astype(BF16)
        wr_rest = (wr_t - wr_head.astype(F32)).astype(BF16)
        base, h2, scores_t = _post_mix(
            attn, ret, proj, x2, mod, norm2_g[l].reshape(1, d),
            w_branch_a[l].astype(BF16), w_branch_b[l].astype(BF16), w_out[l].astype(BF16),
            wr_head, wr_rest, w_sh_gate[l].astype(BF16), w_sh_up[l].astype(BF16),
            w_sh_down[l].astype(BF16), seq)

        top_e, top_w, rank, counts = _route(scores_t,
                                            router_bias[l].astype(F32).reshape(N_EXPERTS, 1))
        first, block_expert, block_valid, block_start, n_used = _block_tables(counts, t * TOP_K)
        sorted_pos = _dest_rows(top_e, rank, first).reshape(TOP_K * t)
        order = jnp.concatenate([jnp.argsort(sorted_pos).astype(jnp.int32),
                                 jnp.zeros((ROWS_MOE,), jnp.int32)])
        y = _moe_experts(order, block_expert, block_valid, block_start, n_used,
                         h2.reshape(t, 1, d // 2),
                         w_exp_gate[l], w_exp_up[l], w_exp_down[l])
        x2 = _combine(base, top_w.T, mod, final_g.reshape(1, d), y, seq, l == depth - 1)
    return x2.reshape(batch, seq, d)
```

```python
import functools

import jax
import jax.numpy as jnp
import numpy as np
from jax import lax
from jax.experimental import pallas as pl
from jax.experimental.pallas import tpu as pltpu

F32 = jnp.float32
BF16 = jnp.bfloat16

D_MODEL = 1024
FOX_HEADS = 8
FOX_HEAD_DIM = 64
FOX_WIDTH = 512
RET_HEADS = 8
RET_QK_DIM = 64
RET_V_DIM = 128
RET_QK_WIDTH = 512
RET_V_WIDTH = 1024
RET_CHUNK = 256
ROPE_BASE = 10000.0
N_EXPERTS = 256
TOP_K = 8
N_GROUPS = 8
TOPK_GROUPS = 4
EXPERT_DIM = 256
ROUTED_SCALE = 2.5
NORM_EPS = 1e-6
LOG2_E = 1.4426950408889634

LANES = 128
PACKED_ROWS = D_MODEL // 2 // LANES
VMEM_LIMIT = 56 * 1024 * 1024

COL_RV, COL_RG, COL_GA, COL_GB = 0, 1024, 2048, 3072
COL_FQ, COL_FK, COL_FV, COL_RQ, COL_RK = 4096, 4608, 5120, 5632, 6144
PROJ_WIDTH = 6656

TM_PROJ = 512
TN_PROJ = 512
TQ_ATTN = 512
TK_ATTN = 512
T_RET = 512
TM_POST = 512
T_ROUTE = 512
ROWS_MOE = 256
MOE_BUFS = 3
T_COMBINE = 256


def _sigmoid(z):
    return 1.0 / (1.0 + jnp.exp(-z))


def _silu(z):
    return z * _sigmoid(z)


def _pack_halves(xb):
    n = xb.shape[1] // 2
    lo = lax.bitcast_convert_type(xb[:, :n].astype(F32), jnp.uint32)
    hi = lax.bitcast_convert_type(xb[:, n:].astype(F32), jnp.uint32)
    return (lo >> 16) | hi


def _unpack_halves(xp):
    lo = lax.bitcast_convert_type(xp << 16, F32)
    hi = lax.bitcast_convert_type(xp & jnp.uint32(0xFFFF0000), F32)
    return lo.astype(BF16), hi.astype(BF16)


def _split_bf16(x):
    head = lax.bitcast_convert_type(
        lax.bitcast_convert_type(x, jnp.uint32) & jnp.uint32(0xFFFF0000), F32)
    return head.astype(BF16), (x - head).astype(BF16)


def _params(*sem):
    return pltpu.CompilerParams(dimension_semantics=sem, vmem_limit_bytes=VMEM_LIMIT)


def _adaln_kernel(c_ref, w_ref, b_ref, o_ref):
    a = _silu(c_ref[...]).astype(BF16)
    o_ref[...] = jnp.dot(a, w_ref[...].astype(BF16), preferred_element_type=F32) + b_ref[...]


def _adaln_mod(c_pad, w_ada, b_ada):
    n = w_ada.shape[1]
    tn = 1024
    return pl.pallas_call(
        _adaln_kernel,
        grid=(n // tn,),
        in_specs=[pl.BlockSpec((8, D_MODEL), lambda j: (0, 0)),
                  pl.BlockSpec((D_MODEL, tn), lambda j: (0, j)),
                  pl.BlockSpec((1, tn), lambda j: (0, j))],
        out_specs=pl.BlockSpec((8, tn), lambda j: (0, j)),
        out_shape=jax.ShapeDtypeStruct((8, n), F32),
        compiler_params=_params("arbitrary"),
        name="adaln_mod",
    )(c_pad, w_ada, b_ada.reshape(1, n))


def _inproj_kernel(x_ref, mod_ref, g_ref, w_ref, wff_ref, bf_ref, cos_ref, sin_ref,
                   proj_ref, logf_ref, *, tiles_per_batch):
    b = pl.program_id(0) // tiles_per_batch
    x = x_ref[...]
    y = x * lax.rsqrt(jnp.mean(x * x, axis=-1, keepdims=True) + NORM_EPS) * g_ref[...]
    shift = mod_ref[pl.ds(b, 1), 0:D_MODEL]
    scale = mod_ref[pl.ds(b, 1), D_MODEL:2 * D_MODEL]
    hb = (y * (1.0 + scale) + shift).astype(BF16)
    z = jnp.dot(hb, wff_ref[...], preferred_element_type=F32) + bf_ref[...]
    logf_ref[...] = jnp.minimum(z, 0.0) - jnp.log1p(jnp.exp(-jnp.abs(z)))

    cos = cos_ref[...]
    sin = sin_ref[...]
    lane = lax.broadcasted_iota(jnp.int32, cos.shape, 1)
    first_half = (lane % RET_QK_DIM) < (RET_QK_DIM // 2)

    for c0 in range(0, PROJ_WIDTH, TN_PROJ):
        cols = slice(c0, c0 + TN_PROJ)
        acc = jnp.dot(hb, w_ref[:, cols], preferred_element_type=F32)
        if COL_GA <= c0 < COL_FQ:
            proj_ref[:, cols] = _sigmoid(acc).astype(BF16)
        elif COL_FQ <= c0 < COL_FK:
            proj_ref[:, cols] = (acc * (LOG2_E * FOX_HEAD_DIM ** -0.5)).astype(BF16)
        elif c0 >= COL_RQ:
            k_scale = RET_QK_DIM ** -0.5 if c0 >= COL_RK else 1.0
            for cb in range(c0, c0 + TN_PROJ, LANES):
                xs = acc[:, cb - c0:cb - c0 + LANES]
                up = pltpu.roll(xs, LANES - RET_QK_DIM // 2, axis=1)
                dn = pltpu.roll(xs, RET_QK_DIM // 2, axis=1)
                rot = xs * cos + jnp.where(first_half, up, dn) * sin
                proj_ref[:, cb:cb + LANES] = (rot * k_scale).astype(BF16)
        else:
            proj_ref[:, cols] = acc.astype(BF16)


def _in_proj(x2, mod, norm_g, w_p, w_ff, b_ff, cos_t, sin_t, seq):
    t = x2.shape[0]
    tiles_per_batch = seq // TM_PROJ
    kern = functools.partial(_inproj_kernel, tiles_per_batch=tiles_per_batch)
    return pl.pallas_call(
        kern,
        grid=(t // TM_PROJ,),
        in_specs=[pl.BlockSpec((TM_PROJ, D_MODEL), lambda i: (i, 0)),
                  pl.BlockSpec(mod.shape, lambda i: (0, 0)),
                  pl.BlockSpec((1, D_MODEL), lambda i: (0, 0)),
                  pl.BlockSpec((D_MODEL, PROJ_WIDTH), lambda i: (0, 0)),
                  pl.BlockSpec((D_MODEL, LANES), lambda i: (0, 0)),
                  pl.BlockSpec((1, LANES), lambda i: (0, 0)),
                  pl.BlockSpec((TM_PROJ, LANES), lambda i: (i % tiles_per_batch, 0)),
                  pl.BlockSpec((TM_PROJ, LANES), lambda i: (i % tiles_per_batch, 0))],
        out_specs=[pl.BlockSpec((TM_PROJ, PROJ_WIDTH), lambda i: (i, 0)),
                   pl.BlockSpec((TM_PROJ, LANES), lambda i: (i, 0))],
        out_shape=[jax.ShapeDtypeStruct((t, PROJ_WIDTH), BF16),
                   jax.ShapeDtypeStruct((t, LANES), F32)],
        compiler_params=_params("arbitrary"),
        name="in_proj",
    )(x2, mod, norm_g, w_p, w_ff, b_ff, cos_t, sin_t)


def _attn_kernel(q_ref, k_ref, v_ref, f_ref, o_ref):
    hp = pl.program_id(1)
    qi = pl.program_id(2)
    tq, tk = TQ_ATTN, TK_ATTN
    q = q_ref[...]
    lane = lax.broadcasted_iota(jnp.int32, (tq, LANES), 1)
    row = qi * tq + lax.broadcasted_iota(jnp.int32, (tq, tk), 0)
    col = lax.broadcasted_iota(jnp.int32, (tq, tk), 1)
    zeros = jnp.zeros_like(q)
    qm = (jnp.where(lane < FOX_HEAD_DIM, q, zeros), jnp.where(lane >= FOX_HEAD_DIM, q, zeros))

    def step(kb, carry, masked):
        ks = pl.multiple_of(kb * tk, tk)
        k = k_ref[pl.ds(ks, tk), :]
        v = v_ref[pl.ds(ks, tk), :]
        out = []
        for hh in range(2):
            m, l, acc = carry[hh]
            s = lax.dot_general(qm[hh], k, (((1,), (1,)), ((), ())), preferred_element_type=F32)
            s = s - f_ref[0, pl.ds(2 * hp + hh, 1), pl.ds(ks, tk)]
            if masked:
                s = jnp.where(col + ks <= row, s, -jnp.inf)
            m_new = jnp.maximum(m, jnp.max(s, axis=-1, keepdims=True))
            alpha = jnp.exp2(m - m_new)
            p = jnp.exp2(s - m_new)
            l = alpha * l + jnp.sum(p, axis=-1, keepdims=True)
            acc = alpha * acc + jnp.dot(p.astype(BF16), v, preferred_element_type=F32)
            out.append((m_new, l, acc))
        return tuple(out)

    init = (jnp.full((tq, 1), -jnp.inf, F32), jnp.zeros((tq, 1), F32),
            jnp.zeros((tq, LANES), F32))
    n_full = (qi * tq) // tk
    def pair(kp, c):
        return step(2 * kp + 1, step(2 * kp, c, False), False)

    carry = lax.fori_loop(0, n_full // 2, pair, (init, init))
    carry = lax.cond(n_full % 2 == 1, lambda c: step(n_full - 1, c, False), lambda c: c, carry)
    for d in range(max(tq // tk, 1)):
        carry = step(n_full + d, carry, True)
    (_, l0, acc0), (_, l1, acc1) = carry
    o_ref[...] = jnp.where(lane < FOX_HEAD_DIM, acc0 / l0, acc1 / l1).astype(BF16)


def _fox_attn(proj, cum_f, batch, seq):
    t = proj.shape[0]
    nq = seq // TQ_ATTN
    cq, ck, cv = COL_FQ // LANES, COL_FK // LANES, COL_FV // LANES
    return pl.pallas_call(
        _attn_kernel,
        grid=(batch, FOX_HEADS // 2, nq),
        in_specs=[pl.BlockSpec((TQ_ATTN, LANES), lambda b, hp, qi: (b * nq + qi, cq + hp)),
                  pl.BlockSpec((seq, LANES), lambda b, hp, qi: (b, ck + hp)),
                  pl.BlockSpec((seq, LANES), lambda b, hp, qi: (b, cv + hp)),
                  pl.BlockSpec((1, FOX_HEADS, seq), lambda b, hp, qi: (b, 0, 0))],
        out_specs=pl.BlockSpec((TQ_ATTN, LANES), lambda b, hp, qi: (b * nq + qi, hp)),
        out_shape=jax.ShapeDtypeStruct((t, FOX_WIDTH), BF16),
        compiler_params=_params("arbitrary", "arbitrary", "arbitrary"),
        name="fox_attn",
    )(proj, proj, proj, cum_f)


def _ret_kernel(q_ref, k_ref, v_ref, rg_ref, gn_ref, dec_ref, zeta_ref, xi_ref, cd_ref,
                o_ref, st_sc):
    ri = pl.program_id(1)
    c = RET_CHUNK

    @pl.when(ri == 0)
    def _():
        st_sc[...] = jnp.zeros_like(st_sc)

    lane = lax.broadcasted_iota(jnp.int32, (c, LANES), 1)
    zeros = jnp.zeros((c, LANES), BF16)
    states = [st_sc[h] for h in range(RET_HEADS)]
    for n in range(T_RET // c):
        rows = slice(n * c, (n + 1) * c)
        for h in range(RET_HEADS):
            pair = slice((h // 2) * LANES, (h // 2 + 1) * LANES)
            vcol = slice(h * RET_V_DIM, (h + 1) * RET_V_DIM)
            sel = (lane < RET_QK_DIM) if h % 2 == 0 else (lane >= RET_QK_DIM)
            qc = jnp.where(sel, q_ref[rows, pair], zeros)
            kc = jnp.where(sel, k_ref[rows, pair], zeros)
            vc = v_ref[rows, vcol]
            sc = lax.dot_general(qc, kc, (((1,), (1,)), ((), ())),
                                 preferred_element_type=F32) * dec_ref[h]
            intra = jnp.dot(sc.astype(BF16), vc, preferred_element_type=F32)
            cross = jnp.dot(qc, states[h].astype(BF16), preferred_element_type=F32) * xi_ref[h]
            o = intra + cross
            kz = (kc.astype(F32) * zeta_ref[h]).astype(BF16)
            kv = lax.dot_general(kz, vc, (((0,), (0,)), ((), ())), preferred_element_type=F32)
            states[h] = cd_ref[h] * states[h] + kv
            mu = jnp.mean(o, axis=-1, keepdims=True)
            d = o - mu
            var = jnp.mean(d * d, axis=-1, keepdims=True)
            on = d * lax.rsqrt(var + NORM_EPS) * gn_ref[:, vcol]
            rg = rg_ref[rows, vcol].astype(F32)
            o_ref[rows, vcol] = (on * _silu(rg)).astype(BF16)
    for h in range(RET_HEADS):
        st_sc[h] = states[h]


def _retention(proj, gn_g, dec_t, zeta_t, xi_t, cd_t, batch, seq):
    t = proj.shape[0]
    nr = seq // T_RET
    tok = lambda b, ri: b * nr + ri
    full = lambda a: pl.BlockSpec(a.shape, lambda b, ri: (0,) * a.ndim)
    return pl.pallas_call(
        _ret_kernel,
        grid=(batch, nr),
        in_specs=[pl.BlockSpec((T_RET, RET_QK_WIDTH), lambda b, ri: (tok(b, ri), COL_RQ // RET_QK_WIDTH)),
                  pl.BlockSpec((T_RET, RET_QK_WIDTH), lambda b, ri: (tok(b, ri), COL_RK // RET_QK_WIDTH)),
                  pl.BlockSpec((T_RET, RET_V_WIDTH), lambda b, ri: (tok(b, ri), COL_RV // RET_V_WIDTH)),
                  pl.BlockSpec((T_RET, RET_V_WIDTH), lambda b, ri: (tok(b, ri), COL_RG // RET_V_WIDTH)),
                  pl.BlockSpec((1, RET_V_WIDTH), lambda b, ri: (0, 0)),
                  full(dec_t), full(zeta_t), full(xi_t), full(cd_t)],
        out_specs=pl.BlockSpec((T_RET, RET_V_WIDTH), lambda b, ri: (tok(b, ri), 0)),
        out_shape=jax.ShapeDtypeStruct((t, RET_V_WIDTH), BF16),
        scratch_shapes=[pltpu.VMEM((RET_HEADS, LANES, RET_V_DIM), F32)],
        compiler_params=_params("arbitrary", "arbitrary"),
        name="retention",
    )(proj, proj, proj, proj, gn_g, dec_t, zeta_t, xi_t, cd_t)


def _retention_tables():
    c = RET_CHUNK
    log_gamma = jnp.log1p(-jnp.exp2(-5.0 - jnp.arange(RET_HEADS, dtype=F32)))
    idx = jnp.arange(c, dtype=F32)
    diff = idx[:, None] - idx[None, :]
    dec = jnp.where(diff >= 0,
                    jnp.exp(jnp.maximum(diff, 0.0)[None] * log_gamma[:, None, None]), 0.0)
    zeta = jnp.exp((c - 1.0 - idx)[None, :] * log_gamma[:, None])
    xi = jnp.exp((idx + 1.0)[None, :] * log_gamma[:, None])
    cd = jnp.exp(c * log_gamma)
    bc = lambda v: jnp.broadcast_to(v[:, :, None], (RET_HEADS, c, LANES))
    cd_t = jnp.broadcast_to(cd[:, None, None], (RET_HEADS, LANES, RET_V_DIM))
    return dec, bc(zeta), bc(xi), cd_t


def _post_kernel(attn_ref, ret_ref, ga_ref, gb_ref, x_ref, mod_ref, g2_ref,
                 wa_ref, wb_ref, wo_ref, wr_ref, wrr_ref, wsg_ref, wsu_ref, wsd_ref,
                 base_ref, h2_ref, score_ref, *, tiles_per_batch):
    b = pl.program_id(0) // tiles_per_batch
    ya = jnp.dot(attn_ref[...], wa_ref[...], preferred_element_type=F32)
    yb = jnp.dot(ret_ref[...], wb_ref[...], preferred_element_type=F32)
    merged = ga_ref[...].astype(F32) * ya + gb_ref[...].astype(F32) * yb
    mix = jnp.dot(merged.astype(BF16), wo_ref[...], preferred_element_type=F32)
    d = D_MODEL
    gate1 = mod_ref[pl.ds(b, 1), 2 * d:3 * d]
    shift2 = mod_ref[pl.ds(b, 1), 3 * d:4 * d]
    scale2 = mod_ref[pl.ds(b, 1), 4 * d:5 * d]
    gate2 = mod_ref[pl.ds(b, 1), 5 * d:6 * d]
    x1 = x_ref[...] + gate1 * mix
    y = x1 * lax.rsqrt(jnp.mean(x1 * x1, axis=-1, keepdims=True) + NORM_EPS) * g2_ref[...]
    h2 = y * (1.0 + scale2) + shift2
    h2_head, h2_rest = _split_bf16(h2)
    nt = (((1,), (1,)), ((), ()))
    logits_t = (lax.dot_general(wr_ref[...], h2_head, nt, preferred_element_type=F32)
                + lax.dot_general(wr_ref[...], h2_rest, nt, preferred_element_type=F32)
                + lax.dot_general(wrr_ref[...], h2_head, nt, preferred_element_type=F32))
    score_ref[...] = _sigmoid(logits_t)
    h2b = h2.astype(BF16)
    h2p = _pack_halves(h2b)
    n_chunks = h2p.shape[1] // LANES
    for cb in range(n_chunks):
        h2_ref[pl.ds(cb, h2p.shape[0], stride=n_chunks), :] = h2p[:, cb * LANES:(cb + 1) * LANES]
    g = jnp.dot(h2b, wsg_ref[...], preferred_element_type=F32)
    u = jnp.dot(h2b, wsu_ref[...], preferred_element_type=F32)
    shared = jnp.dot((_silu(g) * u).astype(BF16), wsd_ref[...], preferred_element_type=F32)
    base_ref[...] = x1 + gate2 * shared


def _post_mix(attn, ret, proj, x2, mod, norm2_g, wa, wb, wo, wr, wrr, wsg, wsu, wsd, seq):
    t = x2.shape[0]
    tm = TM_POST
    kern = functools.partial(_post_kernel, tiles_per_batch=seq // tm)
    full = lambda a: pl.BlockSpec(a.shape, lambda i: (0,) * a.ndim)
    return pl.pallas_call(
        kern,
        grid=(t // tm,),
        in_specs=[pl.BlockSpec((tm, FOX_WIDTH), lambda i: (i, 0)),
                  pl.BlockSpec((tm, RET_V_WIDTH), lambda i: (i, 0)),
                  pl.BlockSpec((tm, D_MODEL), lambda i: (i, COL_GA // D_MODEL)),
                  pl.BlockSpec((tm, D_MODEL), lambda i: (i, COL_GB // D_MODEL)),
                  pl.BlockSpec((tm, D_MODEL), lambda i: (i, 0)),
                  full(mod), full(norm2_g), full(wa), full(wb), full(wo), full(wr), full(wrr),
                  full(wsg), full(wsu), full(wsd)],
        out_specs=[pl.BlockSpec((tm, D_MODEL), lambda i: (i, 0)),
                   pl.BlockSpec((tm * PACKED_ROWS, LANES), lambda i: (i, 0)),
                   pl.BlockSpec((N_EXPERTS, tm), lambda i: (0, i))],
        out_shape=[jax.ShapeDtypeStruct((t, D_MODEL), F32),
                   jax.ShapeDtypeStruct((t * PACKED_ROWS, LANES), jnp.uint32),
                   jax.ShapeDtypeStruct((N_EXPERTS, t), F32)],
        compiler_params=_params("arbitrary"),
        name="post_mix",
    )(attn, ret, proj, proj, x2, mod, norm2_g, wa, wb, wo, wr, wrr, wsg, wsu, wsd)


def _route_kernel(s_ref, bias_ref, e_ref, w_ref, rank_ref, cnt_ref):
    tr = s_ref.shape[1]
    gsz = N_EXPERTS // N_GROUPS

    @pl.when(pl.program_id(0) == 0)
    def _():
        cnt_ref[...] = jnp.zeros_like(cnt_ref)

    s = s_ref[...]
    sel = s + bias_ref[...]
    neg = jnp.float32(-jnp.inf)

    g3 = sel.reshape(N_GROUPS, gsz, tr)
    m1 = jnp.max(g3, axis=1)
    n_max = jnp.sum((g3 == m1[:, None, :]).astype(F32), axis=1)
    m2 = jnp.max(jnp.where(g3 < m1[:, None, :], g3, neg), axis=1)
    gs = m1 + jnp.where(n_max >= 2.0, m1, m2)

    gi = lax.broadcasted_iota(jnp.int32, (N_GROUPS, tr), 0)
    beaten = jnp.zeros((N_GROUPS, tr), F32)
    for j in range(N_GROUPS):
        gj = gs[j:j + 1, :]
        beaten = beaten + jnp.where((gj > gs) | ((gj == gs) & (gi > j)), 1.0, 0.0)
    keep = jnp.where(beaten < float(TOPK_GROUPS), 1.0, 0.0)
    keep_e = jnp.broadcast_to(keep[:, None, :], (N_GROUPS, gsz, tr)).reshape(N_EXPERTS, tr)

    ei = lax.broadcasted_iota(jnp.int32, (N_EXPERTS, tr), 0)
    val = jnp.where(keep_e > 0.0, sel, neg)
    member = jnp.zeros((N_EXPERTS, tr), F32)
    idx_rows, w_rows = [], []
    for _ in range(TOP_K):
        m = jnp.max(val, axis=0, keepdims=True)
        idx = jnp.min(jnp.where(val == m, ei, N_EXPERTS), axis=0, keepdims=True)
        hit = ei == idx
        w_rows.append(jnp.sum(jnp.where(hit, s, 0.0), axis=0, keepdims=True))
        idx_rows.append(idx)
        val = jnp.where(hit, neg, val)
        member = jnp.where(hit, 1.0, member)

    w_sum = w_rows[0]
    for k in range(1, TOP_K):
        w_sum = w_sum + w_rows[k]

    r_i = lax.broadcasted_iota(jnp.int32, (tr, tr), 0)
    c_i = lax.broadcasted_iota(jnp.int32, (tr, tr), 1)
    upper = jnp.where(r_i < c_i, 1.0, 0.0).astype(BF16)
    before = jnp.dot(member.astype(BF16), upper, preferred_element_type=F32) + cnt_ref[...]

    for k in range(TOP_K):
        hit = ei == idx_rows[k]
        e_ref[k:k + 1, :] = idx_rows[k]
        w_ref[k:k + 1, :] = w_rows[k] / w_sum * ROUTED_SCALE
        rank_ref[k:k + 1, :] = jnp.sum(jnp.where(hit, before, 0.0), axis=0,
                                       keepdims=True).astype(jnp.int32)
    cnt_ref[...] = cnt_ref[...] + jnp.sum(member, axis=1, keepdims=True)


def _route(scores_t, bias_col):
    t = scores_t.shape[1]
    tr = T_ROUTE
    row8 = lambda dt: jax.ShapeDtypeStruct((TOP_K, t), dt)
    return pl.pallas_call(
        _route_kernel,
        grid=(t // tr,),
        in_specs=[pl.BlockSpec((N_EXPERTS, tr), lambda i: (0, i)),
                  pl.BlockSpec((N_EXPERTS, 1), lambda i: (0, 0))],
        out_specs=[pl.BlockSpec((TOP_K, tr), lambda i: (0, i)),
                   pl.BlockSpec((TOP_K, tr), lambda i: (0, i)),
                   pl.BlockSpec((TOP_K, tr), lambda i: (0, i)),
                   pl.BlockSpec((N_EXPERTS, 1), lambda i: (0, 0))],
        out_shape=[row8(jnp.int32), row8(F32), row8(jnp.int32),
                   jax.ShapeDtypeStruct((N_EXPERTS, 1), F32)],
        compiler_params=_params("arbitrary"),
        name="route",
    )(scores_t, bias_col)


def _dest_kernel(e_ref, rank_ref, start_ref, dest_ref):
    tr = e_ref.shape[1]
    ei = lax.broadcasted_iota(jnp.int32, (N_EXPERTS, tr), 0)
    start = start_ref[...]
    for k in range(TOP_K):
        hit = ei == e_ref[k:k + 1, :]
        off = jnp.sum(jnp.where(hit, start, 0.0), axis=0, keepdims=True)
        dest_ref[k:k + 1, :] = rank_ref[k:k + 1, :] + off.astype(jnp.int32)


def _dest_rows(top_e, rank, start_col):
    t = top_e.shape[1]
    tr = T_ROUTE
    spec = pl.BlockSpec((TOP_K, tr), lambda i: (0, i))
    return pl.pallas_call(
        _dest_kernel,
        grid=(t // tr,),
        in_specs=[spec, spec, pl.BlockSpec((N_EXPERTS, 1), lambda i: (0, 0))],
        out_specs=spec,
        out_shape=jax.ShapeDtypeStruct((TOP_K, t), jnp.int32),
        compiler_params=_params("arbitrary"),
        name="dest_rows",
    )(top_e, rank, start_col)


def _moe_kernel(order_ref, be_ref, nv_ref, cs_ref, nu_ref, h_ref, wg_ref, wu_ref, wd_ref, y_ref,
                hbuf, *scratch, n_tokens, n_assign):
    xbufs, ybufs = scratch[:MOE_BUFS], scratch[MOE_BUFS:2 * MOE_BUFS]
    wg_sc, wu_sc, wd_sc, hsem = scratch[2 * MOE_BUFS:2 * MOE_BUFS + 4]
    ssems = scratch[2 * MOE_BUFS + 4:]
    i = pl.program_id(0)
    n_steps = pl.num_programs(0)
    r = ROWS_MOE
    half = D_MODEL // 2
    n_used = nu_ref[0]
    active = i <= n_used

    def copy_rows(blk, xdst):
        base = cs_ref[blk]
        for j in range(r):
            tok = order_ref[base + j] & (n_tokens - 1)
            xdst[j // 8, pl.ds(j % 8, 1), :] = hbuf[tok]

    def compute(blk, xsrc, ydst):
        row = lax.broadcasted_iota(jnp.int32, (r, half), 0)
        xp = jnp.where(row < nv_ref[blk], xsrc[...].reshape(r, half), jnp.uint32(0))
        x_lo, x_hi = _unpack_halves(xp)
        g = (jnp.dot(x_lo, wg_sc[:half, :], preferred_element_type=F32)
             + jnp.dot(x_hi, wg_sc[half:, :], preferred_element_type=F32))
        u = (jnp.dot(x_lo, wu_sc[:half, :], preferred_element_type=F32)
             + jnp.dot(x_hi, wu_sc[half:, :], preferred_element_type=F32))
        y = jnp.dot((_silu(g) * u).astype(BF16), wd_sc[...], preferred_element_type=F32)
        ydst[...] = _pack_halves(y.astype(BF16)).reshape(r // 8, 8, half)

    def scatter(blk, n_rows, v):
        base = cs_ref[blk]
        for j in range(r):
            dst = jnp.where(j < n_rows, order_ref[base + j], n_assign + v * r + j)
            pltpu.make_async_copy(ybufs[v].at[j // 8, pl.ds(j % 8, 1), :], y_ref.at[dst],
                                  ssems[v]).start(priority=j % 2)

    def scatter_wait(v):
        pltpu.make_async_copy(y_ref.at[pl.ds(0, r)], y_ref.at[pl.ds(r, r)], ssems[v]).wait()

    def for_buffer(step_no, fn):
        for v in range(MOE_BUFS):
            pl.when(step_no % MOE_BUFS == v)(functools.partial(fn, v))

    @pl.when(i == 0)
    def _():
        load = pltpu.make_async_copy(h_ref, hbuf, hsem)
        load.start()
        load.wait()
        copy_rows(0, xbufs[0])
        zbuf = ybufs[MOE_BUFS - 1]
        zbuf[...] = jnp.zeros_like(zbuf)
        for j in range(MOE_BUFS * r):
            pltpu.make_async_copy(zbuf.at[(j % r) // 8, pl.ds(j % 8, 1), :],
                                  y_ref.at[n_assign + j], hsem).start(priority=j % 2)
        for _ in range(MOE_BUFS):
            pltpu.make_async_copy(y_ref.at[pl.ds(0, r)], y_ref.at[pl.ds(r, r)], hsem).wait()

    @pl.when((i >= 2) & (i - 2 <= n_used))
    def _():
        for_buffer(i + MOE_BUFS - 3, scatter_wait)

    prev = be_ref[jnp.maximum(i - 1, 0)]

    @pl.when((i == 0) | (be_ref[i] != prev))
    def _():
        wg_sc[...] = wg_ref[0].astype(BF16)
        wu_sc[...] = wu_ref[0].astype(BF16)
        wd_sc[...] = wd_ref[0].astype(BF16)

    def step(v):
        last = jnp.maximum(i - 1, 0)
        scatter(last, jnp.where(i > 0, nv_ref[last], 0), (v - 1) % MOE_BUFS)
        copy_rows(i + 1, xbufs[(v + 1) % MOE_BUFS])
        compute(i, xbufs[v], ybufs[v])

    @pl.when(active)
    def _():
        for_buffer(i, step)

    @pl.when(i == n_steps - 1)
    def _():
        @pl.when(i - 1 <= n_used)
        def _():
            for_buffer(i + MOE_BUFS - 2, scatter_wait)

        @pl.when(active)
        def _():
            for_buffer(i + MOE_BUFS - 1, scatter_wait)


def _moe_experts(order, block_expert, block_valid, block_start, n_used, h2, w_gate, w_up, w_down):
    t = h2.shape[0]
    assert t & (t - 1) == 0, "token count must be a power of two"
    n_blocks = block_expert.shape[0] - 2
    n_assign = TOP_K * t
    wspec = lambda shp: pl.BlockSpec((1,) + shp, lambda i, od, be, nv, cs, nu: (be[i], 0, 0))
    row_buf = pltpu.VMEM((ROWS_MOE // 8, 8, D_MODEL // 2), jnp.uint32)
    grid_spec = pltpu.PrefetchScalarGridSpec(
        num_scalar_prefetch=5,
        grid=(n_blocks + 1,),
        in_specs=[pl.BlockSpec(memory_space=pl.ANY),
                  wspec((D_MODEL, EXPERT_DIM)), wspec((D_MODEL, EXPERT_DIM)),
                  wspec((EXPERT_DIM, D_MODEL))],
        out_specs=pl.BlockSpec(memory_space=pl.ANY),
        scratch_shapes=([pltpu.VMEM((t, 1, D_MODEL // 2), jnp.uint32)]
                        + [row_buf] * (2 * MOE_BUFS)
                        + [pltpu.VMEM((D_MODEL, EXPERT_DIM), BF16),
                           pltpu.VMEM((D_MODEL, EXPERT_DIM), BF16),
                           pltpu.VMEM((EXPERT_DIM, D_MODEL), BF16),
                           pltpu.SemaphoreType.DMA]
                        + [pltpu.SemaphoreType.DMA] * MOE_BUFS),
    )
    return pl.pallas_call(
        functools.partial(_moe_kernel, n_tokens=t, n_assign=n_assign),
        grid_spec=grid_spec,
        out_shape=jax.ShapeDtypeStruct((n_assign + MOE_BUFS * ROWS_MOE, 1, D_MODEL // 2),
                                       jnp.uint32),
        compiler_params=_params("arbitrary"),
        name="moe_experts",
    )(order, block_expert, block_valid, block_start, n_used, h2, w_gate, w_up, w_down)


def _combine_kernel(base_ref, w_ref, mod_ref, g_ref, *refs, tiles_per_batch, last_layer):
    y_refs, o_ref = refs[:TOP_K], refs[TOP_K]
    tc = base_ref.shape[0]
    lo = [None] * PACKED_ROWS
    hi = [None] * PACKED_ROWS
    for k in range(TOP_K):
        wk = w_ref[:, k:k + 1]
        for cb in range(PACKED_ROWS):
            yp = y_refs[k][pl.ds(cb, tc, stride=PACKED_ROWS), :]
            y_lo = lax.bitcast_convert_type(yp << 16, F32) * wk
            y_hi = lax.bitcast_convert_type(yp & jnp.uint32(0xFFFF0000), F32) * wk
            lo[cb] = y_lo if lo[cb] is None else lo[cb] + y_lo
            hi[cb] = y_hi if hi[cb] is None else hi[cb] + y_hi
    routed = jnp.concatenate(lo + hi, axis=1)
    b = pl.program_id(0) // tiles_per_batch
    gate2 = mod_ref[pl.ds(b, 1), 5 * D_MODEL:6 * D_MODEL]
    x = base_ref[...] + gate2 * routed
    if last_layer:
        x = x * lax.rsqrt(jnp.mean(x * x, axis=-1, keepdims=True) + NORM_EPS) * g_ref[...]
    o_ref[...] = x


def _combine(base, top_w, mod, final_g, y, seq, last_layer):
    t = base.shape[0]
    tc = T_COMBINE
    kern = functools.partial(_combine_kernel, tiles_per_batch=seq // tc, last_layer=last_layer)
    y_rows = y.reshape(y.shape[0] * PACKED_ROWS, LANES)
    tiles = t // tc
    y_specs = [pl.BlockSpec((tc * PACKED_ROWS, LANES), lambda i, k=k: (k * tiles + i, 0))
               for k in range(TOP_K)]
    return pl.pallas_call(
        kern,
        grid=(tiles,),
        in_specs=[pl.BlockSpec((tc, D_MODEL), lambda i: (i, 0)),
                  pl.BlockSpec((tc, TOP_K), lambda i: (i, 0)),
                  pl.BlockSpec(mod.shape, lambda i: (0, 0)),
                  pl.BlockSpec((1, D_MODEL), lambda i: (0, 0))] + y_specs,
        out_specs=pl.BlockSpec((tc, D_MODEL), lambda i: (i, 0)),
        out_shape=jax.ShapeDtypeStruct((t, D_MODEL), F32),
        compiler_params=_params("arbitrary"),
        name="combine",
    )(base, top_w, mod, final_g, *([y_rows] * TOP_K))


def _block_tables(counts, n_assign):
    r = ROWS_MOE
    cnt = counts.reshape(N_EXPERTS).astype(jnp.int32)
    first = jnp.cumsum(cnt) - cnt
    nblk = (cnt + r - 1) // r
    blk_end = jnp.cumsum(nblk)
    blk_start = blk_end - nblk
    n_blocks = (n_assign + N_EXPERTS * (r - 1) + r - 1) // r
    bi = jnp.arange(n_blocks + 2, dtype=jnp.int32)
    bc = jnp.minimum(bi, blk_end[-1] - 1)
    block_expert = jnp.minimum(jnp.sum((blk_end[None, :] <= bc[:, None]).astype(jnp.int32), axis=1),
                               N_EXPERTS - 1)
    onehot = (block_expert[:, None] == jnp.arange(N_EXPERTS, dtype=jnp.int32)[None, :])
    pick = lambda v: jnp.sum(jnp.where(onehot, v[None, :], 0), axis=1)
    within = bc - pick(blk_start)
    block_valid = jnp.where(bi < blk_end[-1], jnp.clip(pick(cnt) - within * r, 0, r), 0)
    block_start = pick(first) + within * r
    return (first.astype(F32).reshape(N_EXPERTS, 1), block_expert.astype(jnp.int32),
            block_valid.astype(jnp.int32), block_start.astype(jnp.int32),
            blk_end[-1:].astype(jnp.int32))


def kernel(x, c, w_ada, b_ada, norm1_g, w_in, b_forget, ret_gn_g, w_branch_a, w_branch_b, w_out,
           norm2_g, w_router, router_bias, w_exp_gate, w_exp_up, w_exp_down, w_sh_gate, w_sh_up,
           w_sh_down, final_g):
    batch, seq, d = x.shape
    t = batch * seq
    depth = w_ada.shape[0]
    x2 = x.reshape(t, d)

    half = RET_QK_DIM // 2
    inv_freq = ROPE_BASE ** (-jnp.arange(half, dtype=F32) / half)
    ang = jnp.arange(seq, dtype=F32)[:, None] * inv_freq[None, :]
    cos32, sin32 = jnp.cos(ang), jnp.sin(ang)
    cos_t = jnp.concatenate([cos32, cos32, cos32, cos32], axis=-1)
    sin_t = jnp.concatenate([-sin32, sin32, -sin32, sin32], axis=-1)
    dec_t, zeta_t, xi_t, cd_t = _retention_tables()
    c_pad = jnp.zeros((8, d), F32).at[:batch].set(c)

    for l in range(depth):
        mod = _adaln_mod(c_pad, w_ada[l], b_ada[l])

        w = w_in[l]
        o_fq, o_fk, o_fv, o_ff = 0, FOX_WIDTH, 2 * FOX_WIDTH, 3 * FOX_WIDTH
        o_rq = o_ff + FOX_HEADS
        o_rk = o_rq + RET_QK_WIDTH
        o_rv = o_rk + RET_QK_WIDTH
        o_rg = o_rv + RET_V_WIDTH
        o_ga = o_rg + RET_V_WIDTH
        o_gb = o_ga + D_MODEL
        w_p = jnp.concatenate(
            [w[:, o_rv:o_rg], w[:, o_rg:o_ga], w[:, o_ga:o_gb], w[:, o_gb:o_gb + D_MODEL],
             w[:, o_fq:o_fk], w[:, o_fk:o_fv], w[:, o_fv:o_ff], w[:, o_rq:o_rk], w[:, o_rk:o_rv]],
            axis=1).astype(BF16)
        w_ff = jnp.zeros((d, LANES), BF16).at[:, :FOX_HEADS].set(w[:, o_ff:o_rq].astype(BF16))
        b_ff = jnp.zeros((1, LANES), F32).at[0, :FOX_HEADS].set(b_forget[l].astype(F32))

        proj, logf = _in_proj(x2, mod, norm1_g[l].reshape(1, d), w_p, w_ff, b_ff, cos_t, sin_t, seq)
        cum_f = jnp.cumsum(logf[:, :FOX_HEADS].reshape(batch, seq, FOX_HEADS), axis=1)
        cum_f = cum_f.transpose(0, 2, 1) * LOG2_E

        attn = _fox_attn(proj, cum_f, batch, seq)
        ret = _retention(proj, ret_gn_g[l].reshape(1, RET_V_WIDTH), dec_t, zeta_t, xi_t, cd_t,
                         batch, seq)

        wr_head, wr_rest = _split_bf16(w_router[l].T.astype(F32))
        base, h2, scores_t = _post_mix(
            attn, ret, proj, x2, mod, norm2_g[l].reshape(1, d),
            w_branch_a[l].astype(BF16), w_branch_b[l].astype(BF16), w_out[l].astype(BF16),
            wr_head, wr_rest, w_sh_gate[l].astype(BF16), w_sh_up[l].astype(BF16),
            w_sh_down[l].astype(BF16), seq)

        top_e, top_w, rank, counts = _route(scores_t,
                                            router_bias[l].astype(F32).reshape(N_EXPERTS, 1))
        first, block_expert, block_valid, block_start, n_used = _block_tables(counts, t * TOP_K)
        sorted_pos = _dest_rows(top_e, rank, first).reshape(TOP_K * t)
        order = jnp.concatenate([jnp.argsort(sorted_pos).astype(jnp.int32),
                                 jnp.zeros((ROWS_MOE,), jnp.int32)])
        y = _moe_experts(order, block_expert, block_valid, block_start, n_used,
                         h2.reshape(t, 1, d // 2),
                         w_exp_gate[l], w_exp_up[l], w_exp_down[l])
        x2 = _combine(base, top_w.T, mod, final_g.reshape(1, d), y, seq, l == depth - 1)
    return x2.reshape(batch, seq, d)
```

```python
import functools

import jax
import jax.numpy as jnp
import numpy as np
from jax import lax
from jax.experimental import pallas as pl
from jax.experimental.pallas import tpu as pltpu

F32 = jnp.float32
BF16 = jnp.bfloat16

D_MODEL = 1024
FOX_HEADS = 8
FOX_HEAD_DIM = 64
FOX_WIDTH = 512
RET_HEADS = 8
RET_QK_DIM = 64
RET_V_DIM = 128
RET_QK_WIDTH = 512
RET_V_WIDTH = 1024
RET_CHUNK = 256
ROPE_BASE = 10000.0
N_EXPERTS = 256
TOP_K = 8
N_GROUPS = 8
TOPK_GROUPS = 4
EXPERT_DIM = 256
ROUTED_SCALE = 2.5
NORM_EPS = 1e-6
LOG2_E = 1.4426950408889634

LANES = 128
PACKED_ROWS = D_MODEL // 2 // LANES
VMEM_LIMIT = 56 * 1024 * 1024

COL_RV, COL_RG, COL_GA, COL_GB = 0, 1024, 2048, 3072
COL_FQ, COL_FK, COL_FV, COL_RQ, COL_RK = 4096, 4608, 5120, 5632, 6144
PROJ_WIDTH = 6656

TM_PROJ = 512
TN_PROJ = 512
TQ_ATTN = 512
TK_ATTN = 512
T_RET = 512
TM_POST = 512
T_ROUTE = 512
ROWS_MOE = 256
MOE_BUFS = 3
MOE_HEAD = 128
MOE_CHUNK = 32
T_COMBINE = 256


def _sigmoid(z):
    return 1.0 / (1.0 + jnp.exp(-z))


def _silu(z):
    return z * _sigmoid(z)


def _pack_halves(xb):
    n = xb.shape[1] // 2
    lo = lax.bitcast_convert_type(xb[:, :n].astype(F32), jnp.uint32)
    hi = lax.bitcast_convert_type(xb[:, n:].astype(F32), jnp.uint32)
    return (lo >> 16) | hi


def _unpack_halves(xp):
    lo = lax.bitcast_convert_type(xp << 16, F32)
    hi = lax.bitcast_convert_type(xp & jnp.uint32(0xFFFF0000), F32)
    return lo.astype(BF16), hi.astype(BF16)


def _split_bf16(x):
    head = lax.bitcast_convert_type(
        lax.bitcast_convert_type(x, jnp.uint32) & jnp.uint32(0xFFFF0000), F32)
    return head.astype(BF16), (x - head).astype(BF16)


def _params(*sem):
    return pltpu.CompilerParams(dimension_semantics=sem, vmem_limit_bytes=VMEM_LIMIT)


def _adaln_kernel(c_ref, w_ref, b_ref, o_ref):
    a = _silu(c_ref[...]).astype(BF16)
    o_ref[...] = jnp.dot(a, w_ref[...].astype(BF16), preferred_element_type=F32) + b_ref[...]


def _adaln_mod(c_pad, w_ada, b_ada):
    n = w_ada.shape[1]
    tn = 1024
    return pl.pallas_call(
        _adaln_kernel,
        grid=(n // tn,),
        in_specs=[pl.BlockSpec((8, D_MODEL), lambda j: (0, 0)),
                  pl.BlockSpec((D_MODEL, tn), lambda j: (0, j)),
                  pl.BlockSpec((1, tn), lambda j: (0, j))],
        out_specs=pl.BlockSpec((8, tn), lambda j: (0, j)),
        out_shape=jax.ShapeDtypeStruct((8, n), F32),
        compiler_params=_params("arbitrary"),
        name="adaln_mod",
    )(c_pad, w_ada, b_ada.reshape(1, n))


def _inproj_kernel(x_ref, mod_ref, g_ref, w_ref, wff_ref, bf_ref, cos_ref, sin_ref,
                   proj_ref, logf_ref, *, tiles_per_batch):
    b = pl.program_id(0) // tiles_per_batch
    x = x_ref[...]
    y = x * lax.rsqrt(jnp.mean(x * x, axis=-1, keepdims=True) + NORM_EPS) * g_ref[...]
    shift = mod_ref[pl.ds(b, 1), 0:D_MODEL]
    scale = mod_ref[pl.ds(b, 1), D_MODEL:2 * D_MODEL]
    hb = (y * (1.0 + scale) + shift).astype(BF16)
    z = jnp.dot(hb, wff_ref[...], preferred_element_type=F32) + bf_ref[...]
    logf_ref[...] = jnp.minimum(z, 0.0) - jnp.log1p(jnp.exp(-jnp.abs(z)))

    cos = cos_ref[...]
    sin = sin_ref[...]
    lane = lax.broadcasted_iota(jnp.int32, cos.shape, 1)
    first_half = (lane % RET_QK_DIM) < (RET_QK_DIM // 2)

    for c0 in range(0, PROJ_WIDTH, TN_PROJ):
        cols = slice(c0, c0 + TN_PROJ)
        acc = jnp.dot(hb, w_ref[:, cols], preferred_element_type=F32)
        if COL_GA <= c0 < COL_FQ:
            proj_ref[:, cols] = _sigmoid(acc).astype(BF16)
        elif COL_FQ <= c0 < COL_FK:
            proj_ref[:, cols] = (acc * (LOG2_E * FOX_HEAD_DIM ** -0.5)).astype(BF16)
        elif c0 >= COL_RQ:
            k_scale = RET_QK_DIM ** -0.5 if c0 >= COL_RK else 1.0
            for cb in range(c0, c0 + TN_PROJ, LANES):
                xs = acc[:, cb - c0:cb - c0 + LANES]
                up = pltpu.roll(xs, LANES - RET_QK_DIM // 2, axis=1)
                dn = pltpu.roll(xs, RET_QK_DIM // 2, axis=1)
                rot = xs * cos + jnp.where(first_half, up, dn) * sin
                proj_ref[:, cb:cb + LANES] = (rot * k_scale).astype(BF16)
        else:
            proj_ref[:, cols] = acc.astype(BF16)


def _in_proj(x2, mod, norm_g, w_p, w_ff, b_ff, cos_t, sin_t, seq):
    t = x2.shape[0]
    tiles_per_batch = seq // TM_PROJ
    kern = functools.partial(_inproj_kernel, tiles_per_batch=tiles_per_batch)
    return pl.pallas_call(
        kern,
        grid=(t // TM_PROJ,),
        in_specs=[pl.BlockSpec((TM_PROJ, D_MODEL), lambda i: (i, 0)),
                  pl.BlockSpec(mod.shape, lambda i: (0, 0)),
                  pl.BlockSpec((1, D_MODEL), lambda i: (0, 0)),
                  pl.BlockSpec((D_MODEL, PROJ_WIDTH), lambda i: (0, 0)),
                  pl.BlockSpec((D_MODEL, LANES), lambda i: (0, 0)),
                  pl.BlockSpec((1, LANES), lambda i: (0, 0)),
                  pl.BlockSpec((TM_PROJ, LANES), lambda i: (i % tiles_per_batch, 0)),
                  pl.BlockSpec((TM_PROJ, LANES), lambda i: (i % tiles_per_batch, 0))],
        out_specs=[pl.BlockSpec((TM_PROJ, PROJ_WIDTH), lambda i: (i, 0)),
                   pl.BlockSpec((TM_PROJ, LANES), lambda i: (i, 0))],
        out_shape=[jax.ShapeDtypeStruct((t, PROJ_WIDTH), BF16),
                   jax.ShapeDtypeStruct((t, LANES), F32)],
        compiler_params=_params("arbitrary"),
        name="in_proj",
    )(x2, mod, norm_g, w_p, w_ff, b_ff, cos_t, sin_t)


def _attn_kernel(q_ref, k_ref, v_ref, f_ref, o_ref):
    hp = pl.program_id(1)
    qi = pl.program_id(2)
    tq, tk = TQ_ATTN, TK_ATTN
    q = q_ref[...]
    lane = lax.broadcasted_iota(jnp.int32, (tq, LANES), 1)
    row = qi * tq + lax.broadcasted_iota(jnp.int32, (tq, tk), 0)
    col = lax.broadcasted_iota(jnp.int32, (tq, tk), 1)
    zeros = jnp.zeros_like(q)
    qm = (jnp.where(lane < FOX_HEAD_DIM, q, zeros), jnp.where(lane >= FOX_HEAD_DIM, q, zeros))

    def step(kb, carry, masked):
        ks = pl.multiple_of(kb * tk, tk)
        k = k_ref[pl.ds(ks, tk), :]
        v = v_ref[pl.ds(ks, tk), :]
        out = []
        for hh in range(2):
            m, l, acc = carry[hh]
            s = lax.dot_general(qm[hh], k, (((1,), (1,)), ((), ())), preferred_element_type=F32)
            s = s - f_ref[0, pl.ds(2 * hp + hh, 1), pl.ds(ks, tk)]
            if masked:
                s = jnp.where(col + ks <= row, s, -jnp.inf)
            m_new = jnp.maximum(m, jnp.max(s, axis=-1, keepdims=True))
            alpha = jnp.exp2(m - m_new)
            p = jnp.exp2(s - m_new)
            l = alpha * l + jnp.sum(p, axis=-1, keepdims=True)
            acc = alpha * acc + jnp.dot(p.astype(BF16), v, preferred_element_type=F32)
            out.append((m_new, l, acc))
        return tuple(out)

    init = (jnp.full((tq, 1), -jnp.inf, F32), jnp.zeros((tq, 1), F32),
            jnp.zeros((tq, LANES), F32))
    n_full = (qi * tq) // tk
    def pair(kp, c):
        return step(2 * kp + 1, step(2 * kp, c, False), False)

    carry = lax.fori_loop(0, n_full // 2, pair, (init, init))
    carry = lax.cond(n_full % 2 == 1, lambda c: step(n_full - 1, c, False), lambda c: c, carry)
    for d in range(max(tq // tk, 1)):
        carry = step(n_full + d, carry, True)
    (_, l0, acc0), (_, l1, acc1) = carry
    o_ref[...] = jnp.where(lane < FOX_HEAD_DIM, acc0 / l0, acc1 / l1).astype(BF16)


def _fox_attn(proj, cum_f, batch, seq):
    t = proj.shape[0]
    nq = seq // TQ_ATTN
    cq, ck, cv = COL_FQ // LANES, COL_FK // LANES, COL_FV // LANES
    return pl.pallas_call(
        _attn_kernel,
        grid=(batch, FOX_HEADS // 2, nq),
        in_specs=[pl.BlockSpec((TQ_ATTN, LANES), lambda b, hp, qi: (b * nq + qi, cq + hp)),
                  pl.BlockSpec((seq, LANES), lambda b, hp, qi: (b, ck + hp)),
                  pl.BlockSpec((seq, LANES), lambda b, hp, qi: (b, cv + hp)),
                  pl.BlockSpec((1, FOX_HEADS, seq), lambda b, hp, qi: (b, 0, 0))],
        out_specs=pl.BlockSpec((TQ_ATTN, LANES), lambda b, hp, qi: (b * nq + qi, hp)),
        out_shape=jax.ShapeDtypeStruct((t, FOX_WIDTH), BF16),
        compiler_params=_params("arbitrary", "arbitrary", "arbitrary"),
        name="fox_attn",
    )(proj, proj, proj, cum_f)


def _ret_kernel(q_ref, k_ref, v_ref, rg_ref, gn_ref, dec_ref, zeta_ref, xi_ref, cd_ref,
                o_ref, st_sc):
    ri = pl.program_id(1)
    c = RET_CHUNK

    @pl.when(ri == 0)
    def _():
        st_sc[...] = jnp.zeros_like(st_sc)

    lane = lax.broadcasted_iota(jnp.int32, (c, LANES), 1)
    zeros = jnp.zeros((c, LANES), BF16)
    states = [st_sc[h] for h in range(RET_HEADS)]
    for n in range(T_RET // c):
        rows = slice(n * c, (n + 1) * c)
        for h in range(RET_HEADS):
            pair = slice((h // 2) * LANES, (h // 2 + 1) * LANES)
            vcol = slice(h * RET_V_DIM, (h + 1) * RET_V_DIM)
            sel = (lane < RET_QK_DIM) if h % 2 == 0 else (lane >= RET_QK_DIM)
            qc = jnp.where(sel, q_ref[rows, pair], zeros)
            kc = jnp.where(sel, k_ref[rows, pair], zeros)
            vc = v_ref[rows, vcol]
            sc = lax.dot_general(qc, kc, (((1,), (1,)), ((), ())),
                                 preferred_element_type=F32) * dec_ref[h]
            intra = jnp.dot(sc.astype(BF16), vc, preferred_element_type=F32)
            cross = jnp.dot(qc, states[h].astype(BF16), preferred_element_type=F32) * xi_ref[h]
            o = intra + cross
            kz = (kc.astype(F32) * zeta_ref[h]).astype(BF16)
            kv = lax.dot_general(kz, vc, (((0,), (0,)), ((), ())), preferred_element_type=F32)
            states[h] = cd_ref[h] * states[h] + kv
            mu = jnp.mean(o, axis=-1, keepdims=True)
            d = o - mu
            var = jnp.mean(d * d, axis=-1, keepdims=True)
            on = d * lax.rsqrt(var + NORM_EPS) * gn_ref[:, vcol]
            rg = rg_ref[rows, vcol].astype(F32)
            o_ref[rows, vcol] = (on * _silu(rg)).astype(BF16)
    for h in range(RET_HEADS):
        st_sc[h] = states[h]


def _retention(proj, gn_g, dec_t, zeta_t, xi_t, cd_t, batch, seq):
    t = proj.shape[0]
    nr = seq // T_RET
    tok = lambda b, ri: b * nr + ri
    full = lambda a: pl.BlockSpec(a.shape, lambda b, ri: (0,) * a.ndim)
    return pl.pallas_call(
        _ret_kernel,
        grid=(batch, nr),
        in_specs=[pl.BlockSpec((T_RET, RET_QK_WIDTH), lambda b, ri: (tok(b, ri), COL_RQ // RET_QK_WIDTH)),
                  pl.BlockSpec((T_RET, RET_QK_WIDTH), lambda b, ri: (tok(b, ri), COL_RK // RET_QK_WIDTH)),
                  pl.BlockSpec((T_RET, RET_V_WIDTH), lambda b, ri: (tok(b, ri), COL_RV // RET_V_WIDTH)),
                  pl.BlockSpec((T_RET, RET_V_WIDTH), lambda b, ri: (tok(b, ri), COL_RG // RET_V_WIDTH)),
                  pl.BlockSpec((1, RET_V_WIDTH), lambda b, ri: (0, 0)),
                  full(dec_t), full(zeta_t), full(xi_t), full(cd_t)],
        out_specs=pl.BlockSpec((T_RET, RET_V_WIDTH), lambda b, ri: (tok(b, ri), 0)),
        out_shape=jax.ShapeDtypeStruct((t, RET_V_WIDTH), BF16),
        scratch_shapes=[pltpu.VMEM((RET_HEADS, LANES, RET_V_DIM), F32)],
        compiler_params=_params("arbitrary", "arbitrary"),
        name="retention",
    )(proj, proj, proj, proj, gn_g, dec_t, zeta_t, xi_t, cd_t)


def _retention_tables():
    c = RET_CHUNK
    log_gamma = jnp.log1p(-jnp.exp2(-5.0 - jnp.arange(RET_HEADS, dtype=F32)))
    idx = jnp.arange(c, dtype=F32)
    diff = idx[:, None] - idx[None, :]
    dec = jnp.where(diff >= 0,
                    jnp.exp(jnp.maximum(diff, 0.0)[None] * log_gamma[:, None, None]), 0.0)
    zeta = jnp.exp((c - 1.0 - idx)[None, :] * log_gamma[:, None])
    xi = jnp.exp((idx + 1.0)[None, :] * log_gamma[:, None])
    cd = jnp.exp(c * log_gamma)
    bc = lambda v: jnp.broadcast_to(v[:, :, None], (RET_HEADS, c, LANES))
    cd_t = jnp.broadcast_to(cd[:, None, None], (RET_HEADS, LANES, RET_V_DIM))
    return dec, bc(zeta), bc(xi), cd_t


def _post_kernel(attn_ref, ret_ref, ga_ref, gb_ref, x_ref, mod_ref, g2_ref,
                 wa_ref, wb_ref, wo_ref, wr_ref, wrr_ref, wsg_ref, wsu_ref, wsd_ref,
                 base_ref, h2_ref, score_ref, *, tiles_per_batch):
    b = pl.program_id(0) // tiles_per_batch
    ya = jnp.dot(attn_ref[...], wa_ref[...], preferred_element_type=F32)
    yb = jnp.dot(ret_ref[...], wb_ref[...], preferred_element_type=F32)
    merged = ga_ref[...].astype(F32) * ya + gb_ref[...].astype(F32) * yb
    mix = jnp.dot(merged.astype(BF16), wo_ref[...], preferred_element_type=F32)
    d = D_MODEL
    gate1 = mod_ref[pl.ds(b, 1), 2 * d:3 * d]
    shift2 = mod_ref[pl.ds(b, 1), 3 * d:4 * d]
    scale2 = mod_ref[pl.ds(b, 1), 4 * d:5 * d]
    gate2 = mod_ref[pl.ds(b, 1), 5 * d:6 * d]
    x1 = x_ref[...] + gate1 * mix
    y = x1 * lax.rsqrt(jnp.mean(x1 * x1, axis=-1, keepdims=True) + NORM_EPS) * g2_ref[...]
    h2 = y * (1.0 + scale2) + shift2
    h2_head, h2_rest = _split_bf16(h2)
    nt = (((1,), (1,)), ((), ()))
    logits_t = (lax.dot_general(wr_ref[...], h2_head, nt, preferred_element_type=F32)
                + lax.dot_general(wr_ref[...], h2_rest, nt, preferred_element_type=F32)
                + lax.dot_general(wrr_ref[...], h2_head, nt, preferred_element_type=F32))
    score_ref[...] = _sigmoid(logits_t)
    h2b = h2.astype(BF16)
    h2p = _pack_halves(h2b)
    n_chunks = h2p.shape[1] // LANES
    for cb in range(n_chunks):
        h2_ref[pl.ds(cb, h2p.shape[0], stride=n_chunks), :] = h2p[:, cb * LANES:(cb + 1) * LANES]
    g = jnp.dot(h2b, wsg_ref[...], preferred_element_type=F32)
    u = jnp.dot(h2b, wsu_ref[...], preferred_element_type=F32)
    shared = jnp.dot((_silu(g) * u).astype(BF16), wsd_ref[...], preferred_element_type=F32)
    base_ref[...] = x1 + gate2 * shared


def _post_mix(attn, ret, proj, x2, mod, norm2_g, wa, wb, wo, wr, wrr, wsg, wsu, wsd, seq):
    t = x2.shape[0]
    tm = TM_POST
    kern = functools.partial(_post_kernel, tiles_per_batch=seq // tm)
    full = lambda a: pl.BlockSpec(a.shape, lambda i: (0,) * a.ndim)
    return pl.pallas_call(
        kern,
        grid=(t // tm,),
        in_specs=[pl.BlockSpec((tm, FOX_WIDTH), lambda i: (i, 0)),
                  pl.BlockSpec((tm, RET_V_WIDTH), lambda i: (i, 0)),
                  pl.BlockSpec((tm, D_MODEL), lambda i: (i, COL_GA // D_MODEL)),
                  pl.BlockSpec((tm, D_MODEL), lambda i: (i, COL_GB // D_MODEL)),
                  pl.BlockSpec((tm, D_MODEL), lambda i: (i, 0)),
                  full(mod), full(norm2_g), full(wa), full(wb), full(wo), full(wr), full(wrr),
                  full(wsg), full(wsu), full(wsd)],
        out_specs=[pl.BlockSpec((tm, D_MODEL), lambda i: (i, 0)),
                   pl.BlockSpec((tm * PACKED_ROWS, LANES), lambda i: (i, 0)),
                   pl.BlockSpec((N_EXPERTS, tm), lambda i: (0, i))],
        out_shape=[jax.ShapeDtypeStruct((t, D_MODEL), F32),
                   jax.ShapeDtypeStruct((t * PACKED_ROWS, LANES), jnp.uint32),
                   jax.ShapeDtypeStruct((N_EXPERTS, t), F32)],
        compiler_params=_params("arbitrary"),
        name="post_mix",
    )(attn, ret, proj, proj, x2, mod, norm2_g, wa, wb, wo, wr, wrr, wsg, wsu, wsd)


def _route_kernel(s_ref, bias_ref, e_ref, w_ref, rank_ref, cnt_ref):
    tr = s_ref.shape[1]
    gsz = N_EXPERTS // N_GROUPS

    @pl.when(pl.program_id(0) == 0)
    def _():
        cnt_ref[...] = jnp.zeros_like(cnt_ref)

    s = s_ref[...]
    sel = s + bias_ref[...]
    neg = jnp.float32(-jnp.inf)

    g3 = sel.reshape(N_GROUPS, gsz, tr)
    m1 = jnp.max(g3, axis=1)
    n_max = jnp.sum((g3 == m1[:, None, :]).astype(F32), axis=1)
    m2 = jnp.max(jnp.where(g3 < m1[:, None, :], g3, neg), axis=1)
    gs = m1 + jnp.where(n_max >= 2.0, m1, m2)

    gi = lax.broadcasted_iota(jnp.int32, (N_GROUPS, tr), 0)
    beaten = jnp.zeros((N_GROUPS, tr), F32)
    for j in range(N_GROUPS):
        gj = gs[j:j + 1, :]
        beaten = beaten + jnp.where((gj > gs) | ((gj == gs) & (gi > j)), 1.0, 0.0)
    keep = jnp.where(beaten < float(TOPK_GROUPS), 1.0, 0.0)
    keep_e = jnp.broadcast_to(keep[:, None, :], (N_GROUPS, gsz, tr)).reshape(N_EXPERTS, tr)

    ei = lax.broadcasted_iota(jnp.int32, (N_EXPERTS, tr), 0)
    val = jnp.where(keep_e > 0.0, sel, neg)
    member = jnp.zeros((N_EXPERTS, tr), F32)
    idx_rows, w_rows = [], []
    for _ in range(TOP_K):
        m = jnp.max(val, axis=0, keepdims=True)
        idx = jnp.min(jnp.where(val == m, ei, N_EXPERTS), axis=0, keepdims=True)
        hit = ei == idx
        w_rows.append(jnp.sum(jnp.where(hit, s, 0.0), axis=0, keepdims=True))
        idx_rows.append(idx)
        val = jnp.where(hit, neg, val)
        member = jnp.where(hit, 1.0, member)

    w_sum = w_rows[0]
    for k in range(1, TOP_K):
        w_sum = w_sum + w_rows[k]

    r_i = lax.broadcasted_iota(jnp.int32, (tr, tr), 0)
    c_i = lax.broadcasted_iota(jnp.int32, (tr, tr), 1)
    upper = jnp.where(r_i < c_i, 1.0, 0.0).astype(BF16)
    before = jnp.dot(member.astype(BF16), upper, preferred_element_type=F32) + cnt_ref[...]

    for k in range(TOP_K):
        hit = ei == idx_rows[k]
        e_ref[k:k + 1, :] = idx_rows[k]
        w_ref[k:k + 1, :] = w_rows[k] / w_sum * ROUTED_SCALE
        rank_ref[k:k + 1, :] = jnp.sum(jnp.where(hit, before, 0.0), axis=0,
                                       keepdims=True).astype(jnp.int32)
    cnt_ref[...] = cnt_ref[...] + jnp.sum(member, axis=1, keepdims=True)


def _route(scores_t, bias_col):
    t = scores_t.shape[1]
    tr = T_ROUTE
    row8 = lambda dt: jax.ShapeDtypeStruct((TOP_K, t), dt)
    return pl.pallas_call(
        _route_kernel,
        grid=(t // tr,),
        in_specs=[pl.BlockSpec((N_EXPERTS, tr), lambda i: (0, i)),
                  pl.BlockSpec((N_EXPERTS, 1), lambda i: (0, 0))],
        out_specs=[pl.BlockSpec((TOP_K, tr), lambda i: (0, i)),
                   pl.BlockSpec((TOP_K, tr), lambda i: (0, i)),
                   pl.BlockSpec((TOP_K, tr), lambda i: (0, i)),
                   pl.BlockSpec((N_EXPERTS, 1), lambda i: (0, 0))],
        out_shape=[row8(jnp.int32), row8(F32), row8(jnp.int32),
                   jax.ShapeDtypeStruct((N_EXPERTS, 1), F32)],
        compiler_params=_params("arbitrary"),
        name="route",
    )(scores_t, bias_col)


def _dest_kernel(e_ref, rank_ref, start_ref, dest_ref):
    tr = e_ref.shape[1]
    ei = lax.broadcasted_iota(jnp.int32, (N_EXPERTS, tr), 0)
    start = start_ref[...]
    for k in range(TOP_K):
        hit = ei == e_ref[k:k + 1, :]
        off = jnp.sum(jnp.where(hit, start, 0.0), axis=0, keepdims=True)
        dest_ref[k:k + 1, :] = rank_ref[k:k + 1, :] + off.astype(jnp.int32)


def _dest_rows(top_e, rank, start_col):
    t = top_e.shape[1]
    tr = T_ROUTE
    spec = pl.BlockSpec((TOP_K, tr), lambda i: (0, i))
    return pl.pallas_call(
        _dest_kernel,
        grid=(t // tr,),
        in_specs=[spec, spec, pl.BlockSpec((N_EXPERTS, 1), lambda i: (0, 0))],
        out_specs=spec,
        out_shape=jax.ShapeDtypeStruct((TOP_K, t), jnp.int32),
        compiler_params=_params("arbitrary"),
        name="dest_rows",
    )(top_e, rank, start_col)


def _moe_kernel(order_ref, be_ref, nv_ref, cs_ref, nu_ref, h_ref, wg_ref, wu_ref, wd_ref, y_ref,
                hbuf, *scratch, n_tokens, n_assign):
    xbufs, ybufs = scratch[:MOE_BUFS], scratch[MOE_BUFS:2 * MOE_BUFS]
    wg_sc, wu_sc, wd_sc, hsem = scratch[2 * MOE_BUFS:2 * MOE_BUFS + 4]
    ssems = scratch[2 * MOE_BUFS + 4:]
    i = pl.program_id(0)
    n_steps = pl.num_programs(0)
    r = ROWS_MOE
    half = D_MODEL // 2
    n_used = nu_ref[0]
    active = i <= n_used

    def copy_rows(blk, xdst):
        base = cs_ref[blk]
        for j in range(r):
            tok = order_ref[base + j] & (n_tokens - 1)
            xdst[j // 8, pl.ds(j % 8, 1), :] = hbuf[tok]

    def compute(blk, xsrc, ydst):
        row = lax.broadcasted_iota(jnp.int32, (r, half), 0)
        xp = jnp.where(row < nv_ref[blk], xsrc[...].reshape(r, half), jnp.uint32(0))
        x_lo, x_hi = _unpack_halves(xp)
        g = (jnp.dot(x_lo, wg_sc[:half, :], preferred_element_type=F32)
             + jnp.dot(x_hi, wg_sc[half:, :], preferred_element_type=F32))
        u = (jnp.dot(x_lo, wu_sc[:half, :], preferred_element_type=F32)
             + jnp.dot(x_hi, wu_sc[half:, :], preferred_element_type=F32))
        y = jnp.dot((_silu(g) * u).astype(BF16), wd_sc[...], preferred_element_type=F32)
        ydst[...] = _pack_halves(y.astype(BF16)).reshape(r // 8, 8, half)

    def scatter(blk, n_rows, v, j0, j1):
        base = cs_ref[blk]
        for j in range(j0, j1):
            dst = jnp.where(j < n_rows, order_ref[base + j], n_assign + v * r + j)
            pltpu.make_async_copy(ybufs[v].at[j // 8, pl.ds(j % 8, 1), :], y_ref.at[dst],
                                  ssems[v]).start(priority=j % 2)

    def scattered_rows(step_no):
        n_rows = jnp.where(step_no > 0, nv_ref[jnp.maximum(step_no - 1, 0)], 0)
        tail = jnp.maximum(n_rows - MOE_HEAD, 0)
        return MOE_HEAD + (tail + MOE_CHUNK - 1) // MOE_CHUNK * MOE_CHUNK

    def scatter_wait(step_no, v):
        n = scattered_rows(step_no)
        pltpu.make_async_copy(y_ref.at[pl.ds(0, n)], y_ref.at[pl.ds(r, n)], ssems[v]).wait()

    def for_buffer(step_no, fn):
        for v in range(MOE_BUFS):
            pl.when(step_no % MOE_BUFS == v)(functools.partial(fn, v))

    @pl.when(i == 0)
    def _():
        load = pltpu.make_async_copy(h_ref, hbuf, hsem)
        load.start()
        load.wait()
        copy_rows(0, xbufs[0])
        zbuf = ybufs[MOE_BUFS - 1]
        zbuf[...] = jnp.zeros_like(zbuf)
        for j in range(MOE_BUFS * r):
            pltpu.make_async_copy(zbuf.at[(j % r) // 8, pl.ds(j % 8, 1), :],
                                  y_ref.at[n_assign + j], hsem).start(priority=j % 2)
        for _ in range(MOE_BUFS):
            pltpu.make_async_copy(y_ref.at[pl.ds(0, r)], y_ref.at[pl.ds(r, r)], hsem).wait()

    @pl.when((i >= 2) & (i - 2 <= n_used))
    def _():
        for_buffer(i + MOE_BUFS - 3, functools.partial(scatter_wait, i - 2))

    prev = be_ref[jnp.maximum(i - 1, 0)]

    @pl.when((i == 0) | (be_ref[i] != prev))
    def _():
        wg_sc[...] = wg_ref[0].astype(BF16)
        wu_sc[...] = wu_ref[0].astype(BF16)
        wd_sc[...] = wd_ref[0].astype(BF16)

    def step(v):
        last = jnp.maximum(i - 1, 0)
        n_rows = jnp.where(i > 0, nv_ref[last], 0)
        src = (v - 1) % MOE_BUFS
        scatter(last, n_rows, src, 0, MOE_HEAD)
        copy_rows(i + 1, xbufs[(v + 1) % MOE_BUFS])
        compute(i, xbufs[v], ybufs[v])
        for j0 in range(MOE_HEAD, r, MOE_CHUNK):
            pl.when(j0 < n_rows)(functools.partial(scatter, last, n_rows, src, j0, j0 + MOE_CHUNK))

    @pl.when(active)
    def _():
        for_buffer(i, step)

    @pl.when(i == n_steps - 1)
    def _():
        @pl.when(i - 1 <= n_used)
        def _():
            for_buffer(i + MOE_BUFS - 2, functools.partial(scatter_wait, i - 1))

        @pl.when(active)
        def _():
            for_buffer(i + MOE_BUFS - 1, functools.partial(scatter_wait, i))


def _moe_experts(order, block_expert, block_valid, block_start, n_used, h2, w_gate, w_up, w_down):
    t = h2.shape[0]
    assert t & (t - 1) == 0, "token count must be a power of two"
    n_blocks = block_expert.shape[0] - 2
    n_assign = TOP_K * t
    wspec = lambda shp: pl.BlockSpec((1,) + shp, lambda i, od, be, nv, cs, nu: (be[i], 0, 0))
    row_buf = pltpu.VMEM((ROWS_MOE // 8, 8, D_MODEL // 2), jnp.uint32)
    grid_spec = pltpu.PrefetchScalarGridSpec(
        num_scalar_prefetch=5,
        grid=(n_blocks + 1,),
        in_specs=[pl.BlockSpec(memory_space=pl.ANY),
                  wspec((D_MODEL, EXPERT_DIM)), wspec((D_MODEL, EXPERT_DIM)),
                  wspec((EXPERT_DIM, D_MODEL))],
        out_specs=pl.BlockSpec(memory_space=pl.ANY),
        scratch_shapes=([pltpu.VMEM((t, 1, D_MODEL // 2), jnp.uint32)]
                        + [row_buf] * (2 * MOE_BUFS)
                        + [pltpu.VMEM((D_MODEL, EXPERT_DIM), BF16),
                           pltpu.VMEM((D_MODEL, EXPERT_DIM), BF16),
                           pltpu.VMEM((EXPERT_DIM, D_MODEL), BF16),
                           pltpu.SemaphoreType.DMA]
                        + [pltpu.SemaphoreType.DMA] * MOE_BUFS),
    )
    return pl.pallas_call(
        functools.partial(_moe_kernel, n_tokens=t, n_assign=n_assign),
        grid_spec=grid_spec,
        out_shape=jax.ShapeDtypeStruct((n_assign + MOE_BUFS * ROWS_MOE, 1, D_MODEL // 2),
                                       jnp.uint32),
        compiler_params=_params("arbitrary"),
        name="moe_experts",
    )(order, block_expert, block_valid, block_start, n_used, h2, w_gate, w_up, w_down)


def _combine_kernel(base_ref, w_ref, mod_ref, g_ref, *refs, tiles_per_batch, last_layer):
    y_refs, o_ref = refs[:TOP_K], refs[TOP_K]
    tc = base_ref.shape[0]
    lo = [None] * PACKED_ROWS
    hi = [None] * PACKED_ROWS
    for k in range(TOP_K):
        wk = w_ref[:, k:k + 1]
        for cb in range(PACKED_ROWS):
            yp = y_refs[k][pl.ds(cb, tc, stride=PACKED_ROWS), :]
            y_lo = lax.bitcast_convert_type(yp << 16, F32) * wk
            y_hi = lax.bitcast_convert_type(yp & jnp.uint32(0xFFFF0000), F32) * wk
            lo[cb] = y_lo if lo[cb] is None else lo[cb] + y_lo
            hi[cb] = y_hi if hi[cb] is None else hi[cb] + y_hi
    routed = jnp.concatenate(lo + hi, axis=1)
    b = pl.program_id(0) // tiles_per_batch
    gate2 = mod_ref[pl.ds(b, 1), 5 * D_MODEL:6 * D_MODEL]
    x = base_ref[...] + gate2 * routed
    if last_layer:
        x = x * lax.rsqrt(jnp.mean(x * x, axis=-1, keepdims=True) + NORM_EPS) * g_ref[...]
    o_ref[...] = x


def _combine(base, top_w, mod, final_g, y, seq, last_layer):
    t = base.shape[0]
    tc = T_COMBINE
    kern = functools.partial(_combine_kernel, tiles_per_batch=seq // tc, last_layer=last_layer)
    y_rows = y.reshape(y.shape[0] * PACKED_ROWS, LANES)
    tiles = t // tc
    y_specs = [pl.BlockSpec((tc * PACKED_ROWS, LANES), lambda i, k=k: (k * tiles + i, 0))
               for k in range(TOP_K)]
    return pl.pallas_call(
        kern,
        grid=(tiles,),
        in_specs=[pl.BlockSpec((tc, D_MODEL), lambda i: (i, 0)),
                  pl.BlockSpec((tc, TOP_K), lambda i: (i, 0)),
                  pl.BlockSpec(mod.shape, lambda i: (0, 0)),
                  pl.BlockSpec((1, D_MODEL), lambda i: (0, 0))] + y_specs,
        out_specs=pl.BlockSpec((tc, D_MODEL), lambda i: (i, 0)),
        out_shape=jax.ShapeDtypeStruct((t, D_MODEL), F32),
        compiler_params=_params("arbitrary"),
        name="combine",
    )(base, top_w, mod, final_g, *([y_rows] * TOP_K))


def _block_tables(counts, n_assign):
    r = ROWS_MOE
    cnt = counts.reshape(N_EXPERTS).astype(jnp.int32)
    first = jnp.cumsum(cnt) - cnt
    nblk = (cnt + r - 1) // r
    blk_end = jnp.cumsum(nblk)
    blk_start = blk_end - nblk
    n_blocks = (n_assign + N_EXPERTS * (r - 1) + r - 1) // r
    bi = jnp.arange(n_blocks + 2, dtype=jnp.int32)
    bc = jnp.minimum(bi, blk_end[-1] - 1)
    block_expert = jnp.minimum(jnp.sum((blk_end[None, :] <= bc[:, None]).astype(jnp.int32), axis=1),
                               N_EXPERTS - 1)
    onehot = (block_expert[:, None] == jnp.arange(N_EXPERTS, dtype=jnp.int32)[None, :])
    pick = lambda v: jnp.sum(jnp.where(onehot, v[None, :], 0), axis=1)
    within = bc - pick(blk_start)
    block_valid = jnp.where(bi < blk_end[-1], jnp.clip(pick(cnt) - within * r, 0, r), 0)
    block_start = pick(first) + within * r
    return (first.astype(F32).reshape(N_EXPERTS, 1), block_expert.astype(jnp.int32),
            block_valid.astype(jnp.int32), block_start.astype(jnp.int32),
            blk_end[-1:].astype(jnp.int32))


def kernel(x, c, w_ada, b_ada, norm1_g, w_in, b_forget, ret_gn_g, w_branch_a, w_branch_b, w_out,
           norm2_g, w_router, router_bias, w_exp_gate, w_exp_up, w_exp_down, w_sh_gate, w_sh_up,
           w_sh_down, final_g):
    batch, seq, d = x.shape
    t = batch * seq
    depth = w_ada.shape[0]
    x2 = x.reshape(t, d)

    half = RET_QK_DIM // 2
    inv_freq = ROPE_BASE ** (-jnp.arange(half, dtype=F32) / half)
    ang = jnp.arange(seq, dtype=F32)[:, None] * inv_freq[None, :]
    cos32, sin32 = jnp.cos(ang), jnp.sin(ang)
    cos_t = jnp.concatenate([cos32, cos32, cos32, cos32], axis=-1)
    sin_t = jnp.concatenate([-sin32, sin32, -sin32, sin32], axis=-1)
    dec_t, zeta_t, xi_t, cd_t = _retention_tables()
    c_pad = jnp.zeros((8, d), F32).at[:batch].set(c)

    for l in range(depth):
        mod = _adaln_mod(c_pad, w_ada[l], b_ada[l])

        w = w_in[l]
        o_fq, o_fk, o_fv, o_ff = 0, FOX_WIDTH, 2 * FOX_WIDTH, 3 * FOX_WIDTH
        o_rq = o_ff + FOX_HEADS
        o_rk = o_rq + RET_QK_WIDTH
        o_rv = o_rk + RET_QK_WIDTH
        o_rg = o_rv + RET_V_WIDTH
        o_ga = o_rg + RET_V_WIDTH
        o_gb = o_ga + D_MODEL
        w_p = jnp.concatenate(
            [w[:, o_rv:o_rg], w[:, o_rg:o_ga], w[:, o_ga:o_gb], w[:, o_gb:o_gb + D_MODEL],
             w[:, o_fq:o_fk], w[:, o_fk:o_fv], w[:, o_fv:o_ff], w[:, o_rq:o_rk], w[:, o_rk:o_rv]],
            axis=1).astype(BF16)
        w_ff = jnp.zeros((d, LANES), BF16).at[:, :FOX_HEADS].set(w[:, o_ff:o_rq].astype(BF16))
        b_ff = jnp.zeros((1, LANES), F32).at[0, :FOX_HEADS].set(b_forget[l].astype(F32))

        proj, logf = _in_proj(x2, mod, norm1_g[l].reshape(1, d), w_p, w_ff, b_ff, cos_t, sin_t, seq)
        cum_f = jnp.cumsum(logf[:, :FOX_HEADS].reshape(batch, seq, FOX_HEADS), axis=1)
        cum_f = cum_f.transpose(0, 2, 1) * LOG2_E

        attn = _fox_attn(proj, cum_f, batch, seq)
        ret = _retention(proj, ret_gn_g[l].reshape(1, RET_V_WIDTH), dec_t, zeta_t, xi_t, cd_t,
                         batch, seq)

        wr_head, wr_rest = _split_bf16(w_router[l].T.astype(F32))
        base, h2, scores_t = _post_mix(
            attn, ret, proj, x2, mod, norm2_g[l].reshape(1, d),
            w_branch_a[l].astype(BF16), w_branch_b[l].astype(BF16), w_out[l].astype(BF16),
            wr_head, wr_rest, w_sh_gate[l].astype(BF16), w_sh_up[l].astype(BF16),
            w_sh_down[l].astype(BF16), seq)

        top_e, top_w, rank, counts = _route(scores_t,
                                            router_bias[l].astype(F32).reshape(N_EXPERTS, 1))
        first, block_expert, block_valid, block_start, n_used = _block_tables(counts, t * TOP_K)
        sorted_pos = _dest_rows(top_e, rank, first).reshape(TOP_K * t)
        order = jnp.concatenate([jnp.argsort(sorted_pos).astype(jnp.int32),
                                 jnp.zeros((ROWS_MOE,), jnp.int32)])
        y = _moe_experts(order, block_expert, block_valid, block_start, n_used,
                         h2.reshape(t, 1, d // 2),
                         w_exp_gate[l], w_exp_up[l], w_exp_down[l])
        x2 = _combine(base, top_w.T, mod, final_g.reshape(1, d), y, seq, l == depth - 1)
    return x2.reshape(batch, seq, d)
```

```python
import functools

import jax
import jax.numpy as jnp
import numpy as np
from jax import lax
from jax.experimental import pallas as pl
from jax.experimental.pallas import tpu as pltpu

F32 = jnp.float32
BF16 = jnp.bfloat16

D_MODEL = 1024
FOX_HEADS = 8
FOX_HEAD_DIM = 64
FOX_WIDTH = 512
RET_HEADS = 8
RET_QK_DIM = 64
RET_V_DIM = 128
RET_QK_WIDTH = 512
RET_V_WIDTH = 1024
RET_CHUNK = 256
ROPE_BASE = 10000.0
N_EXPERTS = 256
TOP_K = 8
N_GROUPS = 8
TOPK_GROUPS = 4
EXPERT_DIM = 256
ROUTED_SCALE = 2.5
NORM_EPS = 1e-6
LOG2_E = 1.4426950408889634

LANES = 128
PACKED_ROWS = D_MODEL // 2 // LANES
VMEM_LIMIT = 56 * 1024 * 1024

COL_RV, COL_RG, COL_GA, COL_GB = 0, 1024, 2048, 3072
COL_FQ, COL_FK, COL_FV, COL_RQ, COL_RK = 4096, 4608, 5120, 5632, 6144
PROJ_WIDTH = 6656

TM_PROJ = 512
TN_PROJ = 512
T_ATTN = 1024
T_RET = 512
TM_POST = 512
T_ROUTE = 512
ROWS_MOE = 256
MOE_BUFS = 3
MOE_HEAD = 128
MOE_CHUNK = 32
T_COMBINE = 512


def _sigmoid(z):
    return 1.0 / (1.0 + jnp.exp(-z))


def _silu(z):
    return z * _sigmoid(z)


def _pack_halves(xb):
    n = xb.shape[1] // 2
    lo = lax.bitcast_convert_type(xb[:, :n].astype(F32), jnp.uint32)
    hi = lax.bitcast_convert_type(xb[:, n:].astype(F32), jnp.uint32)
    return (lo >> 16) | hi


def _unpack_halves(xp):
    lo = lax.bitcast_convert_type(xp << 16, F32)
    hi = lax.bitcast_convert_type(xp & jnp.uint32(0xFFFF0000), F32)
    return lo.astype(BF16), hi.astype(BF16)


def _split_bf16(x):
    head = lax.bitcast_convert_type(
        lax.bitcast_convert_type(x, jnp.uint32) & jnp.uint32(0xFFFF0000), F32)
    return head.astype(BF16), (x - head).astype(BF16)


def _params(*sem):
    return pltpu.CompilerParams(dimension_semantics=sem, vmem_limit_bytes=VMEM_LIMIT)


def _adaln_kernel(c_ref, w_ref, b_ref, o_ref):
    a = _silu(c_ref[...]).astype(BF16)
    o_ref[...] = jnp.dot(a, w_ref[...].astype(BF16), preferred_element_type=F32) + b_ref[...]


def _adaln_mod(c_pad, w_ada, b_ada):
    n = w_ada.shape[1]
    tn = 1024
    return pl.pallas_call(
        _adaln_kernel,
        grid=(n // tn,),
        in_specs=[pl.BlockSpec((8, D_MODEL), lambda j: (0, 0)),
                  pl.BlockSpec((D_MODEL, tn), lambda j: (0, j)),
                  pl.BlockSpec((1, tn), lambda j: (0, j))],
        out_specs=pl.BlockSpec((8, tn), lambda j: (0, j)),
        out_shape=jax.ShapeDtypeStruct((8, n), F32),
        compiler_params=_params("arbitrary"),
        name="adaln_mod",
    )(c_pad, w_ada, b_ada.reshape(1, n))


def _inproj_kernel(x_ref, mod_ref, g_ref, w_ref, wff_ref, bf_ref, cos_ref, sin_ref,
                   proj_ref, cumf_ref, cf_sc, *, tiles_per_batch):
    i = pl.program_id(0)
    b = i // tiles_per_batch
    x = x_ref[...]
    y = x * lax.rsqrt(jnp.mean(x * x, axis=-1, keepdims=True) + NORM_EPS) * g_ref[...]
    shift = mod_ref[pl.ds(b, 1), 0:D_MODEL]
    scale = mod_ref[pl.ds(b, 1), D_MODEL:2 * D_MODEL]
    hb = (y * (1.0 + scale) + shift).astype(BF16)
    z = jnp.dot(hb, wff_ref[...], preferred_element_type=F32) + bf_ref[...]
    logf = jnp.minimum(z, 0.0) - jnp.log1p(jnp.exp(-jnp.abs(z)))
    tm = logf.shape[0]
    tri = jnp.where(lax.broadcasted_iota(jnp.int32, (tm, tm), 1)
                    <= lax.broadcasted_iota(jnp.int32, (tm, tm), 0), 1.0, 0.0).astype(BF16)
    h1, _ = _split_bf16(logf)
    h2, r2 = _split_bf16(logf - h1.astype(F32))
    carry = jnp.where(i % tiles_per_batch == 0, 0.0, cf_sc[...])
    cum = (jnp.dot(tri, h1, preferred_element_type=F32) + jnp.dot(tri, h2, preferred_element_type=F32)
           + jnp.dot(tri, r2, preferred_element_type=F32)) + carry
    cf_sc[...] = cum[tm - 1:tm, :]
    cumf_ref[0] = cum.T[0:FOX_HEADS, :] * LOG2_E

    cos = cos_ref[...]
    sin = sin_ref[...]
    lane = lax.broadcasted_iota(jnp.int32, cos.shape, 1)
    first_half = (lane % RET_QK_DIM) < (RET_QK_DIM // 2)

    for c0 in range(0, PROJ_WIDTH, TN_PROJ):
        cols = slice(c0, c0 + TN_PROJ)
        acc = jnp.dot(hb, w_ref[:, cols], preferred_element_type=F32)
        if COL_GA <= c0 < COL_FQ:
            proj_ref[:, cols] = _sigmoid(acc).astype(BF16)
        elif COL_FQ <= c0 < COL_FK:
            proj_ref[:, cols] = (acc * (LOG2_E * FOX_HEAD_DIM ** -0.5)).astype(BF16)
        elif c0 >= COL_RQ:
            k_scale = RET_QK_DIM ** -0.5 if c0 >= COL_RK else 1.0
            for cb in range(c0, c0 + TN_PROJ, LANES):
                xs = acc[:, cb - c0:cb - c0 + LANES]
                up = pltpu.roll(xs, LANES - RET_QK_DIM // 2, axis=1)
                dn = pltpu.roll(xs, RET_QK_DIM // 2, axis=1)
                rot = xs * cos + jnp.where(first_half, up, dn) * sin
                proj_ref[:, cb:cb + LANES] = (rot * k_scale).astype(BF16)
        else:
            proj_ref[:, cols] = acc.astype(BF16)


def _in_proj(x2, mod, norm_g, w_p, w_ff, b_ff, cos_t, sin_t, seq):
    t = x2.shape[0]
    tiles_per_batch = seq // TM_PROJ
    kern = functools.partial(_inproj_kernel, tiles_per_batch=tiles_per_batch)
    return pl.pallas_call(
        kern,
        grid=(t // TM_PROJ,),
        in_specs=[pl.BlockSpec((TM_PROJ, D_MODEL), lambda i: (i, 0)),
                  pl.BlockSpec(mod.shape, lambda i: (0, 0)),
                  pl.BlockSpec((1, D_MODEL), lambda i: (0, 0)),
                  pl.BlockSpec((D_MODEL, PROJ_WIDTH), lambda i: (0, 0)),
                  pl.BlockSpec((D_MODEL, LANES), lambda i: (0, 0)),
                  pl.BlockSpec((1, LANES), lambda i: (0, 0)),
                  pl.BlockSpec((TM_PROJ, LANES), lambda i: (i % tiles_per_batch, 0)),
                  pl.BlockSpec((TM_PROJ, LANES), lambda i: (i % tiles_per_batch, 0))],
        out_specs=[pl.BlockSpec((TM_PROJ, PROJ_WIDTH), lambda i: (i, 0)),
                   pl.BlockSpec((1, FOX_HEADS, TM_PROJ),
                                lambda i: (i // tiles_per_batch, 0, i % tiles_per_batch))],
        out_shape=[jax.ShapeDtypeStruct((t, PROJ_WIDTH), BF16),
                   jax.ShapeDtypeStruct((t // seq, FOX_HEADS, seq), F32)],
        scratch_shapes=[pltpu.VMEM((1, LANES), F32)],
        compiler_params=_params("arbitrary"),
        name="in_proj",
    )(x2, mod, norm_g, w_p, w_ff, b_ff, cos_t, sin_t)


def _attn_kernel(q_ref, k_ref, v_ref, f_ref, o_ref):
    hp = pl.program_id(1)
    qi = pl.program_id(2)
    t_attn = T_ATTN
    q = q_ref[...]
    lane = lax.broadcasted_iota(jnp.int32, (t_attn, LANES), 1)
    zeros = jnp.zeros_like(q)
    qm = (jnp.where(lane < FOX_HEAD_DIM, q, zeros), jnp.where(lane >= FOX_HEAD_DIM, q, zeros))
    lane1 = lax.broadcasted_iota(jnp.int32, (1, LANES), 1)
    own = (jnp.where(lane1 < FOX_HEAD_DIM, 1.0, 0.0).astype(BF16),
           jnp.where(lane1 >= FOX_HEAD_DIM, 1.0, 0.0).astype(BF16))

    def block(r0, nr, ks, nk, carry, masked):
        k = k_ref[pl.ds(ks, nk), :]
        v = v_ref[pl.ds(ks, nk), :]
        v1 = (v * own[0] + own[1], v * own[1] + own[0])
        out = []
        for hh in range(2):
            m, acc = carry[hh]
            s = lax.dot_general(qm[hh][r0:r0 + nr], k, (((1,), (1,)), ((), ())),
                                preferred_element_type=F32)
            s = s - f_ref[0, pl.ds(2 * hp + hh, 1), pl.ds(ks, nk)]
            if masked:
                row = qi * t_attn + r0 + lax.broadcasted_iota(jnp.int32, (nr, nk), 0)
                col = ks + lax.broadcasted_iota(jnp.int32, (nr, nk), 1)
                s = jnp.where(col <= row, s, -jnp.inf)
            m_new = jnp.maximum(m, jnp.max(s, axis=-1, keepdims=True))
            p = jnp.exp2(s - m_new).astype(BF16)
            acc = jnp.exp2(m - m_new) * acc + jnp.dot(p, v1[hh], preferred_element_type=F32)
            out.append((m_new, acc))
        return tuple(out)

    def step(kb, carry):
        return block(0, t_attn, pl.multiple_of(kb * t_attn, t_attn), t_attn, carry, False)

    init = (jnp.full((t_attn, 1), -jnp.inf, F32), jnp.zeros((t_attn, LANES), F32))
    def pair(kp, c):
        return step(2 * kp + 1, step(2 * kp, c))

    carry = lax.fori_loop(0, qi // 2, pair, (init, init))
    carry = lax.cond(qi % 2 == 1, lambda c: step(qi - 1, c), lambda c: c, carry)
    half = t_attn // 2
    ks = pl.multiple_of(qi * t_attn, t_attn)
    rows = lambda c, r0: tuple((m[r0:r0 + half], acc[r0:r0 + half]) for m, acc in c)
    top = block(0, half, ks, half, rows(carry, 0), True)
    bot = block(half, half, ks, t_attn, rows(carry, half), True)
    (_, acc0), (_, acc1) = tuple(
        (jnp.concatenate([mt, mb], axis=0), jnp.concatenate([at, ab], axis=0))
        for (mt, at), (mb, ab) in zip(top, bot))
    o_ref[...] = jnp.where(lane < FOX_HEAD_DIM, acc0 / acc0[:, FOX_HEAD_DIM:FOX_HEAD_DIM + 1],
                           acc1 / acc1[:, 0:1]).astype(BF16)


def _fox_attn(proj, cum_f, batch, seq):
    t = proj.shape[0]
    nq = seq // T_ATTN
    cq, ck, cv = COL_FQ // LANES, COL_FK // LANES, COL_FV // LANES
    return pl.pallas_call(
        _attn_kernel,
        grid=(batch, FOX_HEADS // 2, nq),
        in_specs=[pl.BlockSpec((T_ATTN, LANES), lambda b, hp, qi: (b * nq + qi, cq + hp)),
                  pl.BlockSpec((seq, LANES), lambda b, hp, qi: (b, ck + hp)),
                  pl.BlockSpec((seq, LANES), lambda b, hp, qi: (b, cv + hp)),
                  pl.BlockSpec((1, FOX_HEADS, seq), lambda b, hp, qi: (b, 0, 0))],
        out_specs=pl.BlockSpec((T_ATTN, LANES), lambda b, hp, qi: (b * nq + qi, hp)),
        out_shape=jax.ShapeDtypeStruct((t, FOX_WIDTH), BF16),
        compiler_params=_params("arbitrary", "arbitrary", "arbitrary"),
        name="fox_attn",
    )(proj, proj, proj, cum_f)


def _ret_kernel(q_ref, k_ref, v_ref, rg_ref, gn_ref, dec_ref, zeta_ref, xi_ref, cd_ref,
                o_ref, st_sc):
    ri = pl.program_id(1)
    c = RET_CHUNK

    @pl.when(ri == 0)
    def _():
        st_sc[...] = jnp.zeros_like(st_sc)

    lane = lax.broadcasted_iota(jnp.int32, (c, LANES), 1)
    zeros = jnp.zeros((c, LANES), BF16)
    states = [st_sc[h] for h in range(RET_HEADS)]
    for n in range(T_RET // c):
        rows = slice(n * c, (n + 1) * c)
        for h in range(RET_HEADS):
            pair = slice((h // 2) * LANES, (h // 2 + 1) * LANES)
            vcol = slice(h * RET_V_DIM, (h + 1) * RET_V_DIM)
            sel = (lane < RET_QK_DIM) if h % 2 == 0 else (lane >= RET_QK_DIM)
            qc = jnp.where(sel, q_ref[rows, pair], zeros)
            kc = jnp.where(sel, k_ref[rows, pair], zeros)
            vc = v_ref[rows, vcol]
            sc = lax.dot_general(qc, kc, (((1,), (1,)), ((), ())),
                                 preferred_element_type=F32) * dec_ref[h]
            intra = jnp.dot(sc.astype(BF16), vc, preferred_element_type=F32)
            cross = jnp.dot(qc, states[h].astype(BF16), preferred_element_type=F32) * xi_ref[h]
            o = intra + cross
            kz = (kc.astype(F32) * zeta_ref[h]).astype(BF16)
            kv = lax.dot_general(kz, vc, (((0,), (0,)), ((), ())), preferred_element_type=F32)
            states[h] = cd_ref[h] * states[h] + kv
            mu = jnp.mean(o, axis=-1, keepdims=True)
            d = o - mu
            var = jnp.mean(d * d, axis=-1, keepdims=True)
            on = d * lax.rsqrt(var + NORM_EPS) * gn_ref[:, vcol]
            rg = rg_ref[rows, vcol].astype(F32)
            o_ref[rows, vcol] = (on * _silu(rg)).astype(BF16)
    for h in range(RET_HEADS):
        st_sc[h] = states[h]


def _retention(proj, gn_g, dec_t, zeta_t, xi_t, cd_t, batch, seq):
    t = proj.shape[0]
    nr = seq // T_RET
    tok = lambda b, ri: b * nr + ri
    full = lambda a: pl.BlockSpec(a.shape, lambda b, ri: (0,) * a.ndim)
    return pl.pallas_call(
        _ret_kernel,
        grid=(batch, nr),
        in_specs=[pl.BlockSpec((T_RET, RET_QK_WIDTH), lambda b, ri: (tok(b, ri), COL_RQ // RET_QK_WIDTH)),
                  pl.BlockSpec((T_RET, RET_QK_WIDTH), lambda b, ri: (tok(b, ri), COL_RK // RET_QK_WIDTH)),
                  pl.BlockSpec((T_RET, RET_V_WIDTH), lambda b, ri: (tok(b, ri), COL_RV // RET_V_WIDTH)),
                  pl.BlockSpec((T_RET, RET_V_WIDTH), lambda b, ri: (tok(b, ri), COL_RG // RET_V_WIDTH)),
                  pl.BlockSpec((1, RET_V_WIDTH), lambda b, ri: (0, 0)),
                  full(dec_t), full(zeta_t), full(xi_t), full(cd_t)],
        out_specs=pl.BlockSpec((T_RET, RET_V_WIDTH), lambda b, ri: (tok(b, ri), 0)),
        out_shape=jax.ShapeDtypeStruct((t, RET_V_WIDTH), BF16),
        scratch_shapes=[pltpu.VMEM((RET_HEADS, LANES, RET_V_DIM), F32)],
        compiler_params=_params("arbitrary", "arbitrary"),
        name="retention",
    )(proj, proj, proj, proj, gn_g, dec_t, zeta_t, xi_t, cd_t)


def _retention_tables():
    c = RET_CHUNK
    log_gamma = jnp.log1p(-jnp.exp2(-5.0 - jnp.arange(RET_HEADS, dtype=F32)))
    idx = jnp.arange(c, dtype=F32)
    diff = idx[:, None] - idx[None, :]
    dec = jnp.where(diff >= 0,
                    jnp.exp(jnp.maximum(diff, 0.0)[None] * log_gamma[:, None, None]), 0.0)
    zeta = jnp.exp((c - 1.0 - idx)[None, :] * log_gamma[:, None])
    xi = jnp.exp((idx + 1.0)[None, :] * log_gamma[:, None])
    cd = jnp.exp(c * log_gamma)
    bc = lambda v: jnp.broadcast_to(v[:, :, None], (RET_HEADS, c, LANES))
    cd_t = jnp.broadcast_to(cd[:, None, None], (RET_HEADS, LANES, RET_V_DIM))
    return dec, bc(zeta), bc(xi), cd_t


def _post_kernel(attn_ref, ret_ref, ga_ref, gb_ref, x_ref, mod_ref, g2_ref,
                 wa_ref, wb_ref, wo_ref, wr_ref, wrr_ref, wsg_ref, wsu_ref, wsd_ref,
                 base_ref, h2_ref, score_ref, *, tiles_per_batch):
    b = pl.program_id(0) // tiles_per_batch
    ya = jnp.dot(attn_ref[...], wa_ref[...], preferred_element_type=F32)
    yb = jnp.dot(ret_ref[...], wb_ref[...], preferred_element_type=F32)
    merged = ga_ref[...].astype(F32) * ya + gb_ref[...].astype(F32) * yb
    mix = jnp.dot(merged.astype(BF16), wo_ref[...], preferred_element_type=F32)
    d = D_MODEL
    gate1 = mod_ref[pl.ds(b, 1), 2 * d:3 * d]
    shift2 = mod_ref[pl.ds(b, 1), 3 * d:4 * d]
    scale2 = mod_ref[pl.ds(b, 1), 4 * d:5 * d]
    gate2 = mod_ref[pl.ds(b, 1), 5 * d:6 * d]
    x1 = x_ref[...] + gate1 * mix
    y = x1 * lax.rsqrt(jnp.mean(x1 * x1, axis=-1, keepdims=True) + NORM_EPS) * g2_ref[...]
    h2 = y * (1.0 + scale2) + shift2
    h2_head, h2_rest = _split_bf16(h2)
    nt = (((1,), (1,)), ((), ()))
    logits_t = (lax.dot_general(wr_ref[...], h2_head, nt, preferred_element_type=F32)
                + lax.dot_general(wr_ref[...], h2_rest, nt, preferred_element_type=F32)
                + lax.dot_general(wrr_ref[...], h2_head, nt, preferred_element_type=F32))
    score_ref[...] = _sigmoid(logits_t)
    h2b = h2.astype(BF16)
    h2p = _pack_halves(h2b)
    n_chunks = h2p.shape[1] // LANES
    for cb in range(n_chunks):
        h2_ref[pl.ds(cb, h2p.shape[0], stride=n_chunks), :] = h2p[:, cb * LANES:(cb + 1) * LANES]
    g = jnp.dot(h2b, wsg_ref[...], preferred_element_type=F32)
    u = jnp.dot(h2b, wsu_ref[...], preferred_element_type=F32)
    shared = jnp.dot((_silu(g) * u).astype(BF16), wsd_ref[...], preferred_element_type=F32)
    base_ref[...] = x1 + gate2 * shared


def _post_mix(attn, ret, proj, x2, mod, norm2_g, wa, wb, wo, wr, wrr, wsg, wsu, wsd, seq):
    t = x2.shape[0]
    tm = TM_POST
    kern = functools.partial(_post_kernel, tiles_per_batch=seq // tm)
    full = lambda a: pl.BlockSpec(a.shape, lambda i: (0,) * a.ndim)
    return pl.pallas_call(
        kern,
        grid=(t // tm,),
        in_specs=[pl.BlockSpec((tm, FOX_WIDTH), lambda i: (i, 0)),
                  pl.BlockSpec((tm, RET_V_WIDTH), lambda i: (i, 0)),
                  pl.BlockSpec((tm, D_MODEL), lambda i: (i, COL_GA // D_MODEL)),
                  pl.BlockSpec((tm, D_MODEL), lambda i: (i, COL_GB // D_MODEL)),
                  pl.BlockSpec((tm, D_MODEL), lambda i: (i, 0)),
                  full(mod), full(norm2_g), full(wa), full(wb), full(wo), full(wr), full(wrr),
                  full(wsg), full(wsu), full(wsd)],
        out_specs=[pl.BlockSpec((tm, D_MODEL), lambda i: (i, 0)),
                   pl.BlockSpec((tm * PACKED_ROWS, LANES), lambda i: (i, 0)),
                   pl.BlockSpec((N_EXPERTS, tm), lambda i: (0, i))],
        out_shape=[jax.ShapeDtypeStruct((t, D_MODEL), F32),
                   jax.ShapeDtypeStruct((t * PACKED_ROWS, LANES), jnp.uint32),
                   jax.ShapeDtypeStruct((N_EXPERTS, t), F32)],
        compiler_params=_params("arbitrary"),
        name="post_mix",
    )(attn, ret, proj, proj, x2, mod, norm2_g, wa, wb, wo, wr, wrr, wsg, wsu, wsd)


def _route_kernel(s_ref, bias_ref, e_ref, w_ref, rank_ref, cnt_ref):
    tr = s_ref.shape[1]
    gsz = N_EXPERTS // N_GROUPS

    @pl.when(pl.program_id(0) == 0)
    def _():
        cnt_ref[...] = jnp.zeros_like(cnt_ref)

    s = s_ref[...]
    sel = s + bias_ref[...]
    neg = jnp.float32(-jnp.inf)

    g3 = sel.reshape(N_GROUPS, gsz, tr)
    m1 = jnp.max(g3, axis=1)
    n_max = jnp.sum((g3 == m1[:, None, :]).astype(F32), axis=1)
    m2 = jnp.max(jnp.where(g3 < m1[:, None, :], g3, neg), axis=1)
    gs = m1 + jnp.where(n_max >= 2.0, m1, m2)

    gi = lax.broadcasted_iota(jnp.int32, (N_GROUPS, tr), 0)
    beaten = jnp.zeros((N_GROUPS, tr), F32)
    for j in range(N_GROUPS):
        gj = gs[j:j + 1, :]
        beaten = beaten + jnp.where((gj > gs) | ((gj == gs) & (gi > j)), 1.0, 0.0)
    keep = jnp.where(beaten < float(TOPK_GROUPS), 1.0, 0.0)
    keep_e = jnp.broadcast_to(keep[:, None, :], (N_GROUPS, gsz, tr)).reshape(N_EXPERTS, tr)

    ei = lax.broadcasted_iota(jnp.int32, (N_EXPERTS, tr), 0)
    val = jnp.where(keep_e > 0.0, sel, neg)
    member = jnp.zeros((N_EXPERTS, tr), F32)
    idx_rows, w_rows = [], []
    for _ in range(TOP_K):
        m = jnp.max(val, axis=0, keepdims=True)
        idx = jnp.min(jnp.where(val == m, ei, N_EXPERTS), axis=0, keepdims=True)
        hit = ei == idx
        w_rows.append(jnp.sum(jnp.where(hit, s, 0.0), axis=0, keepdims=True))
        idx_rows.append(idx)
        val = jnp.where(hit, neg, val)
        member = jnp.where(hit, 1.0, member)

    w_sum = w_rows[0]
    for k in range(1, TOP_K):
        w_sum = w_sum + w_rows[k]

    r_i = lax.broadcasted_iota(jnp.int32, (tr, tr), 0)
    c_i = lax.broadcasted_iota(jnp.int32, (tr, tr), 1)
    upper = jnp.where(r_i < c_i, 1.0, 0.0).astype(BF16)
    before = jnp.dot(member.astype(BF16), upper, preferred_element_type=F32) + cnt_ref[...]

    for k in range(TOP_K):
        hit = ei == idx_rows[k]
        e_ref[k:k + 1, :] = idx_rows[k]
        w_ref[k:k + 1, :] = w_rows[k] / w_sum * ROUTED_SCALE
        rank_ref[k:k + 1, :] = jnp.sum(jnp.where(hit, before, 0.0), axis=0,
                                       keepdims=True).astype(jnp.int32)
    cnt_ref[...] = cnt_ref[...] + jnp.sum(member, axis=1, keepdims=True)


def _route(scores_t, bias_col):
    t = scores_t.shape[1]
    tr = T_ROUTE
    row8 = lambda dt: jax.ShapeDtypeStruct((TOP_K, t), dt)
    return pl.pallas_call(
        _route_kernel,
        grid=(t // tr,),
        in_specs=[pl.BlockSpec((N_EXPERTS, tr), lambda i: (0, i)),
                  pl.BlockSpec((N_EXPERTS, 1), lambda i: (0, 0))],
        out_specs=[pl.BlockSpec((TOP_K, tr), lambda i: (0, i)),
                   pl.BlockSpec((TOP_K, tr), lambda i: (0, i)),
                   pl.BlockSpec((TOP_K, tr), lambda i: (0, i)),
                   pl.BlockSpec((N_EXPERTS, 1), lambda i: (0, 0))],
        out_shape=[row8(jnp.int32), row8(F32), row8(jnp.int32),
                   jax.ShapeDtypeStruct((N_EXPERTS, 1), F32)],
        compiler_params=_params("arbitrary"),
        name="route",
    )(scores_t, bias_col)


def _dest_kernel(e_ref, rank_ref, start_ref, dest_ref):
    tr = e_ref.shape[1]
    ei = lax.broadcasted_iota(jnp.int32, (N_EXPERTS, tr), 0)
    start = start_ref[...]
    for k in range(TOP_K):
        hit = ei == e_ref[k:k + 1, :]
        off = jnp.sum(jnp.where(hit, start, 0.0), axis=0, keepdims=True)
        dest_ref[k:k + 1, :] = rank_ref[k:k + 1, :] + off.astype(jnp.int32)


def _dest_rows(top_e, rank, start_col):
    t = top_e.shape[1]
    tr = T_ROUTE
    spec = pl.BlockSpec((TOP_K, tr), lambda i: (0, i))
    return pl.pallas_call(
        _dest_kernel,
        grid=(t // tr,),
        in_specs=[spec, spec, pl.BlockSpec((N_EXPERTS, 1), lambda i: (0, 0))],
        out_specs=spec,
        out_shape=jax.ShapeDtypeStruct((TOP_K, t), jnp.int32),
        compiler_params=_params("arbitrary"),
        name="dest_rows",
    )(top_e, rank, start_col)


def _moe_kernel(order_ref, be_ref, nv_ref, cs_ref, nu_ref, h_ref, wg_ref, wu_ref, wd_ref, y_ref,
                hbuf, *scratch, n_tokens, n_assign):
    xbufs, ybufs = scratch[:MOE_BUFS], scratch[MOE_BUFS:2 * MOE_BUFS]
    wg_sc, wu_sc, wd_sc, hsem = scratch[2 * MOE_BUFS:2 * MOE_BUFS + 4]
    ssems = scratch[2 * MOE_BUFS + 4:]
    i = pl.program_id(0)
    n_steps = pl.num_programs(0)
    r = ROWS_MOE
    half = D_MODEL // 2
    n_used = nu_ref[0]
    active = i <= n_used

    def copy_rows(blk, xdst):
        base = cs_ref[blk]
        for j in range(r):
            tok = order_ref[base + j] & (n_tokens - 1)
            xdst[j // 8, pl.ds(j % 8, 1), :] = hbuf[tok]

    def compute(blk, xsrc, ydst):
        row = lax.broadcasted_iota(jnp.int32, (r, half), 0)
        xp = jnp.where(row < nv_ref[blk], xsrc[...].reshape(r, half), jnp.uint32(0))
        x_lo, x_hi = _unpack_halves(xp)
        g = (jnp.dot(x_lo, wg_sc[:half, :], preferred_element_type=F32)
             + jnp.dot(x_hi, wg_sc[half:, :], preferred_element_type=F32))
        u = (jnp.dot(x_lo, wu_sc[:half, :], preferred_element_type=F32)
             + jnp.dot(x_hi, wu_sc[half:, :], preferred_element_type=F32))
        y = jnp.dot((_silu(g) * u).astype(BF16), wd_sc[...], preferred_element_type=F32)
        ydst[...] = _pack_halves(y.astype(BF16)).reshape(r // 8, 8, half)

    def scatter(blk, n_rows, v, j0, j1):
        base = cs_ref[blk]
        for j in range(j0, j1):
            dst = jnp.where(j < n_rows, order_ref[base + j], n_assign + v * r + j)
            pltpu.make_async_copy(ybufs[v].at[j // 8, pl.ds(j % 8, 1), :], y_ref.at[dst],
                                  ssems[v]).start(priority=j % 2)

    def scattered_rows(step_no):
        n_rows = jnp.where(step_no > 0, nv_ref[jnp.maximum(step_no - 1, 0)], 0)
        tail = jnp.maximum(n_rows - MOE_HEAD, 0)
        return MOE_HEAD + (tail + MOE_CHUNK - 1) // MOE_CHUNK * MOE_CHUNK

    def scatter_wait(step_no, v):
        n = scattered_rows(step_no)
        pltpu.make_async_copy(y_ref.at[pl.ds(0, n)], y_ref.at[pl.ds(r, n)], ssems[v]).wait()

    def for_buffer(step_no, fn):
        for v in range(MOE_BUFS):
            pl.when(step_no % MOE_BUFS == v)(functools.partial(fn, v))

    @pl.when(i == 0)
    def _():
        load = pltpu.make_async_copy(h_ref, hbuf, hsem)
        load.start()
        load.wait()
        copy_rows(0, xbufs[0])
        zbuf = ybufs[MOE_BUFS - 1]
        zbuf[...] = jnp.zeros_like(zbuf)
        for j in range(MOE_BUFS * r):
            pltpu.make_async_copy(zbuf.at[(j % r) // 8, pl.ds(j % 8, 1), :],
                                  y_ref.at[n_assign + j], hsem).start(priority=j % 2)
        for _ in range(MOE_BUFS):
            pltpu.make_async_copy(y_ref.at[pl.ds(0, r)], y_ref.at[pl.ds(r, r)], hsem).wait()

    @pl.when((i >= 2) & (i - 2 <= n_used))
    def _():
        for_buffer(i + MOE_BUFS - 3, functools.partial(scatter_wait, i - 2))

    prev = be_ref[jnp.maximum(i - 1, 0)]

    @pl.when((i == 0) | (be_ref[i] != prev))
    def _():
        wg_sc[...] = wg_ref[0].astype(BF16)
        wu_sc[...] = wu_ref[0].astype(BF16)
        wd_sc[...] = wd_ref[0].astype(BF16)

    def step(v):
        last = jnp.maximum(i - 1, 0)
        n_rows = jnp.where(i > 0, nv_ref[last], 0)
        src = (v - 1) % MOE_BUFS
        scatter(last, n_rows, src, 0, MOE_HEAD)
        copy_rows(i + 1, xbufs[(v + 1) % MOE_BUFS])
        compute(i, xbufs[v], ybufs[v])
        for j0 in range(MOE_HEAD, r, MOE_CHUNK):
            pl.when(j0 < n_rows)(functools.partial(scatter, last, n_rows, src, j0, j0 + MOE_CHUNK))

    @pl.when(active)
    def _():
        for_buffer(i, step)

    @pl.when(i == n_steps - 1)
    def _():
        @pl.when(i - 1 <= n_used)
        def _():
            for_buffer(i + MOE_BUFS - 2, functools.partial(scatter_wait, i - 1))

        @pl.when(active)
        def _():
            for_buffer(i + MOE_BUFS - 1, functools.partial(scatter_wait, i))


def _moe_experts(order, block_expert, block_valid, block_start, n_used, h2, w_gate, w_up, w_down):
    t = h2.shape[0]
    assert t & (t - 1) == 0, "token count must be a power of two"
    n_blocks = block_expert.shape[0] - 2
    n_assign = TOP_K * t
    wspec = lambda shp: pl.BlockSpec((1,) + shp, lambda i, od, be, nv, cs, nu: (be[i], 0, 0))
    row_buf = pltpu.VMEM((ROWS_MOE // 8, 8, D_MODEL // 2), jnp.uint32)
    grid_spec = pltpu.PrefetchScalarGridSpec(
        num_scalar_prefetch=5,
        grid=(n_blocks + 1,),
        in_specs=[pl.BlockSpec(memory_space=pl.ANY),
                  wspec((D_MODEL, EXPERT_DIM)), wspec((D_MODEL, EXPERT_DIM)),
                  wspec((EXPERT_DIM, D_MODEL))],
        out_specs=pl.BlockSpec(memory_space=pl.ANY),
        scratch_shapes=([pltpu.VMEM((t, 1, D_MODEL // 2), jnp.uint32)]
                        + [row_buf] * (2 * MOE_BUFS)
                        + [pltpu.VMEM((D_MODEL, EXPERT_DIM), BF16),
                           pltpu.VMEM((D_MODEL, EXPERT_DIM), BF16),
                           pltpu.VMEM((EXPERT_DIM, D_MODEL), BF16),
                           pltpu.SemaphoreType.DMA]
                        + [pltpu.SemaphoreType.DMA] * MOE_BUFS),
    )
    return pl.pallas_call(
        functools.partial(_moe_kernel, n_tokens=t, n_assign=n_assign),
        grid_spec=grid_spec,
        out_shape=jax.ShapeDtypeStruct((n_assign + MOE_BUFS * ROWS_MOE, 1, D_MODEL // 2),
                                       jnp.uint32),
        compiler_params=_params("arbitrary"),
        name="moe_experts",
    )(order, block_expert, block_valid, block_start, n_used, h2, w_gate, w_up, w_down)


def _combine_kernel(base_ref, w_ref, mod_ref, g_ref, *refs, tiles_per_batch, last_layer):
    y_refs, o_ref = refs[:TOP_K], refs[TOP_K]
    tc = base_ref.shape[0]
    lo = [None] * PACKED_ROWS
    hi = [None] * PACKED_ROWS
    for k in range(TOP_K):
        wk = w_ref[:, k:k + 1]
        for cb in range(PACKED_ROWS):
            yp = y_refs[k][pl.ds(cb, tc, stride=PACKED_ROWS), :]
            y_lo = lax.bitcast_convert_type(yp << 16, F32) * wk
            y_hi = lax.bitcast_convert_type(yp & jnp.uint32(0xFFFF0000), F32) * wk
            lo[cb] = y_lo if lo[cb] is None else lo[cb] + y_lo
            hi[cb] = y_hi if hi[cb] is None else hi[cb] + y_hi
    routed = jnp.concatenate(lo + hi, axis=1)
    b = pl.program_id(0) // tiles_per_batch
    gate2 = mod_ref[pl.ds(b, 1), 5 * D_MODEL:6 * D_MODEL]
    x = base_ref[...] + gate2 * routed
    if last_layer:
        x = x * lax.rsqrt(jnp.mean(x * x, axis=-1, keepdims=True) + NORM_EPS) * g_ref[...]
    o_ref[...] = x


def _combine(base, top_w, mod, final_g, y, seq, last_layer):
    t = base.shape[0]
    tc = T_COMBINE
    kern = functools.partial(_combine_kernel, tiles_per_batch=seq // tc, last_layer=last_layer)
    y_rows = y.reshape(y.shape[0] * PACKED_ROWS, LANES)
    tiles = t // tc
    y_specs = [pl.BlockSpec((tc * PACKED_ROWS, LANES), lambda i, k=k: (k * tiles + i, 0))
               for k in range(TOP_K)]
    return pl.pallas_call(
        kern,
        grid=(tiles,),
        in_specs=[pl.BlockSpec((tc, D_MODEL), lambda i: (i, 0)),
                  pl.BlockSpec((tc, TOP_K), lambda i: (i, 0)),
                  pl.BlockSpec(mod.shape, lambda i: (0, 0)),
                  pl.BlockSpec((1, D_MODEL), lambda i: (0, 0))] + y_specs,
        out_specs=pl.BlockSpec((tc, D_MODEL), lambda i: (i, 0)),
        out_shape=jax.ShapeDtypeStruct((t, D_MODEL), F32),
        compiler_params=_params("arbitrary"),
        name="combine",
    )(base, top_w, mod, final_g, *([y_rows] * TOP_K))


def _block_tables(counts, n_assign):
    r = ROWS_MOE
    cnt = counts.reshape(N_EXPERTS).astype(jnp.int32)
    first = jnp.cumsum(cnt) - cnt
    nblk = (cnt + r - 1) // r
    blk_end = jnp.cumsum(nblk)
    blk_start = blk_end - nblk
    n_blocks = (n_assign + N_EXPERTS * (r - 1) + r - 1) // r
    bi = jnp.arange(n_blocks + 2, dtype=jnp.int32)
    bc = jnp.minimum(bi, blk_end[-1] - 1)
    block_expert = jnp.minimum(jnp.sum((blk_end[None, :] <= bc[:, None]).astype(jnp.int32), axis=1),
                               N_EXPERTS - 1)
    onehot = (block_expert[:, None] == jnp.arange(N_EXPERTS, dtype=jnp.int32)[None, :])
    pick = lambda v: jnp.sum(jnp.where(onehot, v[None, :], 0), axis=1)
    within = bc - pick(blk_start)
    block_valid = jnp.where(bi < blk_end[-1], jnp.clip(pick(cnt) - within * r, 0, r), 0)
    block_start = pick(first) + within * r
    return (first.astype(F32).reshape(N_EXPERTS, 1), block_expert.astype(jnp.int32),
            block_valid.astype(jnp.int32), block_start.astype(jnp.int32),
            blk_end[-1:].astype(jnp.int32))


def kernel(x, c, w_ada, b_ada, norm1_g, w_in, b_forget, ret_gn_g, w_branch_a, w_branch_b, w_out,
           norm2_g, w_router, router_bias, w_exp_gate, w_exp_up, w_exp_down, w_sh_gate, w_sh_up,
           w_sh_down, final_g):
    batch, seq, d = x.shape
    t = batch * seq
    depth = w_ada.shape[0]
    x2 = x.reshape(t, d)

    half = RET_QK_DIM // 2
    inv_freq = ROPE_BASE ** (-jnp.arange(half, dtype=F32) / half)
    ang = jnp.arange(seq, dtype=F32)[:, None] * inv_freq[None, :]
    cos32, sin32 = jnp.cos(ang), jnp.sin(ang)
    cos_t = jnp.concatenate([cos32, cos32, cos32, cos32], axis=-1)
    sin_t = jnp.concatenate([-sin32, sin32, -sin32, sin32], axis=-1)
    dec_t, zeta_t, xi_t, cd_t = _retention_tables()
    c_pad = jnp.zeros((8, d), F32).at[:batch].set(c)

    for l in range(depth):
        mod = _adaln_mod(c_pad, w_ada[l], b_ada[l])

        w = w_in[l]
        o_fq, o_fk, o_fv, o_ff = 0, FOX_WIDTH, 2 * FOX_WIDTH, 3 * FOX_WIDTH
        o_rq = o_ff + FOX_HEADS
        o_rk = o_rq + RET_QK_WIDTH
        o_rv = o_rk + RET_QK_WIDTH
        o_rg = o_rv + RET_V_WIDTH
        o_ga = o_rg + RET_V_WIDTH
        o_gb = o_ga + D_MODEL
        w_p = jnp.concatenate(
            [w[:, o_rv:o_rg], w[:, o_rg:o_ga], w[:, o_ga:o_gb], w[:, o_gb:o_gb + D_MODEL],
             w[:, o_fq:o_fk], w[:, o_fk:o_fv], w[:, o_fv:o_ff], w[:, o_rq:o_rk], w[:, o_rk:o_rv]],
            axis=1).astype(BF16)
        w_ff = jnp.zeros((d, LANES), BF16).at[:, :FOX_HEADS].set(w[:, o_ff:o_rq].astype(BF16))
        b_ff = jnp.zeros((1, LANES), F32).at[0, :FOX_HEADS].set(b_forget[l].astype(F32))

        proj, cum_f = _in_proj(x2, mod, norm1_g[l].reshape(1, d), w_p, w_ff, b_ff, cos_t, sin_t, seq)

        attn = _fox_attn(proj, cum_f, batch, seq)
        ret = _retention(proj, ret_gn_g[l].reshape(1, RET_V_WIDTH), dec_t, zeta_t, xi_t, cd_t,
                         batch, seq)

        wr_head, wr_rest = _split_bf16(w_router[l].T.astype(F32))
        base, h2, scores_t = _post_mix(
            attn, ret, proj, x2, mod, norm2_g[l].reshape(1, d),
            w_branch_a[l].astype(BF16), w_branch_b[l].astype(BF16), w_out[l].astype(BF16),
            wr_head, wr_rest, w_sh_gate[l].astype(BF16), w_sh_up[l].astype(BF16),
            w_sh_down[l].astype(BF16), seq)

        top_e, top_w, rank, counts = _route(scores_t,
                                            router_bias[l].astype(F32).reshape(N_EXPERTS, 1))
        first, block_expert, block_valid, block_start, n_used = _block_tables(counts, t * TOP_K)
        sorted_pos = _dest_rows(top_e, rank, first).reshape(TOP_K * t)
        order = jnp.concatenate([jnp.argsort(sorted_pos).astype(jnp.int32),
                                 jnp.zeros((ROWS_MOE,), jnp.int32)])
        y = _moe_experts(order, block_expert, block_valid, block_start, n_used,
                         h2.reshape(t, 1, d // 2),
                         w_exp_gate[l], w_exp_up[l], w_exp_down[l])
        x2 = _combine(base, top_w.T, mod, final_g.reshape(1, d), y, seq, l == depth - 1)
    return x2.reshape(batch, seq, d)
```

```python
import functools

import jax
import jax.numpy as jnp
import numpy as np
from jax import lax
from jax.experimental import pallas as pl
from jax.experimental.pallas import tpu as pltpu

F32 = jnp.float32
BF16 = jnp.bfloat16

D_MODEL = 1024
FOX_HEADS = 8
FOX_HEAD_DIM = 64
FOX_WIDTH = 512
RET_HEADS = 8
RET_QK_DIM = 64
RET_V_DIM = 128
RET_QK_WIDTH = 512
RET_V_WIDTH = 1024
RET_CHUNK = 256
ROPE_BASE = 10000.0
N_EXPERTS = 256
TOP_K = 8
N_GROUPS = 8
TOPK_GROUPS = 4
EXPERT_DIM = 256
ROUTED_SCALE = 2.5
NORM_EPS = 1e-6
LOG2_E = 1.4426950408889634

LANES = 128
PACKED_ROWS = D_MODEL // 2 // LANES
VMEM_LIMIT = 56 * 1024 * 1024

COL_RV, COL_RG, COL_GA, COL_GB = 0, 1024, 2048, 3072
COL_FQ, COL_FK, COL_FV, COL_RQ, COL_RK = 4096, 4608, 5120, 5632, 6144
PROJ_WIDTH = 6656

TM_PROJ = 512
TN_PROJ = 512
T_ATTN = 1024
T_RET = 512
TM_POST = 512
T_ROUTE = 512
ROWS_MOE = 256
MOE_BUFS = 3
MOE_HEAD = 128
MOE_CHUNK = 32
T_COMBINE = 512


def _sigmoid(z):
    return 1.0 / (1.0 + jnp.exp(-z))


def _silu(z):
    return z * _sigmoid(z)


def _pack_halves(xb):
    n = xb.shape[1] // 2
    lo = lax.bitcast_convert_type(xb[:, :n].astype(F32), jnp.uint32)
    hi = lax.bitcast_convert_type(xb[:, n:].astype(F32), jnp.uint32)
    return (lo >> 16) | hi


def _unpack_halves(xp):
    lo = lax.bitcast_convert_type(xp << 16, F32)
    hi = lax.bitcast_convert_type(xp & jnp.uint32(0xFFFF0000), F32)
    return lo.astype(BF16), hi.astype(BF16)


def _split_bf16(x):
    head = lax.bitcast_convert_type(
        lax.bitcast_convert_type(x, jnp.uint32) & jnp.uint32(0xFFFF0000), F32)
    return head.astype(BF16), (x - head).astype(BF16)


def _params(*sem):
    return pltpu.CompilerParams(dimension_semantics=sem, vmem_limit_bytes=VMEM_LIMIT)


def _adaln_kernel(c_ref, w_ref, b_ref, o_ref):
    a = _silu(c_ref[...]).astype(BF16)
    o_ref[...] = jnp.dot(a, w_ref[...].astype(BF16), preferred_element_type=F32) + b_ref[...]


def _adaln_mod(c_pad, w_ada, b_ada):
    n = w_ada.shape[1]
    tn = 1024
    return pl.pallas_call(
        _adaln_kernel,
        grid=(n // tn,),
        in_specs=[pl.BlockSpec((8, D_MODEL), lambda j: (0, 0)),
                  pl.BlockSpec((D_MODEL, tn), lambda j: (0, j)),
                  pl.BlockSpec((1, tn), lambda j: (0, j))],
        out_specs=pl.BlockSpec((8, tn), lambda j: (0, j)),
        out_shape=jax.ShapeDtypeStruct((8, n), F32),
        compiler_params=_params("arbitrary"),
        name="adaln_mod",
    )(c_pad, w_ada, b_ada.reshape(1, n))


def _inproj_kernel(x_ref, mod_ref, g_ref, w_ref, wff_ref, bf_ref, cos_ref, sin_ref,
                   proj_ref, cumf_ref, cf_sc, *, tiles_per_batch):
    i = pl.program_id(0)
    b = i // tiles_per_batch
    x = x_ref[...]
    y = x * lax.rsqrt(jnp.mean(x * x, axis=-1, keepdims=True) + NORM_EPS) * g_ref[...]
    shift = mod_ref[pl.ds(b, 1), 0:D_MODEL]
    scale = mod_ref[pl.ds(b, 1), D_MODEL:2 * D_MODEL]
    hb = (y * (1.0 + scale) + shift).astype(BF16)
    z = jnp.dot(hb, wff_ref[...], preferred_element_type=F32) + bf_ref[...]
    logf = jnp.minimum(z, 0.0) - jnp.log1p(jnp.exp(-jnp.abs(z)))
    tm = logf.shape[0]
    tri = jnp.where(lax.broadcasted_iota(jnp.int32, (tm, tm), 1)
                    <= lax.broadcasted_iota(jnp.int32, (tm, tm), 0), 1.0, 0.0).astype(BF16)
    h1, _ = _split_bf16(logf)
    h2, r2 = _split_bf16(logf - h1.astype(F32))
    carry = jnp.where(i % tiles_per_batch == 0, 0.0, cf_sc[...])
    cum = (jnp.dot(tri, h1, preferred_element_type=F32) + jnp.dot(tri, h2, preferred_element_type=F32)
           + jnp.dot(tri, r2, preferred_element_type=F32)) + carry
    cf_sc[...] = cum[tm - 1:tm, :]
    cumf_ref[0] = cum.T[0:FOX_HEADS, :] * LOG2_E

    cos = cos_ref[...]
    sin = sin_ref[...]
    lane = lax.broadcasted_iota(jnp.int32, cos.shape, 1)
    first_half = (lane % RET_QK_DIM) < (RET_QK_DIM // 2)

    for c0 in range(0, PROJ_WIDTH, TN_PROJ):
        cols = slice(c0, c0 + TN_PROJ)
        acc = jnp.dot(hb, w_ref[:, cols], preferred_element_type=F32)
        if COL_GA <= c0 < COL_FQ:
            proj_ref[:, cols] = _sigmoid(acc).astype(BF16)
        elif COL_FQ <= c0 < COL_FK:
            proj_ref[:, cols] = (acc * (LOG2_E * FOX_HEAD_DIM ** -0.5)).astype(BF16)
        elif c0 >= COL_RQ:
            k_scale = RET_QK_DIM ** -0.5 if c0 >= COL_RK else 1.0
            for cb in range(c0, c0 + TN_PROJ, LANES):
                xs = acc[:, cb - c0:cb - c0 + LANES]
                up = pltpu.roll(xs, LANES - RET_QK_DIM // 2, axis=1)
                dn = pltpu.roll(xs, RET_QK_DIM // 2, axis=1)
                rot = xs * cos + jnp.where(first_half, up, dn) * sin
                proj_ref[:, cb:cb + LANES] = (rot * k_scale).astype(BF16)
        else:
            proj_ref[:, cols] = acc.astype(BF16)


def _in_proj(x2, mod, norm_g, w_p, w_ff, b_ff, cos_t, sin_t, seq):
    t = x2.shape[0]
    tiles_per_batch = seq // TM_PROJ
    kern = functools.partial(_inproj_kernel, tiles_per_batch=tiles_per_batch)
    return pl.pallas_call(
        kern,
        grid=(t // TM_PROJ,),
        in_specs=[pl.BlockSpec((TM_PROJ, D_MODEL), lambda i: (i, 0)),
                  pl.BlockSpec(mod.shape, lambda i: (0, 0)),
                  pl.BlockSpec((1, D_MODEL), lambda i: (0, 0)),
                  pl.BlockSpec((D_MODEL, PROJ_WIDTH), lambda i: (0, 0)),
                  pl.BlockSpec((D_MODEL, LANES), lambda i: (0, 0)),
                  pl.BlockSpec((1, LANES), lambda i: (0, 0)),
                  pl.BlockSpec((TM_PROJ, LANES), lambda i: (i % tiles_per_batch, 0)),
                  pl.BlockSpec((TM_PROJ, LANES), lambda i: (i % tiles_per_batch, 0))],
        out_specs=[pl.BlockSpec((TM_PROJ, PROJ_WIDTH), lambda i: (i, 0)),
                   pl.BlockSpec((1, FOX_HEADS, TM_PROJ),
                                lambda i: (i // tiles_per_batch, 0, i % tiles_per_batch))],
        out_shape=[jax.ShapeDtypeStruct((t, PROJ_WIDTH), BF16),
                   jax.ShapeDtypeStruct((t // seq, FOX_HEADS, seq), F32)],
        scratch_shapes=[pltpu.VMEM((1, LANES), F32)],
        compiler_params=_params("arbitrary"),
        name="in_proj",
    )(x2, mod, norm_g, w_p, w_ff, b_ff, cos_t, sin_t)


def _attn_kernel(q_ref, k_ref, v_ref, f_ref, o_ref):
    hp = pl.program_id(1)
    qi = pl.program_id(2)
    t_attn = T_ATTN
    q = q_ref[...]
    lane = lax.broadcasted_iota(jnp.int32, (t_attn, LANES), 1)
    zeros = jnp.zeros_like(q)
    qm = (jnp.where(lane < FOX_HEAD_DIM, q, zeros), jnp.where(lane >= FOX_HEAD_DIM, q, zeros))
    lane1 = lax.broadcasted_iota(jnp.int32, (1, LANES), 1)
    own = (jnp.where(lane1 < FOX_HEAD_DIM, 1.0, 0.0).astype(BF16),
           jnp.where(lane1 >= FOX_HEAD_DIM, 1.0, 0.0).astype(BF16))

    def block(r0, nr, ks, nk, carry, masked):
        k = k_ref[pl.ds(ks, nk), :]
        v = v_ref[pl.ds(ks, nk), :]
        v1 = (v * own[0] + own[1], v * own[1] + own[0])
        out = []
        for hh in range(2):
            m, acc = carry[hh]
            s = lax.dot_general(qm[hh][r0:r0 + nr], k, (((1,), (1,)), ((), ())),
                                preferred_element_type=F32)
            s = s - f_ref[0, pl.ds(2 * hp + hh, 1), pl.ds(ks, nk)]
            if masked:
                row = qi * t_attn + r0 + lax.broadcasted_iota(jnp.int32, (nr, nk), 0)
                col = ks + lax.broadcasted_iota(jnp.int32, (nr, nk), 1)
                s = jnp.where(col <= row, s, -jnp.inf)
            m_new = jnp.maximum(m, jnp.max(s, axis=-1, keepdims=True))
            p = jnp.exp2(s - m_new).astype(BF16)
            acc = jnp.exp2(m - m_new) * acc + jnp.dot(p, v1[hh], preferred_element_type=F32)
            out.append((m_new, acc))
        return tuple(out)

    def step(kb, carry):
        return block(0, t_attn, pl.multiple_of(kb * t_attn, t_attn), t_attn, carry, False)

    init = (jnp.full((t_attn, 1), -jnp.inf, F32), jnp.zeros((t_attn, LANES), F32))
    def pair(kp, c):
        return step(2 * kp + 1, step(2 * kp, c))

    carry = lax.fori_loop(0, qi // 2, pair, (init, init))
    carry = lax.cond(qi % 2 == 1, lambda c: step(qi - 1, c), lambda c: c, carry)
    half = t_attn // 2
    ks = pl.multiple_of(qi * t_attn, t_attn)
    rows = lambda c, r0: tuple((m[r0:r0 + half], acc[r0:r0 + half]) for m, acc in c)
    top = block(0, half, ks, half, rows(carry, 0), True)
    bot = block(half, half, ks, t_attn, rows(carry, half), True)
    (_, acc0), (_, acc1) = tuple(
        (jnp.concatenate([mt, mb], axis=0), jnp.concatenate([at, ab], axis=0))
        for (mt, at), (mb, ab) in zip(top, bot))
    o_ref[...] = jnp.where(lane < FOX_HEAD_DIM, acc0 / acc0[:, FOX_HEAD_DIM:FOX_HEAD_DIM + 1],
                           acc1 / acc1[:, 0:1]).astype(BF16)


def _fox_attn(proj, cum_f, batch, seq):
    t = proj.shape[0]
    nq = seq // T_ATTN
    cq, ck, cv = COL_FQ // LANES, COL_FK // LANES, COL_FV // LANES
    return pl.pallas_call(
        _attn_kernel,
        grid=(batch, FOX_HEADS // 2, nq),
        in_specs=[pl.BlockSpec((T_ATTN, LANES), lambda b, hp, qi: (b * nq + qi, cq + hp)),
                  pl.BlockSpec((seq, LANES), lambda b, hp, qi: (b, ck + hp)),
                  pl.BlockSpec((seq, LANES), lambda b, hp, qi: (b, cv + hp)),
                  pl.BlockSpec((1, FOX_HEADS, seq), lambda b, hp, qi: (b, 0, 0))],
        out_specs=pl.BlockSpec((T_ATTN, LANES), lambda b, hp, qi: (b * nq + qi, hp)),
        out_shape=jax.ShapeDtypeStruct((t, FOX_WIDTH), BF16),
        compiler_params=_params("arbitrary", "arbitrary", "arbitrary"),
        name="fox_attn",
    )(proj, proj, proj, cum_f)


def _ret_kernel(q_ref, k_ref, v_ref, rg_ref, gn_ref, dec_ref, zeta_ref, xi_ref, cd_ref,
                o_ref, st_sc):
    ri = pl.program_id(1)
    c = RET_CHUNK

    @pl.when(ri == 0)
    def _():
        st_sc[...] = jnp.zeros_like(st_sc)

    lane = lax.broadcasted_iota(jnp.int32, (c, LANES), 1)
    zeros = jnp.zeros((c, LANES), BF16)
    states = [st_sc[h] for h in range(RET_HEADS)]
    for n in range(T_RET // c):
        rows = slice(n * c, (n + 1) * c)
        for h in range(RET_HEADS):
            pair = slice((h // 2) * LANES, (h // 2 + 1) * LANES)
            vcol = slice(h * RET_V_DIM, (h + 1) * RET_V_DIM)
            sel = (lane < RET_QK_DIM) if h % 2 == 0 else (lane >= RET_QK_DIM)
            qc = jnp.where(sel, q_ref[rows, pair], zeros)
            kc = jnp.where(sel, k_ref[rows, pair], zeros)
            vc = v_ref[rows, vcol]
            sc = lax.dot_general(qc, kc, (((1,), (1,)), ((), ())),
                                 preferred_element_type=F32) * dec_ref[h]
            intra = jnp.dot(sc.astype(BF16), vc, preferred_element_type=F32)
            cross = jnp.dot(qc, states[h].astype(BF16), preferred_element_type=F32) * xi_ref[h]
            o = intra + cross
            kz = (kc.astype(F32) * zeta_ref[h]).astype(BF16)
            kv = lax.dot_general(kz, vc, (((0,), (0,)), ((), ())), preferred_element_type=F32)
            states[h] = cd_ref[h] * states[h] + kv
            mu = jnp.mean(o, axis=-1, keepdims=True)
            d = o - mu
            var = jnp.mean(d * d, axis=-1, keepdims=True)
            on = d * lax.rsqrt(var + NORM_EPS) * gn_ref[:, vcol]
            rg = rg_ref[rows, vcol].astype(F32)
            o_ref[rows, vcol] = (on * _silu(rg)).astype(BF16)
    for h in range(RET_HEADS):
        st_sc[h] = states[h]


def _retention(proj, gn_g, dec_t, zeta_t, xi_t, cd_t, batch, seq):
    t = proj.shape[0]
    nr = seq // T_RET
    tok = lambda b, ri: b * nr + ri
    full = lambda a: pl.BlockSpec(a.shape, lambda b, ri: (0,) * a.ndim)
    return pl.pallas_call(
        _ret_kernel,
        grid=(batch, nr),
        in_specs=[pl.BlockSpec((T_RET, RET_QK_WIDTH), lambda b, ri: (tok(b, ri), COL_RQ // RET_QK_WIDTH)),
                  pl.BlockSpec((T_RET, RET_QK_WIDTH), lambda b, ri: (tok(b, ri), COL_RK // RET_QK_WIDTH)),
                  pl.BlockSpec((T_RET, RET_V_WIDTH), lambda b, ri: (tok(b, ri), COL_RV // RET_V_WIDTH)),
                  pl.BlockSpec((T_RET, RET_V_WIDTH), lambda b, ri: (tok(b, ri), COL_RG // RET_V_WIDTH)),
                  pl.BlockSpec((1, RET_V_WIDTH), lambda b, ri: (0, 0)),
                  full(dec_t), full(zeta_t), full(xi_t), full(cd_t)],
        out_specs=pl.BlockSpec((T_RET, RET_V_WIDTH), lambda b, ri: (tok(b, ri), 0)),
        out_shape=jax.ShapeDtypeStruct((t, RET_V_WIDTH), BF16),
        scratch_shapes=[pltpu.VMEM((RET_HEADS, LANES, RET_V_DIM), F32)],
        compiler_params=_params("arbitrary", "arbitrary"),
        name="retention",
    )(proj, proj, proj, proj, gn_g, dec_t, zeta_t, xi_t, cd_t)


def _retention_tables():
    c = RET_CHUNK
    log_gamma = jnp.log1p(-jnp.exp2(-5.0 - jnp.arange(RET_HEADS, dtype=F32)))
    idx = jnp.arange(c, dtype=F32)
    diff = idx[:, None] - idx[None, :]
    dec = jnp.where(diff >= 0,
                    jnp.exp(jnp.maximum(diff, 0.0)[None] * log_gamma[:, None, None]), 0.0)
    zeta = jnp.exp((c - 1.0 - idx)[None, :] * log_gamma[:, None])
    xi = jnp.exp((idx + 1.0)[None, :] * log_gamma[:, None])
    cd = jnp.exp(c * log_gamma)
    bc = lambda v: jnp.broadcast_to(v[:, :, None], (RET_HEADS, c, LANES))
    cd_t = jnp.broadcast_to(cd[:, None, None], (RET_HEADS, LANES, RET_V_DIM))
    return dec, bc(zeta), bc(xi), cd_t


def _post_kernel(attn_ref, ret_ref, ga_ref, gb_ref, x_ref, mod_ref, g2_ref,
                 wa_ref, wb_ref, wo_ref, wr_ref, wrr_ref, wsg_ref, wsu_ref, wsd_ref,
                 base_ref, h2_ref, score_ref, *, tiles_per_batch):
    b = pl.program_id(0) // tiles_per_batch
    ya = jnp.dot(attn_ref[...], wa_ref[...], preferred_element_type=F32)
    yb = jnp.dot(ret_ref[...], wb_ref[...], preferred_element_type=F32)
    merged = ga_ref[...].astype(F32) * ya + gb_ref[...].astype(F32) * yb
    mix = jnp.dot(merged.astype(BF16), wo_ref[...], preferred_element_type=F32)
    d = D_MODEL
    gate1 = mod_ref[pl.ds(b, 1), 2 * d:3 * d]
    shift2 = mod_ref[pl.ds(b, 1), 3 * d:4 * d]
    scale2 = mod_ref[pl.ds(b, 1), 4 * d:5 * d]
    gate2 = mod_ref[pl.ds(b, 1), 5 * d:6 * d]
    x1 = x_ref[...] + gate1 * mix
    y = x1 * lax.rsqrt(jnp.mean(x1 * x1, axis=-1, keepdims=True) + NORM_EPS) * g2_ref[...]
    h2 = y * (1.0 + scale2) + shift2
    h2_head, h2_rest = _split_bf16(h2)
    nt = (((1,), (1,)), ((), ()))
    logits_t = (lax.dot_general(wr_ref[...], h2_head, nt, preferred_element_type=F32)
                + lax.dot_general(wr_ref[...], h2_rest, nt, preferred_element_type=F32)
                + lax.dot_general(wrr_ref[...], h2_head, nt, preferred_element_type=F32))
    score_ref[...] = _sigmoid(logits_t)
    h2b = h2.astype(BF16)
    h2p = _pack_halves(h2b)
    n_chunks = h2p.shape[1] // LANES
    for cb in range(n_chunks):
        h2_ref[pl.ds(cb, h2p.shape[0], stride=n_chunks), :] = h2p[:, cb * LANES:(cb + 1) * LANES]
    g = jnp.dot(h2b, wsg_ref[...], preferred_element_type=F32)
    u = jnp.dot(h2b, wsu_ref[...], preferred_element_type=F32)
    shared = jnp.dot((_silu(g) * u).astype(BF16), wsd_ref[...], preferred_element_type=F32)
    base_ref[...] = x1 + gate2 * shared


def _post_mix(attn, ret, proj, x2, mod, norm2_g, wa, wb, wo, wr, wrr, wsg, wsu, wsd, seq):
    t = x2.shape[0]
    tm = TM_POST
    kern = functools.partial(_post_kernel, tiles_per_batch=seq // tm)
    full = lambda a: pl.BlockSpec(a.shape, lambda i: (0,) * a.ndim)
    return pl.pallas_call(
        kern,
        grid=(t // tm,),
        in_specs=[pl.BlockSpec((tm, FOX_WIDTH), lambda i: (i, 0)),
                  pl.BlockSpec((tm, RET_V_WIDTH), lambda i: (i, 0)),
                  pl.BlockSpec((tm, D_MODEL), lambda i: (i, COL_GA // D_MODEL)),
                  pl.BlockSpec((tm, D_MODEL), lambda i: (i, COL_GB // D_MODEL)),
                  pl.BlockSpec((tm, D_MODEL), lambda i: (i, 0)),
                  full(mod), full(norm2_g), full(wa), full(wb), full(wo), full(wr), full(wrr),
                  full(wsg), full(wsu), full(wsd)],
        out_specs=[pl.BlockSpec((tm, D_MODEL), lambda i: (i, 0)),
                   pl.BlockSpec((tm * PACKED_ROWS, LANES), lambda i: (i, 0)),
                   pl.BlockSpec((N_EXPERTS, tm), lambda i: (0, i))],
        out_shape=[jax.ShapeDtypeStruct((t, D_MODEL), F32),
                   jax.ShapeDtypeStruct((t * PACKED_ROWS, LANES), jnp.uint32),
                   jax.ShapeDtypeStruct((N_EXPERTS, t), F32)],
        compiler_params=_params("arbitrary"),
        name="post_mix",
    )(attn, ret, proj, proj, x2, mod, norm2_g, wa, wb, wo, wr, wrr, wsg, wsu, wsd)


def _route_kernel(s_ref, bias_ref, e_ref, w_ref, rank_ref, cnt_ref):
    tr = s_ref.shape[1]
    gsz = N_EXPERTS // N_GROUPS

    @pl.when(pl.program_id(0) == 0)
    def _():
        cnt_ref[...] = jnp.zeros_like(cnt_ref)

    s = s_ref[...]
    sel = s + bias_ref[...]
    neg = jnp.float32(-jnp.inf)

    g3 = sel.reshape(N_GROUPS, gsz, tr)
    m1 = jnp.max(g3, axis=1)
    n_max = jnp.sum((g3 == m1[:, None, :]).astype(F32), axis=1)
    m2 = jnp.max(jnp.where(g3 < m1[:, None, :], g3, neg), axis=1)
    gs = m1 + jnp.where(n_max >= 2.0, m1, m2)

    gi = lax.broadcasted_iota(jnp.int32, (N_GROUPS, tr), 0)
    beaten = jnp.zeros((N_GROUPS, tr), F32)
    for j in range(N_GROUPS):
        gj = gs[j:j + 1, :]
        beaten = beaten + jnp.where((gj > gs) | ((gj == gs) & (gi > j)), 1.0, 0.0)
    keep = jnp.where(beaten < float(TOPK_GROUPS), 1.0, 0.0)
    keep_e = jnp.broadcast_to(keep[:, None, :], (N_GROUPS, gsz, tr)).reshape(N_EXPERTS, tr)

    ei = lax.broadcasted_iota(jnp.int32, (N_EXPERTS, tr), 0)
    val = jnp.where(keep_e > 0.0, sel, neg)
    member = jnp.zeros((N_EXPERTS, tr), F32)
    idx_rows, w_rows = [], []
    for _ in range(TOP_K):
        m = jnp.max(val, axis=0, keepdims=True)
        idx = jnp.min(jnp.where(val == m, ei, N_EXPERTS), axis=0, keepdims=True)
        hit = ei == idx
        w_rows.append(jnp.sum(jnp.where(hit, s, 0.0), axis=0, keepdims=True))
        idx_rows.append(idx)
        val = jnp.where(hit, neg, val)
        member = jnp.where(hit, 1.0, member)

    w_sum = w_rows[0]
    for k in range(1, TOP_K):
        w_sum = w_sum + w_rows[k]

    r_i = lax.broadcasted_iota(jnp.int32, (tr, tr), 0)
    c_i = lax.broadcasted_iota(jnp.int32, (tr, tr), 1)
    upper = jnp.where(r_i < c_i, 1.0, 0.0).astype(BF16)
    before = jnp.dot(member.astype(BF16), upper, preferred_element_type=F32) + cnt_ref[...]

    for k in range(TOP_K):
        hit = ei == idx_rows[k]
        e_ref[k:k + 1, :] = idx_rows[k]
        w_ref[k:k + 1, :] = w_rows[k] / w_sum * ROUTED_SCALE
        rank_ref[k:k + 1, :] = jnp.sum(jnp.where(hit, before, 0.0), axis=0,
                                       keepdims=True).astype(jnp.int32)
    cnt_ref[...] = cnt_ref[...] + jnp.sum(member, axis=1, keepdims=True)


def _route(scores_t, bias_col):
    t = scores_t.shape[1]
    tr = T_ROUTE
    row8 = lambda dt: jax.ShapeDtypeStruct((TOP_K, t), dt)
    return pl.pallas_call(
        _route_kernel,
        grid=(t // tr,),
        in_specs=[pl.BlockSpec((N_EXPERTS, tr), lambda i: (0, i)),
                  pl.BlockSpec((N_EXPERTS, 1), lambda i: (0, 0))],
        out_specs=[pl.BlockSpec((TOP_K, tr), lambda i: (0, i)),
                   pl.BlockSpec((TOP_K, tr), lambda i: (0, i)),
                   pl.BlockSpec((TOP_K, tr), lambda i: (0, i)),
                   pl.BlockSpec((N_EXPERTS, 1), lambda i: (0, 0))],
        out_shape=[row8(jnp.int32), row8(F32), row8(jnp.int32),
                   jax.ShapeDtypeStruct((N_EXPERTS, 1), F32)],
        compiler_params=_params("arbitrary"),
        name="route",
    )(scores_t, bias_col)


def _dest_kernel(e_ref, rank_ref, start_ref, dest_ref):
    tr = e_ref.shape[1]
    ei = lax.broadcasted_iota(jnp.int32, (N_EXPERTS, tr), 0)
    start = start_ref[...]
    for k in range(TOP_K):
        hit = ei == e_ref[k:k + 1, :]
        off = jnp.sum(jnp.where(hit, start, 0.0), axis=0, keepdims=True)
        dest_ref[k:k + 1, :] = rank_ref[k:k + 1, :] + off.astype(jnp.int32)


def _dest_rows(top_e, rank, start_col):
    t = top_e.shape[1]
    tr = T_ROUTE
    spec = pl.BlockSpec((TOP_K, tr), lambda i: (0, i))
    return pl.pallas_call(
        _dest_kernel,
        grid=(t // tr,),
        in_specs=[spec, spec, pl.BlockSpec((N_EXPERTS, 1), lambda i: (0, 0))],
        out_specs=spec,
        out_shape=jax.ShapeDtypeStruct((TOP_K, t), jnp.int32),
        compiler_params=_params("arbitrary"),
        name="dest_rows",
    )(top_e, rank, start_col)


def _moe_kernel(order_ref, be_ref, nv_ref, cs_ref, nu_ref, h_ref, wg_ref, wu_ref, wd_ref, y_ref,
                hbuf, *scratch, n_tokens, n_assign):
    xbufs, ybufs = scratch[:MOE_BUFS], scratch[MOE_BUFS:2 * MOE_BUFS]
    wg_sc, wu_sc, wd_sc, hsem = scratch[2 * MOE_BUFS:2 * MOE_BUFS + 4]
    ssems = scratch[2 * MOE_BUFS + 4:]
    i = pl.program_id(0)
    n_steps = pl.num_programs(0)
    r = ROWS_MOE
    half = D_MODEL // 2
    n_used = nu_ref[0]
    active = i <= n_used

    def copy_rows(blk, xdst):
        base = cs_ref[blk]
        for j in range(r):
            tok = order_ref[base + j] & (n_tokens - 1)
            xdst[j // 8, pl.ds(j % 8, 1), :] = hbuf[tok]

    def compute(blk, xsrc, ydst):
        row = lax.broadcasted_iota(jnp.int32, (r, half), 0)
        xp = jnp.where(row < nv_ref[blk], xsrc[...].reshape(r, half), jnp.uint32(0))
        x_lo, x_hi = _unpack_halves(xp)
        g = (jnp.dot(x_lo, wg_sc[:half, :], preferred_element_type=F32)
             + jnp.dot(x_hi, wg_sc[half:, :], preferred_element_type=F32))
        u = (jnp.dot(x_lo, wu_sc[:half, :], preferred_element_type=F32)
             + jnp.dot(x_hi, wu_sc[half:, :], preferred_element_type=F32))
        y = jnp.dot((_silu(g) * u).astype(BF16), wd_sc[...], preferred_element_type=F32)
        ydst[...] = _pack_halves(y.astype(BF16)).reshape(r // 8, 8, half)

    def scatter(blk, n_rows, v, j0, j1):
        base = cs_ref[blk]
        for j in range(j0, j1):
            dst = order_ref[base + j]
            if not isinstance(n_rows, int):
                dst = jnp.where(j < n_rows, dst, n_assign + v * r + j)
            pltpu.make_async_copy(ybufs[v].at[j // 8, pl.ds(j % 8, 1), :], y_ref.at[dst],
                                  ssems[v]).start(priority=j % 2)

    def scattered_rows(step_no):
        n_rows = jnp.where(step_no > 0, nv_ref[jnp.maximum(step_no - 1, 0)], 0)
        tail = jnp.maximum(n_rows - MOE_HEAD, 0)
        return MOE_HEAD + (tail + MOE_CHUNK - 1) // MOE_CHUNK * MOE_CHUNK

    def scatter_wait(step_no, v):
        n = scattered_rows(step_no)
        pltpu.make_async_copy(y_ref.at[pl.ds(0, n)], y_ref.at[pl.ds(r, n)], ssems[v]).wait()

    def for_buffer(step_no, fn):
        for v in range(MOE_BUFS):
            pl.when(step_no % MOE_BUFS == v)(functools.partial(fn, v))

    @pl.when(i == 0)
    def _():
        load = pltpu.make_async_copy(h_ref, hbuf, hsem)
        load.start()
        load.wait()
        copy_rows(0, xbufs[0])
        zbuf = ybufs[MOE_BUFS - 1]
        zbuf[...] = jnp.zeros_like(zbuf)
        for j in range(MOE_BUFS * r):
            pltpu.make_async_copy(zbuf.at[(j % r) // 8, pl.ds(j % 8, 1), :],
                                  y_ref.at[n_assign + j], hsem).start(priority=j % 2)
        for _ in range(MOE_BUFS):
            pltpu.make_async_copy(y_ref.at[pl.ds(0, r)], y_ref.at[pl.ds(r, r)], hsem).wait()

    @pl.when((i >= 2) & (i - 2 <= n_used))
    def _():
        for_buffer(i + MOE_BUFS - 3, functools.partial(scatter_wait, i - 2))

    prev = be_ref[jnp.maximum(i - 1, 0)]

    @pl.when((i == 0) | (be_ref[i] != prev))
    def _():
        wg_sc[...] = wg_ref[0].astype(BF16)
        wu_sc[...] = wu_ref[0].astype(BF16)
        wd_sc[...] = wd_ref[0].astype(BF16)

    def step(v):
        last = jnp.maximum(i - 1, 0)
        n_rows = jnp.where(i > 0, nv_ref[last], 0)
        src = (v - 1) % MOE_BUFS

        @pl.when(n_rows == r)
        def _():
            scatter(last, r, src, 0, r)
            copy_rows(i + 1, xbufs[(v + 1) % MOE_BUFS])
            compute(i, xbufs[v], ybufs[v])

        @pl.when(n_rows != r)
        def _():
            scatter(last, n_rows, src, 0, MOE_HEAD)
            copy_rows(i + 1, xbufs[(v + 1) % MOE_BUFS])
            compute(i, xbufs[v], ybufs[v])
            for j0 in range(MOE_HEAD, r, MOE_CHUNK):
                pl.when(j0 < n_rows)(
                    functools.partial(scatter, last, n_rows, src, j0, j0 + MOE_CHUNK))

    @pl.when(active)
    def _():
        for_buffer(i, step)

    @pl.when(i == n_steps - 1)
    def _():
        @pl.when(i - 1 <= n_used)
        def _():
            for_buffer(i + MOE_BUFS - 2, functools.partial(scatter_wait, i - 1))

        @pl.when(active)
        def _():
            for_buffer(i + MOE_BUFS - 1, functools.partial(scatter_wait, i))


def _moe_experts(order, block_expert, block_valid, block_start, n_used, h2, w_gate, w_up, w_down):
    t = h2.shape[0]
    assert t & (t - 1) == 0, "token count must be a power of two"
    n_blocks = block_expert.shape[0] - 2
    n_assign = TOP_K * t
    wspec = lambda shp: pl.BlockSpec((1,) + shp, lambda i, od, be, nv, cs, nu: (be[i], 0, 0))
    row_buf = pltpu.VMEM((ROWS_MOE // 8, 8, D_MODEL // 2), jnp.uint32)
    grid_spec = pltpu.PrefetchScalarGridSpec(
        num_scalar_prefetch=5,
        grid=(n_blocks + 1,),
        in_specs=[pl.BlockSpec(memory_space=pl.ANY),
                  wspec((D_MODEL, EXPERT_DIM)), wspec((D_MODEL, EXPERT_DIM)),
                  wspec((EXPERT_DIM, D_MODEL))],
        out_specs=pl.BlockSpec(memory_space=pl.ANY),
        scratch_shapes=([pltpu.VMEM((t, 1, D_MODEL // 2), jnp.uint32)]
                        + [row_buf] * (2 * MOE_BUFS)
                        + [pltpu.VMEM((D_MODEL, EXPERT_DIM), BF16),
                           pltpu.VMEM((D_MODEL, EXPERT_DIM), BF16),
                           pltpu.VMEM((EXPERT_DIM, D_MODEL), BF16),
                           pltpu.SemaphoreType.DMA]
                        + [pltpu.SemaphoreType.DMA] * MOE_BUFS),
    )
    return pl.pallas_call(
        functools.partial(_moe_kernel, n_tokens=t, n_assign=n_assign),
        grid_spec=grid_spec,
        out_shape=jax.ShapeDtypeStruct((n_assign + MOE_BUFS * ROWS_MOE, 1, D_MODEL // 2),
                                       jnp.uint32),
        compiler_params=_params("arbitrary"),
        name="moe_experts",
    )(order, block_expert, block_valid, block_start, n_used, h2, w_gate, w_up, w_down)


def _combine_kernel(base_ref, w_ref, mod_ref, g_ref, *refs, tiles_per_batch, last_layer):
    y_refs, o_ref = refs[:TOP_K], refs[TOP_K]
    tc = base_ref.shape[0]
    lo = [None] * PACKED_ROWS
    hi = [None] * PACKED_ROWS
    for k in range(TOP_K):
        wk = w_ref[:, k:k + 1]
        for cb in range(PACKED_ROWS):
            yp = y_refs[k][pl.ds(cb, tc, stride=PACKED_ROWS), :]
            y_lo = lax.bitcast_convert_type(yp << 16, F32) * wk
            y_hi = lax.bitcast_convert_type(yp & jnp.uint32(0xFFFF0000), F32) * wk
            lo[cb] = y_lo if lo[cb] is None else lo[cb] + y_lo
            hi[cb] = y_hi if hi[cb] is None else hi[cb] + y_hi
    routed = jnp.concatenate(lo + hi, axis=1)
    b = pl.program_id(0) // tiles_per_batch
    gate2 = mod_ref[pl.ds(b, 1), 5 * D_MODEL:6 * D_MODEL]
    x = base_ref[...] + gate2 * routed
    if last_layer:
        x = x * lax.rsqrt(jnp.mean(x * x, axis=-1, keepdims=True) + NORM_EPS) * g_ref[...]
    o_ref[...] = x


def _combine(base, top_w, mod, final_g, y, seq, last_layer):
    t = base.shape[0]
    tc = T_COMBINE
    kern = functools.partial(_combine_kernel, tiles_per_batch=seq // tc, last_layer=last_layer)
    y_rows = y.reshape(y.shape[0] * PACKED_ROWS, LANES)
    tiles = t // tc
    y_specs = [pl.BlockSpec((tc * PACKED_ROWS, LANES), lambda i, k=k: (k * tiles + i, 0))
               for k in range(TOP_K)]
    return pl.pallas_call(
        kern,
        grid=(tiles,),
        in_specs=[pl.BlockSpec((tc, D_MODEL), lambda i: (i, 0)),
                  pl.BlockSpec((tc, TOP_K), lambda i: (i, 0)),
                  pl.BlockSpec(mod.shape, lambda i: (0, 0)),
                  pl.BlockSpec((1, D_MODEL), lambda i: (0, 0))] + y_specs,
        out_specs=pl.BlockSpec((tc, D_MODEL), lambda i: (i, 0)),
        out_shape=jax.ShapeDtypeStruct((t, D_MODEL), F32),
        compiler_params=_params("arbitrary"),
        name="combine",
    )(base, top_w, mod, final_g, *([y_rows] * TOP_K))


def _block_tables(counts, n_assign):
    r = ROWS_MOE
    cnt = counts.reshape(N_EXPERTS).astype(jnp.int32)
    first = jnp.cumsum(cnt) - cnt
    nblk = (cnt + r - 1) // r
    blk_end = jnp.cumsum(nblk)
    blk_start = blk_end - nblk
    n_blocks = (n_assign + N_EXPERTS * (r - 1) + r - 1) // r
    bi = jnp.arange(n_blocks + 2, dtype=jnp.int32)
    bc = jnp.minimum(bi, blk_end[-1] - 1)
    block_expert = jnp.minimum(jnp.sum((blk_end[None, :] <= bc[:, None]).astype(jnp.int32), axis=1),
                               N_EXPERTS - 1)
    onehot = (block_expert[:, None] == jnp.arange(N_EXPERTS, dtype=jnp.int32)[None, :])
    pick = lambda v: jnp.sum(jnp.where(onehot, v[None, :], 0), axis=1)
    within = bc - pick(blk_start)
    block_valid = jnp.where(bi < blk_end[-1], jnp.clip(pick(cnt) - within * r, 0, r), 0)
    block_start = pick(first) + within * r
    return (first.astype(F32).reshape(N_EXPERTS, 1), block_expert.astype(jnp.int32),
            block_valid.astype(jnp.int32), block_start.astype(jnp.int32),
            blk_end[-1:].astype(jnp.int32))


def kernel(x, c, w_ada, b_ada, norm1_g, w_in, b_forget, ret_gn_g, w_branch_a, w_branch_b, w_out,
           norm2_g, w_router, router_bias, w_exp_gate, w_exp_up, w_exp_down, w_sh_gate, w_sh_up,
           w_sh_down, final_g):
    batch, seq, d = x.shape
    t = batch * seq
    depth = w_ada.shape[0]
    x2 = x.reshape(t, d)

    half = RET_QK_DIM // 2
    inv_freq = ROPE_BASE ** (-jnp.arange(half, dtype=F32) / half)
    ang = jnp.arange(seq, dtype=F32)[:, None] * inv_freq[None, :]
    cos32, sin32 = jnp.cos(ang), jnp.sin(ang)
    cos_t = jnp.concatenate([cos32, cos32, cos32, cos32], axis=-1)
    sin_t = jnp.concatenate([-sin32, sin32, -sin32, sin32], axis=-1)
    dec_t, zeta_t, xi_t, cd_t = _retention_tables()
    c_pad = jnp.zeros((8, d), F32).at[:batch].set(c)

    for l in range(depth):
        mod = _adaln_mod(c_pad, w_ada[l], b_ada[l])

        w = w_in[l]
        o_fq, o_fk, o_fv, o_ff = 0, FOX_WIDTH, 2 * FOX_WIDTH, 3 * FOX_WIDTH
        o_rq = o_ff + FOX_HEADS
        o_rk = o_rq + RET_QK_WIDTH
        o_rv = o_rk + RET_QK_WIDTH
        o_rg = o_rv + RET_V_WIDTH
        o_ga = o_rg + RET_V_WIDTH
        o_gb = o_ga + D_MODEL
        w_p = jnp.concatenate(
            [w[:, o_rv:o_rg], w[:, o_rg:o_ga], w[:, o_ga:o_gb], w[:, o_gb:o_gb + D_MODEL],
             w[:, o_fq:o_fk], w[:, o_fk:o_fv], w[:, o_fv:o_ff], w[:, o_rq:o_rk], w[:, o_rk:o_rv]],
            axis=1).astype(BF16)
        w_ff = jnp.zeros((d, LANES), BF16).at[:, :FOX_HEADS].set(w[:, o_ff:o_rq].astype(BF16))
        b_ff = jnp.zeros((1, LANES), F32).at[0, :FOX_HEADS].set(b_forget[l].astype(F32))

        proj, cum_f = _in_proj(x2, mod, norm1_g[l].reshape(1, d), w_p, w_ff, b_ff, cos_t, sin_t, seq)

        attn = _fox_attn(proj, cum_f, batch, seq)
        ret = _retention(proj, ret_gn_g[l].reshape(1, RET_V_WIDTH), dec_t, zeta_t, xi_t, cd_t,
                         batch, seq)

        wr_head, wr_rest = _split_bf16(w_router[l].T.astype(F32))
        base, h2, scores_t = _post_mix(
            attn, ret, proj, x2, mod, norm2_g[l].reshape(1, d),
            w_branch_a[l].astype(BF16), w_branch_b[l].astype(BF16), w_out[l].astype(BF16),
            wr_head, wr_rest, w_sh_gate[l].astype(BF16), w_sh_up[l].astype(BF16),
            w_sh_down[l].astype(BF16), seq)

        top_e, top_w, rank, counts = _route(scores_t,
                                            router_bias[l].astype(F32).reshape(N_EXPERTS, 1))
        first, block_expert, block_valid, block_start, n_used = _block_tables(counts, t * TOP_K)
        sorted_pos = _dest_rows(top_e, rank, first).reshape(TOP_K * t)
        order = jnp.concatenate([jnp.argsort(sorted_pos).astype(jnp.int32),
                                 jnp.zeros((ROWS_MOE,), jnp.int32)])
        y = _moe_experts(order, block_expert, block_valid, block_start, n_used,
                         h2.reshape(t, 1, d // 2),
                         w_exp_gate[l], w_exp_up[l], w_exp_down[l])
        x2 = _combine(base, top_w.T, mod, final_g.reshape(1, d), y, seq, l == depth - 1)
    return x2.reshape(batch, seq, d)
```

```python
import functools

import jax
import jax.numpy as jnp
import numpy as np
from jax import lax
from jax.experimental import pallas as pl
from jax.experimental.pallas import tpu as pltpu

F32 = jnp.float32
BF16 = jnp.bfloat16

D_MODEL = 1024
FOX_HEADS = 8
FOX_HEAD_DIM = 64
FOX_WIDTH = 512
RET_HEADS = 8
RET_QK_DIM = 64
RET_V_DIM = 128
RET_QK_WIDTH = 512
RET_V_WIDTH = 1024
RET_CHUNK = 256
ROPE_BASE = 10000.0
N_EXPERTS = 256
TOP_K = 8
N_GROUPS = 8
TOPK_GROUPS = 4
EXPERT_DIM = 256
ROUTED_SCALE = 2.5
NORM_EPS = 1e-6
LOG2_E = 1.4426950408889634

LANES = 128
PACKED_ROWS = D_MODEL // 2 // LANES
VMEM_LIMIT = 56 * 1024 * 1024

COL_RV, COL_RG, COL_GA, COL_GB = 0, 1024, 2048, 3072
COL_FQ, COL_FK, COL_FV, COL_RQ, COL_RK = 4096, 4608, 5120, 5632, 6144
PROJ_WIDTH = 6656

TM_PROJ = 512
TN_PROJ = 512
T_ATTN = 1024
T_RET = 512
TM_POST = 512
T_ROUTE = 512
ROWS_MOE = 256
MOE_BUFS = 3
MOE_HEAD = 128
MOE_CHUNK = 32
T_COMBINE = 512


def _sigmoid(z):
    return 1.0 / (1.0 + jnp.exp(-z))


def _silu(z):
    return z * _sigmoid(z)


def _pack_halves(xb):
    n = xb.shape[1] // 2
    lo = lax.bitcast_convert_type(xb[:, :n].astype(F32), jnp.uint32)
    hi = lax.bitcast_convert_type(xb[:, n:].astype(F32), jnp.uint32)
    return (lo >> 16) | hi


def _unpack_halves(xp):
    lo = lax.bitcast_convert_type(xp << 16, F32)
    hi = lax.bitcast_convert_type(xp & jnp.uint32(0xFFFF0000), F32)
    return lo.astype(BF16), hi.astype(BF16)


def _split_bf16(x):
    head = lax.bitcast_convert_type(
        lax.bitcast_convert_type(x, jnp.uint32) & jnp.uint32(0xFFFF0000), F32)
    return head.astype(BF16), (x - head).astype(BF16)


def _params(*sem):
    return pltpu.CompilerParams(dimension_semantics=sem, vmem_limit_bytes=VMEM_LIMIT)


def _adaln_kernel(c_ref, w_ref, b_ref, o_ref):
    a = _silu(c_ref[...]).astype(BF16)
    o_ref[...] = jnp.dot(a, w_ref[...].astype(BF16), preferred_element_type=F32) + b_ref[...]


def _adaln_mod(c_pad, w_ada, b_ada):
    n = w_ada.shape[1]
    tn = 1024
    return pl.pallas_call(
        _adaln_kernel,
        grid=(n // tn,),
        in_specs=[pl.BlockSpec((8, D_MODEL), lambda j: (0, 0)),
                  pl.BlockSpec((D_MODEL, tn), lambda j: (0, j)),
                  pl.BlockSpec((1, tn), lambda j: (0, j))],
        out_specs=pl.BlockSpec((8, tn), lambda j: (0, j)),
        out_shape=jax.ShapeDtypeStruct((8, n), F32),
        compiler_params=_params("arbitrary"),
        name="adaln_mod",
    )(c_pad, w_ada, b_ada.reshape(1, n))


def _inproj_kernel(x_ref, mod_ref, g_ref, w_ref, wff_ref, bf_ref, cos_ref, sin_ref,
                   proj_ref, cumf_ref, cf_sc, *, tiles_per_batch):
    i = pl.program_id(0)
    b = i // tiles_per_batch
    x = x_ref[...]
    y = x * lax.rsqrt(jnp.mean(x * x, axis=-1, keepdims=True) + NORM_EPS) * g_ref[...]
    shift = mod_ref[pl.ds(b, 1), 0:D_MODEL]
    scale = mod_ref[pl.ds(b, 1), D_MODEL:2 * D_MODEL]
    hb = (y * (1.0 + scale) + shift).astype(BF16)
    z = jnp.dot(hb, wff_ref[...], preferred_element_type=F32) + bf_ref[...]
    logf = jnp.minimum(z, 0.0) - jnp.log1p(jnp.exp(-jnp.abs(z)))
    tm = logf.shape[0]
    tri = jnp.where(lax.broadcasted_iota(jnp.int32, (tm, tm), 1)
                    <= lax.broadcasted_iota(jnp.int32, (tm, tm), 0), 1.0, 0.0).astype(BF16)
    h1, _ = _split_bf16(logf)
    h2, r2 = _split_bf16(logf - h1.astype(F32))
    carry = jnp.where(i % tiles_per_batch == 0, 0.0, cf_sc[...])
    cum = (jnp.dot(tri, h1, preferred_element_type=F32) + jnp.dot(tri, h2, preferred_element_type=F32)
           + jnp.dot(tri, r2, preferred_element_type=F32)) + carry
    cf_sc[...] = cum[tm - 1:tm, :]
    cumf_ref[0] = cum.T[0:FOX_HEADS, :] * LOG2_E

    cos = cos_ref[...]
    sin = sin_ref[...]
    lane = lax.broadcasted_iota(jnp.int32, cos.shape, 1)
    first_half = (lane % RET_QK_DIM) < (RET_QK_DIM // 2)

    for c0 in range(0, PROJ_WIDTH, TN_PROJ):
        cols = slice(c0, c0 + TN_PROJ)
        acc = jnp.dot(hb, w_ref[:, cols], preferred_element_type=F32)
        if COL_GA <= c0 < COL_FQ:
            proj_ref[:, cols] = _sigmoid(acc).astype(BF16)
        elif COL_FQ <= c0 < COL_FK:
            proj_ref[:, cols] = (acc * (LOG2_E * FOX_HEAD_DIM ** -0.5)).astype(BF16)
        elif c0 >= COL_RQ:
            k_scale = RET_QK_DIM ** -0.5 if c0 >= COL_RK else 1.0
            for cb in range(c0, c0 + TN_PROJ, LANES):
                xs = acc[:, cb - c0:cb - c0 + LANES]
                up = pltpu.roll(xs, LANES - RET_QK_DIM // 2, axis=1)
                dn = pltpu.roll(xs, RET_QK_DIM // 2, axis=1)
                rot = xs * cos + jnp.where(first_half, up, dn) * sin
                proj_ref[:, cb:cb + LANES] = (rot * k_scale).astype(BF16)
        else:
            proj_ref[:, cols] = acc.astype(BF16)


def _in_proj(x2, mod, norm_g, w_p, w_ff, b_ff, cos_t, sin_t, seq):
    t = x2.shape[0]
    tiles_per_batch = seq // TM_PROJ
    kern = functools.partial(_inproj_kernel, tiles_per_batch=tiles_per_batch)
    return pl.pallas_call(
        kern,
        grid=(t // TM_PROJ,),
        in_specs=[pl.BlockSpec((TM_PROJ, D_MODEL), lambda i: (i, 0)),
                  pl.BlockSpec(mod.shape, lambda i: (0, 0)),
                  pl.BlockSpec((1, D_MODEL), lambda i: (0, 0)),
                  pl.BlockSpec((D_MODEL, PROJ_WIDTH), lambda i: (0, 0)),
                  pl.BlockSpec((D_MODEL, LANES), lambda i: (0, 0)),
                  pl.BlockSpec((1, LANES), lambda i: (0, 0)),
                  pl.BlockSpec((TM_PROJ, LANES), lambda i: (i % tiles_per_batch, 0)),
                  pl.BlockSpec((TM_PROJ, LANES), lambda i: (i % tiles_per_batch, 0))],
        out_specs=[pl.BlockSpec((TM_PROJ, PROJ_WIDTH), lambda i: (i, 0)),
                   pl.BlockSpec((1, FOX_HEADS, TM_PROJ),
                                lambda i: (i // tiles_per_batch, 0, i % tiles_per_batch))],
        out_shape=[jax.ShapeDtypeStruct((t, PROJ_WIDTH), BF16),
                   jax.ShapeDtypeStruct((t // seq, FOX_HEADS, seq), F32)],
        scratch_shapes=[pltpu.VMEM((1, LANES), F32)],
        compiler_params=_params("arbitrary"),
        name="in_proj",
    )(x2, mod, norm_g, w_p, w_ff, b_ff, cos_t, sin_t)


def _attn_kernel(q_ref, k_ref, v_ref, f_ref, o_ref):
    hp = pl.program_id(1)
    qi = pl.program_id(2)
    t_attn = T_ATTN
    q = q_ref[...]
    lane = lax.broadcasted_iota(jnp.int32, (t_attn, LANES), 1)
    zeros = jnp.zeros_like(q)
    qm = (jnp.where(lane < FOX_HEAD_DIM, q, zeros), jnp.where(lane >= FOX_HEAD_DIM, q, zeros))
    lane1 = lax.broadcasted_iota(jnp.int32, (1, LANES), 1)
    own = (jnp.where(lane1 < FOX_HEAD_DIM, 1.0, 0.0).astype(BF16),
           jnp.where(lane1 >= FOX_HEAD_DIM, 1.0, 0.0).astype(BF16))

    def block(r0, nr, ks, nk, carry, masked):
        k = k_ref[pl.ds(ks, nk), :]
        v = v_ref[pl.ds(ks, nk), :]
        v1 = (v * own[0] + own[1], v * own[1] + own[0])
        out = []
        for hh in range(2):
            m, acc = carry[hh]
            s = lax.dot_general(qm[hh][r0:r0 + nr], k, (((1,), (1,)), ((), ())),
                                preferred_element_type=F32)
            s = s - f_ref[0, pl.ds(2 * hp + hh, 1), pl.ds(ks, nk)]
            if masked:
                row = qi * t_attn + r0 + lax.broadcasted_iota(jnp.int32, (nr, nk), 0)
                col = ks + lax.broadcasted_iota(jnp.int32, (nr, nk), 1)
                s = jnp.where(col <= row, s, -jnp.inf)
            m_new = jnp.maximum(m, jnp.max(s, axis=-1, keepdims=True))
            p = jnp.exp2(s - m_new).astype(BF16)
            acc = jnp.exp2(m - m_new) * acc + jnp.dot(p, v1[hh], preferred_element_type=F32)
            out.append((m_new, acc))
        return tuple(out)

    def step(kb, carry):
        return block(0, t_attn, pl.multiple_of(kb * t_attn, t_attn), t_attn, carry, False)

    init = (jnp.full((t_attn, 1), -jnp.inf, F32), jnp.zeros((t_attn, LANES), F32))
    def pair(kp, c):
        return step(2 * kp + 1, step(2 * kp, c))

    carry = lax.fori_loop(0, qi // 2, pair, (init, init))
    carry = lax.cond(qi % 2 == 1, lambda c: step(qi - 1, c), lambda c: c, carry)
    half = t_attn // 2
    ks = pl.multiple_of(qi * t_attn, t_attn)
    rows = lambda c, r0: tuple((m[r0:r0 + half], acc[r0:r0 + half]) for m, acc in c)
    top = block(0, half, ks, half, rows(carry, 0), True)
    bot = block(half, half, ks, t_attn, rows(carry, half), True)
    (_, acc0), (_, acc1) = tuple(
        (jnp.concatenate([mt, mb], axis=0), jnp.concatenate([at, ab], axis=0))
        for (mt, at), (mb, ab) in zip(top, bot))
    o_ref[...] = jnp.where(lane < FOX_HEAD_DIM, acc0 / acc0[:, FOX_HEAD_DIM:FOX_HEAD_DIM + 1],
                           acc1 / acc1[:, 0:1]).astype(BF16)


def _fox_attn(proj, cum_f, batch, seq):
    t = proj.shape[0]
    nq = seq // T_ATTN
    cq, ck, cv = COL_FQ // LANES, COL_FK // LANES, COL_FV // LANES
    return pl.pallas_call(
        _attn_kernel,
        grid=(batch, FOX_HEADS // 2, nq),
        in_specs=[pl.BlockSpec((T_ATTN, LANES), lambda b, hp, qi: (b * nq + qi, cq + hp)),
                  pl.BlockSpec((seq, LANES), lambda b, hp, qi: (b, ck + hp)),
                  pl.BlockSpec((seq, LANES), lambda b, hp, qi: (b, cv + hp)),
                  pl.BlockSpec((1, FOX_HEADS, seq), lambda b, hp, qi: (b, 0, 0))],
        out_specs=pl.BlockSpec((T_ATTN, LANES), lambda b, hp, qi: (b * nq + qi, hp)),
        out_shape=jax.ShapeDtypeStruct((t, FOX_WIDTH), BF16),
        compiler_params=_params("arbitrary", "arbitrary", "arbitrary"),
        name="fox_attn",
    )(proj, proj, proj, cum_f)


def _ret_kernel(q_ref, k_ref, v_ref, rg_ref, gn_ref, dec_ref, zeta_ref, xi_ref, cd_ref,
                o_ref, st_sc):
    ri = pl.program_id(1)
    c = RET_CHUNK

    @pl.when(ri == 0)
    def _():
        st_sc[...] = jnp.zeros_like(st_sc)

    lane = lax.broadcasted_iota(jnp.int32, (c, LANES), 1)
    zeros = jnp.zeros((c, LANES), BF16)
    states = [st_sc[h] for h in range(RET_HEADS)]
    for n in range(T_RET // c):
        rows = slice(n * c, (n + 1) * c)
        for h in range(RET_HEADS):
            pair = slice((h // 2) * LANES, (h // 2 + 1) * LANES)
            vcol = slice(h * RET_V_DIM, (h + 1) * RET_V_DIM)
            sel = (lane < RET_QK_DIM) if h % 2 == 0 else (lane >= RET_QK_DIM)
            qc = jnp.where(sel, q_ref[rows, pair], zeros)
            kc = jnp.where(sel, k_ref[rows, pair], zeros)
            vc = v_ref[rows, vcol]
            sc = lax.dot_general(qc, kc, (((1,), (1,)), ((), ())),
                                 preferred_element_type=F32) * dec_ref[h]
            intra = jnp.dot(sc.astype(BF16), vc, preferred_element_type=F32)
            cross = jnp.dot(qc, states[h].astype(BF16), preferred_element_type=F32) * xi_ref[h]
            o = intra + cross
            kz = (kc.astype(F32) * zeta_ref[h]).astype(BF16)
            kv = lax.dot_general(kz, vc, (((0,), (0,)), ((), ())), preferred_element_type=F32)
            states[h] = cd_ref[h] * states[h] + kv
            mu = jnp.mean(o, axis=-1, keepdims=True)
            d = o - mu
            var = jnp.mean(d * d, axis=-1, keepdims=True)
            on = d * lax.rsqrt(var + NORM_EPS) * gn_ref[:, vcol]
            rg = rg_ref[rows, vcol].astype(F32)
            o_ref[rows, vcol] = (on * _silu(rg)).astype(BF16)
    for h in range(RET_HEADS):
        st_sc[h] = states[h]


def _retention(proj, gn_g, dec_t, zeta_t, xi_t, cd_t, batch, seq):
    t = proj.shape[0]
    nr = seq // T_RET
    tok = lambda b, ri: b * nr + ri
    full = lambda a: pl.BlockSpec(a.shape, lambda b, ri: (0,) * a.ndim)
    return pl.pallas_call(
        _ret_kernel,
        grid=(batch, nr),
        in_specs=[pl.BlockSpec((T_RET, RET_QK_WIDTH), lambda b, ri: (tok(b, ri), COL_RQ // RET_QK_WIDTH)),
                  pl.BlockSpec((T_RET, RET_QK_WIDTH), lambda b, ri: (tok(b, ri), COL_RK // RET_QK_WIDTH)),
                  pl.BlockSpec((T_RET, RET_V_WIDTH), lambda b, ri: (tok(b, ri), COL_RV // RET_V_WIDTH)),
                  pl.BlockSpec((T_RET, RET_V_WIDTH), lambda b, ri: (tok(b, ri), COL_RG // RET_V_WIDTH)),
                  pl.BlockSpec((1, RET_V_WIDTH), lambda b, ri: (0, 0)),
                  full(dec_t), full(zeta_t), full(xi_t), full(cd_t)],
        out_specs=pl.BlockSpec((T_RET, RET_V_WIDTH), lambda b, ri: (tok(b, ri), 0)),
        out_shape=jax.ShapeDtypeStruct((t, RET_V_WIDTH), BF16),
        scratch_shapes=[pltpu.VMEM((RET_HEADS, LANES, RET_V_DIM), F32)],
        compiler_params=_params("arbitrary", "arbitrary"),
        name="retention",
    )(proj, proj, proj, proj, gn_g, dec_t, zeta_t, xi_t, cd_t)


def _retention_tables():
    c = RET_CHUNK
    log_gamma = jnp.log1p(-jnp.exp2(-5.0 - jnp.arange(RET_HEADS, dtype=F32)))
    idx = jnp.arange(c, dtype=F32)
    diff = idx[:, None] - idx[None, :]
    dec = jnp.where(diff >= 0,
                    jnp.exp(jnp.maximum(diff, 0.0)[None] * log_gamma[:, None, None]), 0.0)
    zeta = jnp.exp((c - 1.0 - idx)[None, :] * log_gamma[:, None])
    xi = jnp.exp((idx + 1.0)[None, :] * log_gamma[:, None])
    cd = jnp.exp(c * log_gamma)
    bc = lambda v: jnp.broadcast_to(v[:, :, None], (RET_HEADS, c, LANES))
    cd_t = jnp.broadcast_to(cd[:, None, None], (RET_HEADS, LANES, RET_V_DIM))
    return dec, bc(zeta), bc(xi), cd_t


def _post_kernel(attn_ref, ret_ref, ga_ref, gb_ref, x_ref, mod_ref, g2_ref,
                 wa_ref, wb_ref, wo_ref, wr_ref, wrr_ref, wsg_ref, wsu_ref, wsd_ref,
                 base_ref, h2_ref, score_ref, *, tiles_per_batch):
    b = pl.program_id(0) // tiles_per_batch
    ya = jnp.dot(attn_ref[...], wa_ref[...], preferred_element_type=F32)
    yb = jnp.dot(ret_ref[...], wb_ref[...], preferred_element_type=F32)
    merged = ga_ref[...].astype(F32) * ya + gb_ref[...].astype(F32) * yb
    mix = jnp.dot(merged.astype(BF16), wo_ref[...], preferred_element_type=F32)
    d = D_MODEL
    gate1 = mod_ref[pl.ds(b, 1), 2 * d:3 * d]
    shift2 = mod_ref[pl.ds(b, 1), 3 * d:4 * d]
    scale2 = mod_ref[pl.ds(b, 1), 4 * d:5 * d]
    gate2 = mod_ref[pl.ds(b, 1), 5 * d:6 * d]
    x1 = x_ref[...] + gate1 * mix
    y = x1 * lax.rsqrt(jnp.mean(x1 * x1, axis=-1, keepdims=True) + NORM_EPS) * g2_ref[...]
    h2 = y * (1.0 + scale2) + shift2
    h2_head, h2_rest = _split_bf16(h2)
    nt = (((1,), (1,)), ((), ()))
    logits_t = (lax.dot_general(wr_ref[...], h2_head, nt, preferred_element_type=F32)
                + lax.dot_general(wr_ref[...], h2_rest, nt, preferred_element_type=F32)
                + lax.dot_general(wrr_ref[...], h2_head, nt, preferred_element_type=F32))
    score_ref[...] = _sigmoid(logits_t)
    h2b = h2.astype(BF16)
    h2p = _pack_halves(h2b)
    n_chunks = h2p.shape[1] // LANES
    for cb in range(n_chunks):
        h2_ref[pl.ds(cb, h2p.shape[0], stride=n_chunks), :] = h2p[:, cb * LANES:(cb + 1) * LANES]
    g = jnp.dot(h2b, wsg_ref[...], preferred_element_type=F32)
    u = jnp.dot(h2b, wsu_ref[...], preferred_element_type=F32)
    shared = jnp.dot((_silu(g) * u).astype(BF16), wsd_ref[...], preferred_element_type=F32)
    base_ref[...] = x1 + gate2 * shared


def _post_mix(attn, ret, proj, x2, mod, norm2_g, wa, wb, wo, wr, wrr, wsg, wsu, wsd, seq):
    t = x2.shape[0]
    tm = TM_POST
    kern = functools.partial(_post_kernel, tiles_per_batch=seq // tm)
    full = lambda a: pl.BlockSpec(a.shape, lambda i: (0,) * a.ndim)
    return pl.pallas_call(
        kern,
        grid=(t // tm,),
        in_specs=[pl.BlockSpec((tm, FOX_WIDTH), lambda i: (i, 0)),
                  pl.BlockSpec((tm, RET_V_WIDTH), lambda i: (i, 0)),
                  pl.BlockSpec((tm, D_MODEL), lambda i: (i, COL_GA // D_MODEL)),
                  pl.BlockSpec((tm, D_MODEL), lambda i: (i, COL_GB // D_MODEL)),
                  pl.BlockSpec((tm, D_MODEL), lambda i: (i, 0)),
                  full(mod), full(norm2_g), full(wa), full(wb), full(wo), full(wr), full(wrr),
                  full(wsg), full(wsu), full(wsd)],
        out_specs=[pl.BlockSpec((tm, D_MODEL), lambda i: (i, 0)),
                   pl.BlockSpec((tm * PACKED_ROWS, LANES), lambda i: (i, 0)),
                   pl.BlockSpec((N_EXPERTS, tm), lambda i: (0, i))],
        out_shape=[jax.ShapeDtypeStruct((t, D_MODEL), F32),
                   jax.ShapeDtypeStruct((t * PACKED_ROWS, LANES), jnp.uint32),
                   jax.ShapeDtypeStruct((N_EXPERTS, t), F32)],
        compiler_params=_params("arbitrary"),
        name="post_mix",
    )(attn, ret, proj, proj, x2, mod, norm2_g, wa, wb, wo, wr, wrr, wsg, wsu, wsd)


def _route_kernel(s_ref, bias_ref, e_ref, w_ref, rank_ref, cnt_ref):
    tr = s_ref.shape[1]
    gsz = N_EXPERTS // N_GROUPS

    @pl.when(pl.program_id(0) == 0)
    def _():
        cnt_ref[...] = jnp.zeros_like(cnt_ref)

    s = s_ref[...]
    sel = s + bias_ref[...]
    neg = jnp.float32(-jnp.inf)

    g3 = sel.reshape(N_GROUPS, gsz, tr)
    m1 = jnp.max(g3, axis=1)
    n_max = jnp.sum((g3 == m1[:, None, :]).astype(F32), axis=1)
    m2 = jnp.max(jnp.where(g3 < m1[:, None, :], g3, neg), axis=1)
    gs = m1 + jnp.where(n_max >= 2.0, m1, m2)

    gi = lax.broadcasted_iota(jnp.int32, (N_GROUPS, tr), 0)
    beaten = jnp.zeros((N_GROUPS, tr), F32)
    for j in range(N_GROUPS):
        gj = gs[j:j + 1, :]
        beaten = beaten + jnp.where((gj > gs) | ((gj == gs) & (gi > j)), 1.0, 0.0)
    keep = jnp.where(beaten < float(TOPK_GROUPS), 1.0, 0.0)
    keep_e = jnp.broadcast_to(keep[:, None, :], (N_GROUPS, gsz, tr)).reshape(N_EXPERTS, tr)

    ei = lax.broadcasted_iota(jnp.int32, (N_EXPERTS, tr), 0)
    val = jnp.where(keep_e > 0.0, sel, neg)
    member = jnp.zeros((N_EXPERTS, tr), F32)
    idx_rows, w_rows = [], []
    for _ in range(TOP_K):
        m = jnp.max(val, axis=0, keepdims=True)
        idx = jnp.min(jnp.where(val == m, ei, N_EXPERTS), axis=0, keepdims=True)
        hit = ei == idx
        w_rows.append(jnp.sum(jnp.where(hit, s, 0.0), axis=0, keepdims=True))
        idx_rows.append(idx)
        val = jnp.where(hit, neg, val)
        member = jnp.where(hit, 1.0, member)

    w_sum = w_rows[0]
    for k in range(1, TOP_K):
        w_sum = w_sum + w_rows[k]

    r_i = lax.broadcasted_iota(jnp.int32, (tr, tr), 0)
    c_i = lax.broadcasted_iota(jnp.int32, (tr, tr), 1)
    upper = jnp.where(r_i < c_i, 1.0, 0.0).astype(BF16)
    before = jnp.dot(member.astype(BF16), upper, preferred_element_type=F32) + cnt_ref[...]

    for k in range(TOP_K):
        hit = ei == idx_rows[k]
        e_ref[k:k + 1, :] = idx_rows[k]
        w_ref[k:k + 1, :] = w_rows[k] / w_sum * ROUTED_SCALE
        rank_ref[k:k + 1, :] = jnp.sum(jnp.where(hit, before, 0.0), axis=0,
                                       keepdims=True).astype(jnp.int32)
    cnt_ref[...] = cnt_ref[...] + jnp.sum(member, axis=1, keepdims=True)


def _route(scores_t, bias_col):
    t = scores_t.shape[1]
    tr = T_ROUTE
    row8 = lambda dt: jax.ShapeDtypeStruct((TOP_K, t), dt)
    return pl.pallas_call(
        _route_kernel,
        grid=(t // tr,),
        in_specs=[pl.BlockSpec((N_EXPERTS, tr), lambda i: (0, i)),
                  pl.BlockSpec((N_EXPERTS, 1), lambda i: (0, 0))],
        out_specs=[pl.BlockSpec((TOP_K, tr), lambda i: (0, i)),
                   pl.BlockSpec((TOP_K, tr), lambda i: (0, i)),
                   pl.BlockSpec((TOP_K, tr), lambda i: (0, i)),
                   pl.BlockSpec((N_EXPERTS, 1), lambda i: (0, 0))],
        out_shape=[row8(jnp.int32), row8(F32), row8(jnp.int32),
                   jax.ShapeDtypeStruct((N_EXPERTS, 1), F32)],
        compiler_params=_params("arbitrary"),
        name="route",
    )(scores_t, bias_col)


def _dest_kernel(e_ref, rank_ref, start_ref, dest_ref):
    tr = e_ref.shape[1]
    ei = lax.broadcasted_iota(jnp.int32, (N_EXPERTS, tr), 0)
    start = start_ref[...]
    for k in range(TOP_K):
        hit = ei == e_ref[k:k + 1, :]
        off = jnp.sum(jnp.where(hit, start, 0.0), axis=0, keepdims=True)
        dest_ref[k:k + 1, :] = rank_ref[k:k + 1, :] + off.astype(jnp.int32)


def _dest_rows(top_e, rank, start_col):
    t = top_e.shape[1]
    tr = T_ROUTE
    spec = pl.BlockSpec((TOP_K, tr), lambda i: (0, i))
    return pl.pallas_call(
        _dest_kernel,
        grid=(t // tr,),
        in_specs=[spec, spec, pl.BlockSpec((N_EXPERTS, 1), lambda i: (0, 0))],
        out_specs=spec,
        out_shape=jax.ShapeDtypeStruct((TOP_K, t), jnp.int32),
        compiler_params=_params("arbitrary"),
        name="dest_rows",
    )(top_e, rank, start_col)


def _moe_kernel(order_ref, be_ref, nv_ref, cs_ref, nu_ref, h_ref, wg_ref, wu_ref, wd_ref, y_ref,
                hbuf, *scratch, n_tokens, n_assign):
    xbufs, ybufs = scratch[:MOE_BUFS], scratch[MOE_BUFS:2 * MOE_BUFS]
    wg_sc, wu_sc, wd_sc, hsem = scratch[2 * MOE_BUFS:2 * MOE_BUFS + 4]
    ssems = scratch[2 * MOE_BUFS + 4:]
    i = pl.program_id(0)
    n_steps = pl.num_programs(0)
    r = ROWS_MOE
    half = D_MODEL // 2
    n_used = nu_ref[0]
    active = i <= n_used

    def copy_rows(blk, xdst):
        base = cs_ref[blk]
        for j in range(r):
            tok = order_ref[base + j] & (n_tokens - 1)
            xdst[j // 8, pl.ds(j % 8, 1), :] = hbuf[tok]

    def compute(blk, xsrc, ydst):
        row = lax.broadcasted_iota(jnp.int32, (r, half), 0)
        xp = jnp.where(row < nv_ref[blk], xsrc[...].reshape(r, half), jnp.uint32(0))
        x_lo, x_hi = _unpack_halves(xp)
        g = (jnp.dot(x_lo, wg_sc[:half, :], preferred_element_type=F32)
             + jnp.dot(x_hi, wg_sc[half:, :], preferred_element_type=F32))
        u = (jnp.dot(x_lo, wu_sc[:half, :], preferred_element_type=F32)
             + jnp.dot(x_hi, wu_sc[half:, :], preferred_element_type=F32))
        y = jnp.dot((_silu(g) * u).astype(BF16), wd_sc[...], preferred_element_type=F32)
        yp = _pack_halves(y.astype(BF16))
        for cb in range(PACKED_ROWS):
            ydst[pl.ds(cb, r, stride=PACKED_ROWS), :] = yp[:, cb * LANES:(cb + 1) * LANES]

    def scatter(blk, n_rows, v, j0, j1):
        base = cs_ref[blk]
        for j in range(j0, j1):
            dst = order_ref[base + j]
            if not isinstance(n_rows, int):
                dst = jnp.where(j < n_rows, dst, n_assign + v * r + j)
            pltpu.make_async_copy(ybufs[v].at[pl.ds(j * PACKED_ROWS, PACKED_ROWS), :],
                                  y_ref.at[pl.ds(dst * PACKED_ROWS, PACKED_ROWS), :],
                                  ssems[v]).start(priority=j % 2)

    def scattered_rows(step_no):
        n_rows = jnp.where(step_no > 0, nv_ref[jnp.maximum(step_no - 1, 0)], 0)
        tail = jnp.maximum(n_rows - MOE_HEAD, 0)
        return MOE_HEAD + (tail + MOE_CHUNK - 1) // MOE_CHUNK * MOE_CHUNK

    def scatter_wait(step_no, v):
        n = scattered_rows(step_no)
        n_sub = pl.multiple_of(n * PACKED_ROWS, MOE_CHUNK * PACKED_ROWS)
        pltpu.make_async_copy(y_ref.at[pl.ds(0, n_sub), :], y_ref.at[pl.ds(r * PACKED_ROWS, n_sub), :],
                              ssems[v]).wait()

    def for_buffer(step_no, fn):
        for v in range(MOE_BUFS):
            pl.when(step_no % MOE_BUFS == v)(functools.partial(fn, v))

    @pl.when(i == 0)
    def _():
        load = pltpu.make_async_copy(h_ref, hbuf, hsem)
        load.start()
        load.wait()
        copy_rows(0, xbufs[0])
        zbuf = ybufs[MOE_BUFS - 1]
        zbuf[...] = jnp.zeros_like(zbuf)
        fills = [pltpu.make_async_copy(
            zbuf, y_ref.at[pl.ds((n_assign + v * r) * PACKED_ROWS, r * PACKED_ROWS), :], hsem)
            for v in range(MOE_BUFS)]
        for fill in fills:
            fill.start()
        for fill in fills:
            fill.wait()

    @pl.when((i >= 2) & (i - 2 <= n_used))
    def _():
        for_buffer(i + MOE_BUFS - 3, functools.partial(scatter_wait, i - 2))

    prev = be_ref[jnp.maximum(i - 1, 0)]

    @pl.when((i == 0) | (be_ref[i] != prev))
    def _():
        wg_sc[...] = wg_ref[0].astype(BF16)
        wu_sc[...] = wu_ref[0].astype(BF16)
        wd_sc[...] = wd_ref[0].astype(BF16)

    def step(v):
        last = jnp.maximum(i - 1, 0)
        n_rows = jnp.where(i > 0, nv_ref[last], 0)
        src = (v - 1) % MOE_BUFS

        @pl.when(n_rows == r)
        def _():
            scatter(last, r, src, 0, r)
            copy_rows(i + 1, xbufs[(v + 1) % MOE_BUFS])
            compute(i, xbufs[v], ybufs[v])

        @pl.when(n_rows != r)
        def _():
            scatter(last, n_rows, src, 0, MOE_HEAD)
            copy_rows(i + 1, xbufs[(v + 1) % MOE_BUFS])
            compute(i, xbufs[v], ybufs[v])
            for j0 in range(MOE_HEAD, r, MOE_CHUNK):
                pl.when(j0 < n_rows)(
                    functools.partial(scatter, last, n_rows, src, j0, j0 + MOE_CHUNK))

    @pl.when(active)
    def _():
        for_buffer(i, step)

    @pl.when(i == n_steps - 1)
    def _():
        @pl.when(i - 1 <= n_used)
        def _():
            for_buffer(i + MOE_BUFS - 2, functools.partial(scatter_wait, i - 1))

        @pl.when(active)
        def _():
            for_buffer(i + MOE_BUFS - 1, functools.partial(scatter_wait, i))


def _moe_experts(order, block_expert, block_valid, block_start, n_used, h2, w_gate, w_up, w_down):
    t = h2.shape[0]
    assert t & (t - 1) == 0, "token count must be a power of two"
    n_blocks = block_expert.shape[0] - 2
    n_assign = TOP_K * t
    wspec = lambda shp: pl.BlockSpec((1,) + shp, lambda i, od, be, nv, cs, nu: (be[i], 0, 0))
    row_buf = pltpu.VMEM((ROWS_MOE // 8, 8, D_MODEL // 2), jnp.uint32)
    out_buf = pltpu.VMEM((ROWS_MOE * PACKED_ROWS, LANES), jnp.uint32)
    grid_spec = pltpu.PrefetchScalarGridSpec(
        num_scalar_prefetch=5,
        grid=(n_blocks + 1,),
        in_specs=[pl.BlockSpec(memory_space=pl.ANY),
                  wspec((D_MODEL, EXPERT_DIM)), wspec((D_MODEL, EXPERT_DIM)),
                  wspec((EXPERT_DIM, D_MODEL))],
        out_specs=pl.BlockSpec(memory_space=pl.ANY),
        scratch_shapes=([pltpu.VMEM((t, 1, D_MODEL // 2), jnp.uint32)]
                        + [row_buf] * MOE_BUFS + [out_buf] * MOE_BUFS
                        + [pltpu.VMEM((D_MODEL, EXPERT_DIM), BF16),
                           pltpu.VMEM((D_MODEL, EXPERT_DIM), BF16),
                           pltpu.VMEM((EXPERT_DIM, D_MODEL), BF16),
                           pltpu.SemaphoreType.DMA]
                        + [pltpu.SemaphoreType.DMA] * MOE_BUFS),
    )
    return pl.pallas_call(
        functools.partial(_moe_kernel, n_tokens=t, n_assign=n_assign),
        grid_spec=grid_spec,
        out_shape=jax.ShapeDtypeStruct(((n_assign + MOE_BUFS * ROWS_MOE) * PACKED_ROWS, LANES),
                                       jnp.uint32),
        compiler_params=_params("arbitrary"),
        name="moe_experts",
    )(order, block_expert, block_valid, block_start, n_used, h2, w_gate, w_up, w_down)


def _combine_kernel(base_ref, w_ref, mod_ref, g_ref, *refs, tiles_per_batch, last_layer):
    y_refs, o_ref = refs[:TOP_K], refs[TOP_K]
    tc = base_ref.shape[0]
    lo = [None] * PACKED_ROWS
    hi = [None] * PACKED_ROWS
    for k in range(TOP_K):
        wk = w_ref[:, k:k + 1]
        for cb in range(PACKED_ROWS):
            yp = y_refs[k][pl.ds(cb, tc, stride=PACKED_ROWS), :]
            y_lo = lax.bitcast_convert_type(yp << 16, F32) * wk
            y_hi = lax.bitcast_convert_type(yp & jnp.uint32(0xFFFF0000), F32) * wk
            lo[cb] = y_lo if lo[cb] is None else lo[cb] + y_lo
            hi[cb] = y_hi if hi[cb] is None else hi[cb] + y_hi
    routed = jnp.concatenate(lo + hi, axis=1)
    b = pl.program_id(0) // tiles_per_batch
    gate2 = mod_ref[pl.ds(b, 1), 5 * D_MODEL:6 * D_MODEL]
    x = base_ref[...] + gate2 * routed
    if last_layer:
        x = x * lax.rsqrt(jnp.mean(x * x, axis=-1, keepdims=True) + NORM_EPS) * g_ref[...]
    o_ref[...] = x


def _combine(base, top_w, mod, final_g, y_rows, seq, last_layer):
    t = base.shape[0]
    tc = T_COMBINE
    kern = functools.partial(_combine_kernel, tiles_per_batch=seq // tc, last_layer=last_layer)
    tiles = t // tc
    y_specs = [pl.BlockSpec((tc * PACKED_ROWS, LANES), lambda i, k=k: (k * tiles + i, 0))
               for k in range(TOP_K)]
    return pl.pallas_call(
        kern,
        grid=(tiles,),
        in_specs=[pl.BlockSpec((tc, D_MODEL), lambda i: (i, 0)),
                  pl.BlockSpec((tc, TOP_K), lambda i: (i, 0)),
                  pl.BlockSpec(mod.shape, lambda i: (0, 0)),
                  pl.BlockSpec((1, D_MODEL), lambda i: (0, 0))] + y_specs,
        out_specs=pl.BlockSpec((tc, D_MODEL), lambda i: (i, 0)),
        out_shape=jax.ShapeDtypeStruct((t, D_MODEL), F32),
        compiler_params=_params("arbitrary"),
        name="combine",
    )(base, top_w, mod, final_g, *([y_rows] * TOP_K))


def _block_tables(counts, n_assign):
    r = ROWS_MOE
    cnt = counts.reshape(N_EXPERTS).astype(jnp.int32)
    first = jnp.cumsum(cnt) - cnt
    nblk = (cnt + r - 1) // r
    blk_end = jnp.cumsum(nblk)
    blk_start = blk_end - nblk
    n_blocks = (n_assign + N_EXPERTS * (r - 1) + r - 1) // r
    bi = jnp.arange(n_blocks + 2, dtype=jnp.int32)
    bc = jnp.minimum(bi, blk_end[-1] - 1)
    block_expert = jnp.minimum(jnp.sum((blk_end[None, :] <= bc[:, None]).astype(jnp.int32), axis=1),
                               N_EXPERTS - 1)
    onehot = (block_expert[:, None] == jnp.arange(N_EXPERTS, dtype=jnp.int32)[None, :])
    pick = lambda v: jnp.sum(jnp.where(onehot, v[None, :], 0), axis=1)
    within = bc - pick(blk_start)
    block_valid = jnp.where(bi < blk_end[-1], jnp.clip(pick(cnt) - within * r, 0, r), 0)
    block_start = pick(first) + within * r
    return (first.astype(F32).reshape(N_EXPERTS, 1), block_expert.astype(jnp.int32),
            block_valid.astype(jnp.int32), block_start.astype(jnp.int32),
            blk_end[-1:].astype(jnp.int32))


def kernel(x, c, w_ada, b_ada, norm1_g, w_in, b_forget, ret_gn_g, w_branch_a, w_branch_b, w_out,
           norm2_g, w_router, router_bias, w_exp_gate, w_exp_up, w_exp_down, w_sh_gate, w_sh_up,
           w_sh_down, final_g):
    batch, seq, d = x.shape
    t = batch * seq
    depth = w_ada.shape[0]
    x2 = x.reshape(t, d)

    half = RET_QK_DIM // 2
    inv_freq = ROPE_BASE ** (-jnp.arange(half, dtype=F32) / half)
    ang = jnp.arange(seq, dtype=F32)[:, None] * inv_freq[None, :]
    cos32, sin32 = jnp.cos(ang), jnp.sin(ang)
    cos_t = jnp.concatenate([cos32, cos32, cos32, cos32], axis=-1)
    sin_t = jnp.concatenate([-sin32, sin32, -sin32, sin32], axis=-1)
    dec_t, zeta_t, xi_t, cd_t = _retention_tables()
    c_pad = jnp.zeros((8, d), F32).at[:batch].set(c)

    for l in range(depth):
        mod = _adaln_mod(c_pad, w_ada[l], b_ada[l])

        w = w_in[l]
        o_fq, o_fk, o_fv, o_ff = 0, FOX_WIDTH, 2 * FOX_WIDTH, 3 * FOX_WIDTH
        o_rq = o_ff + FOX_HEADS
        o_rk = o_rq + RET_QK_WIDTH
        o_rv = o_rk + RET_QK_WIDTH
        o_rg = o_rv + RET_V_WIDTH
        o_ga = o_rg + RET_V_WIDTH
        o_gb = o_ga + D_MODEL
        w_p = jnp.concatenate(
            [w[:, o_rv:o_rg], w[:, o_rg:o_ga], w[:, o_ga:o_gb], w[:, o_gb:o_gb + D_MODEL],
             w[:, o_fq:o_fk], w[:, o_fk:o_fv], w[:, o_fv:o_ff], w[:, o_rq:o_rk], w[:, o_rk:o_rv]],
            axis=1).astype(BF16)
        w_ff = jnp.zeros((d, LANES), BF16).at[:, :FOX_HEADS].set(w[:, o_ff:o_rq].astype(BF16))
        b_ff = jnp.zeros((1, LANES), F32).at[0, :FOX_HEADS].set(b_forget[l].astype(F32))

        proj, cum_f = _in_proj(x2, mod, norm1_g[l].reshape(1, d), w_p, w_ff, b_ff, cos_t, sin_t, seq)

        attn = _fox_attn(proj, cum_f, batch, seq)
        ret = _retention(proj, ret_gn_g[l].reshape(1, RET_V_WIDTH), dec_t, zeta_t, xi_t, cd_t,
                         batch, seq)

        wr_head, wr_rest = _split_bf16(w_router[l].T.astype(F32))
        base, h2, scores_t = _post_mix(
            attn, ret, proj, x2, mod, norm2_g[l].reshape(1, d),
            w_branch_a[l].astype(BF16), w_branch_b[l].astype(BF16), w_out[l].astype(BF16),
            wr_head, wr_rest, w_sh_gate[l].astype(BF16), w_sh_up[l].astype(BF16),
            w_sh_down[l].astype(BF16), seq)

        top_e, top_w, rank, counts = _route(scores_t,
                                            router_bias[l].astype(F32).reshape(N_EXPERTS, 1))
        first, block_expert, block_valid, block_start, n_used = _block_tables(counts, t * TOP_K)
        sorted_pos = _dest_rows(top_e, rank, first).reshape(TOP_K * t)
        order = jnp.concatenate([jnp.argsort(sorted_pos).astype(jnp.int32),
                                 jnp.zeros((ROWS_MOE,), jnp.int32)])
        y = _moe_experts(order, block_expert, block_valid, block_start, n_used,
                         h2.reshape(t, 1, d // 2),
                         w_exp_gate[l], w_exp_up[l], w_exp_down[l])
        x2 = _combine(base, top_w.T, mod, final_g.reshape(1, d), y, seq, l == depth - 1)
    return x2.reshape(batch, seq, d)
```

```python
import functools

import jax
import jax.numpy as jnp
from jax import lax
from jax.experimental import pallas as pl
from jax.experimental.pallas import tpu as pltpu

F32 = jnp.float32
BF16 = jnp.bfloat16

D_MODEL = 1024
FOX_HEADS = 8
FOX_HEAD_DIM = 64
FOX_WIDTH = 512
RET_HEADS = 8
RET_QK_DIM = 64
RET_V_DIM = 128
RET_QK_WIDTH = 512
RET_V_WIDTH = 1024
RET_CHUNK = 256
ROPE_BASE = 10000.0
N_EXPERTS = 256
TOP_K = 8
N_GROUPS = 8
TOPK_GROUPS = 4
EXPERT_DIM = 256
ROUTED_SCALE = 2.5
NORM_EPS = 1e-6
LOG2_E = 1.4426950408889634

LANES = 128
PACKED_ROWS = D_MODEL // 2 // LANES
VMEM_LIMIT = 56 * 1024 * 1024

COL_RV, COL_RG, COL_GA, COL_GB = 0, 1024, 2048, 3072
COL_FQ, COL_FK, COL_FV, COL_RQ, COL_RK = 4096, 4608, 5120, 5632, 6144
PROJ_WIDTH = 6656

TR_REPACK = 128
TM_PROJ = 512
TN_PROJ = 512
T_ATTN = 1024
T_RET = 512
TM_POST = 512
T_ROUTE = 512
ROWS_MOE = 256
MOE_BUFS = 3
MOE_WEIGHT_SLOTS = 3
MOE_HEAD = 128
MOE_CHUNK = 32
T_COMBINE = 512


def _sigmoid(z):
    return 1.0 / (1.0 + jnp.exp(-z))


def _silu(z):
    return z * _sigmoid(z)


def _pack_halves(xb):
    n = xb.shape[1] // 2
    lo = lax.bitcast_convert_type(xb[:, :n].astype(F32), jnp.uint32)
    hi = lax.bitcast_convert_type(xb[:, n:].astype(F32), jnp.uint32)
    return (lo >> 16) | hi


def _unpack_halves(xp):
    lo = lax.bitcast_convert_type(xp << 16, F32)
    hi = lax.bitcast_convert_type(xp & jnp.uint32(0xFFFF0000), F32)
    return lo.astype(BF16), hi.astype(BF16)


def _split_bf16(x):
    head = lax.bitcast_convert_type(
        lax.bitcast_convert_type(x, jnp.uint32) & jnp.uint32(0xFFFF0000), F32)
    return head.astype(BF16), (x - head).astype(BF16)


def _params(*sem):
    return pltpu.CompilerParams(dimension_semantics=sem, vmem_limit_bytes=VMEM_LIMIT)


def _adaln_kernel(c_ref, w_ref, b_ref, o_ref):
    a = _silu(c_ref[...]).astype(BF16)
    o_ref[...] = jnp.dot(a, w_ref[...].astype(BF16), preferred_element_type=F32) + b_ref[...]


def _adaln_mod(c_pad, w_ada, b_ada):
    n = w_ada.shape[1]
    tn = 1024
    return pl.pallas_call(
        _adaln_kernel,
        grid=(n // tn,),
        in_specs=[pl.BlockSpec((8, D_MODEL), lambda j: (0, 0)),
                  pl.BlockSpec((D_MODEL, tn), lambda j: (0, j)),
                  pl.BlockSpec((1, tn), lambda j: (0, j))],
        out_specs=pl.BlockSpec((8, tn), lambda j: (0, j)),
        out_shape=jax.ShapeDtypeStruct((8, n), F32),
        compiler_params=_params("arbitrary"),
        name="adaln_mod",
    )(c_pad, w_ada, b_ada.reshape(1, n))


def _w_in_segments():
    o_ff = 3 * FOX_WIDTH
    o_rq = o_ff + FOX_HEADS
    o_rk = o_rq + RET_QK_WIDTH
    o_rv = o_rk + RET_QK_WIDTH
    o_rg = o_rv + RET_V_WIDTH
    o_ga = o_rg + RET_V_WIDTH
    o_gb = o_ga + D_MODEL
    return ((o_rv, COL_RV, RET_V_WIDTH), (o_rg, COL_RG, RET_V_WIDTH), (o_ga, COL_GA, D_MODEL),
            (o_gb, COL_GB, D_MODEL), (0, COL_FQ, FOX_WIDTH), (FOX_WIDTH, COL_FK, FOX_WIDTH),
            (2 * FOX_WIDTH, COL_FV, FOX_WIDTH), (o_rq, COL_RQ, RET_QK_WIDTH),
            (o_rk, COL_RK, RET_QK_WIDTH)), o_ff


def _repack_kernel(w_ref, wp_ref, wff_ref):
    segments, o_ff = _w_in_segments()
    for src, dst, n in segments:
        wp_ref[:, dst:dst + n] = w_ref[:, src:src + n].astype(BF16)
    rows = w_ref.shape[0]
    wff_ref[...] = jnp.concatenate([w_ref[:, o_ff:o_ff + FOX_HEADS].astype(BF16),
                                    jnp.zeros((rows, LANES - FOX_HEADS), BF16)], axis=1)


def _repack_w_in(w):
    d, n_in = w.shape
    tr = TR_REPACK
    return pl.pallas_call(
        _repack_kernel,
        grid=(d // tr,),
        in_specs=[pl.BlockSpec((tr, n_in), lambda i: (i, 0))],
        out_specs=[pl.BlockSpec((tr, PROJ_WIDTH), lambda i: (i, 0)),
                   pl.BlockSpec((tr, LANES), lambda i: (i, 0))],
        out_shape=[jax.ShapeDtypeStruct((d, PROJ_WIDTH), BF16),
                   jax.ShapeDtypeStruct((d, LANES), BF16)],
        compiler_params=_params("arbitrary"),
        name="repack_w_in",
    )(w)


def _inproj_kernel(x_ref, mod_ref, g_ref, w_ref, wff_ref, bf_ref, cos_ref, sin_ref,
                   proj_ref, cumf_ref, cf_sc, *, tiles_per_batch):
    i = pl.program_id(0)
    b = i // tiles_per_batch
    x = x_ref[...]
    y = x * lax.rsqrt(jnp.mean(x * x, axis=-1, keepdims=True) + NORM_EPS) * g_ref[...]
    shift = mod_ref[pl.ds(b, 1), 0:D_MODEL]
    scale = mod_ref[pl.ds(b, 1), D_MODEL:2 * D_MODEL]
    hb = (y * (1.0 + scale) + shift).astype(BF16)
    z = jnp.dot(hb, wff_ref[...], preferred_element_type=F32) + bf_ref[...]
    logf = jnp.minimum(z, 0.0) - jnp.log1p(jnp.exp(-jnp.abs(z)))
    tm = logf.shape[0]
    tri = jnp.where(lax.broadcasted_iota(jnp.int32, (tm, tm), 1)
                    <= lax.broadcasted_iota(jnp.int32, (tm, tm), 0), 1.0, 0.0).astype(BF16)
    h1, _ = _split_bf16(logf)
    h2, r2 = _split_bf16(logf - h1.astype(F32))
    carry = jnp.where(i % tiles_per_batch == 0, 0.0, cf_sc[...])
    cum = (jnp.dot(tri, h1, preferred_element_type=F32) + jnp.dot(tri, h2, preferred_element_type=F32)
           + jnp.dot(tri, r2, preferred_element_type=F32)) + carry
    cf_sc[...] = cum[tm - 1:tm, :]
    cumf_ref[0] = cum.T[0:FOX_HEADS, :] * LOG2_E

    cos = cos_ref[...]
    sin = sin_ref[...]
    lane = lax.broadcasted_iota(jnp.int32, cos.shape, 1)
    first_half = (lane % RET_QK_DIM) < (RET_QK_DIM // 2)

    for c0 in range(0, PROJ_WIDTH, TN_PROJ):
        cols = slice(c0, c0 + TN_PROJ)
        acc = jnp.dot(hb, w_ref[:, cols], preferred_element_type=F32)
        if COL_GA <= c0 < COL_FQ:
            proj_ref[:, cols] = _sigmoid(acc).astype(BF16)
        elif COL_FQ <= c0 < COL_FK:
            proj_ref[:, cols] = (acc * (LOG2_E * FOX_HEAD_DIM ** -0.5)).astype(BF16)
        elif c0 >= COL_RQ:
            k_scale = RET_QK_DIM ** -0.5 if c0 >= COL_RK else 1.0
            for cb in range(c0, c0 + TN_PROJ, LANES):
                xs = acc[:, cb - c0:cb - c0 + LANES]
                up = pltpu.roll(xs, LANES - RET_QK_DIM // 2, axis=1)
                dn = pltpu.roll(xs, RET_QK_DIM // 2, axis=1)
                rot = xs * cos + jnp.where(first_half, up, dn) * sin
                proj_ref[:, cb:cb + LANES] = (rot * k_scale).astype(BF16)
        else:
            proj_ref[:, cols] = acc.astype(BF16)


def _in_proj(x2, mod, norm_g, w_p, w_ff, b_ff, cos_t, sin_t, seq):
    t = x2.shape[0]
    tiles_per_batch = seq // TM_PROJ
    kern = functools.partial(_inproj_kernel, tiles_per_batch=tiles_per_batch)
    return pl.pallas_call(
        kern,
        grid=(t // TM_PROJ,),
        in_specs=[pl.BlockSpec((TM_PROJ, D_MODEL), lambda i: (i, 0)),
                  pl.BlockSpec(mod.shape, lambda i: (0, 0)),
                  pl.BlockSpec((1, D_MODEL), lambda i: (0, 0)),
                  pl.BlockSpec((D_MODEL, PROJ_WIDTH), lambda i: (0, 0)),
                  pl.BlockSpec((D_MODEL, LANES), lambda i: (0, 0)),
                  pl.BlockSpec((1, LANES), lambda i: (0, 0)),
                  pl.BlockSpec((TM_PROJ, LANES), lambda i: (i % tiles_per_batch, 0)),
                  pl.BlockSpec((TM_PROJ, LANES), lambda i: (i % tiles_per_batch, 0))],
        out_specs=[pl.BlockSpec((TM_PROJ, PROJ_WIDTH), lambda i: (i, 0)),
                   pl.BlockSpec((1, FOX_HEADS, TM_PROJ),
                                lambda i: (i // tiles_per_batch, 0, i % tiles_per_batch))],
        out_shape=[jax.ShapeDtypeStruct((t, PROJ_WIDTH), BF16),
                   jax.ShapeDtypeStruct((t // seq, FOX_HEADS, seq), F32)],
        scratch_shapes=[pltpu.VMEM((1, LANES), F32)],
        compiler_params=_params("arbitrary"),
        name="in_proj",
    )(x2, mod, norm_g, w_p, w_ff, b_ff, cos_t, sin_t)


def _attn_kernel(q_ref, k_ref, v_ref, f_ref, o_ref):
    hp = pl.program_id(1)
    qi = pl.program_id(2)
    t_attn = T_ATTN
    q = q_ref[...]
    lane = lax.broadcasted_iota(jnp.int32, (t_attn, LANES), 1)
    zeros = jnp.zeros_like(q)
    qm = (jnp.where(lane < FOX_HEAD_DIM, q, zeros), jnp.where(lane >= FOX_HEAD_DIM, q, zeros))
    lane1 = lax.broadcasted_iota(jnp.int32, (1, LANES), 1)
    own = (jnp.where(lane1 < FOX_HEAD_DIM, 1.0, 0.0).astype(BF16),
           jnp.where(lane1 >= FOX_HEAD_DIM, 1.0, 0.0).astype(BF16))

    def block(r0, nr, ks, nk, carry, masked):
        k = k_ref[pl.ds(ks, nk), :]
        v = v_ref[pl.ds(ks, nk), :]
        v1 = (v * own[0] + own[1], v * own[1] + own[0])
        out = []
        for hh in range(2):
            m, acc = carry[hh]
            s = lax.dot_general(qm[hh][r0:r0 + nr], k, (((1,), (1,)), ((), ())),
                                preferred_element_type=F32)
            s = s - f_ref[0, pl.ds(2 * hp + hh, 1), pl.ds(ks, nk)]
            if masked:
                row = qi * t_attn + r0 + lax.broadcasted_iota(jnp.int32, (nr, nk), 0)
                col = ks + lax.broadcasted_iota(jnp.int32, (nr, nk), 1)
                s = jnp.where(col <= row, s, -jnp.inf)
            m_new = jnp.maximum(m, jnp.max(s, axis=-1, keepdims=True))
            p = jnp.exp2(s - m_new).astype(BF16)
            acc = jnp.exp2(m - m_new) * acc + jnp.dot(p, v1[hh], preferred_element_type=F32)
            out.append((m_new, acc))
        return tuple(out)

    def step(kb, carry):
        return block(0, t_attn, pl.multiple_of(kb * t_attn, t_attn), t_attn, carry, False)

    init = (jnp.full((t_attn, 1), -jnp.inf, F32), jnp.zeros((t_attn, LANES), F32))
    def pair(kp, c):
        return step(2 * kp + 1, step(2 * kp, c))

    carry = lax.fori_loop(0, qi // 2, pair, (init, init))
    carry = lax.cond(qi % 2 == 1, lambda c: step(qi - 1, c), lambda c: c, carry)
    half = t_attn // 2
    ks = pl.multiple_of(qi * t_attn, t_attn)
    rows = lambda c, r0: tuple((m[r0:r0 + half], acc[r0:r0 + half]) for m, acc in c)
    top = block(0, half, ks, half, rows(carry, 0), True)
    bot = block(half, half, ks, t_attn, rows(carry, half), True)
    (_, acc0), (_, acc1) = tuple(
        (jnp.concatenate([mt, mb], axis=0), jnp.concatenate([at, ab], axis=0))
        for (mt, at), (mb, ab) in zip(top, bot))
    o_ref[...] = jnp.where(lane < FOX_HEAD_DIM, acc0 / acc0[:, FOX_HEAD_DIM:FOX_HEAD_DIM + 1],
                           acc1 / acc1[:, 0:1]).astype(BF16)


def _fox_attn(proj, cum_f, batch, seq):
    t = proj.shape[0]
    nq = seq // T_ATTN
    cq, ck, cv = COL_FQ // LANES, COL_FK // LANES, COL_FV // LANES
    return pl.pallas_call(
        _attn_kernel,
        grid=(batch, FOX_HEADS // 2, nq),
        in_specs=[pl.BlockSpec((T_ATTN, LANES), lambda b, hp, qi: (b * nq + qi, cq + hp)),
                  pl.BlockSpec((seq, LANES), lambda b, hp, qi: (b, ck + hp)),
                  pl.BlockSpec((seq, LANES), lambda b, hp, qi: (b, cv + hp)),
                  pl.BlockSpec((1, FOX_HEADS, seq), lambda b, hp, qi: (b, 0, 0))],
        out_specs=pl.BlockSpec((T_ATTN, LANES), lambda b, hp, qi: (b * nq + qi, hp)),
        out_shape=jax.ShapeDtypeStruct((t, FOX_WIDTH), BF16),
        compiler_params=_params("arbitrary", "arbitrary", "arbitrary"),
        name="fox_attn",
    )(proj, proj, proj, cum_f)


def _ret_kernel(q_ref, k_ref, v_ref, rg_ref, gn_ref, dec_ref, zeta_ref, xi_ref, cd_ref,
                o_ref, st_sc):
    ri = pl.program_id(1)
    c = RET_CHUNK

    @pl.when(ri == 0)
    def _():
        st_sc[...] = jnp.zeros_like(st_sc)

    lane = lax.broadcasted_iota(jnp.int32, (c, LANES), 1)
    zeros = jnp.zeros((c, LANES), BF16)
    states = [st_sc[h] for h in range(RET_HEADS)]
    for n in range(T_RET // c):
        rows = slice(n * c, (n + 1) * c)
        for h in range(RET_HEADS):
            pair = slice((h // 2) * LANES, (h // 2 + 1) * LANES)
            vcol = slice(h * RET_V_DIM, (h + 1) * RET_V_DIM)
            sel = (lane < RET_QK_DIM) if h % 2 == 0 else (lane >= RET_QK_DIM)
            qc = jnp.where(sel, q_ref[rows, pair], zeros)
            kc = jnp.where(sel, k_ref[rows, pair], zeros)
            vc = v_ref[rows, vcol]
            sc = lax.dot_general(qc, kc, (((1,), (1,)), ((), ())),
                                 preferred_element_type=F32) * dec_ref[h]
            intra = jnp.dot(sc.astype(BF16), vc, preferred_element_type=F32)
            cross = jnp.dot(qc, states[h].astype(BF16), preferred_element_type=F32) * xi_ref[h]
            o = intra + cross
            kz = (kc.astype(F32) * zeta_ref[h]).astype(BF16)
            kv = lax.dot_general(kz, vc, (((0,), (0,)), ((), ())), preferred_element_type=F32)
            states[h] = cd_ref[h] * states[h] + kv
            mu = jnp.mean(o, axis=-1, keepdims=True)
            d = o - mu
            var = jnp.mean(d * d, axis=-1, keepdims=True)
            on = d * lax.rsqrt(var + NORM_EPS) * gn_ref[:, vcol]
            rg = rg_ref[rows, vcol].astype(F32)
            o_ref[rows, vcol] = (on * _silu(rg)).astype(BF16)
    for h in range(RET_HEADS):
        st_sc[h] = states[h]


def _retention(proj, gn_g, dec_t, zeta_t, xi_t, cd_t, batch, seq):
    t = proj.shape[0]
    nr = seq // T_RET
    tok = lambda b, ri: b * nr + ri
    full = lambda a: pl.BlockSpec(a.shape, lambda b, ri: (0,) * a.ndim)
    return pl.pallas_call(
        _ret_kernel,
        grid=(batch, nr),
        in_specs=[pl.BlockSpec((T_RET, RET_QK_WIDTH), lambda b, ri: (tok(b, ri), COL_RQ // RET_QK_WIDTH)),
                  pl.BlockSpec((T_RET, RET_QK_WIDTH), lambda b, ri: (tok(b, ri), COL_RK // RET_QK_WIDTH)),
                  pl.BlockSpec((T_RET, RET_V_WIDTH), lambda b, ri: (tok(b, ri), COL_RV // RET_V_WIDTH)),
                  pl.BlockSpec((T_RET, RET_V_WIDTH), lambda b, ri: (tok(b, ri), COL_RG // RET_V_WIDTH)),
                  pl.BlockSpec((1, RET_V_WIDTH), lambda b, ri: (0, 0)),
                  full(dec_t), full(zeta_t), full(xi_t), full(cd_t)],
        out_specs=pl.BlockSpec((T_RET, RET_V_WIDTH), lambda b, ri: (tok(b, ri), 0)),
        out_shape=jax.ShapeDtypeStruct((t, RET_V_WIDTH), BF16),
        scratch_shapes=[pltpu.VMEM((RET_HEADS, LANES, RET_V_DIM), F32)],
        compiler_params=_params("arbitrary", "arbitrary"),
        name="retention",
    )(proj, proj, proj, proj, gn_g, dec_t, zeta_t, xi_t, cd_t)


def _retention_tables():
    c = RET_CHUNK
    log_gamma = jnp.log1p(-jnp.exp2(-5.0 - jnp.arange(RET_HEADS, dtype=F32)))
    idx = jnp.arange(c, dtype=F32)
    diff = idx[:, None] - idx[None, :]
    dec = jnp.where(diff >= 0,
                    jnp.exp(jnp.maximum(diff, 0.0)[None] * log_gamma[:, None, None]), 0.0)
    zeta = jnp.exp((c - 1.0 - idx)[None, :] * log_gamma[:, None])
    xi = jnp.exp((idx + 1.0)[None, :] * log_gamma[:, None])
    cd = jnp.exp(c * log_gamma)
    bc = lambda v: jnp.broadcast_to(v[:, :, None], (RET_HEADS, c, LANES))
    cd_t = jnp.broadcast_to(cd[:, None, None], (RET_HEADS, LANES, RET_V_DIM))
    return dec, bc(zeta), bc(xi), cd_t


def _post_kernel(attn_ref, ret_ref, ga_ref, gb_ref, x_ref, mod_ref, g2_ref,
                 wa_ref, wb_ref, wo_ref, wr_ref, wrr_ref, wsg_ref, wsu_ref, wsd_ref,
                 base_ref, h2_ref, score_ref, *, tiles_per_batch):
    b = pl.program_id(0) // tiles_per_batch
    ya = jnp.dot(attn_ref[...], wa_ref[...], preferred_element_type=F32)
    yb = jnp.dot(ret_ref[...], wb_ref[...], preferred_element_type=F32)
    merged = ga_ref[...].astype(F32) * ya + gb_ref[...].astype(F32) * yb
    mix = jnp.dot(merged.astype(BF16), wo_ref[...], preferred_element_type=F32)
    d = D_MODEL
    gate1 = mod_ref[pl.ds(b, 1), 2 * d:3 * d]
    shift2 = mod_ref[pl.ds(b, 1), 3 * d:4 * d]
    scale2 = mod_ref[pl.ds(b, 1), 4 * d:5 * d]
    gate2 = mod_ref[pl.ds(b, 1), 5 * d:6 * d]
    x1 = x_ref[...] + gate1 * mix
    y = x1 * lax.rsqrt(jnp.mean(x1 * x1, axis=-1, keepdims=True) + NORM_EPS) * g2_ref[...]
    h2 = y * (1.0 + scale2) + shift2
    h2_head, h2_rest = _split_bf16(h2)
    nt = (((1,), (1,)), ((), ()))
    logits_t = (lax.dot_general(wr_ref[...], h2_head, nt, preferred_element_type=F32)
                + lax.dot_general(wr_ref[...], h2_rest, nt, preferred_element_type=F32)
                + lax.dot_general(wrr_ref[...], h2_head, nt, preferred_element_type=F32))
    score_ref[...] = _sigmoid(logits_t)
    h2b = h2.astype(BF16)
    h2p = _pack_halves(h2b)
    n_chunks = h2p.shape[1] // LANES
    for cb in range(n_chunks):
        h2_ref[pl.ds(cb, h2p.shape[0], stride=n_chunks), :] = h2p[:, cb * LANES:(cb + 1) * LANES]
    g = jnp.dot(h2b, wsg_ref[...], preferred_element_type=F32)
    u = jnp.dot(h2b, wsu_ref[...], preferred_element_type=F32)
    shared = jnp.dot((_silu(g) * u).astype(BF16), wsd_ref[...], preferred_element_type=F32)
    base_ref[...] = x1 + gate2 * shared


def _post_mix(attn, ret, proj, x2, mod, norm2_g, wa, wb, wo, wr, wrr, wsg, wsu, wsd, seq):
    t = x2.shape[0]
    tm = TM_POST
    kern = functools.partial(_post_kernel, tiles_per_batch=seq // tm)
    full = lambda a: pl.BlockSpec(a.shape, lambda i: (0,) * a.ndim)
    return pl.pallas_call(
        kern,
        grid=(t // tm,),
        in_specs=[pl.BlockSpec((tm, FOX_WIDTH), lambda i: (i, 0)),
                  pl.BlockSpec((tm, RET_V_WIDTH), lambda i: (i, 0)),
                  pl.BlockSpec((tm, D_MODEL), lambda i: (i, COL_GA // D_MODEL)),
                  pl.BlockSpec((tm, D_MODEL), lambda i: (i, COL_GB // D_MODEL)),
                  pl.BlockSpec((tm, D_MODEL), lambda i: (i, 0)),
                  full(mod), full(norm2_g), full(wa), full(wb), full(wo), full(wr), full(wrr),
                  full(wsg), full(wsu), full(wsd)],
        out_specs=[pl.BlockSpec((tm, D_MODEL), lambda i: (i, 0)),
                   pl.BlockSpec((tm * PACKED_ROWS, LANES), lambda i: (i, 0)),
                   pl.BlockSpec((N_EXPERTS, tm), lambda i: (0, i))],
        out_shape=[jax.ShapeDtypeStruct((t, D_MODEL), F32),
                   jax.ShapeDtypeStruct((t * PACKED_ROWS, LANES), jnp.uint32),
                   jax.ShapeDtypeStruct((N_EXPERTS, t), F32)],
        compiler_params=_params("arbitrary"),
        name="post_mix",
    )(attn, ret, proj, proj, x2, mod, norm2_g, wa, wb, wo, wr, wrr, wsg, wsu, wsd)


def _route_kernel(s_ref, bias_ref, e_ref, w_ref, rank_ref, cnt_ref):
    tr = s_ref.shape[1]
    gsz = N_EXPERTS // N_GROUPS

    @pl.when(pl.program_id(0) == 0)
    def _():
        cnt_ref[...] = jnp.zeros_like(cnt_ref)

    s = s_ref[...]
    sel = s + bias_ref[...]
    neg = jnp.float32(-jnp.inf)

    g3 = sel.reshape(N_GROUPS, gsz, tr)
    m1 = jnp.max(g3, axis=1)
    n_max = jnp.sum((g3 == m1[:, None, :]).astype(F32), axis=1)
    m2 = jnp.max(jnp.where(g3 < m1[:, None, :], g3, neg), axis=1)
    gs = m1 + jnp.where(n_max >= 2.0, m1, m2)

    gi = lax.broadcasted_iota(jnp.int32, (N_GROUPS, tr), 0)
    beaten = jnp.zeros((N_GROUPS, tr), F32)
    for j in range(N_GROUPS):
        gj = gs[j:j + 1, :]
        beaten = beaten + jnp.where((gj > gs) | ((gj == gs) & (gi > j)), 1.0, 0.0)
    keep = jnp.where(beaten < float(TOPK_GROUPS), 1.0, 0.0)
    keep_e = jnp.broadcast_to(keep[:, None, :], (N_GROUPS, gsz, tr)).reshape(N_EXPERTS, tr)

    ei = lax.broadcasted_iota(jnp.int32, (N_EXPERTS, tr), 0)
    val = jnp.where(keep_e > 0.0, sel, neg)
    member = jnp.zeros((N_EXPERTS, tr), F32)
    idx_rows, w_rows = [], []
    for _ in range(TOP_K):
        m = jnp.max(val, axis=0, keepdims=True)
        idx = jnp.min(jnp.where(val == m, ei, N_EXPERTS), axis=0, keepdims=True)
        hit = ei == idx
        w_rows.append(jnp.sum(jnp.where(hit, s, 0.0), axis=0, keepdims=True))
        idx_rows.append(idx)
        val = jnp.where(hit, neg, val)
        member = jnp.where(hit, 1.0, member)

    w_sum = w_rows[0]
    for k in range(1, TOP_K):
        w_sum = w_sum + w_rows[k]

    r_i = lax.broadcasted_iota(jnp.int32, (tr, tr), 0)
    c_i = lax.broadcasted_iota(jnp.int32, (tr, tr), 1)
    upper = jnp.where(r_i < c_i, 1.0, 0.0).astype(BF16)
    before = jnp.dot(member.astype(BF16), upper, preferred_element_type=F32) + cnt_ref[...]

    for k in range(TOP_K):
        hit = ei == idx_rows[k]
        e_ref[k:k + 1, :] = idx_rows[k]
        w_ref[k:k + 1, :] = w_rows[k] / w_sum * ROUTED_SCALE
        rank_ref[k:k + 1, :] = jnp.sum(jnp.where(hit, before, 0.0), axis=0,
                                       keepdims=True).astype(jnp.int32)
    cnt_ref[...] = cnt_ref[...] + jnp.sum(member, axis=1, keepdims=True)


def _route(scores_t, bias_col):
    t = scores_t.shape[1]
    tr = T_ROUTE
    row8 = lambda dt: jax.ShapeDtypeStruct((TOP_K, t), dt)
    return pl.pallas_call(
        _route_kernel,
        grid=(t // tr,),
        in_specs=[pl.BlockSpec((N_EXPERTS, tr), lambda i: (0, i)),
                  pl.BlockSpec((N_EXPERTS, 1), lambda i: (0, 0))],
        out_specs=[pl.BlockSpec((TOP_K, tr), lambda i: (0, i)),
                   pl.BlockSpec((TOP_K, tr), lambda i: (0, i)),
                   pl.BlockSpec((TOP_K, tr), lambda i: (0, i)),
                   pl.BlockSpec((N_EXPERTS, 1), lambda i: (0, 0))],
        out_shape=[row8(jnp.int32), row8(F32), row8(jnp.int32),
                   jax.ShapeDtypeStruct((N_EXPERTS, 1), F32)],
        compiler_params=_params("arbitrary"),
        name="route",
    )(scores_t, bias_col)


def _dest_kernel(e_ref, rank_ref, start_ref, dest_ref):
    tr = e_ref.shape[1]
    ei = lax.broadcasted_iota(jnp.int32, (N_EXPERTS, tr), 0)
    start = start_ref[...]
    for k in range(TOP_K):
        hit = ei == e_ref[k:k + 1, :]
        off = jnp.sum(jnp.where(hit, start, 0.0), axis=0, keepdims=True)
        dest_ref[k:k + 1, :] = rank_ref[k:k + 1, :] + off.astype(jnp.int32)


def _dest_rows(top_e, rank, start_col):
    t = top_e.shape[1]
    tr = T_ROUTE
    spec = pl.BlockSpec((TOP_K, tr), lambda i: (0, i))
    return pl.pallas_call(
        _dest_kernel,
        grid=(t // tr,),
        in_specs=[spec, spec, pl.BlockSpec((N_EXPERTS, 1), lambda i: (0, 0))],
        out_specs=spec,
        out_shape=jax.ShapeDtypeStruct((TOP_K, t), jnp.int32),
        compiler_params=_params("arbitrary"),
        name="dest_rows",
    )(top_e, rank, start_col)


def _moe_kernel(order_ref, be_ref, nv_ref, cs_ref, nu_ref, ex_ref, nex_ref, bq_ref,
                h_ref, wg_ref, wu_ref, wd_ref, y_ref, hbuf, *scratch, n_tokens, n_assign):
    xbufs, ybufs = scratch[:MOE_BUFS], scratch[MOE_BUFS:2 * MOE_BUFS]
    wg_sc, wu_sc, wd_sc, hsem = scratch[2 * MOE_BUFS:2 * MOE_BUFS + 4]
    ssems = scratch[2 * MOE_BUFS + 4:3 * MOE_BUFS + 4]
    wg_buf, wu_buf, wd_buf, wsems = scratch[3 * MOE_BUFS + 4:]
    i = pl.program_id(0)
    n_steps = pl.num_programs(0)
    r = ROWS_MOE
    half = D_MODEL // 2
    n_used = nu_ref[0]
    active = i <= n_used

    def copy_rows(blk, xdst):
        base = cs_ref[blk]
        for j in range(r):
            tok = order_ref[base + j] & (n_tokens - 1)
            xdst[j // 8, pl.ds(j % 8, 1), :] = hbuf[tok]

    def compute(blk, xsrc, ydst):
        row = lax.broadcasted_iota(jnp.int32, (r, half), 0)
        xp = jnp.where(row < nv_ref[blk], xsrc[...].reshape(r, half), jnp.uint32(0))
        x_lo, x_hi = _unpack_halves(xp)
        g = (jnp.dot(x_lo, wg_sc[:half, :], preferred_element_type=F32)
             + jnp.dot(x_hi, wg_sc[half:, :], preferred_element_type=F32))
        u = (jnp.dot(x_lo, wu_sc[:half, :], preferred_element_type=F32)
             + jnp.dot(x_hi, wu_sc[half:, :], preferred_element_type=F32))
        y = jnp.dot((_silu(g) * u).astype(BF16), wd_sc[...], preferred_element_type=F32)
        yp = _pack_halves(y.astype(BF16))
        for cb in range(PACKED_ROWS):
            ydst[pl.ds(cb, r, stride=PACKED_ROWS), :] = yp[:, cb * LANES:(cb + 1) * LANES]

    def scatter(blk, n_rows, v, j0, j1):
        base = cs_ref[blk]
        for j in range(j0, j1):
            dst = order_ref[base + j]
            if not isinstance(n_rows, int):
                dst = jnp.where(j < n_rows, dst, n_assign + v * r + j)
            pltpu.make_async_copy(ybufs[v].at[pl.ds(j * PACKED_ROWS, PACKED_ROWS), :],
                                  y_ref.at[pl.ds(dst * PACKED_ROWS, PACKED_ROWS), :],
                                  ssems[v]).start(priority=1)

    def scattered_rows(step_no):
        n_rows = jnp.where(step_no > 0, nv_ref[jnp.maximum(step_no - 1, 0)], 0)
        tail = jnp.maximum(n_rows - MOE_HEAD, 0)
        return MOE_HEAD + (tail + MOE_CHUNK - 1) // MOE_CHUNK * MOE_CHUNK

    def scatter_wait(step_no, v):
        n = scattered_rows(step_no)
        n_sub = pl.multiple_of(n * PACKED_ROWS, MOE_CHUNK * PACKED_ROWS)
        pltpu.make_async_copy(y_ref.at[pl.ds(0, n_sub), :], y_ref.at[pl.ds(r * PACKED_ROWS, n_sub), :],
                              ssems[v]).wait()

    def for_buffer(step_no, fn):
        for v in range(MOE_BUFS):
            pl.when(step_no % MOE_BUFS == v)(functools.partial(fn, v))

    @pl.when(i == 0)
    def _():
        load = pltpu.make_async_copy(h_ref, hbuf, hsem)
        load.start()
        load.wait()
        copy_rows(0, xbufs[0])
        zbuf = ybufs[MOE_BUFS - 1]
        zbuf[...] = jnp.zeros_like(zbuf)
        fills = [pltpu.make_async_copy(
            zbuf, y_ref.at[pl.ds((n_assign + v * r) * PACKED_ROWS, r * PACKED_ROWS), :], hsem)
            for v in range(MOE_BUFS)]
        for fill in fills:
            fill.start()
        for fill in fills:
            fill.wait()

    @pl.when((i >= 2) & (i - 2 <= n_used))
    def _():
        for_buffer(i + MOE_BUFS - 3, functools.partial(scatter_wait, i - 2))

    def weight_copies(q):
        e, s = ex_ref[q], q % MOE_WEIGHT_SLOTS
        return [pltpu.make_async_copy(w_hbm.at[e], w_buf.at[s], wsems.at[s])
                for w_hbm, w_buf in ((wg_ref, wg_buf), (wu_ref, wu_buf), (wd_ref, wd_buf))]

    def fetch(q):
        for c in weight_copies(q):
            c.start()

    @pl.when(i == 0)
    def _():
        fetch(0)

        @pl.when(nex_ref[0] > 1)
        def _():
            fetch(1)

    q = bq_ref[i]

    @pl.when((i == 0) | (be_ref[i] != be_ref[jnp.maximum(i - 1, 0)]))
    def _():
        for c in weight_copies(q):
            c.wait()

        @pl.when(q + 2 < nex_ref[0])
        def _():
            fetch(q + 2)

        s = q % MOE_WEIGHT_SLOTS
        wg_sc[...] = wg_buf[s].astype(BF16)
        wu_sc[...] = wu_buf[s].astype(BF16)
        wd_sc[...] = wd_buf[s].astype(BF16)

    def step(v):
        last = jnp.maximum(i - 1, 0)
        n_rows = jnp.where(i > 0, nv_ref[last], 0)
        src = (v - 1) % MOE_BUFS

        @pl.when(n_rows == r)
        def _():
            scatter(last, r, src, 0, r)
            copy_rows(i + 1, xbufs[(v + 1) % MOE_BUFS])
            compute(i, xbufs[v], ybufs[v])

        @pl.when(n_rows != r)
        def _():
            scatter(last, n_rows, src, 0, MOE_HEAD)
            copy_rows(i + 1, xbufs[(v + 1) % MOE_BUFS])
            compute(i, xbufs[v], ybufs[v])
            for j0 in range(MOE_HEAD, r, MOE_CHUNK):
                pl.when(j0 < n_rows)(
                    functools.partial(scatter, last, n_rows, src, j0, j0 + MOE_CHUNK))

    @pl.when(active)
    def _():
        for_buffer(i, step)

    @pl.when(i == n_steps - 1)
    def _():
        @pl.when(i - 1 <= n_used)
        def _():
            for_buffer(i + MOE_BUFS - 2, functools.partial(scatter_wait, i - 1))

        @pl.when(active)
        def _():
            for_buffer(i + MOE_BUFS - 1, functools.partial(scatter_wait, i))


def _moe_experts(order, tables, h2, w_gate, w_up, w_down):
    t = h2.shape[0]
    assert t & (t - 1) == 0, "token count must be a power of two"
    n_blocks = tables[0].shape[0] - 2
    n_assign = TOP_K * t
    hbm = pl.BlockSpec(memory_space=pl.ANY)
    row_buf = pltpu.VMEM((ROWS_MOE // 8, 8, D_MODEL // 2), jnp.uint32)
    out_buf = pltpu.VMEM((ROWS_MOE * PACKED_ROWS, LANES), jnp.uint32)
    slots = MOE_WEIGHT_SLOTS
    grid_spec = pltpu.PrefetchScalarGridSpec(
        num_scalar_prefetch=1 + len(tables),
        grid=(n_blocks + 1,),
        in_specs=[hbm, hbm, hbm, hbm],
        out_specs=hbm,
        scratch_shapes=([pltpu.VMEM((t, 1, D_MODEL // 2), jnp.uint32)]
                        + [row_buf] * MOE_BUFS + [out_buf] * MOE_BUFS
                        + [pltpu.VMEM((D_MODEL, EXPERT_DIM), BF16),
                           pltpu.VMEM((D_MODEL, EXPERT_DIM), BF16),
                           pltpu.VMEM((EXPERT_DIM, D_MODEL), BF16),
                           pltpu.SemaphoreType.DMA]
                        + [pltpu.SemaphoreType.DMA] * MOE_BUFS
                        + [pltpu.VMEM((slots, D_MODEL, EXPERT_DIM), F32),
                           pltpu.VMEM((slots, D_MODEL, EXPERT_DIM), F32),
                           pltpu.VMEM((slots, EXPERT_DIM, D_MODEL), F32),
                           pltpu.SemaphoreType.DMA((slots,))]),
    )
    return pl.pallas_call(
        functools.partial(_moe_kernel, n_tokens=t, n_assign=n_assign),
        grid_spec=grid_spec,
        out_shape=jax.ShapeDtypeStruct(((n_assign + MOE_BUFS * ROWS_MOE) * PACKED_ROWS, LANES),
                                       jnp.uint32),
        compiler_params=_params("arbitrary"),
        name="moe_experts",
    )(order, *tables, h2, w_gate, w_up, w_down)


def _combine_kernel(base_ref, w_ref, mod_ref, g_ref, *refs, tiles_per_batch, last_layer):
    y_refs, o_ref = refs[:TOP_K], refs[TOP_K]
    tc = base_ref.shape[0]
    lo = [None] * PACKED_ROWS
    hi = [None] * PACKED_ROWS
    for k in range(TOP_K):
        wk = w_ref[:, k:k + 1]
        for cb in range(PACKED_ROWS):
            yp = y_refs[k][pl.ds(cb, tc, stride=PACKED_ROWS), :]
            y_lo = lax.bitcast_convert_type(yp << 16, F32) * wk
            y_hi = lax.bitcast_convert_type(yp & jnp.uint32(0xFFFF0000), F32) * wk
            lo[cb] = y_lo if lo[cb] is None else lo[cb] + y_lo
            hi[cb] = y_hi if hi[cb] is None else hi[cb] + y_hi
    routed = jnp.concatenate(lo + hi, axis=1)
    b = pl.program_id(0) // tiles_per_batch
    gate2 = mod_ref[pl.ds(b, 1), 5 * D_MODEL:6 * D_MODEL]
    x = base_ref[...] + gate2 * routed
    if last_layer:
        x = x * lax.rsqrt(jnp.mean(x * x, axis=-1, keepdims=True) + NORM_EPS) * g_ref[...]
    o_ref[...] = x


def _combine(base, top_w, mod, final_g, y_rows, seq, last_layer):
    t = base.shape[0]
    tc = T_COMBINE
    kern = functools.partial(_combine_kernel, tiles_per_batch=seq // tc, last_layer=last_layer)
    tiles = t // tc
    y_specs = [pl.BlockSpec((tc * PACKED_ROWS, LANES), lambda i, k=k: (k * tiles + i, 0))
               for k in range(TOP_K)]
    return pl.pallas_call(
        kern,
        grid=(tiles,),
        in_specs=[pl.BlockSpec((tc, D_MODEL), lambda i: (i, 0)),
                  pl.BlockSpec((tc, TOP_K), lambda i: (i, 0)),
                  pl.BlockSpec(mod.shape, lambda i: (0, 0)),
                  pl.BlockSpec((1, D_MODEL), lambda i: (0, 0))] + y_specs,
        out_specs=pl.BlockSpec((tc, D_MODEL), lambda i: (i, 0)),
        out_shape=jax.ShapeDtypeStruct((t, D_MODEL), F32),
        compiler_params=_params("arbitrary"),
        name="combine",
    )(base, top_w, mod, final_g, *([y_rows] * TOP_K))


def _block_tables(counts, n_assign):
    r = ROWS_MOE
    cnt = counts.reshape(N_EXPERTS).astype(jnp.int32)
    first = jnp.cumsum(cnt) - cnt
    nblk = (cnt + r - 1) // r
    blk_end = jnp.cumsum(nblk)
    blk_start = blk_end - nblk
    n_blocks = (n_assign + N_EXPERTS * (r - 1) + r - 1) // r
    bi = jnp.arange(n_blocks + 2, dtype=jnp.int32)
    bc = jnp.minimum(bi, blk_end[-1] - 1)
    block_expert = jnp.minimum(jnp.sum((blk_end[None, :] <= bc[:, None]).astype(jnp.int32), axis=1),
                               N_EXPERTS - 1)
    onehot = (block_expert[:, None] == jnp.arange(N_EXPERTS, dtype=jnp.int32)[None, :])
    pick = lambda v: jnp.sum(jnp.where(onehot, v[None, :], 0), axis=1)
    within = bc - pick(blk_start)
    block_valid = jnp.where(bi < blk_end[-1], jnp.clip(pick(cnt) - within * r, 0, r), 0)
    block_start = pick(first) + within * r
    used_upto = jnp.cumsum((cnt > 0).astype(jnp.int32))
    qi = jnp.arange(N_EXPERTS, dtype=jnp.int32)
    expert_list = jnp.minimum(jnp.sum((used_upto[None, :] <= qi[:, None]).astype(jnp.int32), axis=1),
                              N_EXPERTS - 1)
    block_seq = pick(used_upto) - 1
    i32 = lambda v: v.astype(jnp.int32)
    tables = (i32(block_expert), i32(block_valid), i32(block_start), i32(blk_end[-1:]),
              i32(expert_list), i32(used_upto[-1:]), i32(block_seq))
    return first.astype(F32).reshape(N_EXPERTS, 1), tables


def kernel(x, c, w_ada, b_ada, norm1_g, w_in, b_forget, ret_gn_g, w_branch_a, w_branch_b, w_out,
           norm2_g, w_router, router_bias, w_exp_gate, w_exp_up, w_exp_down, w_sh_gate, w_sh_up,
           w_sh_down, final_g):
    batch, seq, d = x.shape
    t = batch * seq
    depth = w_ada.shape[0]
    x2 = x.reshape(t, d)

    half = RET_QK_DIM // 2
    inv_freq = ROPE_BASE ** (-jnp.arange(half, dtype=F32) / half)
    ang = jnp.arange(seq, dtype=F32)[:, None] * inv_freq[None, :]
    cos32, sin32 = jnp.cos(ang), jnp.sin(ang)
    cos_t = jnp.concatenate([cos32, cos32, cos32, cos32], axis=-1)
    sin_t = jnp.concatenate([-sin32, sin32, -sin32, sin32], axis=-1)
    dec_t, zeta_t, xi_t, cd_t = _retention_tables()
    c_pad = jnp.zeros((8, d), F32).at[:batch].set(c)

    for l in range(depth):
        mod = _adaln_mod(c_pad, w_ada[l], b_ada[l])

        w_p, w_ff = _repack_w_in(w_in[l])
        b_ff = jnp.zeros((1, LANES), F32).at[0, :FOX_HEADS].set(b_forget[l].astype(F32))

        proj, cum_f = _in_proj(x2, mod, norm1_g[l].reshape(1, d), w_p, w_ff, b_ff, cos_t, sin_t, seq)

        attn = _fox_attn(proj, cum_f, batch, seq)
        ret = _retention(proj, ret_gn_g[l].reshape(1, RET_V_WIDTH), dec_t, zeta_t, xi_t, cd_t,
                         batch, seq)

        wr_head, wr_rest = _split_bf16(w_router[l].T.astype(F32))
        base, h2, scores_t = _post_mix(
            attn, ret, proj, x2, mod, norm2_g[l].reshape(1, d),
            w_branch_a[l].astype(BF16), w_branch_b[l].astype(BF16), w_out[l].astype(BF16),
            wr_head, wr_rest, w_sh_gate[l].astype(BF16), w_sh_up[l].astype(BF16),
            w_sh_down[l].astype(BF16), seq)

        top_e, top_w, rank, counts = _route(scores_t,
                                            router_bias[l].astype(F32).reshape(N_EXPERTS, 1))
        first, tables = _block_tables(counts, t * TOP_K)
        sorted_pos = _dest_rows(top_e, rank, first).reshape(TOP_K * t)
        order = jnp.concatenate([jnp.argsort(sorted_pos).astype(jnp.int32),
                                 jnp.zeros((ROWS_MOE,), jnp.int32)])
        y = _moe_experts(order, tables, h2.reshape(t, 1, d // 2),
                         w_exp_gate[l], w_exp_up[l], w_exp_down[l])
        x2 = _combine(base, top_w.T, mod, final_g.reshape(1, d), y, seq, l == depth - 1)
    return x2.reshape(batch, seq, d)
```

```python
import functools

import jax
import jax.numpy as jnp
from jax import lax
from jax.experimental import pallas as pl
from jax.experimental.pallas import tpu as pltpu

F32 = jnp.float32
BF16 = jnp.bfloat16

D_MODEL = 1024
FOX_HEADS = 8
FOX_HEAD_DIM = 64
FOX_WIDTH = 512
RET_HEADS = 8
RET_QK_DIM = 64
RET_V_DIM = 128
RET_QK_WIDTH = 512
RET_V_WIDTH = 1024
RET_CHUNK = 256
ROPE_BASE = 10000.0
N_EXPERTS = 256
TOP_K = 8
N_GROUPS = 8
TOPK_GROUPS = 4
EXPERT_DIM = 256
ROUTED_SCALE = 2.5
NORM_EPS = 1e-6
LOG2_E = 1.4426950408889634

LANES = 128
PACKED_ROWS = D_MODEL // 2 // LANES
VMEM_LIMIT = 56 * 1024 * 1024

COL_RV, COL_RG, COL_GA, COL_GB = 0, 1024, 2048, 3072
COL_FQ, COL_FK, COL_FV, COL_RQ, COL_RK = 4096, 4608, 5120, 5632, 6144
PROJ_WIDTH = 6656

TR_REPACK = 128
TM_PROJ = 512
TN_PROJ = 512
T_ATTN = 1024
T_RET = 512
TM_POST = 512
T_ROUTE = 512
ROWS_MOE = 256
MOE_BUFS = 3
MOE_WEIGHT_SLOTS = 3
MOE_HEAD = 128
MOE_CHUNK = 32
T_COMBINE = 512


def _sigmoid(z):
    return 1.0 / (1.0 + jnp.exp(-z))


def _silu(z):
    return z * _sigmoid(z)


def _pack_halves(xb):
    n = xb.shape[1] // 2
    lo = lax.bitcast_convert_type(xb[:, :n].astype(F32), jnp.uint32)
    hi = lax.bitcast_convert_type(xb[:, n:].astype(F32), jnp.uint32)
    return (lo >> 16) | hi


def _unpack_halves(xp):
    lo = lax.bitcast_convert_type(xp << 16, F32)
    hi = lax.bitcast_convert_type(xp & jnp.uint32(0xFFFF0000), F32)
    return lo.astype(BF16), hi.astype(BF16)


def _split_bf16(x):
    head = lax.bitcast_convert_type(
        lax.bitcast_convert_type(x, jnp.uint32) & jnp.uint32(0xFFFF0000), F32)
    return head.astype(BF16), (x - head).astype(BF16)


def _params(*sem):
    return pltpu.CompilerParams(dimension_semantics=sem, vmem_limit_bytes=VMEM_LIMIT)


def _adaln_kernel(c_ref, w_ref, b_ref, o_ref):
    a = _silu(c_ref[...]).astype(BF16)
    o_ref[...] = jnp.dot(a, w_ref[...].astype(BF16), preferred_element_type=F32) + b_ref[...]


def _adaln_mod(c_pad, w_ada, b_ada):
    n = w_ada.shape[1]
    tn = 1024
    return pl.pallas_call(
        _adaln_kernel,
        grid=(n // tn,),
        in_specs=[pl.BlockSpec((8, D_MODEL), lambda j: (0, 0)),
                  pl.BlockSpec((D_MODEL, tn), lambda j: (0, j)),
                  pl.BlockSpec((1, tn), lambda j: (0, j))],
        out_specs=pl.BlockSpec((8, tn), lambda j: (0, j)),
        out_shape=jax.ShapeDtypeStruct((8, n), F32),
        compiler_params=_params("arbitrary"),
        name="adaln_mod",
    )(c_pad, w_ada, b_ada.reshape(1, n))


def _w_in_segments():
    o_ff = 3 * FOX_WIDTH
    o_rq = o_ff + FOX_HEADS
    o_rk = o_rq + RET_QK_WIDTH
    o_rv = o_rk + RET_QK_WIDTH
    o_rg = o_rv + RET_V_WIDTH
    o_ga = o_rg + RET_V_WIDTH
    o_gb = o_ga + D_MODEL
    return ((o_rv, COL_RV, RET_V_WIDTH), (o_rg, COL_RG, RET_V_WIDTH), (o_ga, COL_GA, D_MODEL),
            (o_gb, COL_GB, D_MODEL), (0, COL_FQ, FOX_WIDTH), (FOX_WIDTH, COL_FK, FOX_WIDTH),
            (2 * FOX_WIDTH, COL_FV, FOX_WIDTH), (o_rq, COL_RQ, RET_QK_WIDTH),
            (o_rk, COL_RK, RET_QK_WIDTH)), o_ff


def _repack_kernel(w_ref, wp_ref, wff_ref):
    segments, o_ff = _w_in_segments()
    for src, dst, n in segments:
        wp_ref[:, dst:dst + n] = w_ref[:, src:src + n].astype(BF16)
    rows = w_ref.shape[0]
    wff_ref[...] = jnp.concatenate([w_ref[:, o_ff:o_ff + FOX_HEADS].astype(BF16),
                                    jnp.zeros((rows, LANES - FOX_HEADS), BF16)], axis=1)


def _repack_w_in(w):
    d, n_in = w.shape
    tr = TR_REPACK
    return pl.pallas_call(
        _repack_kernel,
        grid=(d // tr,),
        in_specs=[pl.BlockSpec((tr, n_in), lambda i: (i, 0))],
        out_specs=[pl.BlockSpec((tr, PROJ_WIDTH), lambda i: (i, 0)),
                   pl.BlockSpec((tr, LANES), lambda i: (i, 0))],
        out_shape=[jax.ShapeDtypeStruct((d, PROJ_WIDTH), BF16),
                   jax.ShapeDtypeStruct((d, LANES), BF16)],
        compiler_params=_params("arbitrary"),
        name="repack_w_in",
    )(w)


def _inproj_kernel(x_ref, mod_ref, g_ref, w_ref, wff_ref, bf_ref, cos_ref, sin_ref,
                   proj_ref, cumf_ref, cf_sc, *, tiles_per_batch):
    i = pl.program_id(0)
    b = i // tiles_per_batch
    x = x_ref[...]
    y = x * lax.rsqrt(jnp.mean(x * x, axis=-1, keepdims=True) + NORM_EPS) * g_ref[...]
    shift = mod_ref[pl.ds(b, 1), 0:D_MODEL]
    scale = mod_ref[pl.ds(b, 1), D_MODEL:2 * D_MODEL]
    hb = (y * (1.0 + scale) + shift).astype(BF16)
    z = jnp.dot(hb, wff_ref[...], preferred_element_type=F32) + bf_ref[...]
    logf = jnp.minimum(z, 0.0) - jnp.log1p(jnp.exp(-jnp.abs(z)))
    tm = logf.shape[0]
    tri = jnp.where(lax.broadcasted_iota(jnp.int32, (tm, tm), 1)
                    <= lax.broadcasted_iota(jnp.int32, (tm, tm), 0), 1.0, 0.0).astype(BF16)
    h1, _ = _split_bf16(logf)
    h2, r2 = _split_bf16(logf - h1.astype(F32))
    carry = jnp.where(i % tiles_per_batch == 0, 0.0, cf_sc[...])
    cum = (jnp.dot(tri, h1, preferred_element_type=F32) + jnp.dot(tri, h2, preferred_element_type=F32)
           + jnp.dot(tri, r2, preferred_element_type=F32)) + carry
    cf_sc[...] = cum[tm - 1:tm, :]
    cumf_ref[0] = cum.T[0:FOX_HEADS, :] * LOG2_E

    cos = cos_ref[...]
    sin = sin_ref[...]
    lane = lax.broadcasted_iota(jnp.int32, cos.shape, 1)
    first_half = (lane % RET_QK_DIM) < (RET_QK_DIM // 2)

    for c0 in range(0, PROJ_WIDTH, TN_PROJ):
        cols = slice(c0, c0 + TN_PROJ)
        acc = jnp.dot(hb, w_ref[:, cols], preferred_element_type=F32)
        if COL_GA <= c0 < COL_FQ:
            proj_ref[:, cols] = _sigmoid(acc).astype(BF16)
        elif COL_FQ <= c0 < COL_FK:
            proj_ref[:, cols] = (acc * (LOG2_E * FOX_HEAD_DIM ** -0.5)).astype(BF16)
        elif c0 >= COL_RQ:
            k_scale = RET_QK_DIM ** -0.5 if c0 >= COL_RK else 1.0
            for cb in range(c0, c0 + TN_PROJ, LANES):
                xs = acc[:, cb - c0:cb - c0 + LANES]
                up = pltpu.roll(xs, LANES - RET_QK_DIM // 2, axis=1)
                dn = pltpu.roll(xs, RET_QK_DIM // 2, axis=1)
                rot = xs * cos + jnp.where(first_half, up, dn) * sin
                proj_ref[:, cb:cb + LANES] = (rot * k_scale).astype(BF16)
        else:
            proj_ref[:, cols] = acc.astype(BF16)


def _in_proj(x2, mod, norm_g, w_p, w_ff, b_ff, cos_t, sin_t, seq):
    t = x2.shape[0]
    tiles_per_batch = seq // TM_PROJ
    kern = functools.partial(_inproj_kernel, tiles_per_batch=tiles_per_batch)
    return pl.pallas_call(
        kern,
        grid=(t // TM_PROJ,),
        in_specs=[pl.BlockSpec((TM_PROJ, D_MODEL), lambda i: (i, 0)),
                  pl.BlockSpec(mod.shape, lambda i: (0, 0)),
                  pl.BlockSpec((1, D_MODEL), lambda i: (0, 0)),
                  pl.BlockSpec((D_MODEL, PROJ_WIDTH), lambda i: (0, 0)),
                  pl.BlockSpec((D_MODEL, LANES), lambda i: (0, 0)),
                  pl.BlockSpec((1, LANES), lambda i: (0, 0)),
                  pl.BlockSpec((TM_PROJ, LANES), lambda i: (i % tiles_per_batch, 0)),
                  pl.BlockSpec((TM_PROJ, LANES), lambda i: (i % tiles_per_batch, 0))],
        out_specs=[pl.BlockSpec((TM_PROJ, PROJ_WIDTH), lambda i: (i, 0)),
                   pl.BlockSpec((1, FOX_HEADS, TM_PROJ),
                                lambda i: (i // tiles_per_batch, 0, i % tiles_per_batch))],
        out_shape=[jax.ShapeDtypeStruct((t, PROJ_WIDTH), BF16),
                   jax.ShapeDtypeStruct((t // seq, FOX_HEADS, seq), F32)],
        scratch_shapes=[pltpu.VMEM((1, LANES), F32)],
        compiler_params=_params("arbitrary"),
        name="in_proj",
    )(x2, mod, norm_g, w_p, w_ff, b_ff, cos_t, sin_t)


def _attn_kernel(q_ref, k_ref, v_ref, f_ref, o_ref):
    hp = pl.program_id(1)
    qi = pl.program_id(2)
    t_attn = T_ATTN
    q = q_ref[...]
    lane = lax.broadcasted_iota(jnp.int32, (t_attn, LANES), 1)
    zeros = jnp.zeros_like(q)
    qm = (jnp.where(lane < FOX_HEAD_DIM, q, zeros), jnp.where(lane >= FOX_HEAD_DIM, q, zeros))
    lane1 = lax.broadcasted_iota(jnp.int32, (1, LANES), 1)
    own = (jnp.where(lane1 < FOX_HEAD_DIM, 1.0, 0.0).astype(BF16),
           jnp.where(lane1 >= FOX_HEAD_DIM, 1.0, 0.0).astype(BF16))

    def block(r0, nr, ks, nk, carry, masked):
        k = k_ref[pl.ds(ks, nk), :]
        v = v_ref[pl.ds(ks, nk), :]
        v1 = (v * own[0] + own[1], v * own[1] + own[0])
        out = []
        for hh in range(2):
            m, acc = carry[hh]
            s = lax.dot_general(qm[hh][r0:r0 + nr], k, (((1,), (1,)), ((), ())),
                                preferred_element_type=F32)
            s = s - f_ref[0, pl.ds(2 * hp + hh, 1), pl.ds(ks, nk)]
            if masked:
                row = qi * t_attn + r0 + lax.broadcasted_iota(jnp.int32, (nr, nk), 0)
                col = ks + lax.broadcasted_iota(jnp.int32, (nr, nk), 1)
                s = jnp.where(col <= row, s, -jnp.inf)
            m_new = jnp.maximum(m, jnp.max(s, axis=-1, keepdims=True))
            p = jnp.exp2(s - m_new).astype(BF16)
            acc = jnp.exp2(m - m_new) * acc + jnp.dot(p, v1[hh], preferred_element_type=F32)
            out.append((m_new, acc))
        return tuple(out)

    def step(kb, carry):
        return block(0, t_attn, pl.multiple_of(kb * t_attn, t_attn), t_attn, carry, False)

    init = (jnp.full((t_attn, 1), -jnp.inf, F32), jnp.zeros((t_attn, LANES), F32))
    def pair(kp, c):
        return step(2 * kp + 1, step(2 * kp, c))

    carry = lax.fori_loop(0, qi // 2, pair, (init, init))
    carry = lax.cond(qi % 2 == 1, lambda c: step(qi - 1, c), lambda c: c, carry)
    half = t_attn // 2
    ks = pl.multiple_of(qi * t_attn, t_attn)
    rows = lambda c, r0: tuple((m[r0:r0 + half], acc[r0:r0 + half]) for m, acc in c)
    top = block(0, half, ks, half, rows(carry, 0), True)
    bot = block(half, half, ks, t_attn, rows(carry, half), True)
    (_, acc0), (_, acc1) = tuple(
        (jnp.concatenate([mt, mb], axis=0), jnp.concatenate([at, ab], axis=0))
        for (mt, at), (mb, ab) in zip(top, bot))
    o_ref[...] = jnp.where(lane < FOX_HEAD_DIM, acc0 / acc0[:, FOX_HEAD_DIM:FOX_HEAD_DIM + 1],
                           acc1 / acc1[:, 0:1]).astype(BF16)


def _fox_attn(proj, cum_f, batch, seq):
    t = proj.shape[0]
    nq = seq // T_ATTN
    cq, ck, cv = COL_FQ // LANES, COL_FK // LANES, COL_FV // LANES
    return pl.pallas_call(
        _attn_kernel,
        grid=(batch, FOX_HEADS // 2, nq),
        in_specs=[pl.BlockSpec((T_ATTN, LANES), lambda b, hp, qi: (b * nq + qi, cq + hp)),
                  pl.BlockSpec((seq, LANES), lambda b, hp, qi: (b, ck + hp)),
                  pl.BlockSpec((seq, LANES), lambda b, hp, qi: (b, cv + hp)),
                  pl.BlockSpec((1, FOX_HEADS, seq), lambda b, hp, qi: (b, 0, 0))],
        out_specs=pl.BlockSpec((T_ATTN, LANES), lambda b, hp, qi: (b * nq + qi, hp)),
        out_shape=jax.ShapeDtypeStruct((t, FOX_WIDTH), BF16),
        compiler_params=_params("arbitrary", "arbitrary", "arbitrary"),
        name="fox_attn",
    )(proj, proj, proj, cum_f)


def _ret_kernel(q_ref, k_ref, v_ref, rg_ref, gn_ref, dec_ref, zeta_ref, xi_ref, cd_ref,
                o_ref, st_sc):
    ri = pl.program_id(1)
    c = RET_CHUNK

    @pl.when(ri == 0)
    def _():
        st_sc[...] = jnp.zeros_like(st_sc)

    lane = lax.broadcasted_iota(jnp.int32, (c, LANES), 1)
    zeros = jnp.zeros((c, LANES), BF16)
    states = [st_sc[h] for h in range(RET_HEADS)]
    for n in range(T_RET // c):
        rows = slice(n * c, (n + 1) * c)
        for h in range(RET_HEADS):
            pair = slice((h // 2) * LANES, (h // 2 + 1) * LANES)
            vcol = slice(h * RET_V_DIM, (h + 1) * RET_V_DIM)
            sel = (lane < RET_QK_DIM) if h % 2 == 0 else (lane >= RET_QK_DIM)
            qc = jnp.where(sel, q_ref[rows, pair], zeros)
            kc = jnp.where(sel, k_ref[rows, pair], zeros)
            vc = v_ref[rows, vcol]
            sc = lax.dot_general(qc, kc, (((1,), (1,)), ((), ())),
                                 preferred_element_type=F32) * dec_ref[h]
            intra = jnp.dot(sc.astype(BF16), vc, preferred_element_type=F32)
            cross = jnp.dot(qc, states[h].astype(BF16), preferred_element_type=F32) * xi_ref[h]
            o = intra + cross
            kz = (kc.astype(F32) * zeta_ref[h]).astype(BF16)
            kv = lax.dot_general(kz, vc, (((0,), (0,)), ((), ())), preferred_element_type=F32)
            states[h] = cd_ref[h] * states[h] + kv
            mu = jnp.mean(o, axis=-1, keepdims=True)
            d = o - mu
            var = jnp.mean(d * d, axis=-1, keepdims=True)
            on = d * lax.rsqrt(var + NORM_EPS) * gn_ref[:, vcol]
            rg = rg_ref[rows, vcol].astype(F32)
            o_ref[rows, vcol] = (on * _silu(rg)).astype(BF16)
    for h in range(RET_HEADS):
        st_sc[h] = states[h]


def _retention(proj, gn_g, dec_t, zeta_t, xi_t, cd_t, batch, seq):
    t = proj.shape[0]
    nr = seq // T_RET
    tok = lambda b, ri: b * nr + ri
    full = lambda a: pl.BlockSpec(a.shape, lambda b, ri: (0,) * a.ndim)
    return pl.pallas_call(
        _ret_kernel,
        grid=(batch, nr),
        in_specs=[pl.BlockSpec((T_RET, RET_QK_WIDTH), lambda b, ri: (tok(b, ri), COL_RQ // RET_QK_WIDTH)),
                  pl.BlockSpec((T_RET, RET_QK_WIDTH), lambda b, ri: (tok(b, ri), COL_RK // RET_QK_WIDTH)),
                  pl.BlockSpec((T_RET, RET_V_WIDTH), lambda b, ri: (tok(b, ri), COL_RV // RET_V_WIDTH)),
                  pl.BlockSpec((T_RET, RET_V_WIDTH), lambda b, ri: (tok(b, ri), COL_RG // RET_V_WIDTH)),
                  pl.BlockSpec((1, RET_V_WIDTH), lambda b, ri: (0, 0)),
                  full(dec_t), full(zeta_t), full(xi_t), full(cd_t)],
        out_specs=pl.BlockSpec((T_RET, RET_V_WIDTH), lambda b, ri: (tok(b, ri), 0)),
        out_shape=jax.ShapeDtypeStruct((t, RET_V_WIDTH), BF16),
        scratch_shapes=[pltpu.VMEM((RET_HEADS, LANES, RET_V_DIM), F32)],
        compiler_params=_params("arbitrary", "arbitrary"),
        name="retention",
    )(proj, proj, proj, proj, gn_g, dec_t, zeta_t, xi_t, cd_t)


def _retention_tables():
    c = RET_CHUNK
    log_gamma = jnp.log1p(-jnp.exp2(-5.0 - jnp.arange(RET_HEADS, dtype=F32)))
    idx = jnp.arange(c, dtype=F32)
    diff = idx[:, None] - idx[None, :]
    dec = jnp.where(diff >= 0,
                    jnp.exp(jnp.maximum(diff, 0.0)[None] * log_gamma[:, None, None]), 0.0)
    zeta = jnp.exp((c - 1.0 - idx)[None, :] * log_gamma[:, None])
    xi = jnp.exp((idx + 1.0)[None, :] * log_gamma[:, None])
    cd = jnp.exp(c * log_gamma)
    bc = lambda v: jnp.broadcast_to(v[:, :, None], (RET_HEADS, c, LANES))
    cd_t = jnp.broadcast_to(cd[:, None, None], (RET_HEADS, LANES, RET_V_DIM))
    return dec, bc(zeta), bc(xi), cd_t


def _post_kernel(attn_ref, ret_ref, ga_ref, gb_ref, x_ref, mod_ref, g2_ref,
                 wa_ref, wb_ref, wo_ref, wr_ref, wrr_ref, wsg_ref, wsu_ref, wsd_ref,
                 base_ref, h2_ref, score_ref, *, tiles_per_batch):
    b = pl.program_id(0) // tiles_per_batch
    ya = jnp.dot(attn_ref[...], wa_ref[...], preferred_element_type=F32)
    yb = jnp.dot(ret_ref[...], wb_ref[...], preferred_element_type=F32)
    merged = ga_ref[...].astype(F32) * ya + gb_ref[...].astype(F32) * yb
    mix = jnp.dot(merged.astype(BF16), wo_ref[...], preferred_element_type=F32)
    d = D_MODEL
    gate1 = mod_ref[pl.ds(b, 1), 2 * d:3 * d]
    shift2 = mod_ref[pl.ds(b, 1), 3 * d:4 * d]
    scale2 = mod_ref[pl.ds(b, 1), 4 * d:5 * d]
    gate2 = mod_ref[pl.ds(b, 1), 5 * d:6 * d]
    x1 = x_ref[...] + gate1 * mix
    y = x1 * lax.rsqrt(jnp.mean(x1 * x1, axis=-1, keepdims=True) + NORM_EPS) * g2_ref[...]
    h2 = y * (1.0 + scale2) + shift2
    h2_head, h2_rest = _split_bf16(h2)
    nt = (((1,), (1,)), ((), ()))
    logits_t = (lax.dot_general(wr_ref[...], h2_head, nt, preferred_element_type=F32)
                + lax.dot_general(wr_ref[...], h2_rest, nt, preferred_element_type=F32)
                + lax.dot_general(wrr_ref[...], h2_head, nt, preferred_element_type=F32))
    score_ref[...] = _sigmoid(logits_t)
    h2b = h2.astype(BF16)
    h2p = _pack_halves(h2b)
    n_chunks = h2p.shape[1] // LANES
    for cb in range(n_chunks):
        h2_ref[pl.ds(cb, h2p.shape[0], stride=n_chunks), :] = h2p[:, cb * LANES:(cb + 1) * LANES]
    g = jnp.dot(h2b, wsg_ref[...], preferred_element_type=F32)
    u = jnp.dot(h2b, wsu_ref[...], preferred_element_type=F32)
    shared = jnp.dot((_silu(g) * u).astype(BF16), wsd_ref[...], preferred_element_type=F32)
    base_ref[...] = x1 + gate2 * shared


def _post_mix(attn, ret, proj, x2, mod, norm2_g, wa, wb, wo, wr, wrr, wsg, wsu, wsd, seq):
    t = x2.shape[0]
    tm = TM_POST
    kern = functools.partial(_post_kernel, tiles_per_batch=seq // tm)
    full = lambda a: pl.BlockSpec(a.shape, lambda i: (0,) * a.ndim)
    return pl.pallas_call(
        kern,
        grid=(t // tm,),
        in_specs=[pl.BlockSpec((tm, FOX_WIDTH), lambda i: (i, 0)),
                  pl.BlockSpec((tm, RET_V_WIDTH), lambda i: (i, 0)),
                  pl.BlockSpec((tm, D_MODEL), lambda i: (i, COL_GA // D_MODEL)),
                  pl.BlockSpec((tm, D_MODEL), lambda i: (i, COL_GB // D_MODEL)),
                  pl.BlockSpec((tm, D_MODEL), lambda i: (i, 0)),
                  full(mod), full(norm2_g), full(wa), full(wb), full(wo), full(wr), full(wrr),
                  full(wsg), full(wsu), full(wsd)],
        out_specs=[pl.BlockSpec((tm, D_MODEL), lambda i: (i, 0)),
                   pl.BlockSpec((tm * PACKED_ROWS, LANES), lambda i: (i, 0)),
                   pl.BlockSpec((N_EXPERTS, tm), lambda i: (0, i))],
        out_shape=[jax.ShapeDtypeStruct((t, D_MODEL), F32),
                   jax.ShapeDtypeStruct((t * PACKED_ROWS, LANES), jnp.uint32),
                   jax.ShapeDtypeStruct((N_EXPERTS, t), F32)],
        compiler_params=_params("arbitrary"),
        name="post_mix",
    )(attn, ret, proj, proj, x2, mod, norm2_g, wa, wb, wo, wr, wrr, wsg, wsu, wsd)


def _route_kernel(s_ref, bias_ref, e_ref, w_ref, rank_ref, cnt_ref):
    tr = s_ref.shape[1]
    gsz = N_EXPERTS // N_GROUPS

    @pl.when(pl.program_id(0) == 0)
    def _():
        cnt_ref[...] = jnp.zeros_like(cnt_ref)

    s = s_ref[...]
    sel = s + bias_ref[...]
    neg = jnp.float32(-jnp.inf)

    g3 = sel.reshape(N_GROUPS, gsz, tr)
    m1 = jnp.max(g3, axis=1)
    n_max = jnp.sum((g3 == m1[:, None, :]).astype(F32), axis=1)
    m2 = jnp.max(jnp.where(g3 < m1[:, None, :], g3, neg), axis=1)
    gs = m1 + jnp.where(n_max >= 2.0, m1, m2)

    gi = lax.broadcasted_iota(jnp.int32, (N_GROUPS, tr), 0)
    beaten = jnp.zeros((N_GROUPS, tr), F32)
    for j in range(N_GROUPS):
        gj = gs[j:j + 1, :]
        beaten = beaten + jnp.where((gj > gs) | ((gj == gs) & (gi > j)), 1.0, 0.0)
    keep = jnp.where(beaten < float(TOPK_GROUPS), 1.0, 0.0)
    keep_e = jnp.broadcast_to(keep[:, None, :], (N_GROUPS, gsz, tr)).reshape(N_EXPERTS, tr)

    ei = lax.broadcasted_iota(jnp.int32, (N_EXPERTS, tr), 0)
    val = jnp.where(keep_e > 0.0, sel, neg)
    member = jnp.zeros((N_EXPERTS, tr), F32)
    idx_rows, w_rows = [], []
    for _ in range(TOP_K):
        m = jnp.max(val, axis=0, keepdims=True)
        idx = jnp.min(jnp.where(val == m, ei, N_EXPERTS), axis=0, keepdims=True)
        hit = ei == idx
        w_rows.append(jnp.sum(jnp.where(hit, s, 0.0), axis=0, keepdims=True))
        idx_rows.append(idx)
        val = jnp.where(hit, neg, val)
        member = jnp.where(hit, 1.0, member)

    w_sum = w_rows[0]
    for k in range(1, TOP_K):
        w_sum = w_sum + w_rows[k]

    r_i = lax.broadcasted_iota(jnp.int32, (tr, tr), 0)
    c_i = lax.broadcasted_iota(jnp.int32, (tr, tr), 1)
    upper = jnp.where(r_i < c_i, 1.0, 0.0).astype(BF16)
    before = jnp.dot(member.astype(BF16), upper, preferred_element_type=F32) + cnt_ref[...]

    for k in range(TOP_K):
        hit = ei == idx_rows[k]
        e_ref[k:k + 1, :] = idx_rows[k]
        w_ref[k:k + 1, :] = w_rows[k] / w_sum * ROUTED_SCALE
        rank_ref[k:k + 1, :] = jnp.sum(jnp.where(hit, before, 0.0), axis=0,
                                       keepdims=True).astype(jnp.int32)
    cnt_ref[...] = cnt_ref[...] + jnp.sum(member, axis=1, keepdims=True)


def _route(scores_t, bias_col):
    t = scores_t.shape[1]
    tr = T_ROUTE
    row8 = lambda dt: jax.ShapeDtypeStruct((TOP_K, t), dt)
    return pl.pallas_call(
        _route_kernel,
        grid=(t // tr,),
        in_specs=[pl.BlockSpec((N_EXPERTS, tr), lambda i: (0, i)),
                  pl.BlockSpec((N_EXPERTS, 1), lambda i: (0, 0))],
        out_specs=[pl.BlockSpec((TOP_K, tr), lambda i: (0, i)),
                   pl.BlockSpec((TOP_K, tr), lambda i: (0, i)),
                   pl.BlockSpec((TOP_K, tr), lambda i: (0, i)),
                   pl.BlockSpec((N_EXPERTS, 1), lambda i: (0, 0))],
        out_shape=[row8(jnp.int32), row8(F32), row8(jnp.int32),
                   jax.ShapeDtypeStruct((N_EXPERTS, 1), F32)],
        compiler_params=_params("arbitrary"),
        name="route",
    )(scores_t, bias_col)


def _dest_kernel(e_ref, rank_ref, start_ref, dest_ref):
    tr = e_ref.shape[1]
    ei = lax.broadcasted_iota(jnp.int32, (N_EXPERTS, tr), 0)
    start = start_ref[...]
    for k in range(TOP_K):
        hit = ei == e_ref[k:k + 1, :]
        off = jnp.sum(jnp.where(hit, start, 0.0), axis=0, keepdims=True)
        dest_ref[k:k + 1, :] = rank_ref[k:k + 1, :] + off.astype(jnp.int32)


def _dest_rows(top_e, rank, start_col):
    t = top_e.shape[1]
    tr = T_ROUTE
    spec = pl.BlockSpec((TOP_K, tr), lambda i: (0, i))
    return pl.pallas_call(
        _dest_kernel,
        grid=(t // tr,),
        in_specs=[spec, spec, pl.BlockSpec((N_EXPERTS, 1), lambda i: (0, 0))],
        out_specs=spec,
        out_shape=jax.ShapeDtypeStruct((TOP_K, t), jnp.int32),
        compiler_params=_params("arbitrary"),
        name="dest_rows",
    )(top_e, rank, start_col)


def _moe_kernel(order_ref, be_ref, nv_ref, cs_ref, nu_ref, ex_ref, nex_ref, bq_ref,
                h_ref, wg_ref, wu_ref, wd_ref, y_ref, hbuf, *scratch, n_tokens, n_assign):
    xbufs, ybufs = scratch[:MOE_BUFS], scratch[MOE_BUFS:2 * MOE_BUFS]
    wg_sc, wu_sc, wd_sc, hsem = scratch[2 * MOE_BUFS:2 * MOE_BUFS + 4]
    ssems = scratch[2 * MOE_BUFS + 4:3 * MOE_BUFS + 4]
    wg_buf, wu_buf, wd_buf, wsems = scratch[3 * MOE_BUFS + 4:]
    i = pl.program_id(0)
    n_steps = pl.num_programs(0)
    r = ROWS_MOE
    half = D_MODEL // 2
    n_used = nu_ref[0]
    active = i <= n_used

    def copy_rows(blk, xdst):
        base = cs_ref[blk]
        for j in range(r):
            tok = order_ref[base + j] & (n_tokens - 1)
            xdst[j // 8, pl.ds(j % 8, 1), :] = hbuf[tok]

    def compute(blk, xsrc, ydst):
        row = lax.broadcasted_iota(jnp.int32, (r, half), 0)
        xp = jnp.where(row < nv_ref[blk], xsrc[...].reshape(r, half), jnp.uint32(0))
        x_lo, x_hi = _unpack_halves(xp)
        g = (jnp.dot(x_lo, wg_sc[:half, :], preferred_element_type=F32)
             + jnp.dot(x_hi, wg_sc[half:, :], preferred_element_type=F32))
        u = (jnp.dot(x_lo, wu_sc[:half, :], preferred_element_type=F32)
             + jnp.dot(x_hi, wu_sc[half:, :], preferred_element_type=F32))
        y = jnp.dot((_silu(g) * u).astype(BF16), wd_sc[...], preferred_element_type=F32)
        yp = _pack_halves(y.astype(BF16))
        for cb in range(PACKED_ROWS):
            ydst[pl.ds(cb, r, stride=PACKED_ROWS), :] = yp[:, cb * LANES:(cb + 1) * LANES]

    def scatter(blk, n_rows, v, j0, j1):
        base = cs_ref[blk]
        for j in range(j0, j1):
            dst = order_ref[base + j]
            if not isinstance(n_rows, int):
                dst = jnp.where(j < n_rows, dst, n_assign + v * r + j)
            pltpu.make_async_copy(ybufs[v].at[pl.ds(j * PACKED_ROWS, PACKED_ROWS), :],
                                  y_ref.at[pl.ds(dst * PACKED_ROWS, PACKED_ROWS), :],
                                  ssems[v]).start(priority=j % 2)

    def scattered_rows(step_no):
        n_rows = jnp.where(step_no > 0, nv_ref[jnp.maximum(step_no - 1, 0)], 0)
        tail = jnp.maximum(n_rows - MOE_HEAD, 0)
        return MOE_HEAD + (tail + MOE_CHUNK - 1) // MOE_CHUNK * MOE_CHUNK

    def scatter_wait(step_no, v):
        n = scattered_rows(step_no)
        n_sub = pl.multiple_of(n * PACKED_ROWS, MOE_CHUNK * PACKED_ROWS)
        pltpu.make_async_copy(y_ref.at[pl.ds(0, n_sub), :], y_ref.at[pl.ds(r * PACKED_ROWS, n_sub), :],
                              ssems[v]).wait()

    def for_buffer(step_no, fn):
        for v in range(MOE_BUFS):
            pl.when(step_no % MOE_BUFS == v)(functools.partial(fn, v))

    @pl.when(i == 0)
    def _():
        load = pltpu.make_async_copy(h_ref, hbuf, hsem)
        load.start()
        load.wait()
        copy_rows(0, xbufs[0])
        zbuf = ybufs[MOE_BUFS - 1]
        zbuf[...] = jnp.zeros_like(zbuf)
        fills = [pltpu.make_async_copy(
            zbuf, y_ref.at[pl.ds((n_assign + v * r) * PACKED_ROWS, r * PACKED_ROWS), :], hsem)
            for v in range(MOE_BUFS)]
        for fill in fills:
            fill.start()
        for fill in fills:
            fill.wait()

    @pl.when((i >= 2) & (i - 2 <= n_used))
    def _():
        for_buffer(i + MOE_BUFS - 3, functools.partial(scatter_wait, i - 2))

    def weight_copies(q):
        e, s = ex_ref[q], q % MOE_WEIGHT_SLOTS
        return [pltpu.make_async_copy(w_hbm.at[e], w_buf.at[s], wsems.at[s])
                for w_hbm, w_buf in ((wg_ref, wg_buf), (wu_ref, wu_buf), (wd_ref, wd_buf))]

    def fetch(q):
        for c in weight_copies(q):
            c.start()

    @pl.when(i == 0)
    def _():
        fetch(0)

        @pl.when(nex_ref[0] > 1)
        def _():
            fetch(1)

    q = bq_ref[i]

    @pl.when((i == 0) | (be_ref[i] != be_ref[jnp.maximum(i - 1, 0)]))
    def _():
        for c in weight_copies(q):
            c.wait()

        @pl.when(q + 2 < nex_ref[0])
        def _():
            fetch(q + 2)

        s = q % MOE_WEIGHT_SLOTS
        wg_sc[...] = wg_buf[s].astype(BF16)
        wu_sc[...] = wu_buf[s].astype(BF16)
        wd_sc[...] = wd_buf[s].astype(BF16)

    def step(v):
        last = jnp.maximum(i - 1, 0)
        n_rows = jnp.where(i > 0, nv_ref[last], 0)
        src = (v - 1) % MOE_BUFS

        @pl.when(n_rows == r)
        def _():
            scatter(last, r, src, 0, r)
            copy_rows(i + 1, xbufs[(v + 1) % MOE_BUFS])
            compute(i, xbufs[v], ybufs[v])

        @pl.when(n_rows != r)
        def _():
            scatter(last, n_rows, src, 0, MOE_HEAD)
            copy_rows(i + 1, xbufs[(v + 1) % MOE_BUFS])
            compute(i, xbufs[v], ybufs[v])
            for j0 in range(MOE_HEAD, r, MOE_CHUNK):
                pl.when(j0 < n_rows)(
                    functools.partial(scatter, last, n_rows, src, j0, j0 + MOE_CHUNK))

    @pl.when(active)
    def _():
        for_buffer(i, step)

    @pl.when(i == n_steps - 1)
    def _():
        @pl.when(i - 1 <= n_used)
        def _():
            for_buffer(i + MOE_BUFS - 2, functools.partial(scatter_wait, i - 1))

        @pl.when(active)
        def _():
            for_buffer(i + MOE_BUFS - 1, functools.partial(scatter_wait, i))


def _moe_experts(order, tables, h2, w_gate, w_up, w_down):
    t = h2.shape[0]
    assert t & (t - 1) == 0, "token count must be a power of two"
    n_blocks = tables[0].shape[0] - 2
    n_assign = TOP_K * t
    hbm = pl.BlockSpec(memory_space=pl.ANY)
    row_buf = pltpu.VMEM((ROWS_MOE // 8, 8, D_MODEL // 2), jnp.uint32)
    out_buf = pltpu.VMEM((ROWS_MOE * PACKED_ROWS, LANES), jnp.uint32)
    slots = MOE_WEIGHT_SLOTS
    grid_spec = pltpu.PrefetchScalarGridSpec(
        num_scalar_prefetch=1 + len(tables),
        grid=(n_blocks + 1,),
        in_specs=[hbm, hbm, hbm, hbm],
        out_specs=hbm,
        scratch_shapes=([pltpu.VMEM((t, 1, D_MODEL // 2), jnp.uint32)]
                        + [row_buf] * MOE_BUFS + [out_buf] * MOE_BUFS
                        + [pltpu.VMEM((D_MODEL, EXPERT_DIM), BF16),
                           pltpu.VMEM((D_MODEL, EXPERT_DIM), BF16),
                           pltpu.VMEM((EXPERT_DIM, D_MODEL), BF16),
                           pltpu.SemaphoreType.DMA]
                        + [pltpu.SemaphoreType.DMA] * MOE_BUFS
                        + [pltpu.VMEM((slots, D_MODEL, EXPERT_DIM), F32),
                           pltpu.VMEM((slots, D_MODEL, EXPERT_DIM), F32),
                           pltpu.VMEM((slots, EXPERT_DIM, D_MODEL), F32),
                           pltpu.SemaphoreType.DMA((slots,))]),
    )
    return pl.pallas_call(
        functools.partial(_moe_kernel, n_tokens=t, n_assign=n_assign),
        grid_spec=grid_spec,
        out_shape=jax.ShapeDtypeStruct(((n_assign + MOE_BUFS * ROWS_MOE) * PACKED_ROWS, LANES),
                                       jnp.uint32),
        compiler_params=_params("arbitrary"),
        name="moe_experts",
    )(order, *tables, h2, w_gate, w_up, w_down)


def _combine_kernel(base_ref, w_ref, mod_ref, g_ref, *refs, tiles_per_batch, last_layer):
    y_refs, o_ref = refs[:TOP_K], refs[TOP_K]
    tc = base_ref.shape[0]
    lo = [None] * PACKED_ROWS
    hi = [None] * PACKED_ROWS
    for k in range(TOP_K):
        wk = w_ref[:, k:k + 1]
        for cb in range(PACKED_ROWS):
            yp = y_refs[k][pl.ds(cb, tc, stride=PACKED_ROWS), :]
            y_lo = lax.bitcast_convert_type(yp << 16, F32) * wk
            y_hi = lax.bitcast_convert_type(yp & jnp.uint32(0xFFFF0000), F32) * wk
            lo[cb] = y_lo if lo[cb] is None else lo[cb] + y_lo
            hi[cb] = y_hi if hi[cb] is None else hi[cb] + y_hi
    routed = jnp.concatenate(lo + hi, axis=1)
    b = pl.program_id(0) // tiles_per_batch
    gate2 = mod_ref[pl.ds(b, 1), 5 * D_MODEL:6 * D_MODEL]
    x = base_ref[...] + gate2 * routed
    if last_layer:
        x = x * lax.rsqrt(jnp.mean(x * x, axis=-1, keepdims=True) + NORM_EPS) * g_ref[...]
    o_ref[...] = x


def _combine(base, top_w, mod, final_g, y_rows, seq, last_layer):
    t = base.shape[0]
    tc = T_COMBINE
    kern = functools.partial(_combine_kernel, tiles_per_batch=seq // tc, last_layer=last_layer)
    tiles = t // tc
    y_specs = [pl.BlockSpec((tc * PACKED_ROWS, LANES), lambda i, k=k: (k * tiles + i, 0))
               for k in range(TOP_K)]
    return pl.pallas_call(
        kern,
        grid=(tiles,),
        in_specs=[pl.BlockSpec((tc, D_MODEL), lambda i: (i, 0)),
                  pl.BlockSpec((tc, TOP_K), lambda i: (i, 0)),
                  pl.BlockSpec(mod.shape, lambda i: (0, 0)),
                  pl.BlockSpec((1, D_MODEL), lambda i: (0, 0))] + y_specs,
        out_specs=pl.BlockSpec((tc, D_MODEL), lambda i: (i, 0)),
        out_shape=jax.ShapeDtypeStruct((t, D_MODEL), F32),
        compiler_params=_params("arbitrary"),
        name="combine",
    )(base, top_w, mod, final_g, *([y_rows] * TOP_K))


def _block_tables(counts, n_assign):
    r = ROWS_MOE
    cnt = counts.reshape(N_EXPERTS).astype(jnp.int32)
    first = jnp.cumsum(cnt) - cnt
    nblk = (cnt + r - 1) // r
    blk_end = jnp.cumsum(nblk)
    blk_start = blk_end - nblk
    n_blocks = (n_assign + N_EXPERTS * (r - 1) + r - 1) // r
    bi = jnp.arange(n_blocks + 2, dtype=jnp.int32)
    bc = jnp.minimum(bi, blk_end[-1] - 1)
    block_expert = jnp.minimum(jnp.sum((blk_end[None, :] <= bc[:, None]).astype(jnp.int32), axis=1),
                               N_EXPERTS - 1)
    onehot = (block_expert[:, None] == jnp.arange(N_EXPERTS, dtype=jnp.int32)[None, :])
    pick = lambda v: jnp.sum(jnp.where(onehot, v[None, :], 0), axis=1)
    within = bc - pick(blk_start)
    block_valid = jnp.where(bi < blk_end[-1], jnp.clip(pick(cnt) - within * r, 0, r), 0)
    block_start = pick(first) + within * r
    used_upto = jnp.cumsum((cnt > 0).astype(jnp.int32))
    qi = jnp.arange(N_EXPERTS, dtype=jnp.int32)
    expert_list = jnp.minimum(jnp.sum((used_upto[None, :] <= qi[:, None]).astype(jnp.int32), axis=1),
                              N_EXPERTS - 1)
    block_seq = pick(used_upto) - 1
    i32 = lambda v: v.astype(jnp.int32)
    tables = (i32(block_expert), i32(block_valid), i32(block_start), i32(blk_end[-1:]),
              i32(expert_list), i32(used_upto[-1:]), i32(block_seq))
    return first.astype(F32).reshape(N_EXPERTS, 1), tables


def kernel(x, c, w_ada, b_ada, norm1_g, w_in, b_forget, ret_gn_g, w_branch_a, w_branch_b, w_out,
           norm2_g, w_router, router_bias, w_exp_gate, w_exp_up, w_exp_down, w_sh_gate, w_sh_up,
           w_sh_down, final_g):
    batch, seq, d = x.shape
    t = batch * seq
    depth = w_ada.shape[0]
    x2 = x.reshape(t, d)

    half = RET_QK_DIM // 2
    inv_freq = ROPE_BASE ** (-jnp.arange(half, dtype=F32) / half)
    ang = jnp.arange(seq, dtype=F32)[:, None] * inv_freq[None, :]
    cos32, sin32 = jnp.cos(ang), jnp.sin(ang)
    cos_t = jnp.concatenate([cos32, cos32, cos32, cos32], axis=-1)
    sin_t = jnp.concatenate([-sin32, sin32, -sin32, sin32], axis=-1)
    dec_t, zeta_t, xi_t, cd_t = _retention_tables()
    c_pad = jnp.zeros((8, d), F32).at[:batch].set(c)

    for l in range(depth):
        mod = _adaln_mod(c_pad, w_ada[l], b_ada[l])

        w_p, w_ff = _repack_w_in(w_in[l])
        b_ff = jnp.zeros((1, LANES), F32).at[0, :FOX_HEADS].set(b_forget[l].astype(F32))

        proj, cum_f = _in_proj(x2, mod, norm1_g[l].reshape(1, d), w_p, w_ff, b_ff, cos_t, sin_t, seq)

        attn = _fox_attn(proj, cum_f, batch, seq)
        ret = _retention(proj, ret_gn_g[l].reshape(1, RET_V_WIDTH), dec_t, zeta_t, xi_t, cd_t,
                         batch, seq)

        wr_head, wr_rest = _split_bf16(w_router[l].T.astype(F32))
        base, h2, scores_t = _post_mix(
            attn, ret, proj, x2, mod, norm2_g[l].reshape(1, d),
            w_branch_a[l].astype(BF16), w_branch_b[l].astype(BF16), w_out[l].astype(BF16),
            wr_head, wr_rest, w_sh_gate[l].astype(BF16), w_sh_up[l].astype(BF16),
            w_sh_down[l].astype(BF16), seq)

        top_e, top_w, rank, counts = _route(scores_t,
                                            router_bias[l].astype(F32).reshape(N_EXPERTS, 1))
        first, tables = _block_tables(counts, t * TOP_K)
        sorted_pos = _dest_rows(top_e, rank, first).reshape(TOP_K * t)
        _, order = lax.sort_key_val(sorted_pos, jnp.arange(TOP_K * t, dtype=jnp.int32),
                                    is_stable=False)
        order = jnp.concatenate([order, jnp.zeros((ROWS_MOE,), jnp.int32)])
        y = _moe_experts(order, tables, h2.reshape(t, 1, d // 2),
                         w_exp_gate[l], w_exp_up[l], w_exp_down[l])
        x2 = _combine(base, top_w.T, mod, final_g.reshape(1, d), y, seq, l == depth - 1)
    return x2.reshape(batch, seq, d)
```

```python
import functools

import jax
import jax.numpy as jnp
from jax import lax
from jax.experimental import pallas as pl
from jax.experimental.pallas import tpu as pltpu

F32 = jnp.float32
BF16 = jnp.bfloat16

D_MODEL = 1024
FOX_HEADS = 8
FOX_HEAD_DIM = 64
FOX_WIDTH = 512
RET_HEADS = 8
RET_QK_DIM = 64
RET_V_DIM = 128
RET_QK_WIDTH = 512
RET_V_WIDTH = 1024
RET_CHUNK = 256
ROPE_BASE = 10000.0
N_EXPERTS = 256
TOP_K = 8
N_GROUPS = 8
TOPK_GROUPS = 4
EXPERT_DIM = 256
ROUTED_SCALE = 2.5
NORM_EPS = 1e-6
LOG2_E = 1.4426950408889634

LANES = 128
PACKED_ROWS = D_MODEL // 2 // LANES
VMEM_LIMIT = 56 * 1024 * 1024

COL_RV, COL_RG, COL_GA, COL_GB = 0, 1024, 2048, 3072
COL_FQ, COL_FK, COL_FV, COL_RQ, COL_RK = 4096, 4608, 5120, 5632, 6144
PROJ_WIDTH = 6656

TR_REPACK = 128
TM_PROJ = 512
TN_PROJ = 512
T_ATTN = 1024
T_RET = 512
TM_POST = 512
T_ROUTE = 512
ROWS_MOE = 256
MOE_BUFS = 3
MOE_WEIGHT_SLOTS = 3
MOE_HEAD = 192
MOE_CHUNK = 64
T_COMBINE = 512


def _sigmoid(z):
    return 1.0 / (1.0 + jnp.exp(-z))


def _silu(z):
    return z * _sigmoid(z)


def _pack_halves(xb):
    n = xb.shape[1] // 2
    lo = lax.bitcast_convert_type(xb[:, :n].astype(F32), jnp.uint32)
    hi = lax.bitcast_convert_type(xb[:, n:].astype(F32), jnp.uint32)
    return (lo >> 16) | hi


def _unpack_halves(xp):
    lo = lax.bitcast_convert_type(xp << 16, F32)
    hi = lax.bitcast_convert_type(xp & jnp.uint32(0xFFFF0000), F32)
    return lo.astype(BF16), hi.astype(BF16)


def _split_bf16(x):
    head = lax.bitcast_convert_type(
        lax.bitcast_convert_type(x, jnp.uint32) & jnp.uint32(0xFFFF0000), F32)
    return head.astype(BF16), (x - head).astype(BF16)


def _params(*sem):
    return pltpu.CompilerParams(dimension_semantics=sem, vmem_limit_bytes=VMEM_LIMIT)


def _adaln_kernel(c_ref, w_ref, b_ref, o_ref):
    a = _silu(c_ref[...]).astype(BF16)
    o_ref[...] = jnp.dot(a, w_ref[...].astype(BF16), preferred_element_type=F32) + b_ref[...]


def _adaln_mod(c_pad, w_ada, b_ada):
    n = w_ada.shape[1]
    tn = 1024
    return pl.pallas_call(
        _adaln_kernel,
        grid=(n // tn,),
        in_specs=[pl.BlockSpec((8, D_MODEL), lambda j: (0, 0)),
                  pl.BlockSpec((D_MODEL, tn), lambda j: (0, j)),
                  pl.BlockSpec((1, tn), lambda j: (0, j))],
        out_specs=pl.BlockSpec((8, tn), lambda j: (0, j)),
        out_shape=jax.ShapeDtypeStruct((8, n), F32),
        compiler_params=_params("arbitrary"),
        name="adaln_mod",
    )(c_pad, w_ada, b_ada.reshape(1, n))


def _w_in_segments():
    o_ff = 3 * FOX_WIDTH
    o_rq = o_ff + FOX_HEADS
    o_rk = o_rq + RET_QK_WIDTH
    o_rv = o_rk + RET_QK_WIDTH
    o_rg = o_rv + RET_V_WIDTH
    o_ga = o_rg + RET_V_WIDTH
    o_gb = o_ga + D_MODEL
    return ((o_rv, COL_RV, RET_V_WIDTH), (o_rg, COL_RG, RET_V_WIDTH), (o_ga, COL_GA, D_MODEL),
            (o_gb, COL_GB, D_MODEL), (0, COL_FQ, FOX_WIDTH), (FOX_WIDTH, COL_FK, FOX_WIDTH),
            (2 * FOX_WIDTH, COL_FV, FOX_WIDTH), (o_rq, COL_RQ, RET_QK_WIDTH),
            (o_rk, COL_RK, RET_QK_WIDTH)), o_ff


def _repack_kernel(w_ref, wp_ref, wff_ref):
    segments, o_ff = _w_in_segments()
    for src, dst, n in segments:
        wp_ref[:, dst:dst + n] = w_ref[:, src:src + n].astype(BF16)
    rows = w_ref.shape[0]
    wff_ref[...] = jnp.concatenate([w_ref[:, o_ff:o_ff + FOX_HEADS].astype(BF16),
                                    jnp.zeros((rows, LANES - FOX_HEADS), BF16)], axis=1)


def _repack_w_in(w_all, layer):
    _, d, n_in = w_all.shape
    tr = TR_REPACK
    return pl.pallas_call(
        _repack_kernel,
        grid=(d // tr,),
        in_specs=[pl.BlockSpec((None, tr, n_in), lambda i: (layer, i, 0))],
        out_specs=[pl.BlockSpec((tr, PROJ_WIDTH), lambda i: (i, 0)),
                   pl.BlockSpec((tr, LANES), lambda i: (i, 0))],
        out_shape=[jax.ShapeDtypeStruct((d, PROJ_WIDTH), BF16),
                   jax.ShapeDtypeStruct((d, LANES), BF16)],
        compiler_params=_params("arbitrary"),
        name="repack_w_in",
    )(w_all)


def _inproj_kernel(x_ref, mod_ref, g_ref, w_ref, wff_ref, bf_ref, cos_ref, sin_ref,
                   proj_ref, cumf_ref, cf_sc, *, tiles_per_batch):
    i = pl.program_id(0)
    b = i // tiles_per_batch
    x = x_ref[...]
    y = x * lax.rsqrt(jnp.mean(x * x, axis=-1, keepdims=True) + NORM_EPS) * g_ref[...]
    shift = mod_ref[pl.ds(b, 1), 0:D_MODEL]
    scale = mod_ref[pl.ds(b, 1), D_MODEL:2 * D_MODEL]
    hb = (y * (1.0 + scale) + shift).astype(BF16)
    z = jnp.dot(hb, wff_ref[...], preferred_element_type=F32) + bf_ref[...]
    logf = jnp.minimum(z, 0.0) - jnp.log1p(jnp.exp(-jnp.abs(z)))
    tm = logf.shape[0]
    tri = jnp.where(lax.broadcasted_iota(jnp.int32, (tm, tm), 1)
                    <= lax.broadcasted_iota(jnp.int32, (tm, tm), 0), 1.0, 0.0).astype(BF16)
    h1, _ = _split_bf16(logf)
    h2, r2 = _split_bf16(logf - h1.astype(F32))
    carry = jnp.where(i % tiles_per_batch == 0, 0.0, cf_sc[...])
    cum = (jnp.dot(tri, h1, preferred_element_type=F32) + jnp.dot(tri, h2, preferred_element_type=F32)
           + jnp.dot(tri, r2, preferred_element_type=F32)) + carry
    cf_sc[...] = cum[tm - 1:tm, :]
    cumf_ref[0] = cum.T[0:FOX_HEADS, :] * LOG2_E

    cos = cos_ref[...]
    sin = sin_ref[...]
    lane = lax.broadcasted_iota(jnp.int32, cos.shape, 1)
    first_half = (lane % RET_QK_DIM) < (RET_QK_DIM // 2)

    for c0 in range(0, PROJ_WIDTH, TN_PROJ):
        cols = slice(c0, c0 + TN_PROJ)
        acc = jnp.dot(hb, w_ref[:, cols], preferred_element_type=F32)
        if COL_GA <= c0 < COL_FQ:
            proj_ref[:, cols] = _sigmoid(acc).astype(BF16)
        elif COL_FQ <= c0 < COL_FK:
            proj_ref[:, cols] = (acc * (LOG2_E * FOX_HEAD_DIM ** -0.5)).astype(BF16)
        elif c0 >= COL_RQ:
            k_scale = RET_QK_DIM ** -0.5 if c0 >= COL_RK else 1.0
            for cb in range(c0, c0 + TN_PROJ, LANES):
                xs = acc[:, cb - c0:cb - c0 + LANES]
                up = pltpu.roll(xs, LANES - RET_QK_DIM // 2, axis=1)
                dn = pltpu.roll(xs, RET_QK_DIM // 2, axis=1)
                rot = xs * cos + jnp.where(first_half, up, dn) * sin
                proj_ref[:, cb:cb + LANES] = (rot * k_scale).astype(BF16)
        else:
            proj_ref[:, cols] = acc.astype(BF16)


def _in_proj(x2, mod, norm_g, w_p, w_ff, b_ff, cos_t, sin_t, seq):
    t = x2.shape[0]
    tiles_per_batch = seq // TM_PROJ
    kern = functools.partial(_inproj_kernel, tiles_per_batch=tiles_per_batch)
    return pl.pallas_call(
        kern,
        grid=(t // TM_PROJ,),
        in_specs=[pl.BlockSpec((TM_PROJ, D_MODEL), lambda i: (i, 0)),
                  pl.BlockSpec(mod.shape, lambda i: (0, 0)),
                  pl.BlockSpec((1, D_MODEL), lambda i: (0, 0)),
                  pl.BlockSpec((D_MODEL, PROJ_WIDTH), lambda i: (0, 0)),
                  pl.BlockSpec((D_MODEL, LANES), lambda i: (0, 0)),
                  pl.BlockSpec((1, LANES), lambda i: (0, 0)),
                  pl.BlockSpec((TM_PROJ, LANES), lambda i: (i % tiles_per_batch, 0)),
                  pl.BlockSpec((TM_PROJ, LANES), lambda i: (i % tiles_per_batch, 0))],
        out_specs=[pl.BlockSpec((TM_PROJ, PROJ_WIDTH), lambda i: (i, 0)),
                   pl.BlockSpec((1, FOX_HEADS, TM_PROJ),
                                lambda i: (i // tiles_per_batch, 0, i % tiles_per_batch))],
        out_shape=[jax.ShapeDtypeStruct((t, PROJ_WIDTH), BF16),
                   jax.ShapeDtypeStruct((t // seq, FOX_HEADS, seq), F32)],
        scratch_shapes=[pltpu.VMEM((1, LANES), F32)],
        compiler_params=_params("arbitrary"),
        name="in_proj",
    )(x2, mod, norm_g, w_p, w_ff, b_ff, cos_t, sin_t)


def _attn_kernel(q_ref, k_ref, v_ref, f_ref, o_ref):
    hp = pl.program_id(1)
    qi = pl.program_id(2)
    t_attn = T_ATTN
    q = q_ref[...]
    lane = lax.broadcasted_iota(jnp.int32, (t_attn, LANES), 1)
    zeros = jnp.zeros_like(q)
    qm = (jnp.where(lane < FOX_HEAD_DIM, q, zeros), jnp.where(lane >= FOX_HEAD_DIM, q, zeros))
    lane1 = lax.broadcasted_iota(jnp.int32, (1, LANES), 1)
    own = (jnp.where(lane1 < FOX_HEAD_DIM, 1.0, 0.0).astype(BF16),
           jnp.where(lane1 >= FOX_HEAD_DIM, 1.0, 0.0).astype(BF16))

    def block(r0, nr, ks, nk, carry, masked):
        k = k_ref[pl.ds(ks, nk), :]
        v = v_ref[pl.ds(ks, nk), :]
        v1 = (v * own[0] + own[1], v * own[1] + own[0])
        out = []
        for hh in range(2):
            m, acc = carry[hh]
            s = lax.dot_general(qm[hh][r0:r0 + nr], k, (((1,), (1,)), ((), ())),
                                preferred_element_type=F32)
            s = s - f_ref[0, pl.ds(2 * hp + hh, 1), pl.ds(ks, nk)]
            if masked:
                row = qi * t_attn + r0 + lax.broadcasted_iota(jnp.int32, (nr, nk), 0)
                col = ks + lax.broadcasted_iota(jnp.int32, (nr, nk), 1)
                s = jnp.where(col <= row, s, -jnp.inf)
            m_new = jnp.maximum(m, jnp.max(s, axis=-1, keepdims=True))
            p = jnp.exp2(s - m_new).astype(BF16)
            acc = jnp.exp2(m - m_new) * acc + jnp.dot(p, v1[hh], preferred_element_type=F32)
            out.append((m_new, acc))
        return tuple(out)

    def step(kb, carry):
        return block(0, t_attn, pl.multiple_of(kb * t_attn, t_attn), t_attn, carry, False)

    init = (jnp.full((t_attn, 1), -jnp.inf, F32), jnp.zeros((t_attn, LANES), F32))
    def pair(kp, c):
        return step(2 * kp + 1, step(2 * kp, c))

    carry = lax.fori_loop(0, qi // 2, pair, (init, init))
    carry = lax.cond(qi % 2 == 1, lambda c: step(qi - 1, c), lambda c: c, carry)
    half = t_attn // 2
    ks = pl.multiple_of(qi * t_attn, t_attn)
    rows = lambda c, r0: tuple((m[r0:r0 + half], acc[r0:r0 + half]) for m, acc in c)
    top = block(0, half, ks, half, rows(carry, 0), True)
    bot = block(half, half, ks, t_attn, rows(carry, half), True)
    (_, acc0), (_, acc1) = tuple(
        (jnp.concatenate([mt, mb], axis=0), jnp.concatenate([at, ab], axis=0))
        for (mt, at), (mb, ab) in zip(top, bot))
    o_ref[...] = jnp.where(lane < FOX_HEAD_DIM, acc0 / acc0[:, FOX_HEAD_DIM:FOX_HEAD_DIM + 1],
                           acc1 / acc1[:, 0:1]).astype(BF16)


def _fox_attn(proj, cum_f, batch, seq):
    t = proj.shape[0]
    nq = seq // T_ATTN
    cq, ck, cv = COL_FQ // LANES, COL_FK // LANES, COL_FV // LANES
    return pl.pallas_call(
        _attn_kernel,
        grid=(batch, FOX_HEADS // 2, nq),
        in_specs=[pl.BlockSpec((T_ATTN, LANES), lambda b, hp, qi: (b * nq + qi, cq + hp)),
                  pl.BlockSpec((seq, LANES), lambda b, hp, qi: (b, ck + hp)),
                  pl.BlockSpec((seq, LANES), lambda b, hp, qi: (b, cv + hp)),
                  pl.BlockSpec((1, FOX_HEADS, seq), lambda b, hp, qi: (b, 0, 0))],
        out_specs=pl.BlockSpec((T_ATTN, LANES), lambda b, hp, qi: (b * nq + qi, hp)),
        out_shape=jax.ShapeDtypeStruct((t, FOX_WIDTH), BF16),
        compiler_params=_params("arbitrary", "arbitrary", "arbitrary"),
        name="fox_attn",
    )(proj, proj, proj, cum_f)


def _ret_kernel(q_ref, k_ref, v_ref, rg_ref, gn_ref, dec_ref, zeta_ref, xi_ref, cd_ref,
                o_ref, st_sc):
    ri = pl.program_id(1)
    c = RET_CHUNK

    @pl.when(ri == 0)
    def _():
        st_sc[...] = jnp.zeros_like(st_sc)

    lane = lax.broadcasted_iota(jnp.int32, (c, LANES), 1)
    zeros = jnp.zeros((c, LANES), BF16)
    states = [st_sc[h] for h in range(RET_HEADS)]
    for n in range(T_RET // c):
        rows = slice(n * c, (n + 1) * c)
        for h in range(RET_HEADS):
            pair = slice((h // 2) * LANES, (h // 2 + 1) * LANES)
            vcol = slice(h * RET_V_DIM, (h + 1) * RET_V_DIM)
            sel = (lane < RET_QK_DIM) if h % 2 == 0 else (lane >= RET_QK_DIM)
            qc = jnp.where(sel, q_ref[rows, pair], zeros)
            kc = jnp.where(sel, k_ref[rows, pair], zeros)
            vc = v_ref[rows, vcol]
            sc = lax.dot_general(qc, kc, (((1,), (1,)), ((), ())),
                                 preferred_element_type=F32) * dec_ref[h]
            intra = jnp.dot(sc.astype(BF16), vc, preferred_element_type=F32)
            cross = jnp.dot(qc, states[h].astype(BF16), preferred_element_type=F32) * xi_ref[h]
            o = intra + cross
            kz = (kc.astype(F32) * zeta_ref[h]).astype(BF16)
            kv = lax.dot_general(kz, vc, (((0,), (0,)), ((), ())), preferred_element_type=F32)
            states[h] = cd_ref[h] * states[h] + kv
            mu = jnp.mean(o, axis=-1, keepdims=True)
            d = o - mu
            var = jnp.mean(d * d, axis=-1, keepdims=True)
            on = d * lax.rsqrt(var + NORM_EPS) * gn_ref[:, vcol]
            rg = rg_ref[rows, vcol].astype(F32)
            o_ref[rows, vcol] = (on * _silu(rg)).astype(BF16)
    for h in range(RET_HEADS):
        st_sc[h] = states[h]


def _retention(proj, gn_g, dec_t, zeta_t, xi_t, cd_t, batch, seq):
    t = proj.shape[0]
    nr = seq // T_RET
    tok = lambda b, ri: b * nr + ri
    full = lambda a: pl.BlockSpec(a.shape, lambda b, ri: (0,) * a.ndim)
    return pl.pallas_call(
        _ret_kernel,
        grid=(batch, nr),
        in_specs=[pl.BlockSpec((T_RET, RET_QK_WIDTH), lambda b, ri: (tok(b, ri), COL_RQ // RET_QK_WIDTH)),
                  pl.BlockSpec((T_RET, RET_QK_WIDTH), lambda b, ri: (tok(b, ri), COL_RK // RET_QK_WIDTH)),
                  pl.BlockSpec((T_RET, RET_V_WIDTH), lambda b, ri: (tok(b, ri), COL_RV // RET_V_WIDTH)),
                  pl.BlockSpec((T_RET, RET_V_WIDTH), lambda b, ri: (tok(b, ri), COL_RG // RET_V_WIDTH)),
                  pl.BlockSpec((1, RET_V_WIDTH), lambda b, ri: (0, 0)),
                  full(dec_t), full(zeta_t), full(xi_t), full(cd_t)],
        out_specs=pl.BlockSpec((T_RET, RET_V_WIDTH), lambda b, ri: (tok(b, ri), 0)),
        out_shape=jax.ShapeDtypeStruct((t, RET_V_WIDTH), BF16),
        scratch_shapes=[pltpu.VMEM((RET_HEADS, LANES, RET_V_DIM), F32)],
        compiler_params=_params("arbitrary", "arbitrary"),
        name="retention",
    )(proj, proj, proj, proj, gn_g, dec_t, zeta_t, xi_t, cd_t)


def _retention_tables():
    c = RET_CHUNK
    log_gamma = jnp.log1p(-jnp.exp2(-5.0 - jnp.arange(RET_HEADS, dtype=F32)))
    idx = jnp.arange(c, dtype=F32)
    diff = idx[:, None] - idx[None, :]
    dec = jnp.where(diff >= 0,
                    jnp.exp(jnp.maximum(diff, 0.0)[None] * log_gamma[:, None, None]), 0.0)
    zeta = jnp.exp((c - 1.0 - idx)[None, :] * log_gamma[:, None])
    xi = jnp.exp((idx + 1.0)[None, :] * log_gamma[:, None])
    cd = jnp.exp(c * log_gamma)
    bc = lambda v: jnp.broadcast_to(v[:, :, None], (RET_HEADS, c, LANES))
    cd_t = jnp.broadcast_to(cd[:, None, None], (RET_HEADS, LANES, RET_V_DIM))
    return dec, bc(zeta), bc(xi), cd_t


def _post_kernel(attn_ref, ret_ref, ga_ref, gb_ref, x_ref, mod_ref, g2_ref,
                 wa_ref, wb_ref, wo_ref, wr_ref, wrr_ref, wsg_ref, wsu_ref, wsd_ref,
                 base_ref, h2_ref, score_ref, *, tiles_per_batch):
    b = pl.program_id(0) // tiles_per_batch
    ya = jnp.dot(attn_ref[...], wa_ref[...], preferred_element_type=F32)
    yb = jnp.dot(ret_ref[...], wb_ref[...], preferred_element_type=F32)
    merged = ga_ref[...].astype(F32) * ya + gb_ref[...].astype(F32) * yb
    mix = jnp.dot(merged.astype(BF16), wo_ref[...], preferred_element_type=F32)
    d = D_MODEL
    gate1 = mod_ref[pl.ds(b, 1), 2 * d:3 * d]
    shift2 = mod_ref[pl.ds(b, 1), 3 * d:4 * d]
    scale2 = mod_ref[pl.ds(b, 1), 4 * d:5 * d]
    gate2 = mod_ref[pl.ds(b, 1), 5 * d:6 * d]
    x1 = x_ref[...] + gate1 * mix
    y = x1 * lax.rsqrt(jnp.mean(x1 * x1, axis=-1, keepdims=True) + NORM_EPS) * g2_ref[...]
    h2 = y * (1.0 + scale2) + shift2
    h2_head, h2_rest = _split_bf16(h2)
    nt = (((1,), (1,)), ((), ()))
    logits_t = (lax.dot_general(wr_ref[...], h2_head, nt, preferred_element_type=F32)
                + lax.dot_general(wr_ref[...], h2_rest, nt, preferred_element_type=F32)
                + lax.dot_general(wrr_ref[...], h2_head, nt, preferred_element_type=F32))
    score_ref[...] = _sigmoid(logits_t)
    h2b = h2.astype(BF16)
    h2p = _pack_halves(h2b)
    n_chunks = h2p.shape[1] // LANES
    for cb in range(n_chunks):
        h2_ref[pl.ds(cb, h2p.shape[0], stride=n_chunks), :] = h2p[:, cb * LANES:(cb + 1) * LANES]
    g = jnp.dot(h2b, wsg_ref[...], preferred_element_type=F32)
    u = jnp.dot(h2b, wsu_ref[...], preferred_element_type=F32)
    shared = jnp.dot((_silu(g) * u).astype(BF16), wsd_ref[...], preferred_element_type=F32)
    base_ref[...] = x1 + gate2 * shared


def _post_mix(attn, ret, proj, x2, mod, norm2_g, wa, wb, wo, wr, wrr, wsg, wsu, wsd, seq):
    t = x2.shape[0]
    tm = TM_POST
    kern = functools.partial(_post_kernel, tiles_per_batch=seq // tm)
    full = lambda a: pl.BlockSpec(a.shape, lambda i: (0,) * a.ndim)
    return pl.pallas_call(
        kern,
        grid=(t // tm,),
        in_specs=[pl.BlockSpec((tm, FOX_WIDTH), lambda i: (i, 0)),
                  pl.BlockSpec((tm, RET_V_WIDTH), lambda i: (i, 0)),
                  pl.BlockSpec((tm, D_MODEL), lambda i: (i, COL_GA // D_MODEL)),
                  pl.BlockSpec((tm, D_MODEL), lambda i: (i, COL_GB // D_MODEL)),
                  pl.BlockSpec((tm, D_MODEL), lambda i: (i, 0)),
                  full(mod), full(norm2_g), full(wa), full(wb), full(wo), full(wr), full(wrr),
                  full(wsg), full(wsu), full(wsd)],
        out_specs=[pl.BlockSpec((tm, D_MODEL), lambda i: (i, 0)),
                   pl.BlockSpec((tm * PACKED_ROWS, LANES), lambda i: (i, 0)),
                   pl.BlockSpec((N_EXPERTS, tm), lambda i: (0, i))],
        out_shape=[jax.ShapeDtypeStruct((t, D_MODEL), F32),
                   jax.ShapeDtypeStruct((t * PACKED_ROWS, LANES), jnp.uint32),
                   jax.ShapeDtypeStruct((N_EXPERTS, t), F32)],
        compiler_params=_params("arbitrary"),
        name="post_mix",
    )(attn, ret, proj, proj, x2, mod, norm2_g, wa, wb, wo, wr, wrr, wsg, wsu, wsd)


def _route_kernel(s_ref, bias_ref, e_ref, w_ref, rank_ref, cnt_ref):
    tr = s_ref.shape[1]
    gsz = N_EXPERTS // N_GROUPS

    @pl.when(pl.program_id(0) == 0)
    def _():
        cnt_ref[...] = jnp.zeros_like(cnt_ref)

    s = s_ref[...]
    sel = s + bias_ref[...]
    neg = jnp.float32(-jnp.inf)

    g3 = sel.reshape(N_GROUPS, gsz, tr)
    m1 = jnp.max(g3, axis=1)
    n_max = jnp.sum((g3 == m1[:, None, :]).astype(F32), axis=1)
    m2 = jnp.max(jnp.where(g3 < m1[:, None, :], g3, neg), axis=1)
    gs = m1 + jnp.where(n_max >= 2.0, m1, m2)

    gi = lax.broadcasted_iota(jnp.int32, (N_GROUPS, tr), 0)
    beaten = jnp.zeros((N_GROUPS, tr), F32)
    for j in range(N_GROUPS):
        gj = gs[j:j + 1, :]
        beaten = beaten + jnp.where((gj > gs) | ((gj == gs) & (gi > j)), 1.0, 0.0)
    keep = jnp.where(beaten < float(TOPK_GROUPS), 1.0, 0.0)
    keep_e = jnp.broadcast_to(keep[:, None, :], (N_GROUPS, gsz, tr)).reshape(N_EXPERTS, tr)

    ei = lax.broadcasted_iota(jnp.int32, (N_EXPERTS, tr), 0)
    val = jnp.where(keep_e > 0.0, sel, neg)
    member = jnp.zeros((N_EXPERTS, tr), F32)
    idx_rows, w_rows = [], []
    for _ in range(TOP_K):
        m = jnp.max(val, axis=0, keepdims=True)
        idx = jnp.min(jnp.where(val == m, ei, N_EXPERTS), axis=0, keepdims=True)
        hit = ei == idx
        w_rows.append(jnp.sum(jnp.where(hit, s, 0.0), axis=0, keepdims=True))
        idx_rows.append(idx)
        val = jnp.where(hit, neg, val)
        member = jnp.where(hit, 1.0, member)

    w_sum = w_rows[0]
    for k in range(1, TOP_K):
        w_sum = w_sum + w_rows[k]

    r_i = lax.broadcasted_iota(jnp.int32, (tr, tr), 0)
    c_i = lax.broadcasted_iota(jnp.int32, (tr, tr), 1)
    upper = jnp.where(r_i < c_i, 1.0, 0.0).astype(BF16)
    before = jnp.dot(member.astype(BF16), upper, preferred_element_type=F32) + cnt_ref[...]

    for k in range(TOP_K):
        hit = ei == idx_rows[k]
        e_ref[k:k + 1, :] = idx_rows[k]
        w_ref[k:k + 1, :] = w_rows[k] / w_sum * ROUTED_SCALE
        rank_ref[k:k + 1, :] = jnp.sum(jnp.where(hit, before, 0.0), axis=0,
                                       keepdims=True).astype(jnp.int32)
    cnt_ref[...] = cnt_ref[...] + jnp.sum(member, axis=1, keepdims=True)


def _route(scores_t, bias_col):
    t = scores_t.shape[1]
    tr = T_ROUTE
    row8 = lambda dt: jax.ShapeDtypeStruct((TOP_K, t), dt)
    return pl.pallas_call(
        _route_kernel,
        grid=(t // tr,),
        in_specs=[pl.BlockSpec((N_EXPERTS, tr), lambda i: (0, i)),
                  pl.BlockSpec((N_EXPERTS, 1), lambda i: (0, 0))],
        out_specs=[pl.BlockSpec((TOP_K, tr), lambda i: (0, i)),
                   pl.BlockSpec((TOP_K, tr), lambda i: (0, i)),
                   pl.BlockSpec((TOP_K, tr), lambda i: (0, i)),
                   pl.BlockSpec((N_EXPERTS, 1), lambda i: (0, 0))],
        out_shape=[row8(jnp.int32), row8(F32), row8(jnp.int32),
                   jax.ShapeDtypeStruct((N_EXPERTS, 1), F32)],
        compiler_params=_params("arbitrary"),
        name="route",
    )(scores_t, bias_col)


def _dest_kernel(e_ref, rank_ref, start_ref, dest_ref):
    tr = e_ref.shape[1]
    ei = lax.broadcasted_iota(jnp.int32, (N_EXPERTS, tr), 0)
    start = start_ref[...]
    for k in range(TOP_K):
        hit = ei == e_ref[k:k + 1, :]
        off = jnp.sum(jnp.where(hit, start, 0.0), axis=0, keepdims=True)
        dest_ref[k:k + 1, :] = rank_ref[k:k + 1, :] + off.astype(jnp.int32)


def _dest_rows(top_e, rank, start_col):
    t = top_e.shape[1]
    tr = T_ROUTE
    spec = pl.BlockSpec((TOP_K, tr), lambda i: (0, i))
    return pl.pallas_call(
        _dest_kernel,
        grid=(t // tr,),
        in_specs=[spec, spec, pl.BlockSpec((N_EXPERTS, 1), lambda i: (0, 0))],
        out_specs=spec,
        out_shape=jax.ShapeDtypeStruct((TOP_K, t), jnp.int32),
        compiler_params=_params("arbitrary"),
        name="dest_rows",
    )(top_e, rank, start_col)


def _moe_kernel(order_ref, be_ref, nv_ref, cs_ref, nu_ref, ex_ref, nex_ref, bq_ref,
                h_ref, wg_ref, wu_ref, wd_ref, y_ref, hbuf, *scratch, n_tokens, n_assign):
    xbufs, ybufs = scratch[:MOE_BUFS], scratch[MOE_BUFS:2 * MOE_BUFS]
    wg_sc, wu_sc, wd_sc, hsem = scratch[2 * MOE_BUFS:2 * MOE_BUFS + 4]
    ssems = scratch[2 * MOE_BUFS + 4:3 * MOE_BUFS + 4]
    wg_buf, wu_buf, wd_buf, wsems = scratch[3 * MOE_BUFS + 4:]
    i = pl.program_id(0)
    n_steps = pl.num_programs(0)
    r = ROWS_MOE
    half = D_MODEL // 2
    n_used = nu_ref[0]
    active = i <= n_used

    def copy_rows(blk, xdst):
        base = cs_ref[blk]
        for j in range(r):
            tok = order_ref[base + j] & (n_tokens - 1)
            xdst[j // 8, pl.ds(j % 8, 1), :] = hbuf[tok]

    def compute(blk, xsrc, ydst):
        row = lax.broadcasted_iota(jnp.int32, (r, half), 0)
        xp = jnp.where(row < nv_ref[blk], xsrc[...].reshape(r, half), jnp.uint32(0))
        x_lo, x_hi = _unpack_halves(xp)
        g = (jnp.dot(x_lo, wg_sc[:half, :], preferred_element_type=F32)
             + jnp.dot(x_hi, wg_sc[half:, :], preferred_element_type=F32))
        u = (jnp.dot(x_lo, wu_sc[:half, :], preferred_element_type=F32)
             + jnp.dot(x_hi, wu_sc[half:, :], preferred_element_type=F32))
        y = jnp.dot((_silu(g) * u).astype(BF16), wd_sc[...], preferred_element_type=F32)
        yp = _pack_halves(y.astype(BF16))
        for cb in range(PACKED_ROWS):
            ydst[pl.ds(cb, r, stride=PACKED_ROWS), :] = yp[:, cb * LANES:(cb + 1) * LANES]

    def scatter(blk, n_rows, v, j0, j1):
        base = cs_ref[blk]
        for j in range(j0, j1):
            dst = order_ref[base + j]
            if not isinstance(n_rows, int):
                dst = jnp.where(j < n_rows, dst, n_assign + v * r + j)
            pltpu.make_async_copy(ybufs[v].at[pl.ds(j * PACKED_ROWS, PACKED_ROWS), :],
                                  y_ref.at[pl.ds(dst * PACKED_ROWS, PACKED_ROWS), :],
                                  ssems[v]).start(priority=j % 2)

    def scattered_rows(step_no):
        n_rows = jnp.where(step_no > 0, nv_ref[jnp.maximum(step_no - 1, 0)], 0)
        tail = jnp.maximum(n_rows - MOE_HEAD, 0)
        return MOE_HEAD + (tail + MOE_CHUNK - 1) // MOE_CHUNK * MOE_CHUNK

    def scatter_wait(step_no, v):
        n = scattered_rows(step_no)
        n_sub = pl.multiple_of(n * PACKED_ROWS, MOE_CHUNK * PACKED_ROWS)
        pltpu.make_async_copy(y_ref.at[pl.ds(0, n_sub), :], y_ref.at[pl.ds(r * PACKED_ROWS, n_sub), :],
                              ssems[v]).wait()

    def for_buffer(step_no, fn):
        for v in range(MOE_BUFS):
            pl.when(step_no % MOE_BUFS == v)(functools.partial(fn, v))

    @pl.when(i == 0)
    def _():
        load = pltpu.make_async_copy(h_ref, hbuf, hsem)
        load.start()
        load.wait()
        copy_rows(0, xbufs[0])
        zbuf = ybufs[MOE_BUFS - 1]
        zbuf[...] = jnp.zeros_like(zbuf)
        fills = [pltpu.make_async_copy(
            zbuf, y_ref.at[pl.ds((n_assign + v * r) * PACKED_ROWS, r * PACKED_ROWS), :], hsem)
            for v in range(MOE_BUFS)]
        for fill in fills:
            fill.start()
        for fill in fills:
            fill.wait()

    @pl.when((i >= 2) & (i - 2 <= n_used))
    def _():
        for_buffer(i + MOE_BUFS - 3, functools.partial(scatter_wait, i - 2))

    def weight_copies(q):
        e, s = ex_ref[q], q % MOE_WEIGHT_SLOTS
        return [pltpu.make_async_copy(w_hbm.at[e], w_buf.at[s], wsems.at[s])
                for w_hbm, w_buf in ((wg_ref, wg_buf), (wu_ref, wu_buf), (wd_ref, wd_buf))]

    def fetch(q):
        for c in weight_copies(q):
            c.start()

    @pl.when(i == 0)
    def _():
        fetch(0)

        @pl.when(nex_ref[0] > 1)
        def _():
            fetch(1)

    q = bq_ref[i]

    @pl.when((i == 0) | (be_ref[i] != be_ref[jnp.maximum(i - 1, 0)]))
    def _():
        for c in weight_copies(q):
            c.wait()

        @pl.when(q + 2 < nex_ref[0])
        def _():
            fetch(q + 2)

        s = q % MOE_WEIGHT_SLOTS
        wg_sc[...] = wg_buf[s].astype(BF16)
        wu_sc[...] = wu_buf[s].astype(BF16)
        wd_sc[...] = wd_buf[s].astype(BF16)

    def step(v):
        last = jnp.maximum(i - 1, 0)
        n_rows = jnp.where(i > 0, nv_ref[last], 0)
        src = (v - 1) % MOE_BUFS

        @pl.when(n_rows == r)
        def _():
            scatter(last, r, src, 0, r)
            copy_rows(i + 1, xbufs[(v + 1) % MOE_BUFS])
            compute(i, xbufs[v], ybufs[v])

        @pl.when(n_rows != r)
        def _():
            scatter(last, n_rows, src, 0, MOE_HEAD)
            copy_rows(i + 1, xbufs[(v + 1) % MOE_BUFS])
            compute(i, xbufs[v], ybufs[v])
            for j0 in range(MOE_HEAD, r, MOE_CHUNK):
                pl.when(j0 < n_rows)(
                    functools.partial(scatter, last, n_rows, src, j0, j0 + MOE_CHUNK))

    @pl.when(active)
    def _():
        for_buffer(i, step)

    @pl.when(i == n_steps - 1)
    def _():
        @pl.when(i - 1 <= n_used)
        def _():
            for_buffer(i + MOE_BUFS - 2, functools.partial(scatter_wait, i - 1))

        @pl.when(active)
        def _():
            for_buffer(i + MOE_BUFS - 1, functools.partial(scatter_wait, i))


def _moe_experts(order, tables, h2, w_gate, w_up, w_down):
    t = h2.shape[0]
    assert t & (t - 1) == 0, "token count must be a power of two"
    n_blocks = tables[0].shape[0] - 2
    n_assign = TOP_K * t
    hbm = pl.BlockSpec(memory_space=pl.ANY)
    row_buf = pltpu.VMEM((ROWS_MOE // 8, 8, D_MODEL // 2), jnp.uint32)
    out_buf = pltpu.VMEM((ROWS_MOE * PACKED_ROWS, LANES), jnp.uint32)
    slots = MOE_WEIGHT_SLOTS
    grid_spec = pltpu.PrefetchScalarGridSpec(
        num_scalar_prefetch=1 + len(tables),
        grid=(n_blocks + 1,),
        in_specs=[hbm, hbm, hbm, hbm],
        out_specs=hbm,
        scratch_shapes=([pltpu.VMEM((t, 1, D_MODEL // 2), jnp.uint32)]
                        + [row_buf] * MOE_BUFS + [out_buf] * MOE_BUFS
                        + [pltpu.VMEM((D_MODEL, EXPERT_DIM), BF16),
                           pltpu.VMEM((D_MODEL, EXPERT_DIM), BF16),
                           pltpu.VMEM((EXPERT_DIM, D_MODEL), BF16),
                           pltpu.SemaphoreType.DMA]
                        + [pltpu.SemaphoreType.DMA] * MOE_BUFS
                        + [pltpu.VMEM((slots, D_MODEL, EXPERT_DIM), F32),
                           pltpu.VMEM((slots, D_MODEL, EXPERT_DIM), F32),
                           pltpu.VMEM((slots, EXPERT_DIM, D_MODEL), F32),
                           pltpu.SemaphoreType.DMA((slots,))]),
    )
    return pl.pallas_call(
        functools.partial(_moe_kernel, n_tokens=t, n_assign=n_assign),
        grid_spec=grid_spec,
        out_shape=jax.ShapeDtypeStruct(((n_assign + MOE_BUFS * ROWS_MOE) * PACKED_ROWS, LANES),
                                       jnp.uint32),
        compiler_params=_params("arbitrary"),
        name="moe_experts",
    )(order, *tables, h2, w_gate, w_up, w_down)


def _combine_kernel(base_ref, w_ref, mod_ref, g_ref, *refs, tiles_per_batch, last_layer):
    y_refs, o_ref = refs[:TOP_K], refs[TOP_K]
    tc = base_ref.shape[0]
    lo = [None] * PACKED_ROWS
    hi = [None] * PACKED_ROWS
    for k in range(TOP_K):
        wk = w_ref[:, k:k + 1]
        for cb in range(PACKED_ROWS):
            yp = y_refs[k][pl.ds(cb, tc, stride=PACKED_ROWS), :]
            y_lo = lax.bitcast_convert_type(yp << 16, F32) * wk
            y_hi = lax.bitcast_convert_type(yp & jnp.uint32(0xFFFF0000), F32) * wk
            lo[cb] = y_lo if lo[cb] is None else lo[cb] + y_lo
            hi[cb] = y_hi if hi[cb] is None else hi[cb] + y_hi
    routed = jnp.concatenate(lo + hi, axis=1)
    b = pl.program_id(0) // tiles_per_batch
    gate2 = mod_ref[pl.ds(b, 1), 5 * D_MODEL:6 * D_MODEL]
    x = base_ref[...] + gate2 * routed
    if last_layer:
        x = x * lax.rsqrt(jnp.mean(x * x, axis=-1, keepdims=True) + NORM_EPS) * g_ref[...]
    o_ref[...] = x


def _combine(base, top_w, mod, final_g, y_rows, seq, last_layer):
    t = base.shape[0]
    tc = T_COMBINE
    kern = functools.partial(_combine_kernel, tiles_per_batch=seq // tc, last_layer=last_layer)
    tiles = t // tc
    y_specs = [pl.BlockSpec((tc * PACKED_ROWS, LANES), lambda i, k=k: (k * tiles + i, 0))
               for k in range(TOP_K)]
    return pl.pallas_call(
        kern,
        grid=(tiles,),
        in_specs=[pl.BlockSpec((tc, D_MODEL), lambda i: (i, 0)),
                  pl.BlockSpec((tc, TOP_K), lambda i: (i, 0)),
                  pl.BlockSpec(mod.shape, lambda i: (0, 0)),
                  pl.BlockSpec((1, D_MODEL), lambda i: (0, 0))] + y_specs,
        out_specs=pl.BlockSpec((tc, D_MODEL), lambda i: (i, 0)),
        out_shape=jax.ShapeDtypeStruct((t, D_MODEL), F32),
        compiler_params=_params("arbitrary"),
        name="combine",
    )(base, top_w, mod, final_g, *([y_rows] * TOP_K))


def _block_tables(counts, n_assign):
    r = ROWS_MOE
    cnt = counts.reshape(N_EXPERTS).astype(jnp.int32)
    first = jnp.cumsum(cnt) - cnt
    nblk = (cnt + r - 1) // r
    blk_end = jnp.cumsum(nblk)
    blk_start = blk_end - nblk
    n_blocks = (n_assign + N_EXPERTS * (r - 1) + r - 1) // r
    bi = jnp.arange(n_blocks + 2, dtype=jnp.int32)
    bc = jnp.minimum(bi, blk_end[-1] - 1)
    block_expert = jnp.minimum(jnp.sum((blk_end[None, :] <= bc[:, None]).astype(jnp.int32), axis=1),
                               N_EXPERTS - 1)
    onehot = (block_expert[:, None] == jnp.arange(N_EXPERTS, dtype=jnp.int32)[None, :])
    pick = lambda v: jnp.sum(jnp.where(onehot, v[None, :], 0), axis=1)
    within = bc - pick(blk_start)
    block_valid = jnp.where(bi < blk_end[-1], jnp.clip(pick(cnt) - within * r, 0, r), 0)
    block_start = pick(first) + within * r
    used_upto = jnp.cumsum((cnt > 0).astype(jnp.int32))
    qi = jnp.arange(N_EXPERTS, dtype=jnp.int32)
    expert_list = jnp.minimum(jnp.sum((used_upto[None, :] <= qi[:, None]).astype(jnp.int32), axis=1),
                              N_EXPERTS - 1)
    block_seq = pick(used_upto) - 1
    i32 = lambda v: v.astype(jnp.int32)
    tables = (i32(block_expert), i32(block_valid), i32(block_start), i32(blk_end[-1:]),
              i32(expert_list), i32(used_upto[-1:]), i32(block_seq))
    return first.astype(F32).reshape(N_EXPERTS, 1), tables


def kernel(x, c, w_ada, b_ada, norm1_g, w_in, b_forget, ret_gn_g, w_branch_a, w_branch_b, w_out,
           norm2_g, w_router, router_bias, w_exp_gate, w_exp_up, w_exp_down, w_sh_gate, w_sh_up,
           w_sh_down, final_g):
    batch, seq, d = x.shape
    t = batch * seq
    depth = w_ada.shape[0]
    x2 = x.reshape(t, d)

    half = RET_QK_DIM // 2
    inv_freq = ROPE_BASE ** (-jnp.arange(half, dtype=F32) / half)
    ang = jnp.arange(seq, dtype=F32)[:, None] * inv_freq[None, :]
    cos32, sin32 = jnp.cos(ang), jnp.sin(ang)
    cos_t = jnp.concatenate([cos32, cos32, cos32, cos32], axis=-1)
    sin_t = jnp.concatenate([-sin32, sin32, -sin32, sin32], axis=-1)
    dec_t, zeta_t, xi_t, cd_t = _retention_tables()
    c_pad = jnp.zeros((8, d), F32).at[:batch].set(c)

    for l in range(depth):
        mod = _adaln_mod(c_pad, w_ada[l], b_ada[l])

        w_p, w_ff = _repack_w_in(w_in, l)
        b_ff = jnp.zeros((1, LANES), F32).at[0, :FOX_HEADS].set(b_forget[l].astype(F32))

        proj, cum_f = _in_proj(x2, mod, norm1_g[l].reshape(1, d), w_p, w_ff, b_ff, cos_t, sin_t, seq)

        attn = _fox_attn(proj, cum_f, batch, seq)
        ret = _retention(proj, ret_gn_g[l].reshape(1, RET_V_WIDTH), dec_t, zeta_t, xi_t, cd_t,
                         batch, seq)

        wr_head, wr_rest = _split_bf16(w_router[l].T.astype(F32))
        base, h2, scores_t = _post_mix(
            attn, ret, proj, x2, mod, norm2_g[l].reshape(1, d),
            w_branch_a[l].astype(BF16), w_branch_b[l].astype(BF16), w_out[l].astype(BF16),
            wr_head, wr_rest, w_sh_gate[l].astype(BF16), w_sh_up[l].astype(BF16),
            w_sh_down[l].astype(BF16), seq)

        top_e, top_w, rank, counts = _route(scores_t,
                                            router_bias[l].astype(F32).reshape(N_EXPERTS, 1))
        first, tables = _block_tables(counts, t * TOP_K)
        sorted_pos = _dest_rows(top_e, rank, first).reshape(TOP_K * t)
        _, order = lax.sort_key_val(sorted_pos, jnp.arange(TOP_K * t, dtype=jnp.int32),
                                    is_stable=False)
        order = jnp.concatenate([order, jnp.zeros((ROWS_MOE,), jnp.int32)])
        y = _moe_experts(order, tables, h2.reshape(t, 1, d // 2),
                         w_exp_gate[l], w_exp_up[l], w_exp_down[l])
        x2 = _combine(base, top_w.T, mod, final_g.reshape(1, d), y, seq, l == depth - 1)
    return x2.reshape(batch, seq, d)
```
